```python
import jax, jax.numpy as jnp
from jax import lax
import numpy as np

D_MODEL = 1024
BATCH = 2
SEQ = 8192
DEPTH = 2

N_MEM = 256
N_EVEN = (DEPTH + 1) // 2
N_ODD = DEPTH // 2
ALPHA = (2.0 * DEPTH) ** 0.25
BETA = (8.0 * DEPTH) ** -0.25
LN_EPS = 1e-5
NEG = -1e30
POOL_WIDTH = D_MODEL // 2
POOL_WINDOWS = (2, 4, 8, 16)
POOL_GROUP = POOL_WIDTH // len(POOL_WINDOWS)
NSA_HEADS = 8
NSA_KV_HEADS = 2
NSA_HEAD_DIM = (D_MODEL // 2) // NSA_HEADS
NSA_GROUP = NSA_HEADS // NSA_KV_HEADS
CMP_BLOCK = 32
CMP_STRIDE = 16
SEL_BLOCK = 64
SEL_COUNT = 16
WINDOW = 512
Q_BLOCK = 128
N_BRANCH = 3
FORCE_SCORE = 1e4
ROPE_THETA = 500000.0
ROT_DIM = NSA_HEAD_DIM // 4
Q_WIDTH = NSA_HEADS * NSA_HEAD_DIM
KV_WIDTH = NSA_KV_HEADS * NSA_HEAD_DIM
GATE_WIDTH = NSA_HEADS * N_BRANCH
EVEN_SPLITS = (POOL_WIDTH, Q_WIDTH, GATE_WIDTH) + (KV_WIDTH,) * (2 * N_BRANCH)
EVEN_IN_WIDTH = sum(EVEN_SPLITS)
RNN_WIDTH = 1280
RNN_BLOCKS = 10
RNN_BLOCK_W = RNN_WIDTH // RNN_BLOCKS
CONV_WIDTH = 4
LRU_C = 8.0
X_HEADS = 4
X_HEAD_DIM = D_MODEL // X_HEADS
FFN_HIDDEN = ((8 * D_MODEL // 3 + 255) // 256) * 256

kernel_name = "hybrid_pool_nsa_rglru_deepnorm"


def layer_norm(x, g, b):
    xf = x.astype(jnp.float32)
    mu = jnp.mean(xf, axis=-1, keepdims=True)
    var = jnp.mean(jnp.square(xf - mu), axis=-1, keepdims=True)
    y = (xf - mu) * lax.rsqrt(var + LN_EPS)
    return (y * g.astype(jnp.float32) + b.astype(jnp.float32)).astype(x.dtype)


def rotary(x, pos):
    half = ROT_DIM // 2
    inv = ROPE_THETA ** (-jnp.arange(half, dtype=jnp.float32) * 2.0 / ROT_DIM)
    ang = pos.astype(jnp.float32)[..., None] * inv
    cos = jnp.cos(ang)[:, :, None, :].astype(x.dtype)
    sin = jnp.sin(ang)[:, :, None, :].astype(x.dtype)
    x1 = x[..., :half]
    x2 = x[..., half:ROT_DIM]
    return jnp.concatenate([x1 * cos - x2 * sin, x2 * cos + x1 * sin, x[..., ROT_DIM:]], axis=-1)


def masked_softmax(s, valid):
    p = jax.nn.softmax(jnp.where(valid, s, NEG), axis=-1)
    return p * valid.astype(p.dtype)


def multiscale_pool(u, w_pool, scale):
    B, S, C = u.shape
    uf = u.astype(jnp.float32)
    csum = jnp.pad(jnp.cumsum(uf, axis=1), ((0, 0), (1, 0), (0, 0)))
    t = jnp.arange(S)
    outs = []
    for g, w in enumerate(POOL_WINDOWS):
        sl = slice(g * POOL_GROUP, (g + 1) * POOL_GROUP)
        cg = csum[..., sl]
        lo = jnp.maximum(t + 1 - w, 0)
        total = cg[:, 1:] - jnp.take(cg, lo, axis=1)
        count = jnp.minimum(t + 1, w).astype(jnp.float32)[None, :, None]
        outs.append(total / count - uf[..., sl])
    pooled = jnp.stack(outs, axis=2).astype(u.dtype)
    mixed = jnp.einsum('bsgc,gcd->bsgd', pooled, w_pool)
    return mixed.reshape(B, S, C) * scale


def nsa_attention(q, gates, k_cmp, v_cmp, k_sel, v_sel, k_win, v_win, positions,
                  cmp_pos_k, cmp_pos_v, cmp_wk, cmp_wv):
    B, S = q.shape[:2]
    G, R, HD = NSA_KV_HEADS, NSA_GROUP, NSA_HEAD_DIM
    scale = HD ** -0.5
    q = rotary(q, positions)
    k_sel = rotary(k_sel, positions)
    k_win = rotary(k_win, positions)

    n_cmp = (S - CMP_BLOCK) // CMP_STRIDE + 1
    starts = jnp.arange(n_cmp) * CMP_STRIDE
    cmp_end = starts + CMP_BLOCK - 1
    idx = starts[:, None] + jnp.arange(CMP_BLOCK)[None, :]

    def compress(raw, pos_emb, w):
        blocks = raw[:, idx] + pos_emb[None, None, :, None, :]
        blocks = blocks.transpose(0, 1, 3, 2, 4).reshape(B, n_cmp, G, CMP_BLOCK * HD)
        return blocks @ w

    kc = rotary(compress(k_cmp, cmp_pos_k, cmp_wk), positions[:, cmp_end])
    vc = compress(v_cmp, cmp_pos_v, cmp_wv)

    n_sb = S // SEL_BLOCK
    n_pick = min(SEL_COUNT, n_sb)
    jb = jnp.arange(n_sb)
    overlap = ((starts[:, None] < (jb[None, :] + 1) * SEL_BLOCK)
               & (starts[:, None] + CMP_BLOCK > jb[None, :] * SEL_BLOCK)).astype(jnp.float32)
    ks_blk = k_sel.reshape(B, n_sb, SEL_BLOCK, G, HD).transpose(0, 3, 1, 2, 4)
    vs_blk = v_sel.reshape(B, n_sb, SEL_BLOCK, G, HD).transpose(0, 3, 1, 2, 4)
    b_ix = jnp.arange(B)[:, None, None, None]
    g_ix = jnp.arange(G)[None, :, None, None]

    kw_pad = jnp.pad(k_win, ((0, 0), (WINDOW, 0), (0, 0), (0, 0)))
    vw_pad = jnp.pad(v_win, ((0, 0), (WINDOW, 0), (0, 0), (0, 0)))

    n_qb = S // Q_BLOCK
    q_blocks = q.reshape(B, n_qb, Q_BLOCK, G, R, HD).transpose(1, 0, 2, 3, 4, 5)
    g_blocks = gates.reshape(B, n_qb, Q_BLOCK, G, R, N_BRANCH).transpose(1, 0, 2, 3, 4, 5)

    def one_block(args):
        qb_i, qb, gb = args
        q0 = qb_i * Q_BLOCK
        t = q0 + jnp.arange(Q_BLOCK)
        s = jnp.einsum('bqgrd,bngd->bgrqn', qb, kc).astype(jnp.float32) * scale
        p_c = masked_softmax(s, cmp_end[None, :] <= t[:, None])
        o_c = jnp.einsum('bgrqn,bngd->bqgrd', p_c.astype(qb.dtype), vc)
        imp = jnp.einsum('bgrqn,nj->bgqj', p_c, overlap)
        cur = t // SEL_BLOCK
        forced = (jb[None, :] == 0) | (jb[None, :] == cur[:, None]) | (jb[None, :] == cur[:, None] - 1)
        future = jb[None, :] > cur[:, None]
        imp = jnp.where(forced, FORCE_SCORE, jnp.where(future, -1.0, imp))
        _, sel = lax.top_k(imp, n_pick)
        k_g = ks_blk[b_ix, g_ix, sel]
        v_g = vs_blk[b_ix, g_ix, sel]
        kpos = sel[..., None] * SEL_BLOCK + jnp.arange(SEL_BLOCK)
        s = jnp.einsum('bqgrd,bgqnld->bgrqnl', qb, k_g).astype(jnp.float32) * scale
        valid = (kpos <= t[:, None, None])[:, :, None]
        m = n_pick * SEL_BLOCK
        p_s = masked_softmax(s.reshape(B, G, R, Q_BLOCK, m), valid.reshape(B, G, 1, Q_BLOCK, m))
        o_s = jnp.einsum('bgrqm,bgqmd->bqgrd', p_s.astype(qb.dtype), v_g.reshape(B, G, Q_BLOCK, m, HD))
        kwb = lax.dynamic_slice_in_dim(kw_pad, q0, WINDOW + Q_BLOCK, axis=1)
        vwb = lax.dynamic_slice_in_dim(vw_pad, q0, WINDOW + Q_BLOCK, axis=1)
        kpw = q0 - WINDOW + jnp.arange(WINDOW + Q_BLOCK)
        valid_w = ((kpw[None, :] <= t[:, None]) & (kpw[None, :] > t[:, None] - WINDOW)
                   & (kpw[None, :] >= 0))
        s = jnp.einsum('bqgrd,bkgd->bgrqk', qb, kwb).astype(jnp.float32) * scale
        p_w = masked_softmax(s, valid_w)
        o_w = jnp.einsum('bgrqk,bkgd->bqgrd', p_w.astype(qb.dtype), vwb)
        return gb[..., 0:1] * o_c + gb[..., 1:2] * o_s + gb[..., 2:3] * o_w

    out = lax.map(one_block, (jnp.arange(n_qb), q_blocks, g_blocks))
    return out.transpose(1, 0, 2, 3, 4, 5).reshape(B, S, NSA_HEADS * HD)


def pool_nsa_mixer(x, positions, w_in, pool_w, pool_scale, cmp_pos_k, cmp_pos_v,
                   cmp_wk, cmp_wv, w_out):
    B, S, _ = x.shape
    u = x @ w_in
    cuts = [int(c) for c in np.cumsum(EVEN_SPLITS)[:-1]]
    pool_in, q, gate_logits, kc, vc, ks, vs, kw, vw = jnp.split(u, cuts, axis=-1)
    kv = lambda a: a.reshape(B, S, NSA_KV_HEADS, NSA_HEAD_DIM)
    q = q.reshape(B, S, NSA_HEADS, NSA_HEAD_DIM)
    gates = jax.nn.sigmoid(gate_logits).reshape(B, S, NSA_HEADS, N_BRANCH)
    pool_out = multiscale_pool(pool_in, pool_w, pool_scale)
    nsa_out = nsa_attention(q, gates, kv(kc), kv(vc), kv(ks), kv(vs), kv(kw), kv(vw), positions,
                            cmp_pos_k, cmp_pos_v, cmp_wk, cmp_wv)
    return jnp.concatenate([pool_out, nsa_out], axis=-1) @ w_out


def rglru_mixer(x, positions, w_in, conv_w, conv_b, wa, ba, wx, bx, lam, w_out):
    B, S, _ = x.shape
    u = x @ w_in
    gate, xr = u[..., :RNN_WIDTH], u[..., RNN_WIDTH:]
    xp = jnp.pad(xr, ((0, 0), (CONV_WIDTH - 1, 0), (0, 0)))
    xc = conv_b
    for k in range(CONV_WIDTH):
        xc = xc + xp[:, k:k + S] * conv_w[k]
    xb = xc.reshape(B, S, RNN_BLOCKS, RNN_BLOCK_W)
    r = jax.nn.sigmoid(jnp.einsum('bshi,hij->bshj', xb, wa).reshape(B, S, RNN_WIDTH) + ba)
    i = jax.nn.sigmoid(jnp.einsum('bshi,hij->bshj', xb, wx).reshape(B, S, RNN_WIDTH) + bx)
    log_a = -LRU_C * r.astype(jnp.float32) * jax.nn.softplus(-lam.astype(jnp.float32))
    a = jnp.exp(log_a)
    mult = jnp.sqrt(-jnp.expm1(2.0 * log_a))
    reset = (positions == 0)[..., None]
    a = jnp.where(reset, 0.0, a)
    mult = jnp.where(reset, 1.0, mult)
    b = mult * (i * xc).astype(jnp.float32)

    def combine(left, right):
        a1, b1 = left
        a2, b2 = right
        return a1 * a2, a2 * b1 + b2

    _, h = lax.associative_scan(combine, (a, b), axis=1)
    y = h.astype(x.dtype) * jax.nn.gelu(gate)
    return y @ w_out


def memory_cross_attention(x, mem, wq, wkv, wo):
    B, S, D = x.shape
    M = mem.shape[1]
    q = (x @ wq).reshape(B, S, X_HEADS, X_HEAD_DIM)
    kv = mem @ wkv
    k = kv[..., :D].reshape(B, M, X_HEADS, X_HEAD_DIM)
    v = kv[..., D:].reshape(B, M, X_HEADS, X_HEAD_DIM)
    s = jnp.einsum('bshd,bmhd->bhsm', q, k).astype(jnp.float32) * (X_HEAD_DIM ** -0.5)
    p = jax.nn.softmax(s, axis=-1).astype(x.dtype)
    o = jnp.einsum('bhsm,bmhd->bshd', p, v).reshape(B, S, D)
    return o @ wo


def swiglu(x, w_up, w_down):
    h = x @ w_up
    return (jax.nn.silu(h[..., :FFN_HIDDEN]) * h[..., FFN_HIDDEN:]) @ w_down


def setup_inputs(seed: int = 0) -> dict:
    key = jax.random.key(seed)
    ks = iter(jax.random.split(key, 40))
    nrm = lambda shape: jax.random.normal(next(ks), shape, jnp.float32)
    dense = lambda shape, fan_in, gain=1.0: nrm(shape) * (fan_in ** -0.5) * gain
    D, HD = D_MODEL, NSA_HEAD_DIM
    u = jax.random.uniform(next(ks), (N_ODD, RNN_WIDTH), jnp.float32, 0.9, 0.999)
    s = u ** (1.0 / LRU_C)
    return {
        "x": nrm((BATCH, SEQ, D)),
        "mem": nrm((BATCH, N_MEM, D)),
        "positions": jnp.broadcast_to(jnp.arange(SEQ, dtype=jnp.int32), (BATCH, SEQ)),
        "e_w_in": dense((N_EVEN, D, EVEN_IN_WIDTH), D),
        "e_pool_w": dense((N_EVEN, len(POOL_WINDOWS), POOL_GROUP, POOL_GROUP), POOL_GROUP),
        "e_pool_scale": 1.0 + 0.02 * nrm((N_EVEN, POOL_WIDTH)),
        "e_cmp_pos_k": 0.1 * nrm((N_EVEN, CMP_BLOCK, HD)),
        "e_cmp_pos_v": 0.1 * nrm((N_EVEN, CMP_BLOCK, HD)),
        "e_cmp_wk": dense((N_EVEN, CMP_BLOCK * HD, HD), CMP_BLOCK * HD),
        "e_cmp_wv": dense((N_EVEN, CMP_BLOCK * HD, HD), CMP_BLOCK * HD),
        "e_w_out": dense((N_EVEN, D, D), D, BETA),
        "o_w_in": dense((N_ODD, D, 2 * RNN_WIDTH), D),
        "o_conv_w": dense((N_ODD, CONV_WIDTH, RNN_WIDTH), CONV_WIDTH),
        "o_conv_b": 0.01 * nrm((N_ODD, RNN_WIDTH)),
        "o_wa": dense((N_ODD, RNN_BLOCKS, RNN_BLOCK_W, RNN_BLOCK_W), RNN_BLOCK_W),
        "o_ba": 0.01 * nrm((N_ODD, RNN_WIDTH)),
        "o_wx": dense((N_ODD, RNN_BLOCKS, RNN_BLOCK_W, RNN_BLOCK_W), RNN_BLOCK_W),
        "o_bx": 0.01 * nrm((N_ODD, RNN_WIDTH)),
        "o_lambda": jnp.log(s) - jnp.log1p(-s),
        "o_w_out": dense((N_ODD, RNN_WIDTH, D), RNN_WIDTH, BETA),
        "x_wq": dense((DEPTH, D, D), D),
        "x_wkv": dense((DEPTH, D, 2 * D), D),
        "x_wo": dense((DEPTH, D, D), D, BETA),
        "f_w_up": dense((DEPTH, D, 2 * FFN_HIDDEN), D),
        "f_w_down": dense((DEPTH, FFN_HIDDEN, D), FFN_HIDDEN, BETA),
        "ln_g": 1.0 + 0.02 * nrm((DEPTH, 3, D)),
        "ln_b": 0.01 * nrm((DEPTH, 3, D)),
    }


def reference(x, mem, positions, e_w_in, e_pool_w, e_pool_scale, e_cmp_pos_k, e_cmp_pos_v,
              e_cmp_wk, e_cmp_wv, e_w_out, o_w_in, o_conv_w, o_conv_b, o_wa, o_ba, o_wx, o_bx,
              o_lambda, o_w_out, x_wq, x_wkv, x_wo, f_w_up, f_w_down, ln_g, ln_b):
    for layer in range(DEPTH):
        j = layer // 2
        if layer % 2 == 0:
            m = pool_nsa_mixer(x, positions, e_w_in[j], e_pool_w[j], e_pool_scale[j],
                               e_cmp_pos_k[j], e_cmp_pos_v[j], e_cmp_wk[j], e_cmp_wv[j], e_w_out[j])
        else:
            m = rglru_mixer(x, positions, o_w_in[j], o_conv_w[j], o_conv_b[j], o_wa[j], o_ba[j],
                            o_wx[j], o_bx[j], o_lambda[j], o_w_out[j])
        x = layer_norm(ALPHA * x + m, ln_g[layer, 0], ln_b[layer, 0])
        c = memory_cross_attention(x, mem, x_wq[layer], x_wkv[layer], x_wo[layer])
        x = layer_norm(ALPHA * x + c, ln_g[layer, 1], ln_b[layer, 1])
        f = swiglu(x, f_w_up[layer], f_w_down[layer])
        x = layer_norm(ALPHA * x + f, ln_g[layer, 2], ln_b[layer, 2])
    return x
```

```python
import functools

import numpy as np
import jax
import jax.numpy as jnp
from jax import lax
from jax.experimental import pallas as pl
from jax.experimental.pallas import tpu as pltpu

F32 = jnp.float32
BF16 = jnp.bfloat16

D_MODEL = 1024
DEPTH = 2
ALPHA = (2.0 * DEPTH) ** 0.25
LN_EPS = 1e-5
NEG = -1e30
POOL_WIDTH = D_MODEL // 2
POOL_WINDOWS = (2, 4, 8, 16)
POOL_GROUP = POOL_WIDTH // len(POOL_WINDOWS)
POOL_HALO = 16
NSA_HEADS = 8
NSA_KV_HEADS = 2
NSA_HEAD_DIM = 64
NSA_GROUP = NSA_HEADS // NSA_KV_HEADS
CMP_BLOCK = 32
CMP_STRIDE = 16
SEL_BLOCK = 64
SEL_COUNT = 16
WINDOW = 512
Q_BLOCK = 128
N_BRANCH = 3
FORCE_SCORE = 1e4
ROPE_THETA = 500000.0
ROT_DIM = NSA_HEAD_DIM // 4
ROT_HALF = ROT_DIM // 2
Q_WIDTH = NSA_HEADS * NSA_HEAD_DIM
KV_WIDTH = NSA_KV_HEADS * NSA_HEAD_DIM
GATE_WIDTH = NSA_HEADS * N_BRANCH
RNN_WIDTH = 1280
RNN_BLOCKS = 10
RNN_BLOCK_W = RNN_WIDTH // RNN_BLOCKS
CONV_WIDTH = 4
CONV_HALO = 8
LRU_C = 8.0
X_HEADS = 4
X_HEAD_DIM = D_MODEL // X_HEADS
FFN_HIDDEN = 2816

LANES = 128
SUBLANES = 8
BF16_ROWS = 16
VMEM_LIMIT = 56 * 1024 * 1024
ROW_TILE = 512
SEL_KEY_TILE = 512
FFN_CHUNK = 1408


def _params(n_axes, vmem=VMEM_LIMIT):
    return pltpu.CompilerParams(dimension_semantics=("arbitrary",) * n_axes,
                                vmem_limit_bytes=vmem)


def _dot(a, b):
    return jnp.dot(a, b, preferred_element_type=F32)


def _dot_nt(a, b):
    return lax.dot_general(a, b, (((1,), (1,)), ((), ())), preferred_element_type=F32)


def _rep_rows(a, k):
    return jnp.concatenate([a] * k, axis=0)


def _rep_lanes(a, k):
    return jnp.concatenate([a] * k, axis=1) if k > 1 else a


def _layer_norm(z, g, b):
    mu = jnp.mean(z, axis=-1, keepdims=True)
    d = z - mu
    var = jnp.mean(d * d, axis=-1, keepdims=True)
    return d * lax.rsqrt(var + LN_EPS) * g + b


def _rope(v, cos, s_lo, s_hi):
    k = v.shape[1] // LANES
    up = pltpu.roll(v, v.shape[1] - ROT_HALF, 1)
    dn = pltpu.roll(v, ROT_HALF, 1)
    return v * _rep_lanes(cos, k) + up * _rep_lanes(s_lo, k) + dn * _rep_lanes(s_hi, k)


def _rope_tables(pos, inv, m_lo, m_hi):
    ang = pos * inv
    sin = jnp.sin(ang)
    return jnp.cos(ang), sin * m_lo, sin * m_hi


def _proj_even_kernel(x_ref, pos_ref, wp_ref, wq_ref, wkv_ref, wg_ref, poolw_ref, pscale_ref,
                      inv_ref, mlo_ref, mhi_ref,
                      pool_o, q_o, kc_o, vc_o, ks_o, vs_o, kw_o, vw_o, gate_o, pbuf):
    i = pl.program_id(1)
    tm = x_ref.shape[0]
    xb = x_ref[...].astype(BF16)
    cos, s_lo, s_hi = _rope_tables(pos_ref[...], inv_ref[...], mlo_ref[...], mhi_ref[...])

    q_o[...] = _rope(_dot(xb, wq_ref[...]), cos, s_lo, s_hi).astype(BF16)
    kv = _dot(xb, wkv_ref[...])
    kc_o[...] = kv[:, 0 * LANES:1 * LANES]
    vc_o[...] = kv[:, 1 * LANES:2 * LANES]
    ks_o[...] = _rope(kv[:, 2 * LANES:3 * LANES], cos, s_lo, s_hi).astype(BF16)
    vs_o[...] = kv[:, 3 * LANES:4 * LANES].astype(BF16)
    kw_o[...] = _rope(kv[:, 4 * LANES:5 * LANES], cos, s_lo, s_hi).astype(BF16)
    vw_o[...] = kv[:, 5 * LANES:6 * LANES].astype(BF16)
    gate_o[...] = jax.nn.sigmoid(_dot(xb, wg_ref[...]))

    @pl.when(i == 0)
    def _():
        pbuf[0:POOL_HALO, :] = jnp.zeros((POOL_HALO, POOL_WIDTH), F32)

    p = _dot(xb, wp_ref[...])
    pbuf[POOL_HALO:POOL_HALO + tm, :] = p
    t1 = i * tm + lax.broadcasted_iota(jnp.int32, (tm, 1), 0) + 1
    for g, w in enumerate(POOL_WINDOWS):
        sl = slice(g * POOL_GROUP, (g + 1) * POOL_GROUP)
        u = p[:, sl]
        tot = u
        for j in range(1, w):
            tot = tot + pbuf[pl.ds(POOL_HALO - j, tm), sl]
        cnt = jnp.minimum(t1, w).astype(F32)
        pooled = tot / cnt - u
        mixed = _dot(pooled.astype(BF16), poolw_ref[g]) * pscale_ref[:, sl]
        pool_o[:, sl] = mixed.astype(BF16)
    pbuf[0:POOL_HALO, :] = pbuf[tm:tm + POOL_HALO, :]


def _proj_even(xf, posf, wp, wq, wkv, wg, poolw, pscale, rope_rows, batch, seq):
    n = xf.shape[0]
    tm = ROW_TILE
    nt = seq // tm
    row = lambda b, i: (b * nt + i, 0)
    full = lambda b, i: (0, 0)
    inv, mlo, mhi = rope_rows
    outs = [
        jax.ShapeDtypeStruct((n, POOL_WIDTH), BF16),
        jax.ShapeDtypeStruct((n, Q_WIDTH), BF16),
        jax.ShapeDtypeStruct((n, KV_WIDTH), F32),
        jax.ShapeDtypeStruct((n, KV_WIDTH), F32),
        jax.ShapeDtypeStruct((n, KV_WIDTH), BF16),
        jax.ShapeDtypeStruct((n, KV_WIDTH), BF16),
        jax.ShapeDtypeStruct((n, KV_WIDTH), BF16),
        jax.ShapeDtypeStruct((n, KV_WIDTH), BF16),
        jax.ShapeDtypeStruct((n, LANES), F32),
    ]
    return pl.pallas_call(
        _proj_even_kernel,
        grid=(batch, nt),
        in_specs=[
            pl.BlockSpec((tm, D_MODEL), row),
            pl.BlockSpec((tm, 1), row),
            pl.BlockSpec(wp.shape, full),
            pl.BlockSpec(wq.shape, full),
            pl.BlockSpec(wkv.shape, full),
            pl.BlockSpec(wg.shape, full),
            pl.BlockSpec(poolw.shape, lambda b, i: (0, 0, 0)),
            pl.BlockSpec(pscale.shape, full),
            pl.BlockSpec(inv.shape, full),
            pl.BlockSpec(mlo.shape, full),
            pl.BlockSpec(mhi.shape, full),
        ],
        out_specs=[pl.BlockSpec((tm, o.shape[1]), row) for o in outs],
        out_shape=outs,
        scratch_shapes=[pltpu.VMEM((POOL_HALO + tm, POOL_WIDTH), F32)],
        compiler_params=_params(2),
        name="proj_even",
    )(xf, posf, wp, wq, wkv, wg, poolw, pscale, inv, mlo, mhi)


def _compress_kernel(rk_ref, rv_ref, pos_ref, pk1_ref, pk2_ref, pv1_ref, pv2_ref,
                     wk1_ref, wk2_ref, wv1_ref, wv2_ref, inv_ref, mlo_ref, mhi_ref,
                     kc_o, vc_o):
    nc = rk_ref.shape[0]

    def compress(r, p1, p2, w1, w2):
        a = _dot((r + p1[...]).astype(BF16), w1[...])
        b = _dot((r + p2[...]).astype(BF16), w2[...])
        return a + pltpu.roll(b, nc - 1, 0)

    kc = compress(rk_ref[...], pk1_ref, pk2_ref, wk1_ref, wk2_ref)
    cos, s_lo, s_hi = _rope_tables(pos_ref[...], inv_ref[...], mlo_ref[...], mhi_ref[...])
    kc_o[...] = _rope(kc, cos, s_lo, s_hi).astype(BF16)
    vc_o[...] = compress(rv_ref[...], pv1_ref, pv2_ref, wv1_ref, wv2_ref).astype(BF16)


def _compress(rk, rv, posc, consts, rope_rows):
    batch, nc, width = rk.shape
    blk = lambda b: (b, 0, 0)
    full = lambda b: (0, 0)
    out = jax.ShapeDtypeStruct((batch, nc, KV_WIDTH), BF16)
    args = list(consts) + list(rope_rows)
    return pl.pallas_call(
        _compress_kernel,
        grid=(batch,),
        in_specs=[pl.BlockSpec((None, nc, width), blk), pl.BlockSpec((None, nc, width), blk),
                  pl.BlockSpec((None, nc, 1), blk)] + [pl.BlockSpec(a.shape, full) for a in args],
        out_specs=[pl.BlockSpec((None, nc, KV_WIDTH), blk)] * 2,
        out_shape=[out, out],
        compiler_params=_params(1),
        name="compress",
    )(rk, rv, posc, *args)


def _softmax_rows(s):
    m = jnp.max(s, axis=1, keepdims=True)
    e = jnp.exp(s - m)
    return e, jnp.sum(e, axis=1, keepdims=True)


def _dot_exact01(x, m01):
    hi = x.astype(BF16)
    r1 = x - hi.astype(F32)
    mid = r1.astype(BF16)
    lo = (r1 - mid.astype(F32)).astype(BF16)
    return _dot(hi, m01) + _dot(mid, m01) + _dot(lo, m01)


def _unselected_blocks(imp, q0, n_pick):
    nq, nb = imp.shape
    col = lax.broadcasted_iota(jnp.int32, (nq, nb), 1)
    cur = lax.shift_right_arithmetic(q0 + lax.broadcasted_iota(jnp.int32, (nq, nb), 0), 6)
    forced = (col == 0) | (col == cur) | (col == cur - 1)
    val = jnp.where(forced, FORCE_SCORE, jnp.where(col > cur, -1.0, imp))
    vt = val.T
    blk = lax.broadcasted_iota(jnp.int32, (nb, nq), 0).astype(F32)
    unsel = jnp.ones((nb, nq), F32)
    for _ in range(n_pick):
        m = jnp.max(vt, axis=0, keepdims=True)
        first = jnp.min(jnp.where(vt == m, blk, float(nb)), axis=0, keepdims=True)
        hit = blk == first
        unsel = jnp.where(hit, 0.0, unsel)
        vt = jnp.where(hit, -jnp.inf, vt)
    return unsel.T


def _nsa_kernel(q_ref, gate_ref, kc_ref, vc_ref, ks_ref, vs_ref, kw_ref, vw_ref, pt_ref, ov_ref,
                o_ref, m_scr, l_scr, acc_scr, lhs_scr, oc_scr, ow_scr, *, n_pick):
    qb = pl.program_id(1)
    q0 = qb * Q_BLOCK
    tk = SEL_KEY_TILE
    ncmp = kc_ref.shape[0]
    wk = WINDOW + Q_BLOCK
    rows = NSA_GROUP * Q_BLOCK
    lane = lax.broadcasted_iota(jnp.int32, (Q_BLOCK, LANES), 1)

    def trow(width):
        return q0 + lax.broadcasted_iota(jnp.int32, (Q_BLOCK, width), 0)

    def kcol(width):
        return lax.broadcasted_iota(jnp.int32, (Q_BLOCK, width), 1)

    bias_c = jnp.where(kcol(ncmp) * CMP_STRIDE + (CMP_BLOCK - 1) <= trow(ncmp), 0.0, NEG)
    has_cmp = (trow(1) >= CMP_BLOCK - 1).astype(F32)
    wstart = pl.multiple_of(jnp.maximum(q0 - WINDOW, 0), Q_BLOCK)
    kpos_w = wstart + kcol(wk)
    bias_w = jnp.where(kpos_w <= trow(wk), jnp.where(kpos_w > trow(wk) - WINDOW, 0.0, NEG), NEG)

    for g in range(NSA_KV_HEADS):
        mine = (lane >= NSA_HEAD_DIM) if g else (lane < NSA_HEAD_DIM)
        qg = jnp.concatenate(
            [jnp.where(mine, q_ref[:, r * LANES:(r + 1) * LANES], jnp.zeros((), BF16))
             for r in range(NSA_GROUP)], axis=0)

        e, l = _softmax_rows(_dot_nt(qg, kc_ref[...]) + _rep_rows(bias_c, NSA_GROUP))
        p = e * (_rep_rows(has_cmp, NSA_GROUP) / l)
        oc_scr[g] = _dot(p.astype(BF16), vc_ref[...])
        psum = p[0:Q_BLOCK]
        for r in range(1, NSA_GROUP):
            psum = psum + p[r * Q_BLOCK:(r + 1) * Q_BLOCK]
        imp = _dot_exact01(psum, ov_ref[...])
        unsel = _unselected_blocks(imp, q0, n_pick)
        lhs_scr[g] = jnp.concatenate([qg, _rep_rows(unsel.astype(BF16), NSA_GROUP)], axis=1)

        e, l = _softmax_rows(_dot_nt(qg, kw_ref[pl.ds(wstart, wk), :]) + _rep_rows(bias_w, NSA_GROUP))
        ow_scr[g] = _dot((e / l).astype(BF16), vw_ref[pl.ds(wstart, wk), :])

        m_scr[g] = jnp.full((rows, LANES), -jnp.inf, F32)
        l_scr[g] = jnp.zeros((rows, LANES), F32)
        acc_scr[g] = jnp.zeros((rows, LANES), F32)

    def key_tile(kt, causal):
        k0 = pl.multiple_of(kt * tk, tk)
        rhs = jnp.concatenate([ks_ref[pl.ds(k0, tk), :], pt_ref[pl.ds(k0, tk), :]], axis=1)
        v = vs_ref[pl.ds(k0, tk), :]
        for g in range(NSA_KV_HEADS):
            s = _dot_nt(lhs_scr[g], rhs)
            if causal:
                s = s + _rep_rows(jnp.where(k0 + kcol(tk) <= trow(tk), 0.0, NEG), NSA_GROUP)
            m_prev = m_scr[g]
            m_new = jnp.maximum(m_prev, jnp.max(s, axis=1, keepdims=True))
            alpha = jnp.exp(m_prev - m_new)
            pe = jnp.exp(s - _rep_lanes(m_new, tk // LANES))
            part = pe[:, 0:LANES]
            for c in range(1, tk // LANES):
                part = part + pe[:, c * LANES:(c + 1) * LANES]
            l_scr[g] = alpha * l_scr[g] + part
            acc_scr[g] = alpha * acc_scr[g] + _dot(pe.astype(BF16), v)
            m_scr[g] = m_new

    n_full = q0 // tk

    def body(kt, carry):
        key_tile(kt, False)
        return carry

    lax.fori_loop(0, n_full, body, 0)
    key_tile(n_full, True)

    for r in range(NSA_GROUP):
        rs = slice(r * Q_BLOCK, (r + 1) * Q_BLOCK)
        halves = []
        for g in range(NSA_KV_HEADS):
            c0 = (g * NSA_GROUP + r) * N_BRANCH
            o_sel = acc_scr[g, rs, :] / jnp.sum(l_scr[g, rs, :], axis=1, keepdims=True)
            halves.append(gate_ref[:, c0:c0 + 1] * oc_scr[g, rs, :]
                          + gate_ref[:, c0 + 1:c0 + 2] * o_sel
                          + gate_ref[:, c0 + 2:c0 + 3] * ow_scr[g, rs, :])
        o_ref[:, r * LANES:(r + 1) * LANES] = jnp.where(lane < NSA_HEAD_DIM, halves[0], halves[1]).astype(BF16)


def _nsa(q, gates, kc, vc, ks, vs, kw, vw, pt, ov, batch, seq):
    n = q.shape[0]
    nqb = seq // Q_BLOCK
    rows = NSA_GROUP * Q_BLOCK
    row = lambda b, i: (b * nqb + i, 0)
    per_b = lambda b, i: (b, 0, 0)
    full = lambda b, i: (0, 0)
    seq_spec = pl.BlockSpec((None, seq, KV_WIDTH), per_b)
    cmp_spec = pl.BlockSpec((None, kc.shape[1], KV_WIDTH), per_b)
    n_pick = min(SEL_COUNT, seq // SEL_BLOCK)
    return pl.pallas_call(
        functools.partial(_nsa_kernel, n_pick=n_pick),
        grid=(batch, nqb),
        in_specs=[pl.BlockSpec((Q_BLOCK, Q_WIDTH), row), pl.BlockSpec((Q_BLOCK, LANES), row),
                  cmp_spec, cmp_spec, seq_spec, seq_spec, seq_spec, seq_spec,
                  pl.BlockSpec(pt.shape, full), pl.BlockSpec(ov.shape, full)],
        out_specs=pl.BlockSpec((Q_BLOCK, Q_WIDTH), row),
        out_shape=jax.ShapeDtypeStruct((n, Q_WIDTH), BF16),
        scratch_shapes=[pltpu.VMEM((NSA_KV_HEADS, rows, LANES), F32),
                        pltpu.VMEM((NSA_KV_HEADS, rows, LANES), F32),
                        pltpu.VMEM((NSA_KV_HEADS, rows, LANES), F32),
                        pltpu.VMEM((NSA_KV_HEADS, rows, 2 * LANES), BF16),
                        pltpu.VMEM((NSA_KV_HEADS, rows, LANES), F32),
                        pltpu.VMEM((NSA_KV_HEADS, rows, LANES), F32)],
        compiler_params=_params(2),
        name="nsa",
    )(q, gates, kc, vc, ks, vs, kw, vw, pt, ov)


def _out_ln_kernel(*refs, n_parts):
    x_ref = refs[0]
    parts = refs[1:1 + n_parts]
    ws = refs[1 + n_parts:1 + 2 * n_parts]
    g_ref, b_ref, o_ref = refs[1 + 2 * n_parts:]
    z = ALPHA * x_ref[...]
    for a, w in zip(parts, ws):
        z = z + _dot(a[...], w[...])
    o_ref[...] = _layer_norm(z, g_ref[...], b_ref[...])


def _out_ln(xf, parts, ws, g, b):
    n = xf.shape[0]
    tm = ROW_TILE
    row = lambda i: (i, 0)
    full = lambda i: (0, 0)
    return pl.pallas_call(
        functools.partial(_out_ln_kernel, n_parts=len(parts)),
        grid=(n // tm,),
        in_specs=[pl.BlockSpec((tm, D_MODEL), row)]
                 + [pl.BlockSpec((tm, a.shape[1]), row) for a in parts]
                 + [pl.BlockSpec(w.shape, full) for w in ws]
                 + [pl.BlockSpec(g.shape, full), pl.BlockSpec(b.shape, full)],
        out_specs=pl.BlockSpec((tm, D_MODEL), row),
        out_shape=jax.ShapeDtypeStruct((n, D_MODEL), F32),
        compiler_params=_params(1),
        name="out_ln",
    )(xf, *parts, *ws, g, b)


def _matmul_kernel(a_ref, w_ref, o_ref):
    o_ref[...] = _dot(a_ref[...].astype(BF16), w_ref[...]).astype(o_ref.dtype)


def _matmul(a, w, out_dtype, tm):
    m, k = a.shape
    n = w.shape[1]
    return pl.pallas_call(
        _matmul_kernel,
        grid=(m // tm,),
        in_specs=[pl.BlockSpec((tm, k), lambda i: (i, 0)), pl.BlockSpec((k, n), lambda i: (0, 0))],
        out_specs=pl.BlockSpec((tm, n), lambda i: (i, 0)),
        out_shape=jax.ShapeDtypeStruct((m, n), out_dtype),
        compiler_params=_params(1),
        name="matmul",
    )(a, w)


def _xattn_kernel(x_ref, wq_ref, k_ref, v_ref, wo_ref, g_ref, b_ref, o_ref):
    x = x_ref[...]
    q = _dot(x.astype(BF16), wq_ref[...]).astype(BF16)
    outs = []
    for h in range(X_HEADS):
        sl = slice(h * X_HEAD_DIM, (h + 1) * X_HEAD_DIM)
        e, l = _softmax_rows(_dot_nt(q[:, sl], k_ref[:, sl]))
        outs.append(_dot((e / l).astype(BF16), v_ref[:, sl]))
    o = jnp.concatenate(outs, axis=1).astype(BF16)
    z = ALPHA * x + _dot(o, wo_ref[...])
    o_ref[...] = _layer_norm(z, g_ref[...], b_ref[...])


def _xattn(xf, kv, wq, wo, g, b, batch, seq):
    n = xf.shape[0]
    tm = ROW_TILE
    nt = seq // tm
    n_mem = kv.shape[1]
    row = lambda bi, i: (bi * nt + i, 0)
    full = lambda bi, i: (0, 0)
    return pl.pallas_call(
        _xattn_kernel,
        grid=(batch, nt),
        in_specs=[pl.BlockSpec((tm, D_MODEL), row), pl.BlockSpec(wq.shape, full),
                  pl.BlockSpec((None, n_mem, D_MODEL), lambda bi, i: (bi, 0, 0)),
                  pl.BlockSpec((None, n_mem, D_MODEL), lambda bi, i: (bi, 0, 1)),
                  pl.BlockSpec(wo.shape, full), pl.BlockSpec(g.shape, full), pl.BlockSpec(b.shape, full)],
        out_specs=pl.BlockSpec((tm, D_MODEL), row),
        out_shape=jax.ShapeDtypeStruct((n, D_MODEL), F32),
        compiler_params=_params(2),
        name="xattn",
    )(xf, wq, kv, kv, wo, g, b)


def _swiglu_kernel(x_ref, w1_ref, w2_ref, wd_ref, g_ref, b_ref, o_ref, acc_ref):
    j = pl.program_id(1)
    xb = x_ref[...].astype(BF16)
    h1 = _dot(xb, w1_ref[...])
    h2 = _dot(xb, w2_ref[...])
    part = _dot((h1 * jax.nn.sigmoid(h1) * h2).astype(BF16), wd_ref[...])

    @pl.when(j == 0)
    def _():
        acc_ref[...] = part

    @pl.when(j > 0)
    def _():
        acc_ref[...] += part

    @pl.when(j == pl.num_programs(1) - 1)
    def _():
        o_ref[...] = _layer_norm(ALPHA * x_ref[...] + acc_ref[...], g_ref[...], b_ref[...])


def _swiglu(xf, w_up, w_down, g, b):
    n = xf.shape[0]
    tm = ROW_TILE
    th = FFN_CHUNK
    nh = FFN_HIDDEN // th
    row = lambda i, j: (i, 0)
    full = lambda i, j: (0, 0)
    return pl.pallas_call(
        _swiglu_kernel,
        grid=(n // tm, nh),
        in_specs=[pl.BlockSpec((tm, D_MODEL), row),
                  pl.BlockSpec((D_MODEL, th), lambda i, j: (0, j)),
                  pl.BlockSpec((D_MODEL, th), lambda i, j: (0, nh + j)),
                  pl.BlockSpec((th, D_MODEL), lambda i, j: (j, 0)),
                  pl.BlockSpec(g.shape, full), pl.BlockSpec(b.shape, full)],
        out_specs=pl.BlockSpec((tm, D_MODEL), row),
        out_shape=jax.ShapeDtypeStruct((n, D_MODEL), F32),
        scratch_shapes=[pltpu.VMEM((tm, D_MODEL), F32)],
        compiler_params=_params(2),
        name="swiglu",
    )(xf, w_up, w_up, w_down, g, b)


def _gelu_tanh(x):
    return 0.5 * x * (1.0 + jnp.tanh(np.sqrt(2.0 / np.pi) * (x + 0.044715 * (x * x * x))))


def _rglru_kernel(gate_ref, xr_ref, pos_ref, cw_ref, cb_ref, wax_ref, ba_ref, bx_ref, lam_ref,
                  o_ref, xbuf, a_scr, b_scr, h_scr):
    i = pl.program_id(1)
    tm = xr_ref.shape[0]

    @pl.when(i == 0)
    def _():
        xbuf[0:CONV_HALO, :] = jnp.zeros((CONV_HALO, RNN_WIDTH), F32)
        h_scr[...] = jnp.zeros((SUBLANES, RNN_WIDTH), F32)

    xbuf[CONV_HALO:CONV_HALO + tm, :] = xr_ref[...]
    reset = pos_ref[...] == 0
    in_group = lax.broadcasted_iota(jnp.int32, (tm, RNN_BLOCK_W), 0) & (SUBLANES - 1)
    lam = lam_ref[...]
    softplus_neg_lam = jnp.maximum(-lam, 0.0) + jnp.log1p(jnp.exp(-jnp.abs(lam)))

    for h in range(RNN_BLOCKS):
        sl = slice(h * RNN_BLOCK_W, (h + 1) * RNN_BLOCK_W)
        xc = cb_ref[:, sl]
        for k in range(CONV_WIDTH):
            xc = xc + xbuf[pl.ds(CONV_HALO - (CONV_WIDTH - 1) + k, tm), sl] * cw_ref[k:k + 1, sl]
        ri = _dot(xc.astype(BF16), wax_ref[h])
        r = jax.nn.sigmoid(ri[:, :RNN_BLOCK_W] + ba_ref[:, sl])
        ig = jax.nn.sigmoid(ri[:, RNN_BLOCK_W:] + bx_ref[:, sl])
        log_a = -LRU_C * r * softplus_neg_lam[:, sl]
        a = jnp.where(reset, 0.0, jnp.exp(log_a))
        th = jnp.tanh(log_a)
        mult = jnp.where(reset, 1.0, jnp.sqrt(-2.0 * th / (1.0 - th)))
        b = mult * (ig * xc)
        for d in (1, 2, 4):
            a_prev = pltpu.roll(a, d, 0)
            b_prev = pltpu.roll(b, d, 0)
            ok = in_group >= d
            b = jnp.where(ok, a * b_prev + b, b)
            a = jnp.where(ok, a * a_prev, a)
        a_scr[:, sl] = a
        b_scr[:, sl] = b
    xbuf[0:CONV_HALO, :] = xbuf[tm:tm + CONV_HALO, :]

    def body(j, h_prev):
        r0 = pl.multiple_of(j * BF16_ROWS, BF16_ROWS)
        h0 = a_scr[pl.ds(r0, SUBLANES), :] * h_prev + b_scr[pl.ds(r0, SUBLANES), :]
        h_mid = jnp.broadcast_to(h0[SUBLANES - 1:SUBLANES, :], (SUBLANES, RNN_WIDTH))
        r1 = r0 + SUBLANES
        h1 = a_scr[pl.ds(r1, SUBLANES), :] * h_mid + b_scr[pl.ds(r1, SUBLANES), :]
        hh = jnp.concatenate([h0, h1], axis=0)
        o_ref[pl.ds(r0, BF16_ROWS), :] = (hh * _gelu_tanh(gate_ref[pl.ds(r0, BF16_ROWS), :])).astype(BF16)
        return jnp.broadcast_to(h1[SUBLANES - 1:SUBLANES, :], (SUBLANES, RNN_WIDTH))

    h_scr[...] = lax.fori_loop(0, tm // BF16_ROWS, body, h_scr[...])


def _rglru(u, posi, cw, cb, wax, ba, bx, lam, batch, seq):
    n = u.shape[0]
    tm = ROW_TILE
    nt = seq // tm
    full = lambda b, i: (0, 0)
    return pl.pallas_call(
        _rglru_kernel,
        grid=(batch, nt),
        in_specs=[pl.BlockSpec((tm, RNN_WIDTH), lambda b, i: (b * nt + i, 0)),
                  pl.BlockSpec((tm, RNN_WIDTH), lambda b, i: (b * nt + i, 1)),
                  pl.BlockSpec((tm, 1), lambda b, i: (b * nt + i, 0)),
                  pl.BlockSpec(cw.shape, full), pl.BlockSpec(cb.shape, full),
                  pl.BlockSpec(wax.shape, lambda b, i: (0, 0, 0)),
                  pl.BlockSpec(ba.shape, full), pl.BlockSpec(bx.shape, full), pl.BlockSpec(lam.shape, full)],
        out_specs=pl.BlockSpec((tm, RNN_WIDTH), lambda b, i: (b * nt + i, 0)),
        out_shape=jax.ShapeDtypeStruct((n, RNN_WIDTH), BF16),
        scratch_shapes=[pltpu.VMEM((CONV_HALO + tm, RNN_WIDTH), F32),
                        pltpu.VMEM((tm, RNN_WIDTH), F32),
                        pltpu.VMEM((tm, RNN_WIDTH), F32),
                        pltpu.VMEM((SUBLANES, RNN_WIDTH), F32)],
        compiler_params=_params(2),
        name="rglru",
    )(u, u, posi, cw, cb, wax, ba, bx, lam)


def _rope_rows():
    inv = ROPE_THETA ** (-jnp.arange(ROT_HALF, dtype=F32) * 2.0 / ROT_DIM)
    pad = jnp.zeros((NSA_HEAD_DIM - ROT_DIM,), F32)
    ones = jnp.ones((ROT_HALF,), F32)
    zeros = jnp.zeros((ROT_HALF,), F32)
    per_head = lambda parts: jnp.tile(jnp.concatenate(parts), LANES // NSA_HEAD_DIM)[None, :]
    return (per_head([inv, inv, pad]), per_head([-ones, zeros, pad]), per_head([zeros, ones, pad]))


def _even_mixer(xf, posf, positions, w_in, pool_w, pool_scale, cmp_pos_k, cmp_pos_v, cmp_wk, cmp_wv,
                w_out, ln_g, ln_b, batch, seq):
    assert seq // SEL_BLOCK == LANES and seq % SEL_KEY_TILE == 0 and seq >= WINDOW + Q_BLOCK
    hd, g, r = NSA_HEAD_DIM, NSA_KV_HEADS, NSA_GROUP
    c0 = POOL_WIDTH
    c1 = c0 + Q_WIDTH
    c2 = c1 + GATE_WIDTH
    wp = w_in[:, :c0].astype(BF16)
    wq = (w_in[:, c0:c1].reshape(D_MODEL, g, r, hd).transpose(0, 2, 1, 3).reshape(D_MODEL, Q_WIDTH)
          * hd ** -0.5).astype(BF16)
    wg = jnp.pad(w_in[:, c1:c2], ((0, 0), (0, LANES - GATE_WIDTH))).astype(BF16)
    wkv = w_in[:, c2:].astype(BF16)
    rope_rows = _rope_rows()
    pool, q, kc_raw, vc_raw, ks, vs, kw, vw, gates = _proj_even(
        xf, posf, wp, wq, wkv, wg, pool_w.astype(BF16), pool_scale[None, :], rope_rows, batch, seq)

    nc = seq // CMP_STRIDE
    eye = jnp.eye(g, dtype=F32)

    def halves(w):
        w4 = jnp.einsum('lde,gh->lgdhe', w.reshape(CMP_BLOCK, hd, hd), eye)
        w4 = w4.reshape(CMP_BLOCK, g * hd, g * hd)
        return (w4[:CMP_STRIDE].reshape(CMP_STRIDE * g * hd, g * hd).astype(BF16),
                w4[CMP_STRIDE:].reshape(CMP_STRIDE * g * hd, g * hd).astype(BF16))

    def pos_halves(p):
        p2 = jnp.tile(p[:, None, :], (1, g, 1)).reshape(CMP_BLOCK, g * hd)
        return p2[:CMP_STRIDE].reshape(1, -1), p2[CMP_STRIDE:].reshape(1, -1)

    pk1, pk2 = pos_halves(cmp_pos_k)
    pv1, pv2 = pos_halves(cmp_pos_v)
    wk1, wk2 = halves(cmp_wk)
    wv1, wv2 = halves(cmp_wv)
    n_cmp = (seq - CMP_BLOCK) // CMP_STRIDE + 1
    posc = positions[:, CMP_BLOCK - 1::CMP_STRIDE][:, :n_cmp].astype(F32)
    posc = jnp.pad(posc, ((0, 0), (0, nc - n_cmp)))[:, :, None]
    kc, vc = _compress(kc_raw.reshape(batch, nc, CMP_STRIDE * g * hd),
                       vc_raw.reshape(batch, nc, CMP_STRIDE * g * hd), posc,
                       (pk1, pk2, pv1, pv2, wk1, wk2, wv1, wv2), rope_rows)

    n_sb = seq // SEL_BLOCK
    starts = np.arange(nc) * CMP_STRIDE
    jb = np.arange(n_sb)
    ov = ((starts[:, None] < (jb[None, :] + 1) * SEL_BLOCK)
          & (starts[:, None] + CMP_BLOCK > jb[None, :] * SEL_BLOCK) & (np.arange(nc)[:, None] < n_cmp))
    ov = jnp.asarray(ov, BF16)
    pt = jnp.asarray(np.where(np.arange(seq)[:, None] // SEL_BLOCK == jb[None, :], NEG, 0.0), BF16)

    seq3 = lambda a: a.reshape(batch, seq, KV_WIDTH)
    nsa = _nsa(q, gates, kc, vc, seq3(ks), seq3(vs), seq3(kw), seq3(vw), pt, ov, batch, seq)

    w_pool_out = w_out[:POOL_WIDTH].astype(BF16)
    w_nsa_out = (w_out[POOL_WIDTH:].reshape(g, r, hd, D_MODEL).transpose(1, 0, 2, 3)
                 .reshape(Q_WIDTH, D_MODEL).astype(BF16))
    return _out_ln(xf, [pool, nsa], [w_pool_out, w_nsa_out], ln_g, ln_b)


def _odd_mixer(xf, posi, w_in, conv_w, conv_b, wa, ba, wx, bx, lam, w_out, ln_g, ln_b, batch, seq):
    u = _matmul(xf, w_in.astype(BF16), F32, ROW_TILE)
    wax = jnp.concatenate([wa, wx], axis=2).astype(BF16)
    y = _rglru(u, posi, conv_w, conv_b[None, :], wax, ba[None, :], bx[None, :], lam[None, :], batch, seq)
    return _out_ln(xf, [y], [w_out.astype(BF16)], ln_g, ln_b)


def kernel(x, mem, positions, e_w_in, e_pool_w, e_pool_scale, e_cmp_pos_k, e_cmp_pos_v, e_cmp_wk, e_cmp_wv, e_w_out, o_w_in, o_conv_w, o_conv_b, o_wa, o_ba, o_wx, o_bx, o_lambda, o_w_out, x_wq, x_wkv, x_wo, f_w_up, f_w_down, ln_g, ln_b):
    batch, seq, d = x.shape
    n = batch * seq
    assert d == D_MODEL and seq % ROW_TILE == 0
    xf = x.reshape(n, d)
    posi = positions.reshape(n, 1)
    posf = posi.astype(F32)
    memf = mem.reshape(batch * mem.shape[1], d)
    for layer in range(DEPTH):
        j = layer // 2
        lg = lambda k: ln_g[layer, k][None, :]
        lb = lambda k: ln_b[layer, k][None, :]
        if layer % 2 == 0:
            xf = _even_mixer(xf, posf, positions, e_w_in[j], e_pool_w[j], e_pool_scale[j], e_cmp_pos_k[j],
                             e_cmp_pos_v[j], e_cmp_wk[j], e_cmp_wv[j], e_w_out[j], lg(0), lb(0), batch, seq)
        else:
            xf = _odd_mixer(xf, posi, o_w_in[j], o_conv_w[j], o_conv_b[j], o_wa[j], o_ba[j], o_wx[j],
                            o_bx[j], o_lambda[j], o_w_out[j], lg(0), lb(0), batch, seq)
        kv = _matmul(memf, x_wkv[layer].astype(BF16), BF16, memf.shape[0]).reshape(batch, mem.shape[1], 2 * d)
        wq = (x_wq[layer] * X_HEAD_DIM ** -0.5).astype(BF16)
        xf = _xattn(xf, kv, wq, x_wo[layer].astype(BF16), lg(1), lb(1), batch, seq)
        xf = _swiglu(xf, f_w_up[layer].astype(BF16), f_w_down[layer].astype(BF16), lg(2), lb(2))
    return xf.reshape(batch, seq, d)
```

```python
import functools

import numpy as np
import jax
import jax.numpy as jnp
from jax import lax
from jax.experimental import pallas as pl
from jax.experimental.pallas import tpu as pltpu

F32 = jnp.float32
BF16 = jnp.bfloat16

D_MODEL = 1024
DEPTH = 2
ALPHA = (2.0 * DEPTH) ** 0.25
LN_EPS = 1e-5
NEG = -1e30
POOL_WIDTH = D_MODEL // 2
POOL_WINDOWS = (2, 4, 8, 16)
POOL_GROUP = POOL_WIDTH // len(POOL_WINDOWS)
POOL_HALO = 16
NSA_HEADS = 8
NSA_KV_HEADS = 2
NSA_HEAD_DIM = 64
NSA_GROUP = NSA_HEADS // NSA_KV_HEADS
CMP_BLOCK = 32
CMP_STRIDE = 16
SEL_BLOCK = 64
SEL_COUNT = 16
WINDOW = 512
Q_BLOCK = 128
N_BRANCH = 3
N_FORCED = 3
LOG2E = 1.4426950408889634
ROPE_THETA = 500000.0
ROT_DIM = NSA_HEAD_DIM // 4
ROT_HALF = ROT_DIM // 2
Q_WIDTH = NSA_HEADS * NSA_HEAD_DIM
KV_WIDTH = NSA_KV_HEADS * NSA_HEAD_DIM
GATE_WIDTH = NSA_HEADS * N_BRANCH
RNN_WIDTH = 1280
RNN_BLOCKS = 10
RNN_BLOCK_W = RNN_WIDTH // RNN_BLOCKS
CONV_WIDTH = 4
CONV_HALO = 8
LRU_C = 8.0
X_HEADS = 4
X_HEAD_DIM = D_MODEL // X_HEADS
FFN_HIDDEN = 2816

LANES = 128
SUBLANES = 8
BF16_ROWS = 16
VMEM_LIMIT = 56 * 1024 * 1024
ROW_TILE = 512
SEL_KEY_TILE = 512
FFN_CHUNK = 1408


def _params(n_axes, vmem=VMEM_LIMIT):
    return pltpu.CompilerParams(dimension_semantics=("arbitrary",) * n_axes,
                                vmem_limit_bytes=vmem)


def _dot(a, b):
    return jnp.dot(a, b, preferred_element_type=F32)


def _dot_nt(a, b):
    return lax.dot_general(a, b, (((1,), (1,)), ((), ())), preferred_element_type=F32)


def _rep_rows(a, k):
    return jnp.concatenate([a] * k, axis=0)


def _rep_lanes(a, k):
    return jnp.concatenate([a] * k, axis=1) if k > 1 else a


def _layer_norm(z, g, b):
    mu = jnp.mean(z, axis=-1, keepdims=True)
    d = z - mu
    var = jnp.mean(d * d, axis=-1, keepdims=True)
    return d * lax.rsqrt(var + LN_EPS) * g + b


def _rope(v, cos, s_lo, s_hi):
    k = v.shape[1] // LANES
    up = pltpu.roll(v, v.shape[1] - ROT_HALF, 1)
    dn = pltpu.roll(v, ROT_HALF, 1)
    return v * _rep_lanes(cos, k) + up * _rep_lanes(s_lo, k) + dn * _rep_lanes(s_hi, k)


def _rope_tables(pos, inv, m_lo, m_hi):
    ang = pos * inv
    sin = jnp.sin(ang)
    return jnp.cos(ang), sin * m_lo, sin * m_hi


def _proj_even_kernel(x_ref, pos_ref, wp_ref, wq_ref, wkv_ref, wg_ref, poolw_ref, pscale_ref,
                      inv_ref, mlo_ref, mhi_ref,
                      pool_o, q_o, kc_o, vc_o, ks_o, vs_o, kw_o, vw_o, gate_o, pbuf):
    i = pl.program_id(1)
    tm = x_ref.shape[0]
    xb = x_ref[...].astype(BF16)
    cos, s_lo, s_hi = _rope_tables(pos_ref[...], inv_ref[...], mlo_ref[...], mhi_ref[...])

    q_o[...] = _rope(_dot(xb, wq_ref[...]), cos, s_lo, s_hi).astype(BF16)
    kv = _dot(xb, wkv_ref[...])
    kc_o[...] = kv[:, 0 * LANES:1 * LANES]
    vc_o[...] = kv[:, 1 * LANES:2 * LANES]
    ks_o[...] = _rope(kv[:, 2 * LANES:3 * LANES], cos, s_lo, s_hi).astype(BF16)
    vs_o[...] = kv[:, 3 * LANES:4 * LANES].astype(BF16)
    kw_o[...] = _rope(kv[:, 4 * LANES:5 * LANES], cos, s_lo, s_hi).astype(BF16)
    vw_o[...] = kv[:, 5 * LANES:6 * LANES].astype(BF16)
    gate_o[...] = jax.nn.sigmoid(_dot(xb, wg_ref[...]))

    @pl.when(i == 0)
    def _():
        pbuf[0:POOL_HALO, :] = jnp.zeros((POOL_HALO, POOL_WIDTH), F32)

    p = _dot(xb, wp_ref[...])
    pbuf[POOL_HALO:POOL_HALO + tm, :] = p
    t1 = i * tm + lax.broadcasted_iota(jnp.int32, (tm, 1), 0) + 1
    for g, w in enumerate(POOL_WINDOWS):
        sl = slice(g * POOL_GROUP, (g + 1) * POOL_GROUP)
        u = p[:, sl]
        tot = u
        for j in range(1, w):
            tot = tot + pbuf[pl.ds(POOL_HALO - j, tm), sl]
        cnt = jnp.minimum(t1, w).astype(F32)
        pooled = tot / cnt - u
        mixed = _dot(pooled.astype(BF16), poolw_ref[g]) * pscale_ref[:, sl]
        pool_o[:, sl] = mixed.astype(BF16)
    pbuf[0:POOL_HALO, :] = pbuf[tm:tm + POOL_HALO, :]


def _proj_even(xf, posf, wp, wq, wkv, wg, poolw, pscale, rope_rows, batch, seq):
    n = xf.shape[0]
    tm = ROW_TILE
    nt = seq // tm
    row = lambda b, i: (b * nt + i, 0)
    full = lambda b, i: (0, 0)
    inv, mlo, mhi = rope_rows
    outs = [
        jax.ShapeDtypeStruct((n, POOL_WIDTH), BF16),
        jax.ShapeDtypeStruct((n, Q_WIDTH), BF16),
        jax.ShapeDtypeStruct((n, KV_WIDTH), F32),
        jax.ShapeDtypeStruct((n, KV_WIDTH), F32),
        jax.ShapeDtypeStruct((n, KV_WIDTH), BF16),
        jax.ShapeDtypeStruct((n, KV_WIDTH), BF16),
        jax.ShapeDtypeStruct((n, KV_WIDTH), BF16),
        jax.ShapeDtypeStruct((n, KV_WIDTH), BF16),
        jax.ShapeDtypeStruct((n, LANES), F32),
    ]
    return pl.pallas_call(
        _proj_even_kernel,
        grid=(batch, nt),
        in_specs=[
            pl.BlockSpec((tm, D_MODEL), row),
            pl.BlockSpec((tm, 1), row),
            pl.BlockSpec(wp.shape, full),
            pl.BlockSpec(wq.shape, full),
            pl.BlockSpec(wkv.shape, full),
            pl.BlockSpec(wg.shape, full),
            pl.BlockSpec(poolw.shape, lambda b, i: (0, 0, 0)),
            pl.BlockSpec(pscale.shape, full),
            pl.BlockSpec(inv.shape, full),
            pl.BlockSpec(mlo.shape, full),
            pl.BlockSpec(mhi.shape, full),
        ],
        out_specs=[pl.BlockSpec((tm, o.shape[1]), row) for o in outs],
        out_shape=outs,
        scratch_shapes=[pltpu.VMEM((POOL_HALO + tm, POOL_WIDTH), F32)],
        compiler_params=_params(2),
        name="proj_even",
    )(xf, posf, wp, wq, wkv, wg, poolw, pscale, inv, mlo, mhi)


def _compress_kernel(rk_ref, rv_ref, pos_ref, pk1_ref, pk2_ref, pv1_ref, pv2_ref,
                     wk1_ref, wk2_ref, wv1_ref, wv2_ref, inv_ref, mlo_ref, mhi_ref,
                     kc_o, vc_o):
    nc = rk_ref.shape[0]

    def compress(r, p1, p2, w1, w2):
        a = _dot((r + p1[...]).astype(BF16), w1[...])
        b = _dot((r + p2[...]).astype(BF16), w2[...])
        return a + pltpu.roll(b, nc - 1, 0)

    kc = compress(rk_ref[...], pk1_ref, pk2_ref, wk1_ref, wk2_ref)
    cos, s_lo, s_hi = _rope_tables(pos_ref[...], inv_ref[...], mlo_ref[...], mhi_ref[...])
    kc_o[...] = _rope(kc, cos, s_lo, s_hi).astype(BF16)
    vc_o[...] = compress(rv_ref[...], pv1_ref, pv2_ref, wv1_ref, wv2_ref).astype(BF16)


def _compress(rk, rv, posc, consts, rope_rows):
    batch, nc, width = rk.shape
    blk = lambda b: (b, 0, 0)
    full = lambda b: (0, 0)
    out = jax.ShapeDtypeStruct((batch, nc, KV_WIDTH), BF16)
    args = list(consts) + list(rope_rows)
    return pl.pallas_call(
        _compress_kernel,
        grid=(batch,),
        in_specs=[pl.BlockSpec((None, nc, width), blk), pl.BlockSpec((None, nc, width), blk),
                  pl.BlockSpec((None, nc, 1), blk)] + [pl.BlockSpec(a.shape, full) for a in args],
        out_specs=[pl.BlockSpec((None, nc, KV_WIDTH), blk)] * 2,
        out_shape=[out, out],
        compiler_params=_params(1),
        name="compress",
    )(rk, rv, posc, *args)


def _softmax_rows(s):
    m = jnp.max(s, axis=1, keepdims=True)
    e = jnp.exp2(s - m)
    return e, jnp.sum(e, axis=1, keepdims=True)


def _dot_exact01(x, m01):
    hi = x.astype(BF16)
    r1 = x - hi.astype(F32)
    mid = r1.astype(BF16)
    lo = (r1 - mid.astype(F32)).astype(BF16)
    return _dot(hi, m01) + _dot(mid, m01) + _dot(lo, m01)


def _unselected_blocks(imp, q0, n_pick):
    nq, nb = imp.shape
    col = lax.broadcasted_iota(jnp.int32, (nq, nb), 1)
    cur = lax.shift_right_arithmetic(q0 + lax.broadcasted_iota(jnp.int32, (nq, nb), 0), 6)
    forced = (col == 0) | (col == cur) | (col == cur - 1)
    val = jnp.where(forced, -jnp.inf, jnp.where(col > cur, -1.0, imp))
    vt = val.T
    blk = lax.broadcasted_iota(jnp.int32, (nb, nq), 0).astype(F32)
    unsel = jnp.where(forced, 0.0, 1.0).T
    for _ in range(n_pick - N_FORCED):
        m = jnp.max(vt, axis=0, keepdims=True)
        first = jnp.min(jnp.where(vt == m, blk, float(nb)), axis=0, keepdims=True)
        hit = blk == first
        unsel = jnp.where(hit, 0.0, unsel)
        vt = jnp.where(hit, -jnp.inf, vt)
    return unsel.T


def _nsa_kernel(q_ref, gate_ref, kc_ref, vc_ref, ks_ref, vs_ref, kw_ref, vw_ref, pt_ref, ov_ref,
                o_ref, m_scr, l_scr, acc_scr, lhs_scr, oc_scr, ow_scr, s0_scr, s1_scr, *, n_pick):
    qb = pl.program_id(1)
    q0 = qb * Q_BLOCK
    tk = SEL_KEY_TILE
    ncmp = kc_ref.shape[0]
    wk = WINDOW + Q_BLOCK
    rows = NSA_GROUP * Q_BLOCK
    lane = lax.broadcasted_iota(jnp.int32, (Q_BLOCK, LANES), 1)

    def trow(width):
        return q0 + lax.broadcasted_iota(jnp.int32, (Q_BLOCK, width), 0)

    def kcol(width):
        return lax.broadcasted_iota(jnp.int32, (Q_BLOCK, width), 1)

    bias_c = jnp.where(kcol(ncmp) * CMP_STRIDE + (CMP_BLOCK - 1) <= trow(ncmp), 0.0, NEG)
    has_cmp = (trow(1) >= CMP_BLOCK - 1).astype(F32)
    wstart = pl.multiple_of(jnp.maximum(q0 - WINDOW, 0), Q_BLOCK)
    kpos_w = wstart + kcol(wk)
    bias_w = jnp.where(kpos_w <= trow(wk), jnp.where(kpos_w > trow(wk) - WINDOW, 0.0, NEG), NEG)

    for g in range(NSA_KV_HEADS):
        mine = (lane >= NSA_HEAD_DIM) if g else (lane < NSA_HEAD_DIM)
        qg = jnp.concatenate(
            [jnp.where(mine, q_ref[:, r * LANES:(r + 1) * LANES], jnp.zeros((), BF16))
             for r in range(NSA_GROUP)], axis=0)

        e, l = _softmax_rows(_dot_nt(qg, kc_ref[...]) + _rep_rows(bias_c, NSA_GROUP))
        p = e * (_rep_rows(has_cmp, NSA_GROUP) / l)
        oc_scr[g] = _dot(p.astype(BF16), vc_ref[...])
        psum = p[0:Q_BLOCK]
        for r in range(1, NSA_GROUP):
            psum = psum + p[r * Q_BLOCK:(r + 1) * Q_BLOCK]
        imp = _dot_exact01(psum, ov_ref[...])
        unsel = _unselected_blocks(imp, q0, n_pick)
        lhs_scr[g] = jnp.concatenate([qg, _rep_rows(unsel.astype(BF16), NSA_GROUP)], axis=1)

        e, l = _softmax_rows(_dot_nt(qg, kw_ref[pl.ds(wstart, wk), :]) + _rep_rows(bias_w, NSA_GROUP))
        ow_scr[g] = _dot((e / l).astype(BF16), vw_ref[pl.ds(wstart, wk), :])

        m_scr[g] = jnp.full((rows, LANES), -jnp.inf, F32)
        l_scr[g] = jnp.zeros((rows, LANES), F32)
        acc_scr[g] = jnp.zeros((rows, LANES), F32)

    def scores(kt, buf):
        k0 = pl.multiple_of(kt * tk, tk)
        rhs = jnp.concatenate([ks_ref[pl.ds(k0, tk), :], pt_ref[pl.ds(k0, tk), :]], axis=1)
        for g in range(NSA_KV_HEADS):
            buf[g] = _dot_nt(lhs_scr[g], rhs)

    def consume(kt, buf, causal):
        k0 = pl.multiple_of(kt * tk, tk)
        v = vs_ref[pl.ds(k0, tk), :]
        for g in range(NSA_KV_HEADS):
            s = buf[g]
            if causal:
                s = s + _rep_rows(jnp.where(k0 + kcol(tk) <= trow(tk), 0.0, NEG), NSA_GROUP)
            m_prev = m_scr[g]
            m_new = jnp.maximum(m_prev, jnp.max(s, axis=1, keepdims=True))
            alpha = jnp.exp2(m_prev - m_new)
            pe = jnp.exp2(s - _rep_lanes(m_new, tk // LANES))
            part = pe[:, 0:LANES]
            for c in range(1, tk // LANES):
                part = part + pe[:, c * LANES:(c + 1) * LANES]
            l_scr[g] = alpha * l_scr[g] + part
            acc_scr[g] = alpha * acc_scr[g] + _dot(pe.astype(BF16), v)
            m_scr[g] = m_new

    n_full = q0 // tk
    scores(0, s0_scr)

    def pair(j, carry):
        scores(2 * j + 1, s1_scr)
        consume(2 * j, s0_scr, False)
        scores(2 * j + 2, s0_scr)
        consume(2 * j + 1, s1_scr, False)
        return carry

    lax.fori_loop(0, n_full // 2, pair, 0)

    @pl.when(n_full % 2 == 0)
    def _():
        consume(n_full, s0_scr, True)

    @pl.when(n_full % 2 == 1)
    def _():
        scores(n_full, s1_scr)
        consume(n_full - 1, s0_scr, False)
        consume(n_full, s1_scr, True)

    for r in range(NSA_GROUP):
        rs = slice(r * Q_BLOCK, (r + 1) * Q_BLOCK)
        halves = []
        for g in range(NSA_KV_HEADS):
            c0 = (g * NSA_GROUP + r) * N_BRANCH
            o_sel = acc_scr[g, rs, :] / jnp.sum(l_scr[g, rs, :], axis=1, keepdims=True)
            halves.append(gate_ref[:, c0:c0 + 1] * oc_scr[g, rs, :]
                          + gate_ref[:, c0 + 1:c0 + 2] * o_sel
                          + gate_ref[:, c0 + 2:c0 + 3] * ow_scr[g, rs, :])
        o_ref[:, r * LANES:(r + 1) * LANES] = jnp.where(lane < NSA_HEAD_DIM, halves[0], halves[1]).astype(BF16)


def _nsa(q, gates, kc, vc, ks, vs, kw, vw, pt, ov, batch, seq):
    n = q.shape[0]
    nqb = seq // Q_BLOCK
    rows = NSA_GROUP * Q_BLOCK
    row = lambda b, i: (b * nqb + i, 0)
    per_b = lambda b, i: (b, 0, 0)
    full = lambda b, i: (0, 0)
    seq_spec = pl.BlockSpec((None, seq, KV_WIDTH), per_b)
    cmp_spec = pl.BlockSpec((None, kc.shape[1], KV_WIDTH), per_b)
    n_pick = min(SEL_COUNT, seq // SEL_BLOCK)
    return pl.pallas_call(
        functools.partial(_nsa_kernel, n_pick=n_pick),
        grid=(batch, nqb),
        in_specs=[pl.BlockSpec((Q_BLOCK, Q_WIDTH), row), pl.BlockSpec((Q_BLOCK, LANES), row),
                  cmp_spec, cmp_spec, seq_spec, seq_spec, seq_spec, seq_spec,
                  pl.BlockSpec(pt.shape, full), pl.BlockSpec(ov.shape, full)],
        out_specs=pl.BlockSpec((Q_BLOCK, Q_WIDTH), row),
        out_shape=jax.ShapeDtypeStruct((n, Q_WIDTH), BF16),
        scratch_shapes=[pltpu.VMEM((NSA_KV_HEADS, rows, LANES), F32),
                        pltpu.VMEM((NSA_KV_HEADS, rows, LANES), F32),
                        pltpu.VMEM((NSA_KV_HEADS, rows, LANES), F32),
                        pltpu.VMEM((NSA_KV_HEADS, rows, 2 * LANES), BF16),
                        pltpu.VMEM((NSA_KV_HEADS, rows, LANES), F32),
                        pltpu.VMEM((NSA_KV_HEADS, rows, LANES), F32),
                        pltpu.VMEM((NSA_KV_HEADS, rows, SEL_KEY_TILE), F32),
                        pltpu.VMEM((NSA_KV_HEADS, rows, SEL_KEY_TILE), F32)],
        compiler_params=_params(2),
        name="nsa",
    )(q, gates, kc, vc, ks, vs, kw, vw, pt, ov)


def _out_ln_kernel(*refs, n_parts):
    x_ref = refs[0]
    parts = refs[1:1 + n_parts]
    ws = refs[1 + n_parts:1 + 2 * n_parts]
    g_ref, b_ref, o_ref = refs[1 + 2 * n_parts:]
    z = ALPHA * x_ref[...]
    for a, w in zip(parts, ws):
        z = z + _dot(a[...], w[...])
    o_ref[...] = _layer_norm(z, g_ref[...], b_ref[...])


def _out_ln(xf, parts, ws, g, b):
    n = xf.shape[0]
    tm = ROW_TILE
    row = lambda i: (i, 0)
    full = lambda i: (0, 0)
    return pl.pallas_call(
        functools.partial(_out_ln_kernel, n_parts=len(parts)),
        grid=(n // tm,),
        in_specs=[pl.BlockSpec((tm, D_MODEL), row)]
                 + [pl.BlockSpec((tm, a.shape[1]), row) for a in parts]
                 + [pl.BlockSpec(w.shape, full) for w in ws]
                 + [pl.BlockSpec(g.shape, full), pl.BlockSpec(b.shape, full)],
        out_specs=pl.BlockSpec((tm, D_MODEL), row),
        out_shape=jax.ShapeDtypeStruct((n, D_MODEL), F32),
        compiler_params=_params(1),
        name="out_ln",
    )(xf, *parts, *ws, g, b)


def _matmul_kernel(a_ref, w_ref, o_ref):
    o_ref[...] = _dot(a_ref[...].astype(BF16), w_ref[...]).astype(o_ref.dtype)


def _matmul(a, w, out_dtype, tm):
    m, k = a.shape
    n = w.shape[1]
    return pl.pallas_call(
        _matmul_kernel,
        grid=(m // tm,),
        in_specs=[pl.BlockSpec((tm, k), lambda i: (i, 0)), pl.BlockSpec((k, n), lambda i: (0, 0))],
        out_specs=pl.BlockSpec((tm, n), lambda i: (i, 0)),
        out_shape=jax.ShapeDtypeStruct((m, n), out_dtype),
        compiler_params=_params(1),
        name="matmul",
    )(a, w)


def _xattn_kernel(x_ref, wq_ref, k_ref, v_ref, wo_ref, g_ref, b_ref, o_ref):
    x = x_ref[...]
    q = _dot(x.astype(BF16), wq_ref[...]).astype(BF16)
    outs = []
    for h in range(X_HEADS):
        sl = slice(h * X_HEAD_DIM, (h + 1) * X_HEAD_DIM)
        e, l = _softmax_rows(_dot_nt(q[:, sl], k_ref[:, sl]))
        outs.append(_dot((e / l).astype(BF16), v_ref[:, sl]))
    o = jnp.concatenate(outs, axis=1).astype(BF16)
    z = ALPHA * x + _dot(o, wo_ref[...])
    o_ref[...] = _layer_norm(z, g_ref[...], b_ref[...])


def _xattn(xf, kv, wq, wo, g, b, batch, seq):
    n = xf.shape[0]
    tm = ROW_TILE
    nt = seq // tm
    n_mem = kv.shape[1]
    row = lambda bi, i: (bi * nt + i, 0)
    full = lambda bi, i: (0, 0)
    return pl.pallas_call(
        _xattn_kernel,
        grid=(batch, nt),
        in_specs=[pl.BlockSpec((tm, D_MODEL), row), pl.BlockSpec(wq.shape, full),
                  pl.BlockSpec((None, n_mem, D_MODEL), lambda bi, i: (bi, 0, 0)),
                  pl.BlockSpec((None, n_mem, D_MODEL), lambda bi, i: (bi, 0, 1)),
                  pl.BlockSpec(wo.shape, full), pl.BlockSpec(g.shape, full), pl.BlockSpec(b.shape, full)],
        out_specs=pl.BlockSpec((tm, D_MODEL), row),
        out_shape=jax.ShapeDtypeStruct((n, D_MODEL), F32),
        compiler_params=_params(2),
        name="xattn",
    )(xf, wq, kv, kv, wo, g, b)


def _swiglu_kernel(x_ref, w1_ref, w2_ref, wd_ref, g_ref, b_ref, o_ref, acc_ref):
    j = pl.program_id(1)
    xb = x_ref[...].astype(BF16)
    h1 = _dot(xb, w1_ref[...])
    h2 = _dot(xb, w2_ref[...])
    part = _dot((h1 * jax.nn.sigmoid(h1) * h2).astype(BF16), wd_ref[...])

    @pl.when(j == 0)
    def _():
        acc_ref[...] = part

    @pl.when(j > 0)
    def _():
        acc_ref[...] += part

    @pl.when(j == pl.num_programs(1) - 1)
    def _():
        o_ref[...] = _layer_norm(ALPHA * x_ref[...] + acc_ref[...], g_ref[...], b_ref[...])


def _swiglu(xf, w_up, w_down, g, b):
    n = xf.shape[0]
    tm = ROW_TILE
    th = FFN_CHUNK
    nh = FFN_HIDDEN // th
    row = lambda i, j: (i, 0)
    full = lambda i, j: (0, 0)
    return pl.pallas_call(
        _swiglu_kernel,
        grid=(n // tm, nh),
        in_specs=[pl.BlockSpec((tm, D_MODEL), row),
                  pl.BlockSpec((D_MODEL, th), lambda i, j: (0, j)),
                  pl.BlockSpec((D_MODEL, th), lambda i, j: (0, nh + j)),
                  pl.BlockSpec((th, D_MODEL), lambda i, j: (j, 0)),
                  pl.BlockSpec(g.shape, full), pl.BlockSpec(b.shape, full)],
        out_specs=pl.BlockSpec((tm, D_MODEL), row),
        out_shape=jax.ShapeDtypeStruct((n, D_MODEL), F32),
        scratch_shapes=[pltpu.VMEM((tm, D_MODEL), F32)],
        compiler_params=_params(2),
        name="swiglu",
    )(xf, w_up, w_up, w_down, g, b)


def _gelu_tanh(x):
    return 0.5 * x * (1.0 + jnp.tanh(np.sqrt(2.0 / np.pi) * (x + 0.044715 * (x * x * x))))


def _rglru_kernel(gate_ref, xr_ref, pos_ref, cw_ref, cb_ref, wax_ref, ba_ref, bx_ref, lam_ref,
                  o_ref, xbuf, a_scr, b_scr, h_scr):
    i = pl.program_id(1)
    tm = xr_ref.shape[0]

    @pl.when(i == 0)
    def _():
        xbuf[0:CONV_HALO, :] = jnp.zeros((CONV_HALO, RNN_WIDTH), F32)
        h_scr[...] = jnp.zeros((SUBLANES, RNN_WIDTH), F32)

    xbuf[CONV_HALO:CONV_HALO + tm, :] = xr_ref[...]
    reset = pos_ref[...] == 0
    in_group = lax.broadcasted_iota(jnp.int32, (tm, RNN_BLOCK_W), 0) & (SUBLANES - 1)
    lam = lam_ref[...]
    softplus_neg_lam = jnp.maximum(-lam, 0.0) + jnp.log1p(jnp.exp(-jnp.abs(lam)))

    for h in range(RNN_BLOCKS):
        sl = slice(h * RNN_BLOCK_W, (h + 1) * RNN_BLOCK_W)
        xc = cb_ref[:, sl]
        for k in range(CONV_WIDTH):
            xc = xc + xbuf[pl.ds(CONV_HALO - (CONV_WIDTH - 1) + k, tm), sl] * cw_ref[k:k + 1, sl]
        ri = _dot(xc.astype(BF16), wax_ref[h])
        r = jax.nn.sigmoid(ri[:, :RNN_BLOCK_W] + ba_ref[:, sl])
        ig = jax.nn.sigmoid(ri[:, RNN_BLOCK_W:] + bx_ref[:, sl])
        log_a = -LRU_C * r * softplus_neg_lam[:, sl]
        a = jnp.where(reset, 0.0, jnp.exp(log_a))
        th = jnp.tanh(log_a)
        mult = jnp.where(reset, 1.0, jnp.sqrt(-2.0 * th / (1.0 - th)))
        b = mult * (ig * xc)
        for d in (1, 2, 4):
            a_prev = pltpu.roll(a, d, 0)
            b_prev = pltpu.roll(b, d, 0)
            ok = in_group >= d
            b = jnp.where(ok, a * b_prev + b, b)
            a = jnp.where(ok, a * a_prev, a)
        a_scr[:, sl] = a
        b_scr[:, sl] = b
    xbuf[0:CONV_HALO, :] = xbuf[tm:tm + CONV_HALO, :]

    def body(j, h_prev):
        r0 = pl.multiple_of(j * BF16_ROWS, BF16_ROWS)
        h0 = a_scr[pl.ds(r0, SUBLANES), :] * h_prev + b_scr[pl.ds(r0, SUBLANES), :]
        h_mid = jnp.broadcast_to(h0[SUBLANES - 1:SUBLANES, :], (SUBLANES, RNN_WIDTH))
        r1 = r0 + SUBLANES
        h1 = a_scr[pl.ds(r1, SUBLANES), :] * h_mid + b_scr[pl.ds(r1, SUBLANES), :]
        hh = jnp.concatenate([h0, h1], axis=0)
        o_ref[pl.ds(r0, BF16_ROWS), :] = (hh * _gelu_tanh(gate_ref[pl.ds(r0, BF16_ROWS), :])).astype(BF16)
        return jnp.broadcast_to(h1[SUBLANES - 1:SUBLANES, :], (SUBLANES, RNN_WIDTH))

    h_scr[...] = lax.fori_loop(0, tm // BF16_ROWS, body, h_scr[...])


def _rglru(u, posi, cw, cb, wax, ba, bx, lam, batch, seq):
    n = u.shape[0]
    tm = ROW_TILE
    nt = seq // tm
    full = lambda b, i: (0, 0)
    return pl.pallas_call(
        _rglru_kernel,
        grid=(batch, nt),
        in_specs=[pl.BlockSpec((tm, RNN_WIDTH), lambda b, i: (b * nt + i, 0)),
                  pl.BlockSpec((tm, RNN_WIDTH), lambda b, i: (b * nt + i, 1)),
                  pl.BlockSpec((tm, 1), lambda b, i: (b * nt + i, 0)),
                  pl.BlockSpec(cw.shape, full), pl.BlockSpec(cb.shape, full),
                  pl.BlockSpec(wax.shape, lambda b, i: (0, 0, 0)),
                  pl.BlockSpec(ba.shape, full), pl.BlockSpec(bx.shape, full), pl.BlockSpec(lam.shape, full)],
        out_specs=pl.BlockSpec((tm, RNN_WIDTH), lambda b, i: (b * nt + i, 0)),
        out_shape=jax.ShapeDtypeStruct((n, RNN_WIDTH), BF16),
        scratch_shapes=[pltpu.VMEM((CONV_HALO + tm, RNN_WIDTH), F32),
                        pltpu.VMEM((tm, RNN_WIDTH), F32),
                        pltpu.VMEM((tm, RNN_WIDTH), F32),
                        pltpu.VMEM((SUBLANES, RNN_WIDTH), F32)],
        compiler_params=_params(2),
        name="rglru",
    )(u, u, posi, cw, cb, wax, ba, bx, lam)


def _rope_rows():
    inv = ROPE_THETA ** (-jnp.arange(ROT_HALF, dtype=F32) * 2.0 / ROT_DIM)
    pad = jnp.zeros((NSA_HEAD_DIM - ROT_DIM,), F32)
    ones = jnp.ones((ROT_HALF,), F32)
    zeros = jnp.zeros((ROT_HALF,), F32)
    per_head = lambda parts: jnp.tile(jnp.concatenate(parts), LANES // NSA_HEAD_DIM)[None, :]
    return (per_head([inv, inv, pad]), per_head([-ones, zeros, pad]), per_head([zeros, ones, pad]))


def _even_mixer(xf, posf, positions, w_in, pool_w, pool_scale, cmp_pos_k, cmp_pos_v, cmp_wk, cmp_wv,
                w_out, ln_g, ln_b, batch, seq):
    assert seq // SEL_BLOCK == LANES and seq % SEL_KEY_TILE == 0 and seq >= WINDOW + Q_BLOCK
    hd, g, r = NSA_HEAD_DIM, NSA_KV_HEADS, NSA_GROUP
    c0 = POOL_WIDTH
    c1 = c0 + Q_WIDTH
    c2 = c1 + GATE_WIDTH
    wp = w_in[:, :c0].astype(BF16)
    wq = (w_in[:, c0:c1].reshape(D_MODEL, g, r, hd).transpose(0, 2, 1, 3).reshape(D_MODEL, Q_WIDTH)
          * (hd ** -0.5 * LOG2E)).astype(BF16)
    wg = jnp.pad(w_in[:, c1:c2], ((0, 0), (0, LANES - GATE_WIDTH))).astype(BF16)
    wkv = w_in[:, c2:].astype(BF16)
    rope_rows = _rope_rows()
    pool, q, kc_raw, vc_raw, ks, vs, kw, vw, gates = _proj_even(
        xf, posf, wp, wq, wkv, wg, pool_w.astype(BF16), pool_scale[None, :], rope_rows, batch, seq)

    nc = seq // CMP_STRIDE
    eye = jnp.eye(g, dtype=F32)

    def halves(w):
        w4 = jnp.einsum('lde,gh->lgdhe', w.reshape(CMP_BLOCK, hd, hd), eye)
        w4 = w4.reshape(CMP_BLOCK, g * hd, g * hd)
        return (w4[:CMP_STRIDE].reshape(CMP_STRIDE * g * hd, g * hd).astype(BF16),
                w4[CMP_STRIDE:].reshape(CMP_STRIDE * g * hd, g * hd).astype(BF16))

    def pos_halves(p):
        p2 = jnp.tile(p[:, None, :], (1, g, 1)).reshape(CMP_BLOCK, g * hd)
        return p2[:CMP_STRIDE].reshape(1, -1), p2[CMP_STRIDE:].reshape(1, -1)

    pk1, pk2 = pos_halves(cmp_pos_k)
    pv1, pv2 = pos_halves(cmp_pos_v)
    wk1, wk2 = halves(cmp_wk)
    wv1, wv2 = halves(cmp_wv)
    n_cmp = (seq - CMP_BLOCK) // CMP_STRIDE + 1
    posc = positions[:, CMP_BLOCK - 1::CMP_STRIDE][:, :n_cmp].astype(F32)
    posc = jnp.pad(posc, ((0, 0), (0, nc - n_cmp)))[:, :, None]
    kc, vc = _compress(kc_raw.reshape(batch, nc, CMP_STRIDE * g * hd),
                       vc_raw.reshape(batch, nc, CMP_STRIDE * g * hd), posc,
                       (pk1, pk2, pv1, pv2, wk1, wk2, wv1, wv2), rope_rows)

    n_sb = seq // SEL_BLOCK
    starts = np.arange(nc) * CMP_STRIDE
    jb = np.arange(n_sb)
    ov = ((starts[:, None] < (jb[None, :] + 1) * SEL_BLOCK)
          & (starts[:, None] + CMP_BLOCK > jb[None, :] * SEL_BLOCK) & (np.arange(nc)[:, None] < n_cmp))
    ov = jnp.asarray(ov, BF16)
    pt = jnp.asarray(np.where(np.arange(seq)[:, None] // SEL_BLOCK == jb[None, :], NEG, 0.0), BF16)

    seq3 = lambda a: a.reshape(batch, seq, KV_WIDTH)
    nsa = _nsa(q, gates, kc, vc, seq3(ks), seq3(vs), seq3(kw), seq3(vw), pt, ov, batch, seq)

    w_pool_out = w_out[:POOL_WIDTH].astype(BF16)
    w_nsa_out = (w_out[POOL_WIDTH:].reshape(g, r, hd, D_MODEL).transpose(1, 0, 2, 3)
                 .reshape(Q_WIDTH, D_MODEL).astype(BF16))
    return _out_ln(xf, [pool, nsa], [w_pool_out, w_nsa_out], ln_g, ln_b)


def _odd_mixer(xf, posi, w_in, conv_w, conv_b, wa, ba, wx, bx, lam, w_out, ln_g, ln_b, batch, seq):
    u = _matmul(xf, w_in.astype(BF16), F32, ROW_TILE)
    wax = jnp.concatenate([wa, wx], axis=2).astype(BF16)
    y = _rglru(u, posi, conv_w, conv_b[None, :], wax, ba[None, :], bx[None, :], lam[None, :], batch, seq)
    return _out_ln(xf, [y], [w_out.astype(BF16)], ln_g, ln_b)


def kernel(x, mem, positions, e_w_in, e_pool_w, e_pool_scale, e_cmp_pos_k, e_cmp_pos_v, e_cmp_wk, e_cmp_wv, e_w_out, o_w_in, o_conv_w, o_conv_b, o_wa, o_ba, o_wx, o_bx, o_lambda, o_w_out, x_wq, x_wkv, x_wo, f_w_up, f_w_down, ln_g, ln_b):
    batch, seq, d = x.shape
    n = batch * seq
    assert d == D_MODEL and seq % ROW_TILE == 0
    xf = x.reshape(n, d)
    posi = positions.reshape(n, 1)
    posf = posi.astype(F32)
    memf = mem.reshape(batch * mem.shape[1], d)
    for layer in range(DEPTH):
        j = layer // 2
        lg = lambda k: ln_g[layer, k][None, :]
        lb = lambda k: ln_b[layer, k][None, :]
        if layer % 2 == 0:
            xf = _even_mixer(xf, posf, positions, e_w_in[j], e_pool_w[j], e_pool_scale[j], e_cmp_pos_k[j],
                             e_cmp_pos_v[j], e_cmp_wk[j], e_cmp_wv[j], e_w_out[j], lg(0), lb(0), batch, seq)
        else:
            xf = _odd_mixer(xf, posi, o_w_in[j], o_conv_w[j], o_conv_b[j], o_wa[j], o_ba[j], o_wx[j],
                            o_bx[j], o_lambda[j], o_w_out[j], lg(0), lb(0), batch, seq)
        kv = _matmul(memf, x_wkv[layer].astype(BF16), BF16, memf.shape[0]).reshape(batch, mem.shape[1], 2 * d)
        wq = (x_wq[layer] * (X_HEAD_DIM ** -0.5 * LOG2E)).astype(BF16)
        xf = _xattn(xf, kv, wq, x_wo[layer].astype(BF16), lg(1), lb(1), batch, seq)
        xf = _swiglu(xf, f_w_up[layer].astype(BF16), f_w_down[layer].astype(BF16), lg(2), lb(2))
    return xf.reshape(batch, seq, d)
```

```python
import functools

import numpy as np
import jax
import jax.numpy as jnp
from jax import lax
from jax.experimental import pallas as pl
from jax.experimental.pallas import tpu as pltpu

F32 = jnp.float32
BF16 = jnp.bfloat16

D_MODEL = 1024
DEPTH = 2
ALPHA = (2.0 * DEPTH) ** 0.25
LN_EPS = 1e-5
NEG = -1e30
POOL_WIDTH = D_MODEL // 2
POOL_WINDOWS = (2, 4, 8, 16)
POOL_GROUP = POOL_WIDTH // len(POOL_WINDOWS)
POOL_HALO = 16
NSA_HEADS = 8
NSA_KV_HEADS = 2
NSA_HEAD_DIM = 64
NSA_GROUP = NSA_HEADS // NSA_KV_HEADS
CMP_BLOCK = 32
CMP_STRIDE = 16
SEL_BLOCK = 64
SEL_COUNT = 16
WINDOW = 512
Q_BLOCK = 128
N_BRANCH = 3
N_FORCED = 3
LOG2E = 1.4426950408889634
ROPE_THETA = 500000.0
ROT_DIM = NSA_HEAD_DIM // 4
ROT_HALF = ROT_DIM // 2
Q_WIDTH = NSA_HEADS * NSA_HEAD_DIM
KV_WIDTH = NSA_KV_HEADS * NSA_HEAD_DIM
GATE_WIDTH = NSA_HEADS * N_BRANCH
RNN_WIDTH = 1280
RNN_BLOCKS = 10
RNN_BLOCK_W = RNN_WIDTH // RNN_BLOCKS
CONV_WIDTH = 4
CONV_HALO = 8
LRU_C = 8.0
X_HEADS = 4
X_HEAD_DIM = D_MODEL // X_HEADS
FFN_HIDDEN = 2816
GELU_C0 = 0.7978845608028654
GELU_C1 = GELU_C0 * 0.044715

LANES = 128
SUBLANES = 8
BF16_ROWS = 16
VMEM_LIMIT = 56 * 1024 * 1024
ROW_TILE = 512
SEL_KEY_TILE = 512
FFN_CHUNKS = (1024, 1024, 768)


def _params(n_axes, vmem=VMEM_LIMIT):
    return pltpu.CompilerParams(dimension_semantics=("arbitrary",) * n_axes,
                                vmem_limit_bytes=vmem)


def _dot(a, b):
    return jnp.dot(a, b, preferred_element_type=F32)


def _dot_nt(a, b):
    return lax.dot_general(a, b, (((1,), (1,)), ((), ())), preferred_element_type=F32)


def _rep_rows(a, k):
    return jnp.concatenate([a] * k, axis=0)


def _rep_lanes(a, k):
    return jnp.concatenate([a] * k, axis=1) if k > 1 else a


def _layer_norm(z, g, b):
    mu = jnp.mean(z, axis=-1, keepdims=True)
    d = z - mu
    var = jnp.mean(d * d, axis=-1, keepdims=True)
    return d * lax.rsqrt(var + LN_EPS) * g + b


def _rope(v, cos, s_lo, s_hi):
    k = v.shape[1] // LANES
    up = pltpu.roll(v, v.shape[1] - ROT_HALF, 1)
    dn = pltpu.roll(v, ROT_HALF, 1)
    return v * _rep_lanes(cos, k) + up * _rep_lanes(s_lo, k) + dn * _rep_lanes(s_hi, k)


def _rope_tables(pos, inv, m_lo, m_hi):
    ang = pos * inv
    sin = jnp.sin(ang)
    return jnp.cos(ang), sin * m_lo, sin * m_hi


def _proj_even_kernel(x_ref, pos_ref, wp_ref, wq_ref, wkv_ref, wg_ref, poolw_ref, pscale_ref,
                      inv_ref, mlo_ref, mhi_ref,
                      pool_o, q_o, kc_o, vc_o, ks_o, vs_o, kw_o, vw_o, gate_o, pbuf):
    i = pl.program_id(1)
    tm = x_ref.shape[0]
    xb = x_ref[...].astype(BF16)
    cos, s_lo, s_hi = _rope_tables(pos_ref[...], inv_ref[...], mlo_ref[...], mhi_ref[...])

    q_o[...] = _rope(_dot(xb, wq_ref[...]), cos, s_lo, s_hi).astype(BF16)
    kv = _dot(xb, wkv_ref[...])
    kc_o[...] = kv[:, 0 * LANES:1 * LANES]
    vc_o[...] = kv[:, 1 * LANES:2 * LANES]
    ks_o[...] = _rope(kv[:, 2 * LANES:3 * LANES], cos, s_lo, s_hi).astype(BF16)
    vs_o[...] = kv[:, 3 * LANES:4 * LANES].astype(BF16)
    kw_o[...] = _rope(kv[:, 4 * LANES:5 * LANES], cos, s_lo, s_hi).astype(BF16)
    vw_o[...] = kv[:, 5 * LANES:6 * LANES].astype(BF16)
    gate_o[...] = jax.nn.sigmoid(_dot(xb, wg_ref[...]))

    @pl.when(i == 0)
    def _():
        pbuf[0:POOL_HALO, :] = jnp.zeros((POOL_HALO, POOL_WIDTH), F32)

    p = _dot(xb, wp_ref[...])
    pbuf[POOL_HALO:POOL_HALO + tm, :] = p
    t1 = i * tm + lax.broadcasted_iota(jnp.int32, (tm, 1), 0) + 1
    for g, w in enumerate(POOL_WINDOWS):
        sl = slice(g * POOL_GROUP, (g + 1) * POOL_GROUP)
        u = p[:, sl]
        tot = u
        for j in range(1, w):
            tot = tot + pbuf[pl.ds(POOL_HALO - j, tm), sl]
        cnt = jnp.minimum(t1, w).astype(F32)
        pooled = tot / cnt - u
        mixed = _dot(pooled.astype(BF16), poolw_ref[g]) * pscale_ref[:, sl]
        pool_o[:, sl] = mixed.astype(BF16)
    pbuf[0:POOL_HALO, :] = pbuf[tm:tm + POOL_HALO, :]


def _proj_even(xf, posf, wp, wq, wkv, wg, poolw, pscale, rope_rows, batch, seq):
    n = xf.shape[0]
    tm = ROW_TILE
    nt = seq // tm
    row = lambda b, i: (b * nt + i, 0)
    full = lambda b, i: (0, 0)
    inv, mlo, mhi = rope_rows
    outs = [
        jax.ShapeDtypeStruct((n, POOL_WIDTH), BF16),
        jax.ShapeDtypeStruct((n, Q_WIDTH), BF16),
        jax.ShapeDtypeStruct((n, KV_WIDTH), F32),
        jax.ShapeDtypeStruct((n, KV_WIDTH), F32),
        jax.ShapeDtypeStruct((n, KV_WIDTH), BF16),
        jax.ShapeDtypeStruct((n, KV_WIDTH), BF16),
        jax.ShapeDtypeStruct((n, KV_WIDTH), BF16),
        jax.ShapeDtypeStruct((n, KV_WIDTH), BF16),
        jax.ShapeDtypeStruct((n, LANES), F32),
    ]
    return pl.pallas_call(
        _proj_even_kernel,
        grid=(batch, nt),
        in_specs=[
            pl.BlockSpec((tm, D_MODEL), row),
            pl.BlockSpec((tm, 1), row),
            pl.BlockSpec(wp.shape, full),
            pl.BlockSpec(wq.shape, full),
            pl.BlockSpec(wkv.shape, full),
            pl.BlockSpec(wg.shape, full),
            pl.BlockSpec(poolw.shape, lambda b, i: (0, 0, 0)),
            pl.BlockSpec(pscale.shape, full),
            pl.BlockSpec(inv.shape, full),
            pl.BlockSpec(mlo.shape, full),
            pl.BlockSpec(mhi.shape, full),
        ],
        out_specs=[pl.BlockSpec((tm, o.shape[1]), row) for o in outs],
        out_shape=outs,
        scratch_shapes=[pltpu.VMEM((POOL_HALO + tm, POOL_WIDTH), F32)],
        compiler_params=_params(2),
        name="proj_even",
    )(xf, posf, wp, wq, wkv, wg, poolw, pscale, inv, mlo, mhi)


def _compress_kernel(rk_ref, rv_ref, pos_ref, pk1_ref, pk2_ref, pv1_ref, pv2_ref,
                     wk1_ref, wk2_ref, wv1_ref, wv2_ref, inv_ref, mlo_ref, mhi_ref,
                     kc_o, vc_o):
    nc = rk_ref.shape[0]

    def compress(r, p1, p2, w1, w2):
        a = _dot((r + p1[...]).astype(BF16), w1[...])
        b = _dot((r + p2[...]).astype(BF16), w2[...])
        return a + pltpu.roll(b, nc - 1, 0)

    kc = compress(rk_ref[...], pk1_ref, pk2_ref, wk1_ref, wk2_ref)
    cos, s_lo, s_hi = _rope_tables(pos_ref[...], inv_ref[...], mlo_ref[...], mhi_ref[...])
    kc_o[...] = _rope(kc, cos, s_lo, s_hi).astype(BF16)
    vc_o[...] = compress(rv_ref[...], pv1_ref, pv2_ref, wv1_ref, wv2_ref).astype(BF16)


def _compress(rk, rv, posc, consts, rope_rows):
    batch, nc, width = rk.shape
    blk = lambda b: (b, 0, 0)
    full = lambda b: (0, 0)
    out = jax.ShapeDtypeStruct((batch, nc, KV_WIDTH), BF16)
    args = list(consts) + list(rope_rows)
    return pl.pallas_call(
        _compress_kernel,
        grid=(batch,),
        in_specs=[pl.BlockSpec((None, nc, width), blk), pl.BlockSpec((None, nc, width), blk),
                  pl.BlockSpec((None, nc, 1), blk)] + [pl.BlockSpec(a.shape, full) for a in args],
        out_specs=[pl.BlockSpec((None, nc, KV_WIDTH), blk)] * 2,
        out_shape=[out, out],
        compiler_params=_params(1),
        name="compress",
    )(rk, rv, posc, *args)


def _softmax_rows(s):
    m = jnp.max(s, axis=1, keepdims=True)
    e = jnp.exp2(s - m)
    return e, jnp.sum(e, axis=1, keepdims=True)


def _dot_exact01(x, m01):
    hi = x.astype(BF16)
    r1 = x - hi.astype(F32)
    mid = r1.astype(BF16)
    lo = (r1 - mid.astype(F32)).astype(BF16)
    return _dot(hi, m01) + _dot(mid, m01) + _dot(lo, m01)


def _unselected_blocks(imp, q0, n_pick):
    nq, nb = imp.shape
    col = lax.broadcasted_iota(jnp.int32, (nq, nb), 1)
    cur = lax.shift_right_arithmetic(q0 + lax.broadcasted_iota(jnp.int32, (nq, nb), 0), 6)
    forced = (col == 0) | (col == cur) | (col == cur - 1)
    val = jnp.where(forced, -jnp.inf, jnp.where(col > cur, -1.0, imp))
    vt = val.T
    blk = lax.broadcasted_iota(jnp.int32, (nb, nq), 0).astype(F32)
    unsel = jnp.where(forced, 0.0, 1.0).T
    for _ in range(n_pick - N_FORCED):
        m = jnp.max(vt, axis=0, keepdims=True)
        first = jnp.min(jnp.where(vt == m, blk, float(nb)), axis=0, keepdims=True)
        hit = blk == first
        unsel = jnp.where(hit, 0.0, unsel)
        vt = jnp.where(hit, -jnp.inf, vt)
    return unsel.T


def _nsa_kernel(q_ref, gate_ref, kc_ref, vc_ref, ks_ref, vs_ref, kw_ref, vw_ref, pt_ref, ov_ref,
                o_ref, m_scr, l_scr, acc_scr, lhs_scr, oc_scr, ow_scr, s0_scr, s1_scr, *, n_pick):
    qb = pl.program_id(1)
    q0 = qb * Q_BLOCK
    tk = SEL_KEY_TILE
    ncmp = kc_ref.shape[0]
    wk = WINDOW + Q_BLOCK
    rows = NSA_GROUP * Q_BLOCK
    lane = lax.broadcasted_iota(jnp.int32, (Q_BLOCK, LANES), 1)

    def trow(width):
        return q0 + lax.broadcasted_iota(jnp.int32, (Q_BLOCK, width), 0)

    def kcol(width):
        return lax.broadcasted_iota(jnp.int32, (Q_BLOCK, width), 1)

    bias_c = jnp.where(kcol(ncmp) * CMP_STRIDE + (CMP_BLOCK - 1) <= trow(ncmp), 0.0, NEG)
    has_cmp = (trow(1) >= CMP_BLOCK - 1).astype(F32)
    wstart = pl.multiple_of(jnp.maximum(q0 - WINDOW, 0), Q_BLOCK)
    kpos_w = wstart + kcol(wk)
    bias_w = jnp.where(kpos_w <= trow(wk), jnp.where(kpos_w > trow(wk) - WINDOW, 0.0, NEG), NEG)

    for g in range(NSA_KV_HEADS):
        mine = (lane >= NSA_HEAD_DIM) if g else (lane < NSA_HEAD_DIM)
        qg = jnp.concatenate(
            [jnp.where(mine, q_ref[:, r * LANES:(r + 1) * LANES], jnp.zeros((), BF16))
             for r in range(NSA_GROUP)], axis=0)

        e, l = _softmax_rows(_dot_nt(qg, kc_ref[...]) + _rep_rows(bias_c, NSA_GROUP))
        p = e * (_rep_rows(has_cmp, NSA_GROUP) / l)
        oc_scr[g] = _dot(p.astype(BF16), vc_ref[...])
        psum = p[0:Q_BLOCK]
        for r in range(1, NSA_GROUP):
            psum = psum + p[r * Q_BLOCK:(r + 1) * Q_BLOCK]
        imp = _dot_exact01(psum, ov_ref[...])
        unsel = _unselected_blocks(imp, q0, n_pick)
        lhs_scr[g] = jnp.concatenate([qg, _rep_rows(unsel.astype(BF16), NSA_GROUP)], axis=1)

        e, l = _softmax_rows(_dot_nt(qg, kw_ref[pl.ds(wstart, wk), :]) + _rep_rows(bias_w, NSA_GROUP))
        ow_scr[g] = _dot((e / l).astype(BF16), vw_ref[pl.ds(wstart, wk), :])

        m_scr[g] = jnp.full((rows, LANES), -jnp.inf, F32)
        l_scr[g] = jnp.zeros((rows, LANES), F32)
        acc_scr[g] = jnp.zeros((rows, LANES), F32)

    def scores(kt, buf):
        k0 = pl.multiple_of(kt * tk, tk)
        rhs = jnp.concatenate([ks_ref[pl.ds(k0, tk), :], pt_ref[pl.ds(k0, tk), :]], axis=1)
        for g in range(NSA_KV_HEADS):
            buf[g] = _dot_nt(lhs_scr[g], rhs)

    def consume(kt, buf, causal):
        k0 = pl.multiple_of(kt * tk, tk)
        v = vs_ref[pl.ds(k0, tk), :]
        for g in range(NSA_KV_HEADS):
            s = buf[g]
            if causal:
                s = s + _rep_rows(jnp.where(k0 + kcol(tk) <= trow(tk), 0.0, NEG), NSA_GROUP)
            m_prev = m_scr[g]
            m_new = jnp.maximum(m_prev, jnp.max(s, axis=1, keepdims=True))
            alpha = jnp.exp2(m_prev - m_new)
            pe = jnp.exp2(s - _rep_lanes(m_new, tk // LANES))
            part = pe[:, 0:LANES]
            for c in range(1, tk // LANES):
                part = part + pe[:, c * LANES:(c + 1) * LANES]
            l_scr[g] = alpha * l_scr[g] + part
            acc_scr[g] = alpha * acc_scr[g] + _dot(pe.astype(BF16), v)
            m_scr[g] = m_new

    n_full = q0 // tk
    scores(0, s0_scr)

    def pair(j, carry):
        scores(2 * j + 1, s1_scr)
        consume(2 * j, s0_scr, False)
        scores(2 * j + 2, s0_scr)
        consume(2 * j + 1, s1_scr, False)
        return carry

    lax.fori_loop(0, n_full // 2, pair, 0)

    @pl.when(n_full % 2 == 0)
    def _():
        consume(n_full, s0_scr, True)

    @pl.when(n_full % 2 == 1)
    def _():
        scores(n_full, s1_scr)
        consume(n_full - 1, s0_scr, False)
        consume(n_full, s1_scr, True)

    for r in range(NSA_GROUP):
        rs = slice(r * Q_BLOCK, (r + 1) * Q_BLOCK)
        halves = []
        for g in range(NSA_KV_HEADS):
            c0 = (g * NSA_GROUP + r) * N_BRANCH
            o_sel = acc_scr[g, rs, :] / jnp.sum(l_scr[g, rs, :], axis=1, keepdims=True)
            halves.append(gate_ref[:, c0:c0 + 1] * oc_scr[g, rs, :]
                          + gate_ref[:, c0 + 1:c0 + 2] * o_sel
                          + gate_ref[:, c0 + 2:c0 + 3] * ow_scr[g, rs, :])
        o_ref[:, r * LANES:(r + 1) * LANES] = jnp.where(lane < NSA_HEAD_DIM, halves[0], halves[1]).astype(BF16)


def _nsa(q, gates, kc, vc, ks, vs, kw, vw, pt, ov, batch, seq):
    n = q.shape[0]
    nqb = seq // Q_BLOCK
    rows = NSA_GROUP * Q_BLOCK
    row = lambda b, i: (b * nqb + i, 0)
    per_b = lambda b, i: (b, 0, 0)
    full = lambda b, i: (0, 0)
    seq_spec = pl.BlockSpec((None, seq, KV_WIDTH), per_b)
    cmp_spec = pl.BlockSpec((None, kc.shape[1], KV_WIDTH), per_b)
    n_pick = min(SEL_COUNT, seq // SEL_BLOCK)
    return pl.pallas_call(
        functools.partial(_nsa_kernel, n_pick=n_pick),
        grid=(batch, nqb),
        in_specs=[pl.BlockSpec((Q_BLOCK, Q_WIDTH), row), pl.BlockSpec((Q_BLOCK, LANES), row),
                  cmp_spec, cmp_spec, seq_spec, seq_spec, seq_spec, seq_spec,
                  pl.BlockSpec(pt.shape, full), pl.BlockSpec(ov.shape, full)],
        out_specs=pl.BlockSpec((Q_BLOCK, Q_WIDTH), row),
        out_shape=jax.ShapeDtypeStruct((n, Q_WIDTH), BF16),
        scratch_shapes=[pltpu.VMEM((NSA_KV_HEADS, rows, LANES), F32),
                        pltpu.VMEM((NSA_KV_HEADS, rows, LANES), F32),
                        pltpu.VMEM((NSA_KV_HEADS, rows, LANES), F32),
                        pltpu.VMEM((NSA_KV_HEADS, rows, 2 * LANES), BF16),
                        pltpu.VMEM((NSA_KV_HEADS, rows, LANES), F32),
                        pltpu.VMEM((NSA_KV_HEADS, rows, LANES), F32),
                        pltpu.VMEM((NSA_KV_HEADS, rows, SEL_KEY_TILE), F32),
                        pltpu.VMEM((NSA_KV_HEADS, rows, SEL_KEY_TILE), F32)],
        compiler_params=_params(2),
        name="nsa",
    )(q, gates, kc, vc, ks, vs, kw, vw, pt, ov)


def _out_ln_kernel(*refs, n_parts):
    x_ref = refs[0]
    parts = refs[1:1 + n_parts]
    ws = refs[1 + n_parts:1 + 2 * n_parts]
    g_ref, b_ref, o_ref = refs[1 + 2 * n_parts:]
    z = ALPHA * x_ref[...]
    for a, w in zip(parts, ws):
        z = z + _dot(a[...], w[...])
    o_ref[...] = _layer_norm(z, g_ref[...], b_ref[...])


def _out_ln(xf, parts, ws, g, b):
    n = xf.shape[0]
    tm = ROW_TILE
    row = lambda i: (i, 0)
    full = lambda i: (0, 0)
    return pl.pallas_call(
        functools.partial(_out_ln_kernel, n_parts=len(parts)),
        grid=(n // tm,),
        in_specs=[pl.BlockSpec((tm, D_MODEL), row)]
                 + [pl.BlockSpec((tm, a.shape[1]), row) for a in parts]
                 + [pl.BlockSpec(w.shape, full) for w in ws]
                 + [pl.BlockSpec(g.shape, full), pl.BlockSpec(b.shape, full)],
        out_specs=pl.BlockSpec((tm, D_MODEL), row),
        out_shape=jax.ShapeDtypeStruct((n, D_MODEL), F32),
        compiler_params=_params(1),
        name="out_ln",
    )(xf, *parts, *ws, g, b)


def _matmul_kernel(a_ref, w_ref, o_ref):
    o_ref[...] = _dot(a_ref[...].astype(BF16), w_ref[...]).astype(o_ref.dtype)


def _matmul(a, w, out_dtype, tm):
    m, k = a.shape
    n = w.shape[1]
    return pl.pallas_call(
        _matmul_kernel,
        grid=(m // tm,),
        in_specs=[pl.BlockSpec((tm, k), lambda i: (i, 0)), pl.BlockSpec((k, n), lambda i: (0, 0))],
        out_specs=pl.BlockSpec((tm, n), lambda i: (i, 0)),
        out_shape=jax.ShapeDtypeStruct((m, n), out_dtype),
        compiler_params=_params(1),
        name="matmul",
    )(a, w)


def _xattn_kernel(x_ref, wq_ref, k_ref, v_ref, wo_ref, g_ref, b_ref, o_ref):
    x = x_ref[...]
    q = _dot(x.astype(BF16), wq_ref[...]).astype(BF16)
    outs = []
    for h in range(X_HEADS):
        sl = slice(h * X_HEAD_DIM, (h + 1) * X_HEAD_DIM)
        e, l = _softmax_rows(_dot_nt(q[:, sl], k_ref[:, sl]))
        outs.append(_dot((e / l).astype(BF16), v_ref[:, sl]))
    o = jnp.concatenate(outs, axis=1).astype(BF16)
    z = ALPHA * x + _dot(o, wo_ref[...])
    o_ref[...] = _layer_norm(z, g_ref[...], b_ref[...])


def _xattn(xf, kv, wq, wo, g, b, batch, seq):
    n = xf.shape[0]
    tm = ROW_TILE
    nt = seq // tm
    n_mem = kv.shape[1]
    row = lambda bi, i: (bi * nt + i, 0)
    full = lambda bi, i: (0, 0)
    return pl.pallas_call(
        _xattn_kernel,
        grid=(batch, nt),
        in_specs=[pl.BlockSpec((tm, D_MODEL), row), pl.BlockSpec(wq.shape, full),
                  pl.BlockSpec((None, n_mem, D_MODEL), lambda bi, i: (bi, 0, 0)),
                  pl.BlockSpec((None, n_mem, D_MODEL), lambda bi, i: (bi, 0, 1)),
                  pl.BlockSpec(wo.shape, full), pl.BlockSpec(g.shape, full), pl.BlockSpec(b.shape, full)],
        out_specs=pl.BlockSpec((tm, D_MODEL), row),
        out_shape=jax.ShapeDtypeStruct((n, D_MODEL), F32),
        compiler_params=_params(2),
        name="xattn",
    )(xf, wq, kv, kv, wo, g, b)


def _swiglu_kernel(x_ref, wu_ref, wd_ref, g_ref, b_ref, o_ref):
    x = x_ref[...]
    xb = x.astype(BF16)
    z = ALPHA * x
    c0 = 0
    for width in FFN_CHUNKS:
        h1 = _dot(xb, wu_ref[:, c0:c0 + width])
        h2 = _dot(xb, wu_ref[:, FFN_HIDDEN + c0:FFN_HIDDEN + c0 + width])
        act = (h1 * _sigmoid_tanh(h1) * h2).astype(BF16)
        z = z + _dot(act, wd_ref[c0:c0 + width, :])
        c0 += width
    o_ref[...] = _layer_norm(z, g_ref[...], b_ref[...])


def _resident(shape, index_map):
    return pl.BlockSpec(shape, index_map, pipeline_mode=pl.Buffered(1))


def _swiglu(xf, w_up, w_down, g, b):
    n = xf.shape[0]
    tm = ROW_TILE
    row = lambda i: (i, 0)
    full = lambda i: (0, 0)
    return pl.pallas_call(
        _swiglu_kernel,
        grid=(n // tm,),
        in_specs=[pl.BlockSpec((tm, D_MODEL), row), _resident(w_up.shape, full), _resident(w_down.shape, full),
                  pl.BlockSpec(g.shape, full), pl.BlockSpec(b.shape, full)],
        out_specs=pl.BlockSpec((tm, D_MODEL), row),
        out_shape=jax.ShapeDtypeStruct((n, D_MODEL), F32),
        compiler_params=_params(1),
        name="swiglu",
    )(xf, w_up, w_down, g, b)


def _gelu_tanh(x):
    inner = x * (GELU_C0 + GELU_C1 * (x * x))
    return x * (0.5 * jnp.tanh(inner) + 0.5)


def _sigmoid_tanh(z):
    return 0.5 * jnp.tanh(0.5 * z) + 0.5


def _odd_kernel(x_ref, pos_ref, win_ref, cw_ref, cb_ref, wax_ref, ba_ref, bx_ref, lam_ref, wout_ref,
                g_ref, b_ref, o_ref, xbuf, gate_scr, a_scr, b_scr, h_scr):
    i = pl.program_id(1)
    tm = x_ref.shape[0]
    groups = tm // SUBLANES

    @pl.when(i == 0)
    def _():
        xbuf[0:CONV_HALO, :] = jnp.zeros((CONV_HALO, RNN_WIDTH), F32)
        h_scr[...] = jnp.zeros((SUBLANES, RNN_WIDTH), F32)

    x = x_ref[...]
    xb = x.astype(BF16)
    gate_scr[...] = _dot(xb, win_ref[:, :RNN_WIDTH])
    xbuf[CONV_HALO:CONV_HALO + tm, :] = _dot(xb, win_ref[:, RNN_WIDTH:])
    reset = pos_ref[...] == 0
    in_group = lax.broadcasted_iota(jnp.int32, (groups, SUBLANES, RNN_BLOCK_W), 1)
    lam = lam_ref[...]
    log_a_scale = -LRU_C * (jnp.maximum(-lam, 0.0) + jnp.log1p(jnp.exp(-jnp.abs(lam))))

    for h in range(RNN_BLOCKS):
        sl = slice(h * RNN_BLOCK_W, (h + 1) * RNN_BLOCK_W)
        xc = cb_ref[:, sl]
        for k in range(CONV_WIDTH):
            xc = xc + xbuf[pl.ds(CONV_HALO - (CONV_WIDTH - 1) + k, tm), sl] * cw_ref[k:k + 1, sl]
        ri = _dot(xc.astype(BF16), wax_ref[h])
        r = _sigmoid_tanh(ri[:, :RNN_BLOCK_W] + ba_ref[:, sl])
        ig = _sigmoid_tanh(ri[:, RNN_BLOCK_W:] + bx_ref[:, sl])
        log_a = r * log_a_scale[:, sl]
        a = jnp.where(reset, 0.0, jnp.exp(log_a))
        th = jnp.tanh(log_a)
        m2 = -2.0 * th / (1.0 - th)
        mult = jnp.where(reset, 1.0, jnp.where(m2 > 0.0, m2 * lax.rsqrt(m2), 0.0))
        b = mult * (ig * xc)
        a3 = a.reshape(groups, SUBLANES, RNN_BLOCK_W)
        b3 = b.reshape(groups, SUBLANES, RNN_BLOCK_W)
        for d in (1, 2, 4):
            a_prev = pltpu.roll(a3, d, 1)
            b_prev = pltpu.roll(b3, d, 1)
            ok = in_group >= d
            b3 = jnp.where(ok, a3 * b_prev + b3, b3)
            a3 = jnp.where(ok, a3 * a_prev, a3)
        a_scr[:, sl] = a3.reshape(tm, RNN_BLOCK_W)
        b_scr[:, sl] = b3.reshape(tm, RNN_BLOCK_W)
    xbuf[0:CONV_HALO, :] = xbuf[tm:tm + CONV_HALO, :]

    def body(j, h_prev):
        r0 = pl.multiple_of(j * SUBLANES, SUBLANES)
        hh = a_scr[pl.ds(r0, SUBLANES), :] * h_prev + b_scr[pl.ds(r0, SUBLANES), :]
        b_scr[pl.ds(r0, SUBLANES), :] = hh
        return jnp.broadcast_to(hh[SUBLANES - 1:SUBLANES, :], (SUBLANES, RNN_WIDTH))

    h_scr[...] = lax.fori_loop(0, groups, body, h_scr[...], unroll=4)

    y = (b_scr[...] * _gelu_tanh(gate_scr[...])).astype(BF16)
    z = ALPHA * x + _dot(y, wout_ref[...])
    o_ref[...] = _layer_norm(z, g_ref[...], b_ref[...])


def _odd_layer(xf, posi, w_in, cw, cb, wax, ba, bx, lam, w_out, g, b, batch, seq):
    n = xf.shape[0]
    tm = ROW_TILE
    nt = seq // tm
    row = lambda bi, i: (bi * nt + i, 0)
    full = lambda bi, i: (0, 0)
    consts = [w_in, cw, cb]
    rest = [ba, bx, lam, w_out, g, b]
    return pl.pallas_call(
        _odd_kernel,
        grid=(batch, nt),
        in_specs=[pl.BlockSpec((tm, D_MODEL), row), pl.BlockSpec((tm, 1), row)]
                 + [pl.BlockSpec(a.shape, full) for a in consts]
                 + [pl.BlockSpec(wax.shape, lambda bi, i: (0, 0, 0))]
                 + [pl.BlockSpec(a.shape, full) for a in rest],
        out_specs=pl.BlockSpec((tm, D_MODEL), row),
        out_shape=jax.ShapeDtypeStruct((n, D_MODEL), F32),
        scratch_shapes=[pltpu.VMEM((CONV_HALO + tm, RNN_WIDTH), F32),
                        pltpu.VMEM((tm, RNN_WIDTH), F32),
                        pltpu.VMEM((tm, RNN_WIDTH), F32),
                        pltpu.VMEM((tm, RNN_WIDTH), F32),
                        pltpu.VMEM((SUBLANES, RNN_WIDTH), F32)],
        compiler_params=_params(2),
        name="odd_layer",
    )(xf, posi, *consts, wax, *rest)


def _rope_rows():
    inv = ROPE_THETA ** (-jnp.arange(ROT_HALF, dtype=F32) * 2.0 / ROT_DIM)
    pad = jnp.zeros((NSA_HEAD_DIM - ROT_DIM,), F32)
    ones = jnp.ones((ROT_HALF,), F32)
    zeros = jnp.zeros((ROT_HALF,), F32)
    per_head = lambda parts: jnp.tile(jnp.concatenate(parts), LANES // NSA_HEAD_DIM)[None, :]
    return (per_head([inv, inv, pad]), per_head([-ones, zeros, pad]), per_head([zeros, ones, pad]))


def _even_mixer(xf, posf, positions, w_in, pool_w, pool_scale, cmp_pos_k, cmp_pos_v, cmp_wk, cmp_wv,
                w_out, ln_g, ln_b, batch, seq):
    assert seq // SEL_BLOCK == LANES and seq % SEL_KEY_TILE == 0 and seq >= WINDOW + Q_BLOCK
    hd, g, r = NSA_HEAD_DIM, NSA_KV_HEADS, NSA_GROUP
    c0 = POOL_WIDTH
    c1 = c0 + Q_WIDTH
    c2 = c1 + GATE_WIDTH
    wp = w_in[:, :c0].astype(BF16)
    wq = (w_in[:, c0:c1].reshape(D_MODEL, g, r, hd).transpose(0, 2, 1, 3).reshape(D_MODEL, Q_WIDTH)
          * (hd ** -0.5 * LOG2E)).astype(BF16)
    wg = jnp.pad(w_in[:, c1:c2], ((0, 0), (0, LANES - GATE_WIDTH))).astype(BF16)
    wkv = w_in[:, c2:].astype(BF16)
    rope_rows = _rope_rows()
    pool, q, kc_raw, vc_raw, ks, vs, kw, vw, gates = _proj_even(
        xf, posf, wp, wq, wkv, wg, pool_w.astype(BF16), pool_scale[None, :], rope_rows, batch, seq)

    nc = seq // CMP_STRIDE
    eye = jnp.eye(g, dtype=F32)

    def halves(w):
        w4 = jnp.einsum('lde,gh->lgdhe', w.reshape(CMP_BLOCK, hd, hd), eye)
        w4 = w4.reshape(CMP_BLOCK, g * hd, g * hd)
        return (w4[:CMP_STRIDE].reshape(CMP_STRIDE * g * hd, g * hd).astype(BF16),
                w4[CMP_STRIDE:].reshape(CMP_STRIDE * g * hd, g * hd).astype(BF16))

    def pos_halves(p):
        p2 = jnp.tile(p[:, None, :], (1, g, 1)).reshape(CMP_BLOCK, g * hd)
        return p2[:CMP_STRIDE].reshape(1, -1), p2[CMP_STRIDE:].reshape(1, -1)

    pk1, pk2 = pos_halves(cmp_pos_k)
    pv1, pv2 = pos_halves(cmp_pos_v)
    wk1, wk2 = halves(cmp_wk)
    wv1, wv2 = halves(cmp_wv)
    n_cmp = (seq - CMP_BLOCK) // CMP_STRIDE + 1
    posc = positions[:, CMP_BLOCK - 1::CMP_STRIDE][:, :n_cmp].astype(F32)
    posc = jnp.pad(posc, ((0, 0), (0, nc - n_cmp)))[:, :, None]
    kc, vc = _compress(kc_raw.reshape(batch, nc, CMP_STRIDE * g * hd),
                       vc_raw.reshape(batch, nc, CMP_STRIDE * g * hd), posc,
                       (pk1, pk2, pv1, pv2, wk1, wk2, wv1, wv2), rope_rows)

    n_sb = seq // SEL_BLOCK
    starts = np.arange(nc) * CMP_STRIDE
    jb = np.arange(n_sb)
    ov = ((starts[:, None] < (jb[None, :] + 1) * SEL_BLOCK)
          & (starts[:, None] + CMP_BLOCK > jb[None, :] * SEL_BLOCK) & (np.arange(nc)[:, None] < n_cmp))
    ov = jnp.asarray(ov, BF16)
    pt = jnp.asarray(np.where(np.arange(seq)[:, None] // SEL_BLOCK == jb[None, :], NEG, 0.0), BF16)

    seq3 = lambda a: a.reshape(batch, seq, KV_WIDTH)
    nsa = _nsa(q, gates, kc, vc, seq3(ks), seq3(vs), seq3(kw), seq3(vw), pt, ov, batch, seq)

    w_pool_out = w_out[:POOL_WIDTH].astype(BF16)
    w_nsa_out = (w_out[POOL_WIDTH:].reshape(g, r, hd, D_MODEL).transpose(1, 0, 2, 3)
                 .reshape(Q_WIDTH, D_MODEL).astype(BF16))
    return _out_ln(xf, [pool, nsa], [w_pool_out, w_nsa_out], ln_g, ln_b)


def _odd_mixer(xf, posi, w_in, conv_w, conv_b, wa, ba, wx, bx, lam, w_out, ln_g, ln_b, batch, seq):
    wax = jnp.concatenate([wa, wx], axis=2).astype(BF16)
    return _odd_layer(xf, posi, w_in.astype(BF16), conv_w, conv_b[None, :], wax, ba[None, :], bx[None, :],
                      lam[None, :], w_out.astype(BF16), ln_g, ln_b, batch, seq)


def kernel(x, mem, positions, e_w_in, e_pool_w, e_pool_scale, e_cmp_pos_k, e_cmp_pos_v, e_cmp_wk, e_cmp_wv, e_w_out, o_w_in, o_conv_w, o_conv_b, o_wa, o_ba, o_wx, o_bx, o_lambda, o_w_out, x_wq, x_wkv, x_wo, f_w_up, f_w_down, ln_g, ln_b):
    batch, seq, d = x.shape
    n = batch * seq
    assert d == D_MODEL and seq % ROW_TILE == 0
    xf = x.reshape(n, d)
    posi = positions.reshape(n, 1)
    posf = posi.astype(F32)
    memf = mem.reshape(batch * mem.shape[1], d)
    for layer in range(DEPTH):
        j = layer // 2
        lg = lambda k: ln_g[layer, k][None, :]
        lb = lambda k: ln_b[layer, k][None, :]
        if layer % 2 == 0:
            xf = _even_mixer(xf, posf, positions, e_w_in[j], e_pool_w[j], e_pool_scale[j], e_cmp_pos_k[j],
                             e_cmp_pos_v[j], e_cmp_wk[j], e_cmp_wv[j], e_w_out[j], lg(0), lb(0), batch, seq)
        else:
            xf = _odd_mixer(xf, posi, o_w_in[j], o_conv_w[j], o_conv_b[j], o_wa[j], o_ba[j], o_wx[j],
                            o_bx[j], o_lambda[j], o_w_out[j], lg(0), lb(0), batch, seq)
        kv = _matmul(memf, x_wkv[layer].astype(BF16), BF16, memf.shape[0]).reshape(batch, mem.shape[1], 2 * d)
        wq = (x_wq[layer] * (X_HEAD_DIM ** -0.5 * LOG2E)).astype(BF16)
        xf = _xattn(xf, kv, wq, x_wo[layer].astype(BF16), lg(1), lb(1), batch, seq)
        xf = _swiglu(xf, f_w_up[layer].astype(BF16), f_w_down[layer].astype(BF16), lg(2), lb(2))
    return xf.reshape(batch, seq, d)
```

```python
import functools

import numpy as np
import jax
import jax.numpy as jnp
from jax import lax
from jax.experimental import pallas as pl
from jax.experimental.pallas import tpu as pltpu

F32 = jnp.float32
BF16 = jnp.bfloat16

D_MODEL = 1024
DEPTH = 2
ALPHA = (2.0 * DEPTH) ** 0.25
LN_EPS = 1e-5
NEG = -1e30
POOL_WIDTH = D_MODEL // 2
POOL_WINDOWS = (2, 4, 8, 16)
POOL_GROUP = POOL_WIDTH // len(POOL_WINDOWS)
POOL_HALO = 16
NSA_HEADS = 8
NSA_KV_HEADS = 2
NSA_HEAD_DIM = 64
NSA_GROUP = NSA_HEADS // NSA_KV_HEADS
CMP_BLOCK = 32
CMP_STRIDE = 16
SEL_BLOCK = 64
SEL_COUNT = 16
WINDOW = 512
Q_BLOCK = 256
N_BRANCH = 3
N_FORCED = 3
LOG2E = 1.4426950408889634
ROPE_THETA = 500000.0
ROT_DIM = NSA_HEAD_DIM // 4
ROT_HALF = ROT_DIM // 2
Q_WIDTH = NSA_HEADS * NSA_HEAD_DIM
KV_WIDTH = NSA_KV_HEADS * NSA_HEAD_DIM
GATE_WIDTH = NSA_HEADS * N_BRANCH
RNN_WIDTH = 1280
RNN_BLOCKS = 10
RNN_BLOCK_W = RNN_WIDTH // RNN_BLOCKS
CONV_WIDTH = 4
CONV_HALO = 8
LRU_C = 8.0
X_HEADS = 4
X_HEAD_DIM = D_MODEL // X_HEADS
FFN_HIDDEN = 2816
GELU_C0 = 0.7978845608028654
GELU_C1 = GELU_C0 * 0.044715

LANES = 128
SUBLANES = 8
BF16_ROWS = 16
VMEM_LIMIT = 56 * 1024 * 1024
ROW_TILE = 512
SEL_KEY_TILE = 512
FFN_CHUNKS = (1024, 1024, 768)


def _params(n_axes, vmem=VMEM_LIMIT):
    return pltpu.CompilerParams(dimension_semantics=("arbitrary",) * n_axes,
                                vmem_limit_bytes=vmem)


def _resident(shape, index_map):
    return pl.BlockSpec(shape, index_map, pipeline_mode=pl.Buffered(1))


def _dot(a, b):
    return jnp.dot(a, b, preferred_element_type=F32)


def _dot_nt(a, b):
    return lax.dot_general(a, b, (((1,), (1,)), ((), ())), preferred_element_type=F32)


def _rep_rows(a, k):
    return jnp.concatenate([a] * k, axis=0)


def _rep_lanes(a, k):
    return jnp.concatenate([a] * k, axis=1) if k > 1 else a


def _layer_norm(z, g, b):
    mu = jnp.mean(z, axis=-1, keepdims=True)
    d = z - mu
    var = jnp.mean(d * d, axis=-1, keepdims=True)
    return d * lax.rsqrt(var + LN_EPS) * g + b


def _rope(v, cos, s_lo, s_hi):
    k = v.shape[1] // LANES
    up = pltpu.roll(v, v.shape[1] - ROT_HALF, 1)
    dn = pltpu.roll(v, ROT_HALF, 1)
    return v * _rep_lanes(cos, k) + up * _rep_lanes(s_lo, k) + dn * _rep_lanes(s_hi, k)


def _rope_tables(pos, inv, m_lo, m_hi):
    ang = pos * inv
    sin = jnp.sin(ang)
    return jnp.cos(ang), sin * m_lo, sin * m_hi


def _proj_even_kernel(x_ref, pos_ref, wp_ref, wq_ref, wkv_ref, wg_ref, poolw_ref, pscale_ref,
                      inv_ref, mlo_ref, mhi_ref,
                      pool_o, q_o, kc_o, vc_o, ks_o, vs_o, kw_o, vw_o, gate_o, pbuf):
    i = pl.program_id(1)
    tm = x_ref.shape[0]
    xb = x_ref[...].astype(BF16)
    cos, s_lo, s_hi = _rope_tables(pos_ref[...], inv_ref[...], mlo_ref[...], mhi_ref[...])

    q_o[...] = _rope(_dot(xb, wq_ref[...]), cos, s_lo, s_hi).astype(BF16)
    kv = _dot(xb, wkv_ref[...])
    kc_o[...] = kv[:, 0 * LANES:1 * LANES]
    vc_o[...] = kv[:, 1 * LANES:2 * LANES]
    ks_o[...] = _rope(kv[:, 2 * LANES:3 * LANES], cos, s_lo, s_hi).astype(BF16)
    vs_o[...] = kv[:, 3 * LANES:4 * LANES].astype(BF16)
    kw_o[...] = _rope(kv[:, 4 * LANES:5 * LANES], cos, s_lo, s_hi).astype(BF16)
    vw_o[...] = kv[:, 5 * LANES:6 * LANES].astype(BF16)
    gate_o[...] = jax.nn.sigmoid(_dot(xb, wg_ref[...]))

    @pl.when(i == 0)
    def _():
        pbuf[0:POOL_HALO, :] = jnp.zeros((POOL_HALO, POOL_WIDTH), F32)

    p = _dot(xb, wp_ref[...])
    pbuf[POOL_HALO:POOL_HALO + tm, :] = p
    t1 = i * tm + lax.broadcasted_iota(jnp.int32, (tm, 1), 0) + 1
    for g, w in enumerate(POOL_WINDOWS):
        sl = slice(g * POOL_GROUP, (g + 1) * POOL_GROUP)
        u = p[:, sl]
        tot = u
        for j in range(1, w):
            tot = tot + pbuf[pl.ds(POOL_HALO - j, tm), sl]
        cnt = jnp.minimum(t1, w).astype(F32)
        pooled = tot / cnt - u
        mixed = _dot(pooled.astype(BF16), poolw_ref[g]) * pscale_ref[:, sl]
        pool_o[:, sl] = mixed.astype(BF16)
    pbuf[0:POOL_HALO, :] = pbuf[tm:tm + POOL_HALO, :]


def _proj_even(xf, posf, wp, wq, wkv, wg, poolw, pscale, rope_rows, batch, seq):
    n = xf.shape[0]
    tm = ROW_TILE
    nt = seq // tm
    row = lambda b, i: (b * nt + i, 0)
    full = lambda b, i: (0, 0)
    inv, mlo, mhi = rope_rows
    outs = [
        jax.ShapeDtypeStruct((n, POOL_WIDTH), BF16),
        jax.ShapeDtypeStruct((n, Q_WIDTH), BF16),
        jax.ShapeDtypeStruct((n, KV_WIDTH), F32),
        jax.ShapeDtypeStruct((n, KV_WIDTH), F32),
        jax.ShapeDtypeStruct((n, KV_WIDTH), BF16),
        jax.ShapeDtypeStruct((n, KV_WIDTH), BF16),
        jax.ShapeDtypeStruct((n, KV_WIDTH), BF16),
        jax.ShapeDtypeStruct((n, KV_WIDTH), BF16),
        jax.ShapeDtypeStruct((n, LANES), F32),
    ]
    return pl.pallas_call(
        _proj_even_kernel,
        grid=(batch, nt),
        in_specs=[
            pl.BlockSpec((tm, D_MODEL), row),
            pl.BlockSpec((tm, 1), row),
            pl.BlockSpec(wp.shape, full),
            pl.BlockSpec(wq.shape, full),
            pl.BlockSpec(wkv.shape, full),
            pl.BlockSpec(wg.shape, full),
            pl.BlockSpec(poolw.shape, lambda b, i: (0, 0, 0)),
            pl.BlockSpec(pscale.shape, full),
            pl.BlockSpec(inv.shape, full),
            pl.BlockSpec(mlo.shape, full),
            pl.BlockSpec(mhi.shape, full),
        ],
        out_specs=[pl.BlockSpec((tm, o.shape[1]), row) for o in outs],
        out_shape=outs,
        scratch_shapes=[pltpu.VMEM((POOL_HALO + tm, POOL_WIDTH), F32)],
        compiler_params=_params(2),
        name="proj_even",
    )(xf, posf, wp, wq, wkv, wg, poolw, pscale, inv, mlo, mhi)


def _compress_kernel(rk_ref, rv_ref, pos_ref, pk1_ref, pk2_ref, pv1_ref, pv2_ref,
                     wk1_ref, wk2_ref, wv1_ref, wv2_ref, inv_ref, mlo_ref, mhi_ref,
                     kc_o, vc_o):
    nc = rk_ref.shape[0]

    def compress(r, p1, p2, w1, w2):
        a = _dot((r + p1[...]).astype(BF16), w1[...])
        b = _dot((r + p2[...]).astype(BF16), w2[...])
        return a + pltpu.roll(b, nc - 1, 0)

    kc = compress(rk_ref[...], pk1_ref, pk2_ref, wk1_ref, wk2_ref)
    cos, s_lo, s_hi = _rope_tables(pos_ref[...], inv_ref[...], mlo_ref[...], mhi_ref[...])
    kc_o[...] = _rope(kc, cos, s_lo, s_hi).astype(BF16)
    vc_o[...] = compress(rv_ref[...], pv1_ref, pv2_ref, wv1_ref, wv2_ref).astype(BF16)


def _compress(rk, rv, posc, consts, rope_rows):
    batch, nc, width = rk.shape
    blk = lambda b: (b, 0, 0)
    full = lambda b: (0, 0)
    out = jax.ShapeDtypeStruct((batch, nc, KV_WIDTH), BF16)
    args = list(consts) + list(rope_rows)
    return pl.pallas_call(
        _compress_kernel,
        grid=(batch,),
        in_specs=[pl.BlockSpec((None, nc, width), blk), pl.BlockSpec((None, nc, width), blk),
                  pl.BlockSpec((None, nc, 1), blk)] + [pl.BlockSpec(a.shape, full) for a in args],
        out_specs=[pl.BlockSpec((None, nc, KV_WIDTH), blk)] * 2,
        out_shape=[out, out],
        compiler_params=_params(1),
        name="compress",
    )(rk, rv, posc, *args)


def _softmax_rows(s):
    m = jnp.max(s, axis=1, keepdims=True)
    e = jnp.exp2(s - m)
    return e, jnp.sum(e, axis=1, keepdims=True)


def _dot_exact01(x, m01):
    hi = x.astype(BF16)
    r1 = x - hi.astype(F32)
    mid = r1.astype(BF16)
    lo = (r1 - mid.astype(F32)).astype(BF16)
    return _dot(hi, m01) + _dot(mid, m01) + _dot(lo, m01)


def _unselected_blocks(imp, q0, n_pick):
    nq, nb = imp.shape
    col = lax.broadcasted_iota(jnp.int32, (nq, nb), 1)
    cur = lax.shift_right_arithmetic(q0 + lax.broadcasted_iota(jnp.int32, (nq, nb), 0), 6)
    forced = (col == 0) | (col == cur) | (col == cur - 1)
    val = jnp.where(forced, -jnp.inf, jnp.where(col > cur, -1.0, imp))
    vt = val.T
    blk = lax.broadcasted_iota(jnp.int32, (nb, nq), 0).astype(F32)
    unsel = jnp.where(forced, 0.0, 1.0).T
    for _ in range(n_pick - N_FORCED):
        m = jnp.max(vt, axis=0, keepdims=True)
        first = jnp.min(jnp.where(vt == m, blk, float(nb)), axis=0, keepdims=True)
        hit = blk == first
        unsel = jnp.where(hit, 0.0, unsel)
        vt = jnp.where(hit, -jnp.inf, vt)
    return unsel.T


def _nsa_kernel(q_ref, gate_ref, kc_ref, vc_ref, ks_ref, vs_ref, kw_ref, vw_ref, pt_ref, ov_ref,
                o_ref, m_scr, l_scr, acc_scr, lhs_scr, oc_scr, ow_scr, s0_scr, s1_scr, w_scr, gx_scr, *, n_pick):
    qb = pl.program_id(1)
    q0 = qb * Q_BLOCK
    tk = SEL_KEY_TILE
    ncmp = kc_ref.shape[0]
    wk = WINDOW + Q_BLOCK
    rows = NSA_GROUP * Q_BLOCK
    lane = lax.broadcasted_iota(jnp.int32, (Q_BLOCK, LANES), 1)

    def trow(width):
        return q0 + lax.broadcasted_iota(jnp.int32, (Q_BLOCK, width), 0)

    def kcol(width):
        return lax.broadcasted_iota(jnp.int32, (Q_BLOCK, width), 1)

    bias_c = jnp.where(kcol(ncmp) * CMP_STRIDE + (CMP_BLOCK - 1) <= trow(ncmp), 0.0, NEG)
    has_cmp = (trow(1) >= CMP_BLOCK - 1).astype(F32)
    wstart = pl.multiple_of(jnp.maximum(q0 - WINDOW, 0), Q_BLOCK)
    kpos_w = wstart + kcol(wk)
    bias_w = jnp.where(kpos_w <= trow(wk), jnp.where(kpos_w > trow(wk) - WINDOW, 0.0, NEG), NEG)

    for g in range(NSA_KV_HEADS):
        mine = (lane >= NSA_HEAD_DIM) if g else (lane < NSA_HEAD_DIM)
        lhs_scr[g, :, 0:LANES] = jnp.concatenate(
            [jnp.where(mine, q_ref[:, r * LANES:(r + 1) * LANES], jnp.zeros((), BF16))
             for r in range(NSA_GROUP)], axis=0)
        m_scr[g] = jnp.full((rows, LANES), -jnp.inf, F32)
        l_scr[g] = jnp.zeros((rows, LANES), F32)
        acc_scr[g] = jnp.zeros((rows, LANES), F32)

    def cmp_scores(g):
        s0_scr[g] = _dot_nt(lhs_scr[g, :, 0:LANES], kc_ref[...])

    def cmp_attend(g):
        e, l = _softmax_rows(s0_scr[g] + _rep_rows(bias_c, NSA_GROUP))
        p = e * (_rep_rows(has_cmp, NSA_GROUP) / l)
        oc_scr[g] = _dot(p.astype(BF16), vc_ref[...])
        psum = p[0:Q_BLOCK]
        for r in range(1, NSA_GROUP):
            psum = psum + p[r * Q_BLOCK:(r + 1) * Q_BLOCK]
        return _dot_exact01(psum, ov_ref[...])

    def select(g, imp):
        unsel = _unselected_blocks(imp, q0, n_pick)
        lhs_scr[g, :, LANES:2 * LANES] = _rep_rows(unsel.astype(BF16), NSA_GROUP)

    def win_scores(g):
        w_scr[g] = _dot_nt(lhs_scr[g, :, 0:LANES], kw_ref[pl.ds(wstart, wk), :])

    def win_attend(g):
        e, l = _softmax_rows(w_scr[g] + _rep_rows(bias_w, NSA_GROUP))
        ow_scr[g] = _dot(e.astype(BF16), vw_ref[pl.ds(wstart, wk), :]) / l

    def expand_gates():
        low = lane < NSA_HEAD_DIM
        for r in range(NSA_GROUP):
            for br in range(N_BRANCH):
                c_lo = r * N_BRANCH + br
                c_hi = (NSA_GROUP + r) * N_BRANCH + br
                gx_scr[c_lo] = jnp.where(low, gate_ref[:, c_lo:c_lo + 1], gate_ref[:, c_hi:c_hi + 1])

    cmp_scores(0)
    cmp_scores(1)
    expand_gates()
    win_scores(0)
    imp0 = cmp_attend(0)
    win_scores(1)
    imp1 = cmp_attend(1)
    select(0, imp0)
    win_attend(0)
    select(1, imp1)
    win_attend(1)

    def scores(kt, buf, g):
        k0 = pl.multiple_of(kt * tk, tk)
        rhs = jnp.concatenate([ks_ref[pl.ds(k0, tk), :], pt_ref[pl.ds(k0, tk), :]], axis=1)
        buf[g] = _dot_nt(lhs_scr[g], rhs)

    def consume(kt, buf, g, causal):
        k0 = pl.multiple_of(kt * tk, tk)
        s = buf[g]
        if causal:
            s = s + _rep_rows(jnp.where(k0 + kcol(tk) <= trow(tk), 0.0, NEG), NSA_GROUP)
        m_prev = m_scr[g]
        m_new = jnp.maximum(m_prev, jnp.max(s, axis=1, keepdims=True))
        alpha = jnp.exp2(m_prev - m_new)
        pe = jnp.exp2(s - _rep_lanes(m_new, tk // LANES))
        part = pe[:, 0:LANES]
        for c in range(1, tk // LANES):
            part = part + pe[:, c * LANES:(c + 1) * LANES]
        l_scr[g] = alpha * l_scr[g] + part
        acc_scr[g] = alpha * acc_scr[g] + _dot(pe.astype(BF16), vs_ref[pl.ds(k0, tk), :])
        m_scr[g] = m_new

    def step(nxt, nxt_buf, cur, cur_buf, causal=False):
        for g in range(NSA_KV_HEADS):
            if nxt is not None:
                scores(nxt, nxt_buf, g)
        for g in range(NSA_KV_HEADS):
            consume(cur, cur_buf, g, causal)

    n_full = q0 // tk
    for g in range(NSA_KV_HEADS):
        scores(0, s0_scr, g)

    def pair(j, carry):
        step(2 * j + 1, s1_scr, 2 * j, s0_scr)
        step(2 * j + 2, s0_scr, 2 * j + 1, s1_scr)
        return carry

    lax.fori_loop(0, n_full // 2, pair, 0)

    @pl.when(n_full % 2 == 0)
    def _():
        step(None, None, n_full, s0_scr, causal=True)

    @pl.when(n_full % 2 == 1)
    def _():
        step(n_full, s1_scr, n_full - 1, s0_scr)
        step(None, None, n_full, s1_scr, causal=True)

    low = lane < NSA_HEAD_DIM
    for r in range(NSA_GROUP):
        rs = slice(r * Q_BLOCK, (r + 1) * Q_BLOCK)
        merged = lambda scr: jnp.where(low, scr[0, rs, :], scr[1, rs, :])
        l_sel = jnp.where(low, jnp.sum(l_scr[0, rs, :], axis=1, keepdims=True),
                          jnp.sum(l_scr[1, rs, :], axis=1, keepdims=True))
        out = (gx_scr[r * N_BRANCH] * merged(oc_scr)
               + gx_scr[r * N_BRANCH + 1] * (merged(acc_scr) / l_sel)
               + gx_scr[r * N_BRANCH + 2] * merged(ow_scr))
        o_ref[:, r * LANES:(r + 1) * LANES] = out.astype(BF16)


def _nsa(q, gates, kc, vc, ks, vs, kw, vw, pt, ov, batch, seq):
    n = q.shape[0]
    nqb = seq // Q_BLOCK
    rows = NSA_GROUP * Q_BLOCK
    row = lambda b, i: (b * nqb + i, 0)
    per_b = lambda b, i: (b, 0, 0)
    full = lambda b, i: (0, 0)
    seq_spec = _resident((None, seq, KV_WIDTH), per_b)
    cmp_spec = _resident((None, kc.shape[1], KV_WIDTH), per_b)
    n_pick = min(SEL_COUNT, seq // SEL_BLOCK)
    return pl.pallas_call(
        functools.partial(_nsa_kernel, n_pick=n_pick),
        grid=(batch, nqb),
        in_specs=[pl.BlockSpec((Q_BLOCK, Q_WIDTH), row), pl.BlockSpec((Q_BLOCK, LANES), row),
                  cmp_spec, cmp_spec, seq_spec, seq_spec, seq_spec, seq_spec,
                  _resident(pt.shape, full), _resident(ov.shape, full)],
        out_specs=pl.BlockSpec((Q_BLOCK, Q_WIDTH), row),
        out_shape=jax.ShapeDtypeStruct((n, Q_WIDTH), BF16),
        scratch_shapes=[pltpu.VMEM((NSA_KV_HEADS, rows, LANES), F32),
                        pltpu.VMEM((NSA_KV_HEADS, rows, LANES), F32),
                        pltpu.VMEM((NSA_KV_HEADS, rows, LANES), F32),
                        pltpu.VMEM((NSA_KV_HEADS, rows, 2 * LANES), BF16),
                        pltpu.VMEM((NSA_KV_HEADS, rows, LANES), F32),
                        pltpu.VMEM((NSA_KV_HEADS, rows, LANES), F32),
                        pltpu.VMEM((NSA_KV_HEADS, rows, SEL_KEY_TILE), F32),
                        pltpu.VMEM((NSA_KV_HEADS, rows, SEL_KEY_TILE), F32),
                        pltpu.VMEM((NSA_KV_HEADS, rows, WINDOW + Q_BLOCK), F32),
                        pltpu.VMEM((NSA_GROUP * N_BRANCH, Q_BLOCK, LANES), F32)],
        compiler_params=_params(2),
        name="nsa",
    )(q, gates, kc, vc, ks, vs, kw, vw, pt, ov)


def _out_ln_kernel(*refs, n_parts):
    x_ref = refs[0]
    parts = refs[1:1 + n_parts]
    ws = refs[1 + n_parts:1 + 2 * n_parts]
    g_ref, b_ref, o_ref = refs[1 + 2 * n_parts:]
    z = ALPHA * x_ref[...]
    for a, w in zip(parts, ws):
        z = z + _dot(a[...], w[...])
    o_ref[...] = _layer_norm(z, g_ref[...], b_ref[...])


def _out_ln(xf, parts, ws, g, b):
    n = xf.shape[0]
    tm = ROW_TILE
    row = lambda i: (i, 0)
    full = lambda i: (0, 0)
    return pl.pallas_call(
        functools.partial(_out_ln_kernel, n_parts=len(parts)),
        grid=(n // tm,),
        in_specs=[pl.BlockSpec((tm, D_MODEL), row)]
                 + [pl.BlockSpec((tm, a.shape[1]), row) for a in parts]
                 + [pl.BlockSpec(w.shape, full) for w in ws]
                 + [pl.BlockSpec(g.shape, full), pl.BlockSpec(b.shape, full)],
        out_specs=pl.BlockSpec((tm, D_MODEL), row),
        out_shape=jax.ShapeDtypeStruct((n, D_MODEL), F32),
        compiler_params=_params(1),
        name="out_ln",
    )(xf, *parts, *ws, g, b)


def _matmul_kernel(a_ref, w_ref, o_ref):
    o_ref[...] = _dot(a_ref[...].astype(BF16), w_ref[...]).astype(o_ref.dtype)


def _matmul(a, w, out_dtype, tm):
    m, k = a.shape
    n = w.shape[1]
    return pl.pallas_call(
        _matmul_kernel,
        grid=(m // tm,),
        in_specs=[pl.BlockSpec((tm, k), lambda i: (i, 0)), pl.BlockSpec((k, n), lambda i: (0, 0))],
        out_specs=pl.BlockSpec((tm, n), lambda i: (i, 0)),
        out_shape=jax.ShapeDtypeStruct((m, n), out_dtype),
        compiler_params=_params(1),
        name="matmul",
    )(a, w)


def _xattn_kernel(x_ref, wq_ref, k_ref, v_ref, wo_ref, g_ref, b_ref, o_ref):
    x = x_ref[...]
    q = _dot(x.astype(BF16), wq_ref[...]).astype(BF16)
    outs = []
    for h in range(X_HEADS):
        sl = slice(h * X_HEAD_DIM, (h + 1) * X_HEAD_DIM)
        e, l = _softmax_rows(_dot_nt(q[:, sl], k_ref[:, sl]))
        outs.append(_dot((e / l).astype(BF16), v_ref[:, sl]))
    o = jnp.concatenate(outs, axis=1).astype(BF16)
    z = ALPHA * x + _dot(o, wo_ref[...])
    o_ref[...] = _layer_norm(z, g_ref[...], b_ref[...])


def _xattn(xf, kv, wq, wo, g, b, batch, seq):
    n = xf.shape[0]
    tm = ROW_TILE
    nt = seq // tm
    n_mem = kv.shape[1]
    row = lambda bi, i: (bi * nt + i, 0)
    full = lambda bi, i: (0, 0)
    return pl.pallas_call(
        _xattn_kernel,
        grid=(batch, nt),
        in_specs=[pl.BlockSpec((tm, D_MODEL), row), pl.BlockSpec(wq.shape, full),
                  pl.BlockSpec((None, n_mem, D_MODEL), lambda bi, i: (bi, 0, 0)),
                  pl.BlockSpec((None, n_mem, D_MODEL), lambda bi, i: (bi, 0, 1)),
                  pl.BlockSpec(wo.shape, full), pl.BlockSpec(g.shape, full), pl.BlockSpec(b.shape, full)],
        out_specs=pl.BlockSpec((tm, D_MODEL), row),
        out_shape=jax.ShapeDtypeStruct((n, D_MODEL), F32),
        compiler_params=_params(2),
        name="xattn",
    )(xf, wq, kv, kv, wo, g, b)


def _swiglu_kernel(x_ref, wu_ref, wd_ref, g_ref, b_ref, o_ref):
    x = x_ref[...]
    xb = x.astype(BF16)
    z = ALPHA * x
    c0 = 0
    for width in FFN_CHUNKS:
        h1 = _dot(xb, wu_ref[:, c0:c0 + width])
        h2 = _dot(xb, wu_ref[:, FFN_HIDDEN + c0:FFN_HIDDEN + c0 + width])
        act = (h1 * _sigmoid_tanh(h1) * h2).astype(BF16)
        z = z + _dot(act, wd_ref[c0:c0 + width, :])
        c0 += width
    o_ref[...] = _layer_norm(z, g_ref[...], b_ref[...])


def _swiglu(xf, w_up, w_down, g, b):
    n = xf.shape[0]
    tm = ROW_TILE
    row = lambda i: (i, 0)
    full = lambda i: (0, 0)
    return pl.pallas_call(
        _swiglu_kernel,
        grid=(n // tm,),
        in_specs=[pl.BlockSpec((tm, D_MODEL), row), _resident(w_up.shape, full), _resident(w_down.shape, full),
                  pl.BlockSpec(g.shape, full), pl.BlockSpec(b.shape, full)],
        out_specs=pl.BlockSpec((tm, D_MODEL), row),
        out_shape=jax.ShapeDtypeStruct((n, D_MODEL), F32),
        compiler_params=_params(1),
        name="swiglu",
    )(xf, w_up, w_down, g, b)


def _gelu_tanh(x):
    inner = x * (GELU_C0 + GELU_C1 * (x * x))
    return x * (0.5 * jnp.tanh(inner) + 0.5)


def _sigmoid_tanh(z):
    return 0.5 * jnp.tanh(0.5 * z) + 0.5


def _odd_kernel(x_ref, pos_ref, win_ref, cw_ref, cb_ref, wax_ref, ba_ref, bx_ref, lam_ref, wout_ref,
                g_ref, b_ref, o_ref, xbuf, gate_scr, a_scr, b_scr, h_scr):
    i = pl.program_id(1)
    tm = x_ref.shape[0]
    groups = tm // SUBLANES

    @pl.when(i == 0)
    def _():
        xbuf[0:CONV_HALO, :] = jnp.zeros((CONV_HALO, RNN_WIDTH), F32)
        h_scr[...] = jnp.zeros((SUBLANES, RNN_WIDTH), F32)

    x = x_ref[...]
    xb = x.astype(BF16)
    gate_scr[...] = _dot(xb, win_ref[:, :RNN_WIDTH])
    xbuf[CONV_HALO:CONV_HALO + tm, :] = _dot(xb, win_ref[:, RNN_WIDTH:])
    reset = pos_ref[...] == 0
    in_group = lax.broadcasted_iota(jnp.int32, (groups, SUBLANES, RNN_BLOCK_W), 1)
    lam = lam_ref[...]
    log_a_scale = -LRU_C * (jnp.maximum(-lam, 0.0) + jnp.log1p(jnp.exp(-jnp.abs(lam))))

    for h in range(RNN_BLOCKS):
        sl = slice(h * RNN_BLOCK_W, (h + 1) * RNN_BLOCK_W)
        xc = cb_ref[:, sl]
        for k in range(CONV_WIDTH):
            xc = xc + xbuf[pl.ds(CONV_HALO - (CONV_WIDTH - 1) + k, tm), sl] * cw_ref[k:k + 1, sl]
        ri = _dot(xc.astype(BF16), wax_ref[h])
        r = _sigmoid_tanh(ri[:, :RNN_BLOCK_W] + ba_ref[:, sl])
        ig = _sigmoid_tanh(ri[:, RNN_BLOCK_W:] + bx_ref[:, sl])
        log_a = r * log_a_scale[:, sl]
        a = jnp.where(reset, 0.0, jnp.exp(log_a))
        th = jnp.tanh(log_a)
        m2 = -2.0 * th / (1.0 - th)
        mult = jnp.where(reset, 1.0, jnp.where(m2 > 0.0, m2 * lax.rsqrt(m2), 0.0))
        b = mult * (ig * xc)
        a3 = a.reshape(groups, SUBLANES, RNN_BLOCK_W)
        b3 = b.reshape(groups, SUBLANES, RNN_BLOCK_W)
        for d in (1, 2, 4):
            a_prev = pltpu.roll(a3, d, 1)
            b_prev = pltpu.roll(b3, d, 1)
            ok = in_group >= d
            b3 = jnp.where(ok, a3 * b_prev + b3, b3)
            a3 = jnp.where(ok, a3 * a_prev, a3)
        a_scr[:, sl] = a3.reshape(tm, RNN_BLOCK_W)
        b_scr[:, sl] = b3.reshape(tm, RNN_BLOCK_W)
    xbuf[0:CONV_HALO, :] = xbuf[tm:tm + CONV_HALO, :]

    def body(j, h_prev):
        r0 = pl.multiple_of(j * SUBLANES, SUBLANES)
        hh = a_scr[pl.ds(r0, SUBLANES), :] * h_prev + b_scr[pl.ds(r0, SUBLANES), :]
        b_scr[pl.ds(r0, SUBLANES), :] = hh
        return jnp.broadcast_to(hh[SUBLANES - 1:SUBLANES, :], (SUBLANES, RNN_WIDTH))

    h_scr[...] = lax.fori_loop(0, groups, body, h_scr[...], unroll=4)

    y = (b_scr[...] * _gelu_tanh(gate_scr[...])).astype(BF16)
    z = ALPHA * x + _dot(y, wout_ref[...])
    o_ref[...] = _layer_norm(z, g_ref[...], b_ref[...])


def _odd_layer(xf, posi, w_in, cw, cb, wax, ba, bx, lam, w_out, g, b, batch, seq):
    n = xf.shape[0]
    tm = ROW_TILE
    nt = seq // tm
    row = lambda bi, i: (bi * nt + i, 0)
    full = lambda bi, i: (0, 0)
    consts = [w_in, cw, cb]
    rest = [ba, bx, lam, w_out, g, b]
    return pl.pallas_call(
        _odd_kernel,
        grid=(batch, nt),
        in_specs=[pl.BlockSpec((tm, D_MODEL), row), pl.BlockSpec((tm, 1), row)]
                 + [pl.BlockSpec(a.shape, full) for a in consts]
                 + [pl.BlockSpec(wax.shape, lambda bi, i: (0, 0, 0))]
                 + [pl.BlockSpec(a.shape, full) for a in rest],
        out_specs=pl.BlockSpec((tm, D_MODEL), row),
        out_shape=jax.ShapeDtypeStruct((n, D_MODEL), F32),
        scratch_shapes=[pltpu.VMEM((CONV_HALO + tm, RNN_WIDTH), F32),
                        pltpu.VMEM((tm, RNN_WIDTH), F32),
                        pltpu.VMEM((tm, RNN_WIDTH), F32),
                        pltpu.VMEM((tm, RNN_WIDTH), F32),
                        pltpu.VMEM((SUBLANES, RNN_WIDTH), F32)],
        compiler_params=_params(2),
        name="odd_layer",
    )(xf, posi, *consts, wax, *rest)


def _rope_rows():
    inv = ROPE_THETA ** (-jnp.arange(ROT_HALF, dtype=F32) * 2.0 / ROT_DIM)
    pad = jnp.zeros((NSA_HEAD_DIM - ROT_DIM,), F32)
    ones = jnp.ones((ROT_HALF,), F32)
    zeros = jnp.zeros((ROT_HALF,), F32)
    per_head = lambda parts: jnp.tile(jnp.concatenate(parts), LANES // NSA_HEAD_DIM)[None, :]
    return (per_head([inv, inv, pad]), per_head([-ones, zeros, pad]), per_head([zeros, ones, pad]))


def _even_mixer(xf, posf, positions, w_in, pool_w, pool_scale, cmp_pos_k, cmp_pos_v, cmp_wk, cmp_wv,
                w_out, ln_g, ln_b, batch, seq):
    assert seq // SEL_BLOCK == LANES and seq % SEL_KEY_TILE == 0 and seq >= WINDOW + Q_BLOCK
    hd, g, r = NSA_HEAD_DIM, NSA_KV_HEADS, NSA_GROUP
    c0 = POOL_WIDTH
    c1 = c0 + Q_WIDTH
    c2 = c1 + GATE_WIDTH
    wp = w_in[:, :c0].astype(BF16)
    wq = (w_in[:, c0:c1].reshape(D_MODEL, g, r, hd).transpose(0, 2, 1, 3).reshape(D_MODEL, Q_WIDTH)
          * (hd ** -0.5 * LOG2E)).astype(BF16)
    wg = jnp.pad(w_in[:, c1:c2], ((0, 0), (0, LANES - GATE_WIDTH))).astype(BF16)
    wkv = w_in[:, c2:].astype(BF16)
    rope_rows = _rope_rows()
    pool, q, kc_raw, vc_raw, ks, vs, kw, vw, gates = _proj_even(
        xf, posf, wp, wq, wkv, wg, pool_w.astype(BF16), pool_scale[None, :], rope_rows, batch, seq)

    nc = seq // CMP_STRIDE
    eye = jnp.eye(g, dtype=F32)

    def halves(w):
        w4 = jnp.einsum('lde,gh->lgdhe', w.reshape(CMP_BLOCK, hd, hd), eye)
        w4 = w4.reshape(CMP_BLOCK, g * hd, g * hd)
        return (w4[:CMP_STRIDE].reshape(CMP_STRIDE * g * hd, g * hd).astype(BF16),
                w4[CMP_STRIDE:].reshape(CMP_STRIDE * g * hd, g * hd).astype(BF16))

    def pos_halves(p):
        p2 = jnp.tile(p[:, None, :], (1, g, 1)).reshape(CMP_BLOCK, g * hd)
        return p2[:CMP_STRIDE].reshape(1, -1), p2[CMP_STRIDE:].reshape(1, -1)

    pk1, pk2 = pos_halves(cmp_pos_k)
    pv1, pv2 = pos_halves(cmp_pos_v)
    wk1, wk2 = halves(cmp_wk)
    wv1, wv2 = halves(cmp_wv)
    n_cmp = (seq - CMP_BLOCK) // CMP_STRIDE + 1
    posc = positions[:, CMP_BLOCK - 1::CMP_STRIDE][:, :n_cmp].astype(F32)
    posc = jnp.pad(posc, ((0, 0), (0, nc - n_cmp)))[:, :, None]
    kc, vc = _compress(kc_raw.reshape(batch, nc, CMP_STRIDE * g * hd),
                       vc_raw.reshape(batch, nc, CMP_STRIDE * g * hd), posc,
                       (pk1, pk2, pv1, pv2, wk1, wk2, wv1, wv2), rope_rows)

    n_sb = seq // SEL_BLOCK
    starts = np.arange(nc) * CMP_STRIDE
    jb = np.arange(n_sb)
    ov = ((starts[:, None] < (jb[None, :] + 1) * SEL_BLOCK)
          & (starts[:, None] + CMP_BLOCK > jb[None, :] * SEL_BLOCK) & (np.arange(nc)[:, None] < n_cmp))
    ov = jnp.asarray(ov, BF16)
    pt = jnp.asarray(np.where(np.arange(seq)[:, None] // SEL_BLOCK == jb[None, :], NEG, 0.0), BF16)

    seq3 = lambda a: a.reshape(batch, seq, KV_WIDTH)
    nsa = _nsa(q, gates, kc, vc, seq3(ks), seq3(vs), seq3(kw), seq3(vw), pt, ov, batch, seq)

    w_pool_out = w_out[:POOL_WIDTH].astype(BF16)
    w_nsa_out = (w_out[POOL_WIDTH:].reshape(g, r, hd, D_MODEL).transpose(1, 0, 2, 3)
                 .reshape(Q_WIDTH, D_MODEL).astype(BF16))
    return _out_ln(xf, [pool, nsa], [w_pool_out, w_nsa_out], ln_g, ln_b)


def _odd_mixer(xf, posi, w_in, conv_w, conv_b, wa, ba, wx, bx, lam, w_out, ln_g, ln_b, batch, seq):
    wax = jnp.concatenate([wa, wx], axis=2).astype(BF16)
    return _odd_layer(xf, posi, w_in.astype(BF16), conv_w, conv_b[None, :], wax, ba[None, :], bx[None, :],
                      lam[None, :], w_out.astype(BF16), ln_g, ln_b, batch, seq)


def kernel(x, mem, positions, e_w_in, e_pool_w, e_pool_scale, e_cmp_pos_k, e_cmp_pos_v, e_cmp_wk, e_cmp_wv, e_w_out, o_w_in, o_conv_w, o_conv_b, o_wa, o_ba, o_wx, o_bx, o_lambda, o_w_out, x_wq, x_wkv, x_wo, f_w_up, f_w_down, ln_g, ln_b):
    batch, seq, d = x.shape
    n = batch * seq
    assert d == D_MODEL and seq % ROW_TILE == 0
    xf = x.reshape(n, d)
    posi = positions.reshape(n, 1)
    posf = posi.astype(F32)
    memf = mem.reshape(batch * mem.shape[1], d)
    for layer in range(DEPTH):
        j = layer // 2
        lg = lambda k: ln_g[layer, k][None, :]
        lb = lambda k: ln_b[layer, k][None, :]
        if layer % 2 == 0:
            xf = _even_mixer(xf, posf, positions, e_w_in[j], e_pool_w[j], e_pool_scale[j], e_cmp_pos_k[j],
                             e_cmp_pos_v[j], e_cmp_wk[j], e_cmp_wv[j], e_w_out[j], lg(0), lb(0), batch, seq)
        else:
            xf = _odd_mixer(xf, posi, o_w_in[j], o_conv_w[j], o_conv_b[j], o_wa[j], o_ba[j], o_wx[j],
                            o_bx[j], o_lambda[j], o_w_out[j], lg(0), lb(0), batch, seq)
        kv = _matmul(memf, x_wkv[layer].astype(BF16), BF16, memf.shape[0]).reshape(batch, mem.shape[1], 2 * d)
        wq = (x_wq[layer] * (X_HEAD_DIM ** -0.5 * LOG2E)).astype(BF16)
        xf = _xattn(xf, kv, wq, x_wo[layer].astype(BF16), lg(1), lb(1), batch, seq)
        xf = _swiglu(xf, f_w_up[layer].astype(BF16), f_w_down[layer].astype(BF16), lg(2), lb(2))
    return xf.reshape(batch, seq, d)
```

```python
import functools

import numpy as np
import jax
import jax.numpy as jnp
from jax import lax
from jax.experimental import pallas as pl
from jax.experimental.pallas import tpu as pltpu

F32 = jnp.float32
BF16 = jnp.bfloat16

D_MODEL = 1024
DEPTH = 2
ALPHA = (2.0 * DEPTH) ** 0.25
LN_EPS = 1e-5
NEG = -1e30
POOL_WIDTH = D_MODEL // 2
POOL_WINDOWS = (2, 4, 8, 16)
POOL_GROUP = POOL_WIDTH // len(POOL_WINDOWS)
POOL_HALO = 16
NSA_HEADS = 8
NSA_KV_HEADS = 2
NSA_HEAD_DIM = 64
NSA_GROUP = NSA_HEADS // NSA_KV_HEADS
CMP_BLOCK = 32
CMP_STRIDE = 16
SEL_BLOCK = 64
SEL_COUNT = 16
WINDOW = 512
Q_BLOCK = 256
N_BRANCH = 3
N_FORCED = 3
LOG2E = 1.4426950408889634
ROPE_THETA = 500000.0
ROT_DIM = NSA_HEAD_DIM // 4
ROT_HALF = ROT_DIM // 2
Q_WIDTH = NSA_HEADS * NSA_HEAD_DIM
KV_WIDTH = NSA_KV_HEADS * NSA_HEAD_DIM
GATE_WIDTH = NSA_HEADS * N_BRANCH
RNN_WIDTH = 1280
RNN_BLOCKS = 10
RNN_BLOCK_W = RNN_WIDTH // RNN_BLOCKS
CONV_WIDTH = 4
CONV_HALO = 8
LRU_C = 8.0
X_HEADS = 4
X_HEAD_DIM = D_MODEL // X_HEADS
FFN_HIDDEN = 2816
GELU_C0 = 0.7978845608028654
GELU_C1 = GELU_C0 * 0.044715

LANES = 128
SUBLANES = 8
BF16_ROWS = 16
VMEM_LIMIT = 56 * 1024 * 1024
ROW_TILE = 512
SEL_KEY_TILE = 512
FFN_CHUNKS = (1024, 1024, 768)


def _params(n_axes, vmem=VMEM_LIMIT):
    return pltpu.CompilerParams(dimension_semantics=("arbitrary",) * n_axes,
                                vmem_limit_bytes=vmem)


def _resident(shape, index_map):
    return pl.BlockSpec(shape, index_map, pipeline_mode=pl.Buffered(1))


def _dot(a, b):
    return jnp.dot(a, b, preferred_element_type=F32)


def _dot_nt(a, b):
    return lax.dot_general(a, b, (((1,), (1,)), ((), ())), preferred_element_type=F32)


def _rep_rows(a, k):
    return jnp.concatenate([a] * k, axis=0)


def _rep_lanes(a, k):
    return jnp.concatenate([a] * k, axis=1) if k > 1 else a


def _layer_norm(z, g, b):
    mu = jnp.mean(z, axis=-1, keepdims=True)
    d = z - mu
    var = jnp.mean(d * d, axis=-1, keepdims=True)
    return d * lax.rsqrt(var + LN_EPS) * g + b


def _rope(v, cos, sin):
    k = v.shape[1] // LANES
    up = pltpu.roll(v, v.shape[1] - ROT_HALF, 1)
    dn = pltpu.roll(v, ROT_HALF, 1)
    lane = lax.broadcasted_iota(jnp.int32, v.shape, 1)
    partner = jnp.where((lane & (NSA_HEAD_DIM - 1)) < ROT_HALF, up, dn)
    return v * _rep_lanes(cos, k) + partner * _rep_lanes(sin, k)


def _rope_angle_kernel(pos_ref, inv_ref, cos_o, sin_o):
    ang = inv_ref[...] * pos_ref[...]
    cos_o[...] = jnp.cos(ang)
    sin_o[...] = jnp.sin(ang)


def _rope_tables(pos_row):
    n = pos_row.shape[1]
    inv = (ROPE_THETA ** (-jnp.arange(ROT_HALF, dtype=F32) * 2.0 / ROT_DIM))[:, None]
    out = jax.ShapeDtypeStruct((ROT_HALF, n), F32)
    cos_t, sin_t = pl.pallas_call(
        _rope_angle_kernel,
        out_shape=[out, out],
        name="rope_angles",
    )(pos_row, inv)
    c, s = cos_t.T, sin_t.T
    rest = NSA_HEAD_DIM - ROT_DIM
    cos = jnp.concatenate([c, c, jnp.ones((n, rest), F32)], axis=1)
    sin = jnp.concatenate([-s, s, jnp.zeros((n, rest), F32)], axis=1)
    reps = LANES // NSA_HEAD_DIM
    return jnp.tile(cos, (1, reps)), jnp.tile(sin, (1, reps))


def _proj_even_kernel(x_ref, cos_ref, sin_ref, wp_ref, wq_ref, wkv_ref, wg_ref, poolw_ref, pscale_ref,
                      pool_o, q_o, kc_o, vc_o, ks_o, vs_o, kw_o, vw_o, gate_o, pbuf):
    i = pl.program_id(1)
    tm = x_ref.shape[0]
    xb = x_ref[...].astype(BF16)
    cos = cos_ref[...]
    sin = sin_ref[...]

    q_o[...] = _rope(_dot(xb, wq_ref[...]), cos, sin).astype(BF16)
    kv = _dot(xb, wkv_ref[...])
    kc_o[...] = kv[:, 0 * LANES:1 * LANES]
    vc_o[...] = kv[:, 1 * LANES:2 * LANES]
    ks_o[...] = _rope(kv[:, 2 * LANES:3 * LANES], cos, sin).astype(BF16)
    vs_o[...] = kv[:, 3 * LANES:4 * LANES].astype(BF16)
    kw_o[...] = _rope(kv[:, 4 * LANES:5 * LANES], cos, sin).astype(BF16)
    vw_o[...] = kv[:, 5 * LANES:6 * LANES].astype(BF16)
    gate_o[...] = _sigmoid_tanh(_dot(xb, wg_ref[...]))

    @pl.when(i == 0)
    def _():
        pbuf[0:POOL_HALO, :] = jnp.zeros((POOL_HALO, POOL_WIDTH), F32)

    p = _dot(xb, wp_ref[...])
    pbuf[POOL_HALO:POOL_HALO + tm, :] = p
    t1 = i * tm + lax.broadcasted_iota(jnp.int32, (tm, 1), 0) + 1
    for g, w in enumerate(POOL_WINDOWS):
        sl = slice(g * POOL_GROUP, (g + 1) * POOL_GROUP)
        u = p[:, sl]
        tot = u
        for j in range(1, w):
            tot = tot + pbuf[pl.ds(POOL_HALO - j, tm), sl]
        cnt = jnp.minimum(t1, w).astype(F32)
        pooled = tot / cnt - u
        mixed = _dot(pooled.astype(BF16), poolw_ref[g]) * pscale_ref[:, sl]
        pool_o[:, sl] = mixed.astype(BF16)
    pbuf[0:POOL_HALO, :] = pbuf[tm:tm + POOL_HALO, :]


def _proj_even(xf, cos, sin, wp, wq, wkv, wg, poolw, pscale, batch, seq):
    n = xf.shape[0]
    tm = ROW_TILE
    nt = seq // tm
    row = lambda b, i: (b * nt + i, 0)
    full = lambda b, i: (0, 0)
    outs = [
        jax.ShapeDtypeStruct((n, POOL_WIDTH), BF16),
        jax.ShapeDtypeStruct((n, Q_WIDTH), BF16),
        jax.ShapeDtypeStruct((n, KV_WIDTH), F32),
        jax.ShapeDtypeStruct((n, KV_WIDTH), F32),
        jax.ShapeDtypeStruct((n, KV_WIDTH), BF16),
        jax.ShapeDtypeStruct((n, KV_WIDTH), BF16),
        jax.ShapeDtypeStruct((n, KV_WIDTH), BF16),
        jax.ShapeDtypeStruct((n, KV_WIDTH), BF16),
        jax.ShapeDtypeStruct((n, LANES), F32),
    ]
    return pl.pallas_call(
        _proj_even_kernel,
        grid=(batch, nt),
        in_specs=[
            pl.BlockSpec((tm, D_MODEL), row),
            pl.BlockSpec((tm, LANES), row),
            pl.BlockSpec((tm, LANES), row),
            pl.BlockSpec(wp.shape, full),
            pl.BlockSpec(wq.shape, full),
            pl.BlockSpec(wkv.shape, full),
            pl.BlockSpec(wg.shape, full),
            pl.BlockSpec(poolw.shape, lambda b, i: (0, 0, 0)),
            pl.BlockSpec(pscale.shape, full),
        ],
        out_specs=[pl.BlockSpec((tm, o.shape[1]), row) for o in outs],
        out_shape=outs,
        scratch_shapes=[pltpu.VMEM((POOL_HALO + tm, POOL_WIDTH), F32)],
        compiler_params=_params(2),
        name="proj_even",
    )(xf, cos, sin, wp, wq, wkv, wg, poolw, pscale)


def _compress_kernel(rk_ref, rv_ref, cos_ref, sin_ref, pk1_ref, pk2_ref, pv1_ref, pv2_ref,
                     wk1_ref, wk2_ref, wv1_ref, wv2_ref, kc_o, vc_o):
    nc = kc_o.shape[0]

    def compress(raw_ref, p1, p2, w1, w2):
        r = jnp.concatenate([raw_ref[pl.ds(j, nc, stride=CMP_STRIDE), :] for j in range(CMP_STRIDE)], axis=1)
        a = _dot((r + p1[...]).astype(BF16), w1[...])
        b = _dot((r + p2[...]).astype(BF16), w2[...])
        return a + pltpu.roll(b, nc - 1, 0)

    kc = compress(rk_ref, pk1_ref, pk2_ref, wk1_ref, wk2_ref)
    kc_o[...] = _rope(kc, cos_ref[...], sin_ref[...]).astype(BF16)
    vc_o[...] = compress(rv_ref, pv1_ref, pv2_ref, wv1_ref, wv2_ref).astype(BF16)


def _compress(rk, rv, cos, sin, consts):
    batch, seq, width = rk.shape
    nc = seq // CMP_STRIDE
    blk = lambda b: (b, 0, 0)
    full = lambda b: (0, 0)
    out = jax.ShapeDtypeStruct((batch, nc, KV_WIDTH), BF16)
    return pl.pallas_call(
        _compress_kernel,
        grid=(batch,),
        in_specs=[pl.BlockSpec((None, seq, width), blk), pl.BlockSpec((None, seq, width), blk),
                  pl.BlockSpec((None, nc, LANES), blk), pl.BlockSpec((None, nc, LANES), blk)]
                 + [pl.BlockSpec(a.shape, full) for a in consts],
        out_specs=[pl.BlockSpec((None, nc, KV_WIDTH), blk)] * 2,
        out_shape=[out, out],
        compiler_params=_params(1),
        name="compress",
    )(rk, rv, cos, sin, *consts)


def _softmax_rows(s):
    m = jnp.max(s, axis=1, keepdims=True)
    e = jnp.exp2(s - m)
    return e, jnp.sum(e, axis=1, keepdims=True)


def _dot_exact01(x, m01):
    hi = x.astype(BF16)
    r1 = x - hi.astype(F32)
    mid = r1.astype(BF16)
    lo = (r1 - mid.astype(F32)).astype(BF16)
    return _dot(hi, m01) + _dot(mid, m01) + _dot(lo, m01)


def _unselected_blocks(imp, q0, n_pick):
    nq, nb = imp.shape
    col = lax.broadcasted_iota(jnp.int32, (nq, nb), 1)
    cur = lax.shift_right_arithmetic(q0 + lax.broadcasted_iota(jnp.int32, (nq, nb), 0), 6)
    forced = (col == 0) | (col == cur) | (col == cur - 1)
    val = jnp.where(forced, -jnp.inf, jnp.where(col > cur, -1.0, imp))
    vt = val.T
    blk = lax.broadcasted_iota(jnp.int32, (nb, nq), 0).astype(F32)
    unsel = jnp.where(forced, 0.0, 1.0).T
    for _ in range(n_pick - N_FORCED):
        m = jnp.max(vt, axis=0, keepdims=True)
        first = jnp.min(jnp.where(vt == m, blk, float(nb)), axis=0, keepdims=True)
        hit = blk == first
        unsel = jnp.where(hit, 0.0, unsel)
        vt = jnp.where(hit, -jnp.inf, vt)
    return unsel.T


def _nsa_kernel(q_ref, gate_ref, kc_ref, vc_ref, ks_ref, vs_ref, kw_ref, vw_ref, pt_ref, ov_ref,
                o_ref, m_scr, l_scr, acc_scr, lhs_scr, oc_scr, ow_scr, s0_scr, s1_scr, w_scr, gx_scr, *, n_pick):
    qb = pl.program_id(1)
    q0 = qb * Q_BLOCK
    tk = SEL_KEY_TILE
    ncmp = kc_ref.shape[0]
    wk = WINDOW + Q_BLOCK
    rows = NSA_GROUP * Q_BLOCK
    lane = lax.broadcasted_iota(jnp.int32, (Q_BLOCK, LANES), 1)

    def trow(width):
        return q0 + lax.broadcasted_iota(jnp.int32, (Q_BLOCK, width), 0)

    def kcol(width):
        return lax.broadcasted_iota(jnp.int32, (Q_BLOCK, width), 1)

    bias_c = jnp.where(kcol(ncmp) * CMP_STRIDE + (CMP_BLOCK - 1) <= trow(ncmp), 0.0, NEG)
    has_cmp = (trow(1) >= CMP_BLOCK - 1).astype(F32)
    wstart = pl.multiple_of(jnp.maximum(q0 - WINDOW, 0), Q_BLOCK)
    kpos_w = wstart + kcol(wk)
    bias_w = jnp.where(kpos_w <= trow(wk), jnp.where(kpos_w > trow(wk) - WINDOW, 0.0, NEG), NEG)

    for g in range(NSA_KV_HEADS):
        mine = (lane >= NSA_HEAD_DIM) if g else (lane < NSA_HEAD_DIM)
        lhs_scr[g, :, 0:LANES] = jnp.concatenate(
            [jnp.where(mine, q_ref[:, r * LANES:(r + 1) * LANES], jnp.zeros((), BF16))
             for r in range(NSA_GROUP)], axis=0)
        m_scr[g] = jnp.full((rows, LANES), -jnp.inf, F32)
        l_scr[g] = jnp.zeros((rows, LANES), F32)
        acc_scr[g] = jnp.zeros((rows, LANES), F32)

    def cmp_scores(g):
        s0_scr[g] = _dot_nt(lhs_scr[g, :, 0:LANES], kc_ref[...])

    def cmp_attend(g):
        e, l = _softmax_rows(s0_scr[g] + _rep_rows(bias_c, NSA_GROUP))
        p = e * (_rep_rows(has_cmp, NSA_GROUP) / l)
        oc_scr[g] = _dot(p.astype(BF16), vc_ref[...])
        psum = p[0:Q_BLOCK]
        for r in range(1, NSA_GROUP):
            psum = psum + p[r * Q_BLOCK:(r + 1) * Q_BLOCK]
        return _dot_exact01(psum, ov_ref[...])

    def select(g, imp):
        unsel = _unselected_blocks(imp, q0, n_pick)
        lhs_scr[g, :, LANES:2 * LANES] = _rep_rows(unsel.astype(BF16), NSA_GROUP)

    def win_scores(g):
        w_scr[g] = _dot_nt(lhs_scr[g, :, 0:LANES], kw_ref[pl.ds(wstart, wk), :])

    def win_attend(g):
        e, l = _softmax_rows(w_scr[g] + _rep_rows(bias_w, NSA_GROUP))
        ow_scr[g] = _dot(e.astype(BF16), vw_ref[pl.ds(wstart, wk), :]) / l

    def expand_gates():
        low = lane < NSA_HEAD_DIM
        for r in range(NSA_GROUP):
            for br in range(N_BRANCH):
                c_lo = r * N_BRANCH + br
                c_hi = (NSA_GROUP + r) * N_BRANCH + br
                gx_scr[c_lo] = jnp.where(low, gate_ref[:, c_lo:c_lo + 1], gate_ref[:, c_hi:c_hi + 1])

    cmp_scores(0)
    cmp_scores(1)
    expand_gates()
    win_scores(0)
    imp0 = cmp_attend(0)
    win_scores(1)
    imp1 = cmp_attend(1)
    select(0, imp0)
    win_attend(0)
    select(1, imp1)
    win_attend(1)

    def scores(kt, buf, g):
        k0 = pl.multiple_of(kt * tk, tk)
        rhs = jnp.concatenate([ks_ref[pl.ds(k0, tk), :], pt_ref[pl.ds(k0, tk), :]], axis=1)
        buf[g] = _dot_nt(lhs_scr[g], rhs)

    def consume(kt, buf, g, causal):
        k0 = pl.multiple_of(kt * tk, tk)
        s = buf[g]
        if causal:
            s = s + _rep_rows(jnp.where(k0 + kcol(tk) <= trow(tk), 0.0, NEG), NSA_GROUP)
        m_prev = m_scr[g]
        m_new = jnp.maximum(m_prev, jnp.max(s, axis=1, keepdims=True))
        alpha = jnp.exp2(m_prev - m_new)
        pe = jnp.exp2(s - _rep_lanes(m_new, tk // LANES))
        part = pe[:, 0:LANES]
        for c in range(1, tk // LANES):
            part = part + pe[:, c * LANES:(c + 1) * LANES]
        l_scr[g] = alpha * l_scr[g] + part
        acc_scr[g] = alpha * acc_scr[g] + _dot(pe.astype(BF16), vs_ref[pl.ds(k0, tk), :])
        m_scr[g] = m_new

    def step(nxt, nxt_buf, cur, cur_buf, causal=False):
        for g in range(NSA_KV_HEADS):
            if nxt is not None:
                scores(nxt, nxt_buf, g)
        for g in range(NSA_KV_HEADS):
            consume(cur, cur_buf, g, causal)

    n_full = q0 // tk
    for g in range(NSA_KV_HEADS):
        scores(0, s0_scr, g)

    def pair(j, carry):
        step(2 * j + 1, s1_scr, 2 * j, s0_scr)
        step(2 * j + 2, s0_scr, 2 * j + 1, s1_scr)
        return carry

    lax.fori_loop(0, n_full // 2, pair, 0)

    @pl.when(n_full % 2 == 0)
    def _():
        step(None, None, n_full, s0_scr, causal=True)

    @pl.when(n_full % 2 == 1)
    def _():
        step(n_full, s1_scr, n_full - 1, s0_scr)
        step(None, None, n_full, s1_scr, causal=True)

    low = lane < NSA_HEAD_DIM
    for r in range(NSA_GROUP):
        rs = slice(r * Q_BLOCK, (r + 1) * Q_BLOCK)
        merged = lambda scr: jnp.where(low, scr[0, rs, :], scr[1, rs, :])
        l_sel = jnp.where(low, jnp.sum(l_scr[0, rs, :], axis=1, keepdims=True),
                          jnp.sum(l_scr[1, rs, :], axis=1, keepdims=True))
        out = (gx_scr[r * N_BRANCH] * merged(oc_scr)
               + gx_scr[r * N_BRANCH + 1] * (merged(acc_scr) / l_sel)
               + gx_scr[r * N_BRANCH + 2] * merged(ow_scr))
        o_ref[:, r * LANES:(r + 1) * LANES] = out.astype(BF16)


def _nsa(q, gates, kc, vc, ks, vs, kw, vw, pt, ov, batch, seq):
    n = q.shape[0]
    nqb = seq // Q_BLOCK
    rows = NSA_GROUP * Q_BLOCK
    row = lambda b, i: (b * nqb + i, 0)
    per_b = lambda b, i: (b, 0, 0)
    full = lambda b, i: (0, 0)
    seq_spec = _resident((None, seq, KV_WIDTH), per_b)
    cmp_spec = _resident((None, kc.shape[1], KV_WIDTH), per_b)
    n_pick = min(SEL_COUNT, seq // SEL_BLOCK)
    return pl.pallas_call(
        functools.partial(_nsa_kernel, n_pick=n_pick),
        grid=(batch, nqb),
        in_specs=[pl.BlockSpec((Q_BLOCK, Q_WIDTH), row), pl.BlockSpec((Q_BLOCK, LANES), row),
                  cmp_spec, cmp_spec, seq_spec, seq_spec, seq_spec, seq_spec,
                  _resident(pt.shape, full), _resident(ov.shape, full)],
        out_specs=pl.BlockSpec((Q_BLOCK, Q_WIDTH), row),
        out_shape=jax.ShapeDtypeStruct((n, Q_WIDTH), BF16),
        scratch_shapes=[pltpu.VMEM((NSA_KV_HEADS, rows, LANES), F32),
                        pltpu.VMEM((NSA_KV_HEADS, rows, LANES), F32),
                        pltpu.VMEM((NSA_KV_HEADS, rows, LANES), F32),
                        pltpu.VMEM((NSA_KV_HEADS, rows, 2 * LANES), BF16),
                        pltpu.VMEM((NSA_KV_HEADS, rows, LANES), F32),
                        pltpu.VMEM((NSA_KV_HEADS, rows, LANES), F32),
                        pltpu.VMEM((NSA_KV_HEADS, rows, SEL_KEY_TILE), F32),
                        pltpu.VMEM((NSA_KV_HEADS, rows, SEL_KEY_TILE), F32),
                        pltpu.VMEM((NSA_KV_HEADS, rows, WINDOW + Q_BLOCK), F32),
                        pltpu.VMEM((NSA_GROUP * N_BRANCH, Q_BLOCK, LANES), F32)],
        compiler_params=_params(2),
        name="nsa",
    )(q, gates, kc, vc, ks, vs, kw, vw, pt, ov)


def _out_ln_kernel(*refs, n_parts):
    x_ref = refs[0]
    parts = refs[1:1 + n_parts]
    ws = refs[1 + n_parts:1 + 2 * n_parts]
    g_ref, b_ref, o_ref = refs[1 + 2 * n_parts:]
    z = ALPHA * x_ref[...]
    for a, w in zip(parts, ws):
        z = z + _dot(a[...], w[...])
    o_ref[...] = _layer_norm(z, g_ref[...], b_ref[...])


def _out_ln(xf, parts, ws, g, b):
    n = xf.shape[0]
    tm = ROW_TILE
    row = lambda i: (i, 0)
    full = lambda i: (0, 0)
    return pl.pallas_call(
        functools.partial(_out_ln_kernel, n_parts=len(parts)),
        grid=(n // tm,),
        in_specs=[pl.BlockSpec((tm, D_MODEL), row)]
                 + [pl.BlockSpec((tm, a.shape[1]), row) for a in parts]
                 + [pl.BlockSpec(w.shape, full) for w in ws]
                 + [pl.BlockSpec(g.shape, full), pl.BlockSpec(b.shape, full)],
        out_specs=pl.BlockSpec((tm, D_MODEL), row),
        out_shape=jax.ShapeDtypeStruct((n, D_MODEL), F32),
        compiler_params=_params(1),
        name="out_ln",
    )(xf, *parts, *ws, g, b)


def _matmul_kernel(a_ref, w_ref, o_ref):
    o_ref[...] = _dot(a_ref[...].astype(BF16), w_ref[...]).astype(o_ref.dtype)


def _matmul(a, w, out_dtype, tm):
    m, k = a.shape
    n = w.shape[1]
    return pl.pallas_call(
        _matmul_kernel,
        grid=(m // tm,),
        in_specs=[pl.BlockSpec((tm, k), lambda i: (i, 0)), pl.BlockSpec((k, n), lambda i: (0, 0))],
        out_specs=pl.BlockSpec((tm, n), lambda i: (i, 0)),
        out_shape=jax.ShapeDtypeStruct((m, n), out_dtype),
        compiler_params=_params(1),
        name="matmul",
    )(a, w)


def _xattn_kernel(x_ref, wq_ref, k_ref, v_ref, wo_ref, g_ref, b_ref, o_ref):
    half = x_ref.shape[0] // 2
    parts = (slice(0, half), slice(half, 2 * half))

    def project(rs):
        return _dot(x_ref[rs, :].astype(BF16), wq_ref[...]).astype(BF16)

    def attend(q):
        outs = []
        for h in range(X_HEADS):
            sl = slice(h * X_HEAD_DIM, (h + 1) * X_HEAD_DIM)
            e, l = _softmax_rows(_dot_nt(q[:, sl], k_ref[:, sl]))
            outs.append(_dot((e / l).astype(BF16), v_ref[:, sl]))
        return jnp.concatenate(outs, axis=1).astype(BF16)

    def finish(rs, o):
        z = ALPHA * x_ref[rs, :] + _dot(o, wo_ref[...])
        o_ref[rs, :] = _layer_norm(z, g_ref[...], b_ref[...])

    qa = project(parts[0])
    qb = project(parts[1])
    oa = attend(qa)
    ob = attend(qb)
    finish(parts[0], oa)
    finish(parts[1], ob)


def _xattn(xf, kv, wq, wo, g, b, batch, seq):
    n = xf.shape[0]
    tm = 2 * ROW_TILE
    nt = seq // tm
    n_mem = kv.shape[1]
    row = lambda bi, i: (bi * nt + i, 0)
    full = lambda bi, i: (0, 0)
    return pl.pallas_call(
        _xattn_kernel,
        grid=(batch, nt),
        in_specs=[pl.BlockSpec((tm, D_MODEL), row), pl.BlockSpec(wq.shape, full),
                  pl.BlockSpec((None, n_mem, D_MODEL), lambda bi, i: (bi, 0, 0)),
                  pl.BlockSpec((None, n_mem, D_MODEL), lambda bi, i: (bi, 0, 1)),
                  pl.BlockSpec(wo.shape, full), pl.BlockSpec(g.shape, full), pl.BlockSpec(b.shape, full)],
        out_specs=pl.BlockSpec((tm, D_MODEL), row),
        out_shape=jax.ShapeDtypeStruct((n, D_MODEL), F32),
        compiler_params=_params(2),
        name="xattn",
    )(xf, wq, kv, kv, wo, g, b)


def _swiglu_kernel(x_ref, wu_ref, wd_ref, g_ref, b_ref, o_ref):
    x = x_ref[...]
    xb = x.astype(BF16)
    z = ALPHA * x
    c0 = 0
    for width in FFN_CHUNKS:
        h1 = _dot(xb, wu_ref[:, c0:c0 + width])
        h2 = _dot(xb, wu_ref[:, FFN_HIDDEN + c0:FFN_HIDDEN + c0 + width])
        act = (h1 * _sigmoid_tanh(h1) * h2).astype(BF16)
        z = z + _dot(act, wd_ref[c0:c0 + width, :])
        c0 += width
    o_ref[...] = _layer_norm(z, g_ref[...], b_ref[...])


def _swiglu(xf, w_up, w_down, g, b):
    n = xf.shape[0]
    tm = ROW_TILE
    row = lambda i: (i, 0)
    full = lambda i: (0, 0)
    return pl.pallas_call(
        _swiglu_kernel,
        grid=(n // tm,),
        in_specs=[pl.BlockSpec((tm, D_MODEL), row), _resident(w_up.shape, full), _resident(w_down.shape, full),
                  pl.BlockSpec(g.shape, full), pl.BlockSpec(b.shape, full)],
        out_specs=pl.BlockSpec((tm, D_MODEL), row),
        out_shape=jax.ShapeDtypeStruct((n, D_MODEL), F32),
        compiler_params=_params(1),
        name="swiglu",
    )(xf, w_up, w_down, g, b)


def _gelu_tanh(x):
    inner = x * (GELU_C0 + GELU_C1 * (x * x))
    return x * (0.5 * jnp.tanh(inner) + 0.5)


def _sigmoid_tanh(z):
    return 0.5 * jnp.tanh(0.5 * z) + 0.5


def _odd_kernel(x_ref, pos_ref, win_ref, cw_ref, cb_ref, wax_ref, ba_ref, bx_ref, lam_ref, wout_ref,
                g_ref, b_ref, o_ref, xbuf, gate_scr, a_scr, b_scr, h_scr):
    i = pl.program_id(1)
    tm = x_ref.shape[0]
    half = tm // 2
    groups = half // SUBLANES

    @pl.when(i == 0)
    def _():
        xbuf[0:CONV_HALO, :] = jnp.zeros((CONV_HALO, RNN_WIDTH), F32)
        h_scr[...] = jnp.zeros((SUBLANES, RNN_WIDTH), F32)

    in_group = lax.broadcasted_iota(jnp.int32, (groups, SUBLANES, RNN_BLOCK_W), 1)
    lam = lam_ref[...]
    log_a_scale = -LRU_C * (jnp.maximum(-lam, 0.0) + jnp.log1p(jnp.exp(-jnp.abs(lam))))

    def in_proj(h0):
        xb = x_ref[h0:h0 + half, :].astype(BF16)
        gate_scr[h0:h0 + half, :] = _dot(xb, win_ref[:, :RNN_WIDTH])
        xbuf[CONV_HALO + h0:CONV_HALO + h0 + half, :] = _dot(xb, win_ref[:, RNN_WIDTH:])

    def gates(h0):
        reset = pos_ref[h0:h0 + half, :] == 0
        for h in range(RNN_BLOCKS):
            sl = slice(h * RNN_BLOCK_W, (h + 1) * RNN_BLOCK_W)
            xc = cb_ref[:, sl]
            for k in range(CONV_WIDTH):
                xc = xc + xbuf[pl.ds(CONV_HALO - (CONV_WIDTH - 1) + k + h0, half), sl] * cw_ref[k:k + 1, sl]
            ri = _dot(xc.astype(BF16), wax_ref[h])
            r = _sigmoid_tanh(ri[:, :RNN_BLOCK_W] + ba_ref[:, sl])
            ig = _sigmoid_tanh(ri[:, RNN_BLOCK_W:] + bx_ref[:, sl])
            log_a = r * log_a_scale[:, sl]
            a = jnp.where(reset, 0.0, jnp.exp(log_a))
            th = jnp.tanh(log_a)
            m2 = -2.0 * th / (1.0 - th)
            mult = jnp.where(reset, 1.0, jnp.where(m2 > 0.0, m2 * lax.rsqrt(m2), 0.0))
            b = mult * (ig * xc)
            a3 = a.reshape(groups, SUBLANES, RNN_BLOCK_W)
            b3 = b.reshape(groups, SUBLANES, RNN_BLOCK_W)
            for d in (1, 2, 4):
                a_prev = pltpu.roll(a3, d, 1)
                b_prev = pltpu.roll(b3, d, 1)
                ok = in_group >= d
                b3 = jnp.where(ok, a3 * b_prev + b3, b3)
                a3 = jnp.where(ok, a3 * a_prev, a3)
            a_scr[h0:h0 + half, sl] = a3.reshape(half, RNN_BLOCK_W)
            b_scr[h0:h0 + half, sl] = b3.reshape(half, RNN_BLOCK_W)

    def recur(h0, h_prev):
        for j in range(groups):
            rs = slice(h0 + j * SUBLANES, h0 + (j + 1) * SUBLANES)
            hh = a_scr[rs, :] * h_prev + b_scr[rs, :]
            b_scr[rs, :] = hh
            h_prev = jnp.broadcast_to(hh[SUBLANES - 1:SUBLANES, :], (SUBLANES, RNN_WIDTH))
        return h_prev

    def out_proj(h0):
        rs = slice(h0, h0 + half)
        y = (b_scr[rs, :] * _gelu_tanh(gate_scr[rs, :])).astype(BF16)
        z = ALPHA * x_ref[rs, :] + _dot(y, wout_ref[...])
        o_ref[rs, :] = _layer_norm(z, g_ref[...], b_ref[...])

    in_proj(0)
    in_proj(half)
    gates(0)
    h_mid = recur(0, h_scr[...])
    gates(half)
    out_proj(0)
    h_scr[...] = recur(half, h_mid)
    out_proj(half)
    xbuf[0:CONV_HALO, :] = xbuf[tm:tm + CONV_HALO, :]


def _odd_layer(xf, posi, w_in, cw, cb, wax, ba, bx, lam, w_out, g, b, batch, seq):
    n = xf.shape[0]
    tm = ROW_TILE
    nt = seq // tm
    row = lambda bi, i: (bi * nt + i, 0)
    full = lambda bi, i: (0, 0)
    consts = [w_in, cw, cb]
    rest = [ba, bx, lam, w_out, g, b]
    return pl.pallas_call(
        _odd_kernel,
        grid=(batch, nt),
        in_specs=[pl.BlockSpec((tm, D_MODEL), row), pl.BlockSpec((tm, 1), row)]
                 + [pl.BlockSpec(a.shape, full) for a in consts]
                 + [pl.BlockSpec(wax.shape, lambda bi, i: (0, 0, 0))]
                 + [pl.BlockSpec(a.shape, full) for a in rest],
        out_specs=pl.BlockSpec((tm, D_MODEL), row),
        out_shape=jax.ShapeDtypeStruct((n, D_MODEL), F32),
        scratch_shapes=[pltpu.VMEM((CONV_HALO + tm, RNN_WIDTH), F32),
                        pltpu.VMEM((tm, RNN_WIDTH), F32),
                        pltpu.VMEM((tm, RNN_WIDTH), F32),
                        pltpu.VMEM((tm, RNN_WIDTH), F32),
                        pltpu.VMEM((SUBLANES, RNN_WIDTH), F32)],
        compiler_params=_params(2),
        name="odd_layer",
    )(xf, posi, *consts, wax, *rest)


def _even_mixer(xf, positions, w_in, pool_w, pool_scale, cmp_pos_k, cmp_pos_v, cmp_wk, cmp_wv,
                w_out, ln_g, ln_b, batch, seq):
    assert seq // SEL_BLOCK == LANES and seq % SEL_KEY_TILE == 0 and seq >= WINDOW + Q_BLOCK
    hd, g, r = NSA_HEAD_DIM, NSA_KV_HEADS, NSA_GROUP
    c0 = POOL_WIDTH
    c1 = c0 + Q_WIDTH
    c2 = c1 + GATE_WIDTH
    wp = w_in[:, :c0].astype(BF16)
    wq = (w_in[:, c0:c1].reshape(D_MODEL, g, r, hd).transpose(0, 2, 1, 3).reshape(D_MODEL, Q_WIDTH)
          * (hd ** -0.5 * LOG2E)).astype(BF16)
    wg = jnp.pad(w_in[:, c1:c2], ((0, 0), (0, LANES - GATE_WIDTH))).astype(BF16)
    wkv = w_in[:, c2:].astype(BF16)
    n = batch * seq
    nc = seq // CMP_STRIDE
    n_cmp = (seq - CMP_BLOCK) // CMP_STRIDE + 1
    posc = jnp.pad(positions[:, CMP_BLOCK - 1::CMP_STRIDE][:, :n_cmp], ((0, 0), (0, nc - n_cmp)))
    pos_all = jnp.concatenate([positions.reshape(1, n), posc.reshape(1, batch * nc)], axis=1).astype(F32)
    cos_all, sin_all = _rope_tables(pos_all)
    pool, q, kc_raw, vc_raw, ks, vs, kw, vw, gates = _proj_even(
        xf, cos_all[:n], sin_all[:n], wp, wq, wkv, wg, pool_w.astype(BF16), pool_scale[None, :], batch, seq)

    eye = jnp.eye(g, dtype=F32)

    def halves(w):
        w4 = jnp.einsum('lde,gh->lgdhe', w.reshape(CMP_BLOCK, hd, hd), eye)
        w4 = w4.reshape(CMP_BLOCK, g * hd, g * hd)
        return (w4[:CMP_STRIDE].reshape(CMP_STRIDE * g * hd, g * hd).astype(BF16),
                w4[CMP_STRIDE:].reshape(CMP_STRIDE * g * hd, g * hd).astype(BF16))

    def pos_halves(p):
        p2 = jnp.tile(p[:, None, :], (1, g, 1)).reshape(CMP_BLOCK, g * hd)
        return p2[:CMP_STRIDE].reshape(1, -1), p2[CMP_STRIDE:].reshape(1, -1)

    pk1, pk2 = pos_halves(cmp_pos_k)
    pv1, pv2 = pos_halves(cmp_pos_v)
    wk1, wk2 = halves(cmp_wk)
    wv1, wv2 = halves(cmp_wv)
    seq3 = lambda a: a.reshape(batch, seq, KV_WIDTH)
    kc, vc = _compress(seq3(kc_raw), seq3(vc_raw), cos_all[n:].reshape(batch, nc, LANES),
                       sin_all[n:].reshape(batch, nc, LANES), (pk1, pk2, pv1, pv2, wk1, wk2, wv1, wv2))

    n_sb = seq // SEL_BLOCK
    starts = np.arange(nc) * CMP_STRIDE
    jb = np.arange(n_sb)
    ov = ((starts[:, None] < (jb[None, :] + 1) * SEL_BLOCK)
          & (starts[:, None] + CMP_BLOCK > jb[None, :] * SEL_BLOCK) & (np.arange(nc)[:, None] < n_cmp))
    ov = jnp.asarray(ov, BF16)
    pt = jnp.asarray(np.where(np.arange(seq)[:, None] // SEL_BLOCK == jb[None, :], NEG, 0.0), BF16)

    nsa = _nsa(q, gates, kc, vc, seq3(ks), seq3(vs), seq3(kw), seq3(vw), pt, ov, batch, seq)

    w_pool_out = w_out[:POOL_WIDTH].astype(BF16)
    w_nsa_out = (w_out[POOL_WIDTH:].reshape(g, r, hd, D_MODEL).transpose(1, 0, 2, 3)
                 .reshape(Q_WIDTH, D_MODEL).astype(BF16))
    return _out_ln(xf, [pool, nsa], [w_pool_out, w_nsa_out], ln_g, ln_b)


def _odd_mixer(xf, posi, w_in, conv_w, conv_b, wa, ba, wx, bx, lam, w_out, ln_g, ln_b, batch, seq):
    wax = jnp.concatenate([wa, wx], axis=2).astype(BF16)
    return _odd_layer(xf, posi, w_in.astype(BF16), conv_w, conv_b[None, :], wax, ba[None, :], bx[None, :],
                      lam[None, :], w_out.astype(BF16), ln_g, ln_b, batch, seq)


def kernel(x, mem, positions, e_w_in, e_pool_w, e_pool_scale, e_cmp_pos_k, e_cmp_pos_v, e_cmp_wk, e_cmp_wv, e_w_out, o_w_in, o_conv_w, o_conv_b, o_wa, o_ba, o_wx, o_bx, o_lambda, o_w_out, x_wq, x_wkv, x_wo, f_w_up, f_w_down, ln_g, ln_b):
    batch, seq, d = x.shape
    n = batch * seq
    assert d == D_MODEL and seq % ROW_TILE == 0
    xf = x.reshape(n, d)
    posi = positions.reshape(n, 1)
    memf = mem.reshape(batch * mem.shape[1], d)
    for layer in range(DEPTH):
        j = layer // 2
        lg = lambda k: ln_g[layer, k][None, :]
        lb = lambda k: ln_b[layer, k][None, :]
        if layer % 2 == 0:
            xf = _even_mixer(xf, positions, e_w_in[j], e_pool_w[j], e_pool_scale[j], e_cmp_pos_k[j],
                             e_cmp_pos_v[j], e_cmp_wk[j], e_cmp_wv[j], e_w_out[j], lg(0), lb(0), batch, seq)
        else:
            xf = _odd_mixer(xf, posi, o_w_in[j], o_conv_w[j], o_conv_b[j], o_wa[j], o_ba[j], o_wx[j],
                            o_bx[j], o_lambda[j], o_w_out[j], lg(0), lb(0), batch, seq)
        kv = _matmul(memf, x_wkv[layer].astype(BF16), BF16, memf.shape[0]).reshape(batch, mem.shape[1], 2 * d)
        wq = (x_wq[layer] * (X_HEAD_DIM ** -0.5 * LOG2E)).astype(BF16)
        xf = _xattn(xf, kv, wq, x_wo[layer].astype(BF16), lg(1), lb(1), batch, seq)
        xf = _swiglu(xf, f_w_up[layer].astype(BF16), f_w_down[layer].astype(BF16), lg(2), lb(2))
    return xf.reshape(batch, seq, d)
```

```python
import functools

import numpy as np
import jax
import jax.numpy as jnp
from jax import lax
from jax.experimental import pallas as pl
from jax.experimental.pallas import tpu as pltpu

F32 = jnp.float32
BF16 = jnp.bfloat16

D_MODEL = 1024
DEPTH = 2
ALPHA = (2.0 * DEPTH) ** 0.25
LN_EPS = 1e-5
NEG = -1e30
POOL_WIDTH = D_MODEL // 2
POOL_WINDOWS = (2, 4, 8, 16)
POOL_GROUP = POOL_WIDTH // len(POOL_WINDOWS)
POOL_HALO = 16
NSA_HEADS = 8
NSA_KV_HEADS = 2
NSA_HEAD_DIM = 64
NSA_GROUP = NSA_HEADS // NSA_KV_HEADS
CMP_BLOCK = 32
CMP_STRIDE = 16
SEL_BLOCK = 64
SEL_COUNT = 16
WINDOW = 512
Q_BLOCK = 256
N_BRANCH = 3
N_FORCED = 3
LOG2E = 1.4426950408889634
ROPE_THETA = 500000.0
ROT_DIM = NSA_HEAD_DIM // 4
ROT_HALF = ROT_DIM // 2
Q_WIDTH = NSA_HEADS * NSA_HEAD_DIM
KV_WIDTH = NSA_KV_HEADS * NSA_HEAD_DIM
GATE_WIDTH = NSA_HEADS * N_BRANCH
RNN_WIDTH = 1280
RNN_BLOCKS = 10
RNN_BLOCK_W = RNN_WIDTH // RNN_BLOCKS
CONV_WIDTH = 4
CONV_HALO = 8
LRU_C = 8.0
X_HEADS = 4
X_HEAD_DIM = D_MODEL // X_HEADS
FFN_HIDDEN = 2816
GELU_C0 = 0.7978845608028654
GELU_C1 = GELU_C0 * 0.044715

LANES = 128
SUBLANES = 8
BF16_ROWS = 16
VMEM_LIMIT = 56 * 1024 * 1024
ROW_TILE = 512
SEL_KEY_TILE = 512
FFN_CHUNKS = (1024, 1024, 768)


def _params(n_axes, vmem=VMEM_LIMIT):
    return pltpu.CompilerParams(dimension_semantics=("arbitrary",) * n_axes,
                                vmem_limit_bytes=vmem)


def _resident(shape, index_map):
    return pl.BlockSpec(shape, index_map, pipeline_mode=pl.Buffered(1))


def _dot(a, b):
    return jnp.dot(a, b, preferred_element_type=F32)


def _dot_nt(a, b):
    return lax.dot_general(a, b, (((1,), (1,)), ((), ())), preferred_element_type=F32)


def _rep_rows(a, k):
    return jnp.concatenate([a] * k, axis=0)


def _rep_lanes(a, k):
    return jnp.concatenate([a] * k, axis=1) if k > 1 else a


def _layer_norm(z, g, b):
    mu = jnp.mean(z, axis=-1, keepdims=True)
    d = z - mu
    var = jnp.mean(d * d, axis=-1, keepdims=True)
    return d * lax.rsqrt(var + LN_EPS) * g + b


def _rope(v, cos, sin):
    k = v.shape[1] // LANES
    up = pltpu.roll(v, v.shape[1] - ROT_HALF, 1)
    dn = pltpu.roll(v, ROT_HALF, 1)
    lane = lax.broadcasted_iota(jnp.int32, v.shape, 1)
    partner = jnp.where((lane & (NSA_HEAD_DIM - 1)) < ROT_HALF, up, dn)
    return v * _rep_lanes(cos, k) + partner * _rep_lanes(sin, k)


def _rope_angle_kernel(pos_ref, inv_ref, cos_o, sin_o):
    ang = inv_ref[...] * pos_ref[...]
    cos_o[...] = jnp.cos(ang)
    sin_o[...] = jnp.sin(ang)


def _rope_tables(pos_row):
    n = pos_row.shape[1]
    inv = (ROPE_THETA ** (-jnp.arange(ROT_HALF, dtype=F32) * 2.0 / ROT_DIM))[:, None]
    out = jax.ShapeDtypeStruct((ROT_HALF, n), F32)
    cos_t, sin_t = pl.pallas_call(
        _rope_angle_kernel,
        out_shape=[out, out],
        name="rope_angles",
    )(pos_row, inv)
    c, s = cos_t.T, sin_t.T
    rest = NSA_HEAD_DIM - ROT_DIM
    cos = jnp.concatenate([c, c, jnp.ones((n, rest), F32)], axis=1)
    sin = jnp.concatenate([-s, s, jnp.zeros((n, rest), F32)], axis=1)
    reps = LANES // NSA_HEAD_DIM
    return jnp.tile(cos, (1, reps)), jnp.tile(sin, (1, reps))


def _proj_even_kernel(x_ref, cos_ref, sin_ref, wp_ref, wq_ref, wkv_ref, wg_ref, poolw_ref, pscale_ref,
                      pool_o, q_o, kc_o, vc_o, ks_o, vs_o, kw_o, vw_o, gate_o, pbuf):
    i = pl.program_id(1)
    tm = x_ref.shape[0]
    xb = x_ref[...].astype(BF16)
    cos = cos_ref[...]
    sin = sin_ref[...]

    q_o[...] = _rope(_dot(xb, wq_ref[...]), cos, sin).astype(BF16)
    kv = _dot(xb, wkv_ref[...])
    kc_o[...] = kv[:, 0 * LANES:1 * LANES]
    vc_o[...] = kv[:, 1 * LANES:2 * LANES]
    ks_o[...] = _rope(kv[:, 2 * LANES:3 * LANES], cos, sin).astype(BF16)
    vs_o[...] = kv[:, 3 * LANES:4 * LANES].astype(BF16)
    kw_o[...] = _rope(kv[:, 4 * LANES:5 * LANES], cos, sin).astype(BF16)
    vw_o[...] = kv[:, 5 * LANES:6 * LANES].astype(BF16)
    gate_o[...] = _sigmoid_tanh(_dot(xb, wg_ref[...]))

    @pl.when(i == 0)
    def _():
        pbuf[0:POOL_HALO, :] = jnp.zeros((POOL_HALO, POOL_WIDTH), F32)

    p = _dot(xb, wp_ref[...])
    pbuf[POOL_HALO:POOL_HALO + tm, :] = p
    t1 = i * tm + lax.broadcasted_iota(jnp.int32, (tm, 1), 0) + 1
    for g, w in enumerate(POOL_WINDOWS):
        sl = slice(g * POOL_GROUP, (g + 1) * POOL_GROUP)
        u = p[:, sl]
        tot = u
        for j in range(1, w):
            tot = tot + pbuf[pl.ds(POOL_HALO - j, tm), sl]
        cnt = jnp.minimum(t1, w).astype(F32)
        pooled = tot / cnt - u
        mixed = _dot(pooled.astype(BF16), poolw_ref[g]) * pscale_ref[:, sl]
        pool_o[:, sl] = mixed.astype(BF16)
    pbuf[0:POOL_HALO, :] = pbuf[tm:tm + POOL_HALO, :]


def _proj_even(xf, cos, sin, wp, wq, wkv, wg, poolw, pscale, batch, seq):
    n = xf.shape[0]
    tm = ROW_TILE
    nt = seq // tm
    row = lambda b, i: (b * nt + i, 0)
    full = lambda b, i: (0, 0)
    outs = [
        jax.ShapeDtypeStruct((n, POOL_WIDTH), BF16),
        jax.ShapeDtypeStruct((n, Q_WIDTH), BF16),
        jax.ShapeDtypeStruct((n, KV_WIDTH), F32),
        jax.ShapeDtypeStruct((n, KV_WIDTH), F32),
        jax.ShapeDtypeStruct((n, KV_WIDTH), BF16),
        jax.ShapeDtypeStruct((n, KV_WIDTH), BF16),
        jax.ShapeDtypeStruct((n, KV_WIDTH), BF16),
        jax.ShapeDtypeStruct((n, KV_WIDTH), BF16),
        jax.ShapeDtypeStruct((n, LANES), F32),
    ]
    return pl.pallas_call(
        _proj_even_kernel,
        grid=(batch, nt),
        in_specs=[
            pl.BlockSpec((tm, D_MODEL), row),
            pl.BlockSpec((tm, LANES), row),
            pl.BlockSpec((tm, LANES), row),
            pl.BlockSpec(wp.shape, full),
            pl.BlockSpec(wq.shape, full),
            pl.BlockSpec(wkv.shape, full),
            pl.BlockSpec(wg.shape, full),
            pl.BlockSpec(poolw.shape, lambda b, i: (0, 0, 0)),
            pl.BlockSpec(pscale.shape, full),
        ],
        out_specs=[pl.BlockSpec((tm, o.shape[1]), row) for o in outs],
        out_shape=outs,
        scratch_shapes=[pltpu.VMEM((POOL_HALO + tm, POOL_WIDTH), F32)],
        compiler_params=_params(2),
        name="proj_even",
    )(xf, cos, sin, wp, wq, wkv, wg, poolw, pscale)


def _compress_kernel(rk_ref, rv_ref, cos_ref, sin_ref, pk1_ref, pk2_ref, pv1_ref, pv2_ref,
                     wk1_ref, wk2_ref, wv1_ref, wv2_ref, kc_o, vc_o):
    nc = kc_o.shape[0]

    def compress(raw_ref, p1, p2, w1, w2):
        r = jnp.concatenate([raw_ref[pl.ds(j, nc, stride=CMP_STRIDE), :] for j in range(CMP_STRIDE)], axis=1)
        a = _dot((r + p1[...]).astype(BF16), w1[...])
        b = _dot((r + p2[...]).astype(BF16), w2[...])
        return a + pltpu.roll(b, nc - 1, 0)

    kc = compress(rk_ref, pk1_ref, pk2_ref, wk1_ref, wk2_ref)
    kc_o[...] = _rope(kc, cos_ref[...], sin_ref[...]).astype(BF16)
    vc_o[...] = compress(rv_ref, pv1_ref, pv2_ref, wv1_ref, wv2_ref).astype(BF16)


def _compress(rk, rv, cos, sin, consts):
    batch, seq, width = rk.shape
    nc = seq // CMP_STRIDE
    blk = lambda b: (b, 0, 0)
    full = lambda b: (0, 0)
    out = jax.ShapeDtypeStruct((batch, nc, KV_WIDTH), BF16)
    return pl.pallas_call(
        _compress_kernel,
        grid=(batch,),
        in_specs=[pl.BlockSpec((None, seq, width), blk), pl.BlockSpec((None, seq, width), blk),
                  pl.BlockSpec((None, nc, LANES), blk), pl.BlockSpec((None, nc, LANES), blk)]
                 + [pl.BlockSpec(a.shape, full) for a in consts],
        out_specs=[pl.BlockSpec((None, nc, KV_WIDTH), blk)] * 2,
        out_shape=[out, out],
        compiler_params=_params(1),
        name="compress",
    )(rk, rv, cos, sin, *consts)


def _softmax_rows(s):
    m = jnp.max(s, axis=1, keepdims=True)
    e = jnp.exp2(s - m)
    return e, jnp.sum(e, axis=1, keepdims=True)


def _dot_exact01(x, m01):
    hi = x.astype(BF16)
    r1 = x - hi.astype(F32)
    mid = r1.astype(BF16)
    lo = (r1 - mid.astype(F32)).astype(BF16)
    return _dot(hi, m01) + _dot(mid, m01) + _dot(lo, m01)


def _unselected_blocks(imp, q0, n_pick):
    nq, nb = imp.shape
    col = lax.broadcasted_iota(jnp.int32, (nq, nb), 1)
    cur = lax.shift_right_arithmetic(q0 + lax.broadcasted_iota(jnp.int32, (nq, nb), 0), 6)
    forced = (col == 0) | (col == cur) | (col == cur - 1)
    val = jnp.where(forced, -jnp.inf, jnp.where(col > cur, -1.0, imp))
    vt = val.T
    blk = lax.broadcasted_iota(jnp.int32, (nb, nq), 0).astype(F32)
    unsel = jnp.where(forced, 0.0, 1.0).T
    for _ in range(n_pick - N_FORCED):
        m = jnp.max(vt, axis=0, keepdims=True)
        first = jnp.min(jnp.where(vt == m, blk, float(nb)), axis=0, keepdims=True)
        hit = blk == first
        unsel = jnp.where(hit, 0.0, unsel)
        vt = jnp.where(hit, -jnp.inf, vt)
    return unsel.T


def _nsa_kernel(q_ref, gate_ref, kc_ref, vc_ref, ks_ref, vs_ref, kw_ref, vw_ref, pt_ref, ov_ref,
                o_ref, m_scr, l_scr, acc_scr, lhs_scr, oc_scr, ow_scr, s0_scr, s1_scr, w_scr, gx_scr, *, n_pick):
    qb = pl.program_id(1)
    q0 = qb * Q_BLOCK
    tk = SEL_KEY_TILE
    ncmp = kc_ref.shape[0]
    wk = WINDOW + Q_BLOCK
    rows = NSA_GROUP * Q_BLOCK
    lane = lax.broadcasted_iota(jnp.int32, (Q_BLOCK, LANES), 1)

    def trow(width):
        return q0 + lax.broadcasted_iota(jnp.int32, (Q_BLOCK, width), 0)

    def kcol(width):
        return lax.broadcasted_iota(jnp.int32, (Q_BLOCK, width), 1)

    bias_c = jnp.where(kcol(ncmp) * CMP_STRIDE + (CMP_BLOCK - 1) <= trow(ncmp), 0.0, NEG)
    has_cmp = (trow(1) >= CMP_BLOCK - 1).astype(F32)
    wstart = pl.multiple_of(jnp.maximum(q0 - WINDOW, 0), Q_BLOCK)
    kpos_w = wstart + kcol(wk)
    bias_w = jnp.where(kpos_w <= trow(wk), jnp.where(kpos_w > trow(wk) - WINDOW, 0.0, NEG), NEG)

    for g in range(NSA_KV_HEADS):
        mine = (lane >= NSA_HEAD_DIM) if g else (lane < NSA_HEAD_DIM)
        lhs_scr[g, :, 0:LANES] = jnp.concatenate(
            [jnp.where(mine, q_ref[:, r * LANES:(r + 1) * LANES], jnp.zeros((), BF16))
             for r in range(NSA_GROUP)], axis=0)
        m_scr[g] = jnp.full((rows, LANES), -jnp.inf, F32)
        l_scr[g] = jnp.zeros((rows, LANES), F32)
        acc_scr[g] = jnp.zeros((rows, LANES), F32)

    def cmp_scores(g):
        s0_scr[g] = _dot_nt(lhs_scr[g, :, 0:LANES], kc_ref[...])

    def cmp_attend(g):
        e, l = _softmax_rows(s0_scr[g] + _rep_rows(bias_c, NSA_GROUP))
        p = e * (_rep_rows(has_cmp, NSA_GROUP) / l)
        oc_scr[g] = _dot(p.astype(BF16), vc_ref[...])
        psum = p[0:Q_BLOCK]
        for r in range(1, NSA_GROUP):
            psum = psum + p[r * Q_BLOCK:(r + 1) * Q_BLOCK]
        return _dot_exact01(psum, ov_ref[...])

    def select(g, imp):
        unsel = _unselected_blocks(imp, q0, n_pick)
        lhs_scr[g, :, LANES:2 * LANES] = _rep_rows(unsel.astype(BF16), NSA_GROUP)

    def win_scores(g):
        w_scr[g] = _dot_nt(lhs_scr[g, :, 0:LANES], kw_ref[pl.ds(wstart, wk), :])

    def win_attend(g):
        e, l = _softmax_rows(w_scr[g] + _rep_rows(bias_w, NSA_GROUP))
        ow_scr[g] = _dot(e.astype(BF16), vw_ref[pl.ds(wstart, wk), :]) / l

    def expand_gates():
        low = lane < NSA_HEAD_DIM
        for r in range(NSA_GROUP):
            for br in range(N_BRANCH):
                c_lo = r * N_BRANCH + br
                c_hi = (NSA_GROUP + r) * N_BRANCH + br
                gx_scr[c_lo] = jnp.where(low, gate_ref[:, c_lo:c_lo + 1], gate_ref[:, c_hi:c_hi + 1])

    cmp_scores(0)
    cmp_scores(1)
    expand_gates()
    win_scores(0)
    imp0 = cmp_attend(0)
    win_scores(1)
    imp1 = cmp_attend(1)
    select(0, imp0)
    win_attend(0)
    select(1, imp1)
    win_attend(1)

    def scores(kt, buf, g):
        k0 = pl.multiple_of(kt * tk, tk)
        rhs = jnp.concatenate([ks_ref[pl.ds(k0, tk), :], pt_ref[pl.ds(k0, tk), :]], axis=1)
        buf[g] = _dot_nt(lhs_scr[g], rhs)

    def consume(kt, buf, g, causal):
        k0 = pl.multiple_of(kt * tk, tk)
        s = buf[g]
        if causal:
            s = s + _rep_rows(jnp.where(k0 + kcol(tk) <= trow(tk), 0.0, NEG), NSA_GROUP)
        m_prev = m_scr[g]
        m_new = jnp.maximum(m_prev, jnp.max(s, axis=1, keepdims=True))
        alpha = jnp.exp2(m_prev - m_new)
        pe = jnp.exp2(s - _rep_lanes(m_new, tk // LANES))
        part = pe[:, 0:LANES]
        for c in range(1, tk // LANES):
            part = part + pe[:, c * LANES:(c + 1) * LANES]
        l_scr[g] = alpha * l_scr[g] + part
        acc_scr[g] = alpha * acc_scr[g] + _dot(pe.astype(BF16), vs_ref[pl.ds(k0, tk), :])
        m_scr[g] = m_new

    def step(nxt, nxt_buf, cur, cur_buf, causal=False):
        for g in range(NSA_KV_HEADS):
            if nxt is not None:
                scores(nxt, nxt_buf, g)
        for g in range(NSA_KV_HEADS):
            consume(cur, cur_buf, g, causal)

    n_full = q0 // tk
    for g in range(NSA_KV_HEADS):
        scores(0, s0_scr, g)

    def pair(j, carry):
        step(2 * j + 1, s1_scr, 2 * j, s0_scr)
        step(2 * j + 2, s0_scr, 2 * j + 1, s1_scr)
        return carry

    lax.fori_loop(0, n_full // 2, pair, 0)

    @pl.when(n_full % 2 == 0)
    def _():
        step(None, None, n_full, s0_scr, causal=True)

    @pl.when(n_full % 2 == 1)
    def _():
        step(n_full, s1_scr, n_full - 1, s0_scr)
        step(None, None, n_full, s1_scr, causal=True)

    low = lane < NSA_HEAD_DIM
    for r in range(NSA_GROUP):
        rs = slice(r * Q_BLOCK, (r + 1) * Q_BLOCK)
        merged = lambda scr: jnp.where(low, scr[0, rs, :], scr[1, rs, :])
        l_sel = jnp.where(low, jnp.sum(l_scr[0, rs, :], axis=1, keepdims=True),
                          jnp.sum(l_scr[1, rs, :], axis=1, keepdims=True))
        out = (gx_scr[r * N_BRANCH] * merged(oc_scr)
               + gx_scr[r * N_BRANCH + 1] * (merged(acc_scr) / l_sel)
               + gx_scr[r * N_BRANCH + 2] * merged(ow_scr))
        o_ref[:, r * LANES:(r + 1) * LANES] = out.astype(BF16)


def _nsa(q, gates, kc, vc, ks, vs, kw, vw, pt, ov, batch, seq):
    n = q.shape[0]
    nqb = seq // Q_BLOCK
    rows = NSA_GROUP * Q_BLOCK
    row = lambda b, i: (b * nqb + i, 0)
    per_b = lambda b, i: (b, 0, 0)
    full = lambda b, i: (0, 0)
    seq_spec = _resident((None, seq, KV_WIDTH), per_b)
    cmp_spec = _resident((None, kc.shape[1], KV_WIDTH), per_b)
    n_pick = min(SEL_COUNT, seq // SEL_BLOCK)
    return pl.pallas_call(
        functools.partial(_nsa_kernel, n_pick=n_pick),
        grid=(batch, nqb),
        in_specs=[pl.BlockSpec((Q_BLOCK, Q_WIDTH), row), pl.BlockSpec((Q_BLOCK, LANES), row),
                  cmp_spec, cmp_spec, seq_spec, seq_spec, seq_spec, seq_spec,
                  _resident(pt.shape, full), _resident(ov.shape, full)],
        out_specs=pl.BlockSpec((Q_BLOCK, Q_WIDTH), row),
        out_shape=jax.ShapeDtypeStruct((n, Q_WIDTH), BF16),
        scratch_shapes=[pltpu.VMEM((NSA_KV_HEADS, rows, LANES), F32),
                        pltpu.VMEM((NSA_KV_HEADS, rows, LANES), F32),
                        pltpu.VMEM((NSA_KV_HEADS, rows, LANES), F32),
                        pltpu.VMEM((NSA_KV_HEADS, rows, 2 * LANES), BF16),
                        pltpu.VMEM((NSA_KV_HEADS, rows, LANES), F32),
                        pltpu.VMEM((NSA_KV_HEADS, rows, LANES), F32),
                        pltpu.VMEM((NSA_KV_HEADS, rows, SEL_KEY_TILE), F32),
                        pltpu.VMEM((NSA_KV_HEADS, rows, SEL_KEY_TILE), F32),
                        pltpu.VMEM((NSA_KV_HEADS, rows, WINDOW + Q_BLOCK), F32),
                        pltpu.VMEM((NSA_GROUP * N_BRANCH, Q_BLOCK, LANES), F32)],
        compiler_params=_params(2),
        name="nsa",
    )(q, gates, kc, vc, ks, vs, kw, vw, pt, ov)


def _matmul_kernel(a_ref, w_ref, o_ref):
    o_ref[...] = _dot(a_ref[...].astype(BF16), w_ref[...]).astype(o_ref.dtype)


def _mem_kv(memf, wkv_all, layer):
    m, k = memf.shape
    n = wkv_all.shape[2]
    return pl.pallas_call(
        _matmul_kernel,
        grid=(1,),
        in_specs=[pl.BlockSpec((m, k), lambda i: (0, 0)), pl.BlockSpec((None, k, n), lambda i: (layer, 0, 0))],
        out_specs=pl.BlockSpec((m, n), lambda i: (0, 0)),
        out_shape=jax.ShapeDtypeStruct((m, n), BF16),
        compiler_params=_params(1),
        name="mem_kv",
    )(memf, wkv_all)


def _tail_kernel(*refs, n_parts):
    x_ref = refs[0]
    parts = refs[1:1 + n_parts]
    ws = refs[1 + n_parts:1 + 2 * n_parts]
    wq_ref, k_ref, v_ref, wo_ref, wu_ref, wd_ref, g_ref, b_ref, o_ref, x1_scr, x2_scr = refs[1 + 2 * n_parts:]
    tm = x_ref.shape[0]
    half = tm // 2
    halves = (slice(0, half), slice(half, tm))
    x1 = x1_scr if n_parts else x_ref

    def norm(k, z):
        return _layer_norm(z, g_ref[k:k + 1, :], b_ref[k:k + 1, :])

    def mixer_out(rs):
        z = ALPHA * x_ref[rs, :]
        for a, w in zip(parts, ws):
            z = z + _dot(a[rs, :], w[...])
        x1_scr[rs, :] = norm(0, z)

    def project(rs):
        return _dot(x1[rs, :].astype(BF16), wq_ref[...]).astype(BF16)

    def attend(q):
        outs = []
        for h in range(X_HEADS):
            sl = slice(h * X_HEAD_DIM, (h + 1) * X_HEAD_DIM)
            e, l = _softmax_rows(_dot_nt(q[:, sl], k_ref[:, sl]))
            outs.append(_dot((e / l).astype(BF16), v_ref[:, sl]))
        return jnp.concatenate(outs, axis=1).astype(BF16)

    def finish(rs, o):
        x2_scr[rs, :] = norm(1, ALPHA * x1[rs, :] + _dot(o, wo_ref[...]))

    if n_parts:
        mixer_out(halves[0])
        mixer_out(halves[1])
    qa = project(halves[0])
    qb = project(halves[1])
    oa = attend(qa)
    ob = attend(qb)
    finish(halves[0], oa)
    finish(halves[1], ob)

    x2 = x2_scr[...]
    xb = x2.astype(BF16)
    z = ALPHA * x2
    c0 = 0
    for width in FFN_CHUNKS:
        h1 = _dot(xb, wu_ref[:, c0:c0 + width])
        h2 = _dot(xb, wu_ref[:, FFN_HIDDEN + c0:FFN_HIDDEN + c0 + width])
        act = (h1 * _sigmoid_tanh(h1) * h2).astype(BF16)
        z = z + _dot(act, wd_ref[c0:c0 + width, :])
        c0 += width
    o_ref[...] = norm(2, z)


def _tail(xf, parts, ws, kv, wq_all, wo_all, wu_all, wd_all, g_all, b_all, layer, batch, seq):
    n = xf.shape[0]
    tm = ROW_TILE
    nt = seq // tm
    n_mem = kv.shape[1]
    row = lambda bi, i: (bi * nt + i, 0)
    full = lambda bi, i: (0, 0)
    this_layer = lambda bi, i: (layer, 0, 0)
    stacked = lambda a: _resident((None,) + a.shape[1:], this_layer)
    return pl.pallas_call(
        functools.partial(_tail_kernel, n_parts=len(parts)),
        grid=(batch, nt),
        in_specs=[pl.BlockSpec((tm, D_MODEL), row)]
                 + [pl.BlockSpec((tm, a.shape[1]), row) for a in parts]
                 + [_resident(w.shape, full) for w in ws]
                 + [stacked(wq_all),
                    _resident((None, n_mem, D_MODEL), lambda bi, i: (bi, 0, 0)),
                    _resident((None, n_mem, D_MODEL), lambda bi, i: (bi, 0, 1)),
                    stacked(wo_all), stacked(wu_all), stacked(wd_all), stacked(g_all), stacked(b_all)],
        out_specs=pl.BlockSpec((tm, D_MODEL), row),
        out_shape=jax.ShapeDtypeStruct((n, D_MODEL), F32),
        scratch_shapes=[pltpu.VMEM((tm, D_MODEL), F32), pltpu.VMEM((tm, D_MODEL), F32)],
        compiler_params=_params(2),
        name="tail",
    )(xf, *parts, *ws, wq_all, kv, kv, wo_all, wu_all, wd_all, g_all, b_all)


def _gelu_tanh(x):
    inner = x * (GELU_C0 + GELU_C1 * (x * x))
    return x * (0.5 * jnp.tanh(inner) + 0.5)


def _sigmoid_tanh(z):
    return 0.5 * jnp.tanh(0.5 * z) + 0.5


def _odd_kernel(x_ref, pos_ref, win_ref, cw_ref, cb_ref, wax_ref, ba_ref, bx_ref, lam_ref, wout_ref,
                g_ref, b_ref, o_ref, xbuf, gate_scr, a_scr, b_scr, h_scr):
    i = pl.program_id(1)
    tm = x_ref.shape[0]
    half = tm // 2
    groups = half // SUBLANES

    @pl.when(i == 0)
    def _():
        xbuf[0:CONV_HALO, :] = jnp.zeros((CONV_HALO, RNN_WIDTH), F32)
        h_scr[...] = jnp.zeros((SUBLANES, RNN_WIDTH), F32)

    in_group = lax.broadcasted_iota(jnp.int32, (groups, SUBLANES, RNN_BLOCK_W), 1)
    lam = lam_ref[...]
    log_a_scale = -LRU_C * (jnp.maximum(-lam, 0.0) + jnp.log1p(jnp.exp(-jnp.abs(lam))))

    def in_proj(h0):
        xb = x_ref[h0:h0 + half, :].astype(BF16)
        gate_scr[h0:h0 + half, :] = _dot(xb, win_ref[:, :RNN_WIDTH])
        xbuf[CONV_HALO + h0:CONV_HALO + h0 + half, :] = _dot(xb, win_ref[:, RNN_WIDTH:])

    def gates(h0):
        reset = pos_ref[h0:h0 + half, :] == 0
        for h in range(RNN_BLOCKS):
            sl = slice(h * RNN_BLOCK_W, (h + 1) * RNN_BLOCK_W)
            xc = cb_ref[:, sl]
            for k in range(CONV_WIDTH):
                xc = xc + xbuf[pl.ds(CONV_HALO - (CONV_WIDTH - 1) + k + h0, half), sl] * cw_ref[k:k + 1, sl]
            ri = _dot(xc.astype(BF16), wax_ref[h])
            r = _sigmoid_tanh(ri[:, :RNN_BLOCK_W] + ba_ref[:, sl])
            ig = _sigmoid_tanh(ri[:, RNN_BLOCK_W:] + bx_ref[:, sl])
            log_a = r * log_a_scale[:, sl]
            a = jnp.where(reset, 0.0, jnp.exp(log_a))
            th = jnp.tanh(log_a)
            m2 = -2.0 * th / (1.0 - th)
            mult = jnp.where(reset, 1.0, jnp.where(m2 > 0.0, m2 * lax.rsqrt(m2), 0.0))
            b = mult * (ig * xc)
            a3 = a.reshape(groups, SUBLANES, RNN_BLOCK_W)
            b3 = b.reshape(groups, SUBLANES, RNN_BLOCK_W)
            for d in (1, 2, 4):
                a_prev = pltpu.roll(a3, d, 1)
                b_prev = pltpu.roll(b3, d, 1)
                ok = in_group >= d
                b3 = jnp.where(ok, a3 * b_prev + b3, b3)
                a3 = jnp.where(ok, a3 * a_prev, a3)
            a_scr[h0:h0 + half, sl] = a3.reshape(half, RNN_BLOCK_W)
            b_scr[h0:h0 + half, sl] = b3.reshape(half, RNN_BLOCK_W)

    def recur(h0, h_prev):
        for j in range(groups):
            rs = slice(h0 + j * SUBLANES, h0 + (j + 1) * SUBLANES)
            hh = a_scr[rs, :] * h_prev + b_scr[rs, :]
            b_scr[rs, :] = hh
            h_prev = jnp.broadcast_to(hh[SUBLANES - 1:SUBLANES, :], (SUBLANES, RNN_WIDTH))
        return h_prev

    def out_proj(h0):
        rs = slice(h0, h0 + half)
        y = (b_scr[rs, :] * _gelu_tanh(gate_scr[rs, :])).astype(BF16)
        z = ALPHA * x_ref[rs, :] + _dot(y, wout_ref[...])
        o_ref[rs, :] = _layer_norm(z, g_ref[...], b_ref[...])

    in_proj(0)
    in_proj(half)
    gates(0)
    h_mid = recur(0, h_scr[...])
    gates(half)
    out_proj(0)
    h_scr[...] = recur(half, h_mid)
    out_proj(half)
    xbuf[0:CONV_HALO, :] = xbuf[tm:tm + CONV_HALO, :]


def _odd_layer(xf, posi, w_in, cw, cb, wax, ba, bx, lam, w_out, g, b, batch, seq):
    n = xf.shape[0]
    tm = ROW_TILE
    nt = seq // tm
    row = lambda bi, i: (bi * nt + i, 0)
    full = lambda bi, i: (0, 0)
    consts = [w_in, cw, cb]
    rest = [ba, bx, lam, w_out, g, b]
    return pl.pallas_call(
        _odd_kernel,
        grid=(batch, nt),
        in_specs=[pl.BlockSpec((tm, D_MODEL), row), pl.BlockSpec((tm, 1), row)]
                 + [pl.BlockSpec(a.shape, full) for a in consts]
                 + [pl.BlockSpec(wax.shape, lambda bi, i: (0, 0, 0))]
                 + [pl.BlockSpec(a.shape, full) for a in rest],
        out_specs=pl.BlockSpec((tm, D_MODEL), row),
        out_shape=jax.ShapeDtypeStruct((n, D_MODEL), F32),
        scratch_shapes=[pltpu.VMEM((CONV_HALO + tm, RNN_WIDTH), F32),
                        pltpu.VMEM((tm, RNN_WIDTH), F32),
                        pltpu.VMEM((tm, RNN_WIDTH), F32),
                        pltpu.VMEM((tm, RNN_WIDTH), F32),
                        pltpu.VMEM((SUBLANES, RNN_WIDTH), F32)],
        compiler_params=_params(2),
        name="odd_layer",
    )(xf, posi, *consts, wax, *rest)


def _even_mixer(xf, positions, w_in, pool_w, pool_scale, cmp_pos_k, cmp_pos_v, cmp_wk, cmp_wv,
                w_out, batch, seq):
    assert seq // SEL_BLOCK == LANES and seq % SEL_KEY_TILE == 0 and seq >= WINDOW + Q_BLOCK
    hd, g, r = NSA_HEAD_DIM, NSA_KV_HEADS, NSA_GROUP
    c0 = POOL_WIDTH
    c1 = c0 + Q_WIDTH
    c2 = c1 + GATE_WIDTH
    wp = w_in[:, :c0].astype(BF16)
    wq = (w_in[:, c0:c1].reshape(D_MODEL, g, r, hd).transpose(0, 2, 1, 3).reshape(D_MODEL, Q_WIDTH)
          * (hd ** -0.5 * LOG2E)).astype(BF16)
    wg = jnp.pad(w_in[:, c1:c2], ((0, 0), (0, LANES - GATE_WIDTH))).astype(BF16)
    wkv = w_in[:, c2:].astype(BF16)
    n = batch * seq
    nc = seq // CMP_STRIDE
    n_cmp = (seq - CMP_BLOCK) // CMP_STRIDE + 1
    posc = jnp.pad(positions[:, CMP_BLOCK - 1::CMP_STRIDE][:, :n_cmp], ((0, 0), (0, nc - n_cmp)))
    pos_all = jnp.concatenate([positions.reshape(1, n), posc.reshape(1, batch * nc)], axis=1).astype(F32)
    cos_all, sin_all = _rope_tables(pos_all)
    pool, q, kc_raw, vc_raw, ks, vs, kw, vw, gates = _proj_even(
        xf, cos_all, sin_all, wp, wq, wkv, wg, pool_w.astype(BF16), pool_scale[None, :], batch, seq)

    eye = jnp.eye(g, dtype=F32)

    def halves(w):
        w4 = jnp.einsum('lde,gh->lgdhe', w.reshape(CMP_BLOCK, hd, hd), eye)
        w4 = w4.reshape(CMP_BLOCK, g * hd, g * hd)
        return (w4[:CMP_STRIDE].reshape(CMP_STRIDE * g * hd, g * hd).astype(BF16),
                w4[CMP_STRIDE:].reshape(CMP_STRIDE * g * hd, g * hd).astype(BF16))

    def pos_halves(p):
        p2 = jnp.tile(p[:, None, :], (1, g, 1)).reshape(CMP_BLOCK, g * hd)
        return p2[:CMP_STRIDE].reshape(1, -1), p2[CMP_STRIDE:].reshape(1, -1)

    pk1, pk2 = pos_halves(cmp_pos_k)
    pv1, pv2 = pos_halves(cmp_pos_v)
    wk1, wk2 = halves(cmp_wk)
    wv1, wv2 = halves(cmp_wv)
    seq3 = lambda a: a.reshape(batch, seq, KV_WIDTH)
    kc, vc = _compress(seq3(kc_raw), seq3(vc_raw), cos_all[n:].reshape(batch, nc, LANES),
                       sin_all[n:].reshape(batch, nc, LANES), (pk1, pk2, pv1, pv2, wk1, wk2, wv1, wv2))

    n_sb = seq // SEL_BLOCK
    starts = np.arange(nc) * CMP_STRIDE
    jb = np.arange(n_sb)
    ov = ((starts[:, None] < (jb[None, :] + 1) * SEL_BLOCK)
          & (starts[:, None] + CMP_BLOCK > jb[None, :] * SEL_BLOCK) & (np.arange(nc)[:, None] < n_cmp))
    ov = jnp.asarray(ov, BF16)
    pt = jnp.asarray(np.where(np.arange(seq)[:, None] // SEL_BLOCK == jb[None, :], NEG, 0.0), BF16)

    nsa = _nsa(q, gates, kc, vc, seq3(ks), seq3(vs), seq3(kw), seq3(vw), pt, ov, batch, seq)

    w_pool_out = w_out[:POOL_WIDTH].astype(BF16)
    w_nsa_out = (w_out[POOL_WIDTH:].reshape(g, r, hd, D_MODEL).transpose(1, 0, 2, 3)
                 .reshape(Q_WIDTH, D_MODEL).astype(BF16))
    return [pool, nsa], [w_pool_out, w_nsa_out]


def _odd_mixer(xf, posi, w_in, conv_w, conv_b, wa, ba, wx, bx, lam, w_out, ln_g, ln_b, batch, seq):
    wax = jnp.concatenate([wa, wx], axis=2).astype(BF16)
    return _odd_layer(xf, posi, w_in.astype(BF16), conv_w, conv_b[None, :], wax, ba[None, :], bx[None, :],
                      lam[None, :], w_out.astype(BF16), ln_g, ln_b, batch, seq)


def kernel(x, mem, positions, e_w_in, e_pool_w, e_pool_scale, e_cmp_pos_k, e_cmp_pos_v, e_cmp_wk, e_cmp_wv, e_w_out, o_w_in, o_conv_w, o_conv_b, o_wa, o_ba, o_wx, o_bx, o_lambda, o_w_out, x_wq, x_wkv, x_wo, f_w_up, f_w_down, ln_g, ln_b):
    batch, seq, d = x.shape
    n = batch * seq
    assert d == D_MODEL and seq % ROW_TILE == 0
    xf = x.reshape(n, d)
    posi = positions.reshape(n, 1)
    memf = mem.reshape(batch * mem.shape[1], d)
    wq_all = (x_wq * (X_HEAD_DIM ** -0.5 * LOG2E)).astype(BF16)
    wkv_all, wo_all = x_wkv.astype(BF16), x_wo.astype(BF16)
    wu_all, wd_all = f_w_up.astype(BF16), f_w_down.astype(BF16)
    for layer in range(DEPTH):
        j = layer // 2
        if layer % 2 == 0:
            parts, ws = _even_mixer(xf, positions, e_w_in[j], e_pool_w[j], e_pool_scale[j], e_cmp_pos_k[j],
                                    e_cmp_pos_v[j], e_cmp_wk[j], e_cmp_wv[j], e_w_out[j], batch, seq)
        else:
            xf = _odd_mixer(xf, posi, o_w_in[j], o_conv_w[j], o_conv_b[j], o_wa[j], o_ba[j], o_wx[j], o_bx[j],
                            o_lambda[j], o_w_out[j], ln_g[layer, 0][None, :], ln_b[layer, 0][None, :], batch, seq)
            parts, ws = [], []
        kv = _mem_kv(memf, wkv_all, layer).reshape(batch, mem.shape[1], 2 * d)
        xf = _tail(xf, parts, ws, kv, wq_all, wo_all, wu_all, wd_all, ln_g, ln_b, layer, batch, seq)
    return xf.reshape(batch, seq, d)
```

```python
import functools

import numpy as np
import jax
import jax.numpy as jnp
from jax import lax
from jax.experimental import pallas as pl
from jax.experimental.pallas import tpu as pltpu

F32 = jnp.float32
BF16 = jnp.bfloat16

D_MODEL = 1024
DEPTH = 2
ALPHA = (2.0 * DEPTH) ** 0.25
LN_EPS = 1e-5
NEG = -1e30
POOL_WIDTH = D_MODEL // 2
POOL_WINDOWS = (2, 4, 8, 16)
POOL_GROUP = POOL_WIDTH // len(POOL_WINDOWS)
POOL_HALO = 16
NSA_HEADS = 8
NSA_KV_HEADS = 2
NSA_HEAD_DIM = 64
NSA_GROUP = NSA_HEADS // NSA_KV_HEADS
CMP_BLOCK = 32
CMP_STRIDE = 16
SEL_BLOCK = 64
SEL_COUNT = 16
WINDOW = 512
Q_BLOCK = 256
N_BRANCH = 3
N_FORCED = 3
LOG2E = 1.4426950408889634
ROPE_THETA = 500000.0
ROT_DIM = NSA_HEAD_DIM // 4
ROT_HALF = ROT_DIM // 2
Q_WIDTH = NSA_HEADS * NSA_HEAD_DIM
KV_WIDTH = NSA_KV_HEADS * NSA_HEAD_DIM
GATE_WIDTH = NSA_HEADS * N_BRANCH
RNN_WIDTH = 1280
RNN_BLOCKS = 10
RNN_BLOCK_W = RNN_WIDTH // RNN_BLOCKS
CONV_WIDTH = 4
CONV_HALO = 8
LRU_C = 8.0
X_HEADS = 4
X_HEAD_DIM = D_MODEL // X_HEADS
FFN_HIDDEN = 2816
GELU_C0 = 0.7978845608028654
GELU_C1 = GELU_C0 * 0.044715

LANES = 128
SUBLANES = 8
BF16_ROWS = 16
VMEM_LIMIT = 56 * 1024 * 1024
ROW_TILE = 512
SEL_KEY_TILE = 512
FFN_CHUNKS = (1024, 1024, 768)


def _params(n_axes, vmem=VMEM_LIMIT):
    return pltpu.CompilerParams(dimension_semantics=("arbitrary",) * n_axes,
                                vmem_limit_bytes=vmem)


def _resident(shape, index_map):
    return pl.BlockSpec(shape, index_map, pipeline_mode=pl.Buffered(1))


def _dot(a, b):
    return jnp.dot(a, b, preferred_element_type=F32)


def _dot_nt(a, b):
    return lax.dot_general(a, b, (((1,), (1,)), ((), ())), preferred_element_type=F32)


def _rep_rows(a, k):
    return jnp.concatenate([a] * k, axis=0)


def _rep_lanes(a, k):
    return jnp.concatenate([a] * k, axis=1) if k > 1 else a


def _layer_norm(z, g, b):
    mu = jnp.mean(z, axis=-1, keepdims=True)
    d = z - mu
    var = jnp.mean(d * d, axis=-1, keepdims=True)
    return d * lax.rsqrt(var + LN_EPS) * g + b


def _rope(v, cos, sin):
    k = v.shape[1] // LANES
    up = pltpu.roll(v, v.shape[1] - ROT_HALF, 1)
    dn = pltpu.roll(v, ROT_HALF, 1)
    lane = lax.broadcasted_iota(jnp.int32, v.shape, 1)
    partner = jnp.where((lane & (NSA_HEAD_DIM - 1)) < ROT_HALF, up, dn)
    return v * _rep_lanes(cos, k) + partner * _rep_lanes(sin, k)


def _rope_angle_kernel(pos_ref, inv_ref, cos_o, sin_o):
    ang = inv_ref[...] * pos_ref[...]
    cos_o[...] = jnp.cos(ang)
    sin_o[...] = jnp.sin(ang)


def _rope_tables(pos_row):
    n = pos_row.shape[1]
    inv = (ROPE_THETA ** (-jnp.arange(ROT_HALF, dtype=F32) * 2.0 / ROT_DIM))[:, None]
    out = jax.ShapeDtypeStruct((ROT_HALF, n), F32)
    cos_t, sin_t = pl.pallas_call(
        _rope_angle_kernel,
        out_shape=[out, out],
        name="rope_angles",
    )(pos_row, inv)
    c, s = cos_t.T, sin_t.T
    rest = NSA_HEAD_DIM - ROT_DIM
    reps = LANES // NSA_HEAD_DIM
    cos = jnp.concatenate([c, c, jnp.ones((n, rest), F32)] * reps, axis=1)
    sin = jnp.concatenate([-s, s, jnp.zeros((n, rest), F32)] * reps, axis=1)
    return cos, sin


def _proj_even_kernel(x_ref, cos_ref, sin_ref, wp_ref, wq_ref, wkv_ref, wg_ref, poolw_ref, pscale_ref,
                      pool_o, q_o, kc_o, vc_o, ks_o, vs_o, kw_o, vw_o, gate_o, pbuf):
    i = pl.program_id(1)
    tm = x_ref.shape[0]
    half = tm // 2

    @pl.when(i == 0)
    def _():
        pbuf[0:POOL_HALO, :] = jnp.zeros((POOL_HALO, POOL_WIDTH), F32)

    def project(h0):
        rs = slice(h0, h0 + half)
        xb = x_ref[rs, :].astype(BF16)
        pbuf[POOL_HALO + h0:POOL_HALO + h0 + half, :] = _dot(xb, wp_ref[...])
        return _dot(xb, wq_ref[...]), _dot(xb, wkv_ref[...]), _dot(xb, wg_ref[...])

    def emit(h0, q, kv, gl):
        rs = slice(h0, h0 + half)
        cos = cos_ref[rs, :]
        sin = sin_ref[rs, :]
        q_o[rs, :] = _rope(q, cos, sin).astype(BF16)
        kc_o[rs, :] = kv[:, 0 * LANES:1 * LANES]
        vc_o[rs, :] = kv[:, 1 * LANES:2 * LANES]
        ks_o[rs, :] = _rope(kv[:, 2 * LANES:3 * LANES], cos, sin).astype(BF16)
        vs_o[rs, :] = kv[:, 3 * LANES:4 * LANES].astype(BF16)
        kw_o[rs, :] = _rope(kv[:, 4 * LANES:5 * LANES], cos, sin).astype(BF16)
        vw_o[rs, :] = kv[:, 5 * LANES:6 * LANES].astype(BF16)
        gate_o[rs, :] = _sigmoid_tanh(gl)

    def pool(h0):
        rs = slice(h0, h0 + half)
        t1 = i * tm + h0 + lax.broadcasted_iota(jnp.int32, (half, 1), 0) + 1
        for g, w in enumerate(POOL_WINDOWS):
            sl = slice(g * POOL_GROUP, (g + 1) * POOL_GROUP)
            u = pbuf[POOL_HALO + h0:POOL_HALO + h0 + half, sl]
            tot = u
            for j in range(1, w):
                tot = tot + pbuf[pl.ds(POOL_HALO + h0 - j, half), sl]
            cnt = jnp.minimum(t1, w).astype(F32)
            pooled = tot / cnt - u
            mixed = _dot(pooled.astype(BF16), poolw_ref[g]) * pscale_ref[:, sl]
            pool_o[rs, sl] = mixed.astype(BF16)

    first = project(0)
    second = project(half)
    emit(0, *first)
    pool(0)
    emit(half, *second)
    pool(half)
    pbuf[0:POOL_HALO, :] = pbuf[tm:tm + POOL_HALO, :]


def _proj_even(xf, cos, sin, wp, wq, wkv, wg, poolw, pscale, batch, seq):
    n = xf.shape[0]
    tm = ROW_TILE
    nt = seq // tm
    row = lambda b, i: (b * nt + i, 0)
    full = lambda b, i: (0, 0)
    outs = [
        jax.ShapeDtypeStruct((n, POOL_WIDTH), BF16),
        jax.ShapeDtypeStruct((n, Q_WIDTH), BF16),
        jax.ShapeDtypeStruct((n, KV_WIDTH), F32),
        jax.ShapeDtypeStruct((n, KV_WIDTH), F32),
        jax.ShapeDtypeStruct((n, KV_WIDTH), BF16),
        jax.ShapeDtypeStruct((n, KV_WIDTH), BF16),
        jax.ShapeDtypeStruct((n, KV_WIDTH), BF16),
        jax.ShapeDtypeStruct((n, KV_WIDTH), BF16),
        jax.ShapeDtypeStruct((n, LANES), F32),
    ]
    return pl.pallas_call(
        _proj_even_kernel,
        grid=(batch, nt),
        in_specs=[
            pl.BlockSpec((tm, D_MODEL), row),
            pl.BlockSpec((tm, LANES), row),
            pl.BlockSpec((tm, LANES), row),
            pl.BlockSpec(wp.shape, full),
            pl.BlockSpec(wq.shape, full),
            pl.BlockSpec(wkv.shape, full),
            pl.BlockSpec(wg.shape, full),
            pl.BlockSpec(poolw.shape, lambda b, i: (0, 0, 0)),
            pl.BlockSpec(pscale.shape, full),
        ],
        out_specs=[pl.BlockSpec((tm, o.shape[1]), row) for o in outs],
        out_shape=outs,
        scratch_shapes=[pltpu.VMEM((POOL_HALO + tm, POOL_WIDTH), F32)],
        compiler_params=_params(2),
        name="proj_even",
    )(xf, cos, sin, wp, wq, wkv, wg, poolw, pscale)


def _compress_kernel(rk_ref, rv_ref, cos_ref, sin_ref, pk1_ref, pk2_ref, pv1_ref, pv2_ref,
                     wk1_ref, wk2_ref, wv1_ref, wv2_ref, kc_o, vc_o):
    nc = kc_o.shape[0]

    def compress(raw_ref, p1, p2, w1, w2):
        r = jnp.concatenate([raw_ref[pl.ds(j, nc, stride=CMP_STRIDE), :] for j in range(CMP_STRIDE)], axis=1)
        a = _dot((r + p1[...]).astype(BF16), w1[...])
        b = _dot((r + p2[...]).astype(BF16), w2[...])
        return a + pltpu.roll(b, nc - 1, 0)

    kc = compress(rk_ref, pk1_ref, pk2_ref, wk1_ref, wk2_ref)
    kc_o[...] = _rope(kc, cos_ref[...], sin_ref[...]).astype(BF16)
    vc_o[...] = compress(rv_ref, pv1_ref, pv2_ref, wv1_ref, wv2_ref).astype(BF16)


def _compress(rk, rv, cos, sin, consts):
    batch, seq, width = rk.shape
    nc = seq // CMP_STRIDE
    blk = lambda b: (b, 0, 0)
    full = lambda b: (0, 0)
    out = jax.ShapeDtypeStruct((batch, nc, KV_WIDTH), BF16)
    return pl.pallas_call(
        _compress_kernel,
        grid=(batch,),
        in_specs=[pl.BlockSpec((None, seq, width), blk), pl.BlockSpec((None, seq, width), blk),
                  pl.BlockSpec((None, nc, LANES), blk), pl.BlockSpec((None, nc, LANES), blk)]
                 + [pl.BlockSpec(a.shape, full) for a in consts],
        out_specs=[pl.BlockSpec((None, nc, KV_WIDTH), blk)] * 2,
        out_shape=[out, out],
        compiler_params=_params(1),
        name="compress",
    )(rk, rv, cos, sin, *consts)


def _softmax_rows(s):
    m = jnp.max(s, axis=1, keepdims=True)
    e = jnp.exp2(s - m)
    return e, jnp.sum(e, axis=1, keepdims=True)


def _dot_exact01(x, m01):
    hi = x.astype(BF16)
    r1 = x - hi.astype(F32)
    mid = r1.astype(BF16)
    lo = (r1 - mid.astype(F32)).astype(BF16)
    return _dot(hi, m01) + _dot(mid, m01) + _dot(lo, m01)


def _unselected_blocks(imp, q0, n_pick):
    nq, nb = imp.shape
    col = lax.broadcasted_iota(jnp.int32, (nq, nb), 1)
    cur = lax.shift_right_arithmetic(q0 + lax.broadcasted_iota(jnp.int32, (nq, nb), 0), 6)
    forced = (col == 0) | (col == cur) | (col == cur - 1)
    val = jnp.where(forced, -jnp.inf, jnp.where(col > cur, -1.0, imp))
    vt = val.T
    blk = lax.broadcasted_iota(jnp.int32, (nb, nq), 0).astype(F32)
    unsel = jnp.where(forced, 0.0, 1.0).T
    for _ in range(n_pick - N_FORCED):
        m = jnp.max(vt, axis=0, keepdims=True)
        first = jnp.min(jnp.where(vt == m, blk, float(nb)), axis=0, keepdims=True)
        hit = blk == first
        unsel = jnp.where(hit, 0.0, unsel)
        vt = jnp.where(hit, -jnp.inf, vt)
    return unsel.T


def _nsa_kernel(q_ref, gate_ref, kc_ref, vc_ref, ks_ref, vs_ref, kw_ref, vw_ref, pt_ref, ov_ref,
                o_ref, m_scr, l_scr, acc_scr, lhs_scr, oc_scr, ow_scr, s0_scr, s1_scr, w_scr, gx_scr, *, n_pick):
    qb = pl.program_id(1)
    q0 = qb * Q_BLOCK
    tk = SEL_KEY_TILE
    ncmp = kc_ref.shape[0]
    wk = WINDOW + Q_BLOCK
    rows = NSA_GROUP * Q_BLOCK
    lane = lax.broadcasted_iota(jnp.int32, (Q_BLOCK, LANES), 1)

    def trow(width):
        return q0 + lax.broadcasted_iota(jnp.int32, (Q_BLOCK, width), 0)

    def kcol(width):
        return lax.broadcasted_iota(jnp.int32, (Q_BLOCK, width), 1)

    bias_c = jnp.where(kcol(ncmp) * CMP_STRIDE + (CMP_BLOCK - 1) <= trow(ncmp), 0.0, NEG)
    has_cmp = (trow(1) >= CMP_BLOCK - 1).astype(F32)
    wstart = pl.multiple_of(jnp.maximum(q0 - WINDOW, 0), Q_BLOCK)
    kpos_w = wstart + kcol(wk)
    bias_w = jnp.where(kpos_w <= trow(wk), jnp.where(kpos_w > trow(wk) - WINDOW, 0.0, NEG), NEG)

    for g in range(NSA_KV_HEADS):
        mine = (lane >= NSA_HEAD_DIM) if g else (lane < NSA_HEAD_DIM)
        lhs_scr[g, :, 0:LANES] = jnp.concatenate(
            [jnp.where(mine, q_ref[:, r * LANES:(r + 1) * LANES], jnp.zeros((), BF16))
             for r in range(NSA_GROUP)], axis=0)
        m_scr[g] = jnp.full((rows, LANES), -jnp.inf, F32)
        l_scr[g] = jnp.zeros((rows, LANES), F32)
        acc_scr[g] = jnp.zeros((rows, LANES), F32)

    def cmp_scores(g):
        s0_scr[g] = _dot_nt(lhs_scr[g, :, 0:LANES], kc_ref[...])

    def cmp_attend(g):
        e, l = _softmax_rows(s0_scr[g] + _rep_rows(bias_c, NSA_GROUP))
        p = e * (_rep_rows(has_cmp, NSA_GROUP) / l)
        oc_scr[g] = _dot(p.astype(BF16), vc_ref[...])
        psum = p[0:Q_BLOCK]
        for r in range(1, NSA_GROUP):
            psum = psum + p[r * Q_BLOCK:(r + 1) * Q_BLOCK]
        return _dot_exact01(psum, ov_ref[...])

    def select(g, imp):
        unsel = _unselected_blocks(imp, q0, n_pick)
        lhs_scr[g, :, LANES:2 * LANES] = _rep_rows(unsel.astype(BF16), NSA_GROUP)

    def win_scores(g):
        w_scr[g] = _dot_nt(lhs_scr[g, :, 0:LANES], kw_ref[pl.ds(wstart, wk), :])

    def win_attend(g):
        e, l = _softmax_rows(w_scr[g] + _rep_rows(bias_w, NSA_GROUP))
        ow_scr[g] = _dot(e.astype(BF16), vw_ref[pl.ds(wstart, wk), :]) / l

    def expand_gates():
        low = lane < NSA_HEAD_DIM
        for r in range(NSA_GROUP):
            for br in range(N_BRANCH):
                c_lo = r * N_BRANCH + br
                c_hi = (NSA_GROUP + r) * N_BRANCH + br
                gx_scr[c_lo] = jnp.where(low, gate_ref[:, c_lo:c_lo + 1], gate_ref[:, c_hi:c_hi + 1])

    cmp_scores(0)
    cmp_scores(1)
    expand_gates()
    win_scores(0)
    imp0 = cmp_attend(0)
    win_scores(1)
    imp1 = cmp_attend(1)
    select(0, imp0)
    win_attend(0)
    select(1, imp1)
    win_attend(1)

    def scores(kt, buf, g):
        k0 = pl.multiple_of(kt * tk, tk)
        rhs = jnp.concatenate([ks_ref[pl.ds(k0, tk), :], pt_ref[pl.ds(k0, tk), :]], axis=1)
        buf[g] = _dot_nt(lhs_scr[g], rhs)

    def consume(kt, buf, g, causal):
        k0 = pl.multiple_of(kt * tk, tk)
        v = vs_ref[pl.ds(k0, tk), :]
        bias = jnp.where(k0 + kcol(tk) <= trow(tk), 0.0, NEG) if causal else None
        for r in range(NSA_GROUP):
            rs = slice(r * Q_BLOCK, (r + 1) * Q_BLOCK)
            s = buf[g, rs, :]
            if causal:
                s = s + bias
            m_prev = m_scr[g, rs, :]
            m_new = jnp.maximum(m_prev, jnp.max(s, axis=1, keepdims=True))
            alpha = jnp.exp2(m_prev - m_new)
            pe = jnp.exp2(s - _rep_lanes(m_new, tk // LANES))
            part = pe[:, 0:LANES]
            for c in range(1, tk // LANES):
                part = part + pe[:, c * LANES:(c + 1) * LANES]
            l_scr[g, rs, :] = alpha * l_scr[g, rs, :] + part
            acc_scr[g, rs, :] = alpha * acc_scr[g, rs, :] + _dot(pe.astype(BF16), v)
            m_scr[g, rs, :] = m_new

    def step(nxt, nxt_buf, cur, cur_buf, causal=False):
        for g in range(NSA_KV_HEADS):
            if nxt is not None:
                scores(nxt, nxt_buf, g)
        for g in range(NSA_KV_HEADS):
            consume(cur, cur_buf, g, causal)

    n_full = q0 // tk
    for g in range(NSA_KV_HEADS):
        scores(0, s0_scr, g)

    def pair(j, carry):
        step(2 * j + 1, s1_scr, 2 * j, s0_scr)
        step(2 * j + 2, s0_scr, 2 * j + 1, s1_scr)
        return carry

    lax.fori_loop(0, n_full // 2, pair, 0)

    @pl.when(n_full % 2 == 0)
    def _():
        step(None, None, n_full, s0_scr, causal=True)

    @pl.when(n_full % 2 == 1)
    def _():
        step(n_full, s1_scr, n_full - 1, s0_scr)
        step(None, None, n_full, s1_scr, causal=True)

    low = lane < NSA_HEAD_DIM
    for r in range(NSA_GROUP):
        rs = slice(r * Q_BLOCK, (r + 1) * Q_BLOCK)
        merged = lambda scr: jnp.where(low, scr[0, rs, :], scr[1, rs, :])
        l_sel = jnp.where(low, jnp.sum(l_scr[0, rs, :], axis=1, keepdims=True),
                          jnp.sum(l_scr[1, rs, :], axis=1, keepdims=True))
        out = (gx_scr[r * N_BRANCH] * merged(oc_scr)
               + gx_scr[r * N_BRANCH + 1] * (merged(acc_scr) / l_sel)
               + gx_scr[r * N_BRANCH + 2] * merged(ow_scr))
        o_ref[:, r * LANES:(r + 1) * LANES] = out.astype(BF16)


def _nsa(q, gates, kc, vc, ks, vs, kw, vw, pt, ov, batch, seq):
    n = q.shape[0]
    nqb = seq // Q_BLOCK
    rows = NSA_GROUP * Q_BLOCK
    row = lambda b, i: (b * nqb + i, 0)
    per_b = lambda b, i: (b, 0, 0)
    full = lambda b, i: (0, 0)
    seq_spec = _resident((None, seq, KV_WIDTH), per_b)
    cmp_spec = _resident((None, kc.shape[1], KV_WIDTH), per_b)
    n_pick = min(SEL_COUNT, seq // SEL_BLOCK)
    return pl.pallas_call(
        functools.partial(_nsa_kernel, n_pick=n_pick),
        grid=(batch, nqb),
        in_specs=[pl.BlockSpec((Q_BLOCK, Q_WIDTH), row), pl.BlockSpec((Q_BLOCK, LANES), row),
                  cmp_spec, cmp_spec, seq_spec, seq_spec, seq_spec, seq_spec,
                  _resident(pt.shape, full), _resident(ov.shape, full)],
        out_specs=pl.BlockSpec((Q_BLOCK, Q_WIDTH), row),
        out_shape=jax.ShapeDtypeStruct((n, Q_WIDTH), BF16),
        scratch_shapes=[pltpu.VMEM((NSA_KV_HEADS, rows, LANES), F32),
                        pltpu.VMEM((NSA_KV_HEADS, rows, LANES), F32),
                        pltpu.VMEM((NSA_KV_HEADS, rows, LANES), F32),
                        pltpu.VMEM((NSA_KV_HEADS, rows, 2 * LANES), BF16),
                        pltpu.VMEM((NSA_KV_HEADS, rows, LANES), F32),
                        pltpu.VMEM((NSA_KV_HEADS, rows, LANES), F32),
                        pltpu.VMEM((NSA_KV_HEADS, rows, SEL_KEY_TILE), F32),
                        pltpu.VMEM((NSA_KV_HEADS, rows, SEL_KEY_TILE), F32),
                        pltpu.VMEM((NSA_KV_HEADS, rows, WINDOW + Q_BLOCK), F32),
                        pltpu.VMEM((NSA_GROUP * N_BRANCH, Q_BLOCK, LANES), F32)],
        compiler_params=_params(2),
        name="nsa",
    )(q, gates, kc, vc, ks, vs, kw, vw, pt, ov)


def _matmul_kernel(a_ref, w_ref, o_ref):
    o_ref[...] = _dot(a_ref[...].astype(BF16), w_ref[...]).astype(o_ref.dtype)


def _mem_kv(memf, wkv_all, layer):
    m, k = memf.shape
    n = wkv_all.shape[2]
    return pl.pallas_call(
        _matmul_kernel,
        grid=(1,),
        in_specs=[pl.BlockSpec((m, k), lambda i: (0, 0)), pl.BlockSpec((None, k, n), lambda i: (layer, 0, 0))],
        out_specs=pl.BlockSpec((m, n), lambda i: (0, 0)),
        out_shape=jax.ShapeDtypeStruct((m, n), BF16),
        compiler_params=_params(1),
        name="mem_kv",
    )(memf, wkv_all)


def _tail_kernel(*refs, n_parts):
    x_ref = refs[0]
    parts = refs[1:1 + n_parts]
    ws = refs[1 + n_parts:1 + 2 * n_parts]
    wq_ref, k_ref, v_ref, wo_ref, wu_ref, wd_ref, g_ref, b_ref, o_ref, x1_scr, x2_scr = refs[1 + 2 * n_parts:]
    tm = x_ref.shape[0]
    half = tm // 2
    halves = (slice(0, half), slice(half, tm))
    x1 = x1_scr if n_parts else x_ref

    def norm(k, z):
        return _layer_norm(z, g_ref[k:k + 1, :], b_ref[k:k + 1, :])

    def mixer_out(rs):
        z = ALPHA * x_ref[rs, :]
        for a, w in zip(parts, ws):
            z = z + _dot(a[rs, :], w[...])
        x1_scr[rs, :] = norm(0, z)

    def project(rs):
        return _dot(x1[rs, :].astype(BF16), wq_ref[...]).astype(BF16)

    def attend(q):
        outs = []
        for h in range(X_HEADS):
            sl = slice(h * X_HEAD_DIM, (h + 1) * X_HEAD_DIM)
            e, l = _softmax_rows(_dot_nt(q[:, sl], k_ref[:, sl]))
            outs.append(_dot((e / l).astype(BF16), v_ref[:, sl]))
        return jnp.concatenate(outs, axis=1).astype(BF16)

    def finish(rs, o):
        x2_scr[rs, :] = norm(1, ALPHA * x1[rs, :] + _dot(o, wo_ref[...]))

    if n_parts:
        mixer_out(halves[0])
        mixer_out(halves[1])
    qa = project(halves[0])
    qb = project(halves[1])
    oa = attend(qa)
    ob = attend(qb)
    finish(halves[0], oa)
    finish(halves[1], ob)

    x2 = x2_scr[...]
    xb = x2.astype(BF16)
    z = ALPHA * x2
    c0 = 0
    for width in FFN_CHUNKS:
        h1 = _dot(xb, wu_ref[:, c0:c0 + width])
        h2 = _dot(xb, wu_ref[:, FFN_HIDDEN + c0:FFN_HIDDEN + c0 + width])
        act = (h1 * _sigmoid_tanh(h1) * h2).astype(BF16)
        z = z + _dot(act, wd_ref[c0:c0 + width, :])
        c0 += width
    o_ref[...] = norm(2, z)


def _tail(xf, parts, ws, kv, wq_all, wo_all, wu_all, wd_all, g_all, b_all, layer, batch, seq):
    n = xf.shape[0]
    tm = ROW_TILE
    nt = seq // tm
    n_mem = kv.shape[1]
    row = lambda bi, i: (bi * nt + i, 0)
    full = lambda bi, i: (0, 0)
    this_layer = lambda bi, i: (layer, 0, 0)
    stacked = lambda a: _resident((None,) + a.shape[1:], this_layer)
    return pl.pallas_call(
        functools.partial(_tail_kernel, n_parts=len(parts)),
        grid=(batch, nt),
        in_specs=[pl.BlockSpec((tm, D_MODEL), row)]
                 + [pl.BlockSpec((tm, a.shape[1]), row) for a in parts]
                 + [_resident(w.shape, full) for w in ws]
                 + [stacked(wq_all),
                    _resident((None, n_mem, D_MODEL), lambda bi, i: (bi, 0, 0)),
                    _resident((None, n_mem, D_MODEL), lambda bi, i: (bi, 0, 1)),
                    stacked(wo_all), stacked(wu_all), stacked(wd_all), stacked(g_all), stacked(b_all)],
        out_specs=pl.BlockSpec((tm, D_MODEL), row),
        out_shape=jax.ShapeDtypeStruct((n, D_MODEL), F32),
        scratch_shapes=[pltpu.VMEM((tm, D_MODEL), F32), pltpu.VMEM((tm, D_MODEL), F32)],
        compiler_params=_params(2),
        name="tail",
    )(xf, *parts, *ws, wq_all, kv, kv, wo_all, wu_all, wd_all, g_all, b_all)


def _gelu_tanh(x):
    inner = x * (GELU_C0 + GELU_C1 * (x * x))
    return x * (0.5 * jnp.tanh(inner) + 0.5)


def _sigmoid_tanh(z):
    return 0.5 * jnp.tanh(0.5 * z) + 0.5


def _odd_kernel(x_ref, pos_ref, win_ref, cw_ref, cb_ref, wax_ref, ba_ref, bx_ref, lam_ref, wout_ref,
                g_ref, b_ref, o_ref, xbuf, gate_scr, a_scr, b_scr, h_scr):
    i = pl.program_id(1)
    tm = x_ref.shape[0]
    half = tm // 2
    groups = half // SUBLANES

    @pl.when(i == 0)
    def _():
        xbuf[0:CONV_HALO, :] = jnp.zeros((CONV_HALO, RNN_WIDTH), F32)
        h_scr[...] = jnp.zeros((SUBLANES, RNN_WIDTH), F32)

    in_group = lax.broadcasted_iota(jnp.int32, (groups, SUBLANES, RNN_BLOCK_W), 1)
    lam = lam_ref[...]
    log_a_scale = -LRU_C * (jnp.maximum(-lam, 0.0) + jnp.log1p(jnp.exp(-jnp.abs(lam))))

    def in_proj(h0):
        xb = x_ref[h0:h0 + half, :].astype(BF16)
        gate_scr[h0:h0 + half, :] = _dot(xb, win_ref[:, :RNN_WIDTH])
        xbuf[CONV_HALO + h0:CONV_HALO + h0 + half, :] = _dot(xb, win_ref[:, RNN_WIDTH:])

    def gates(h0):
        reset = pos_ref[h0:h0 + half, :] == 0
        for h in range(RNN_BLOCKS):
            sl = slice(h * RNN_BLOCK_W, (h + 1) * RNN_BLOCK_W)
            xc = cb_ref[:, sl]
            for k in range(CONV_WIDTH):
                xc = xc + xbuf[pl.ds(CONV_HALO - (CONV_WIDTH - 1) + k + h0, half), sl] * cw_ref[k:k + 1, sl]
            ri = _dot(xc.astype(BF16), wax_ref[h])
            r = _sigmoid_tanh(ri[:, :RNN_BLOCK_W] + ba_ref[:, sl])
            ig = _sigmoid_tanh(ri[:, RNN_BLOCK_W:] + bx_ref[:, sl])
            log_a = r * log_a_scale[:, sl]
            a = jnp.where(reset, 0.0, jnp.exp(log_a))
            th = jnp.tanh(log_a)
            m2 = -2.0 * th / (1.0 - th)
            mult = jnp.where(reset, 1.0, jnp.where(m2 > 0.0, m2 * lax.rsqrt(m2), 0.0))
            b = mult * (ig * xc)
            a3 = a.reshape(groups, SUBLANES, RNN_BLOCK_W)
            b3 = b.reshape(groups, SUBLANES, RNN_BLOCK_W)
            for d in (1, 2, 4):
                a_prev = pltpu.roll(a3, d, 1)
                b_prev = pltpu.roll(b3, d, 1)
                ok = in_group >= d
                b3 = jnp.where(ok, a3 * b_prev + b3, b3)
                a3 = jnp.where(ok, a3 * a_prev, a3)
            a_scr[h0:h0 + half, sl] = a3.reshape(half, RNN_BLOCK_W)
            b_scr[h0:h0 + half, sl] = b3.reshape(half, RNN_BLOCK_W)

    def recur(h0, h_prev):
        for j in range(groups):
            rs = slice(h0 + j * SUBLANES, h0 + (j + 1) * SUBLANES)
            hh = a_scr[rs, :] * h_prev + b_scr[rs, :]
            b_scr[rs, :] = hh
            h_prev = jnp.broadcast_to(hh[SUBLANES - 1:SUBLANES, :], (SUBLANES, RNN_WIDTH))
        return h_prev

    def out_proj(h0):
        rs = slice(h0, h0 + half)
        y = (b_scr[rs, :] * _gelu_tanh(gate_scr[rs, :])).astype(BF16)
        z = ALPHA * x_ref[rs, :] + _dot(y, wout_ref[...])
        o_ref[rs, :] = _layer_norm(z, g_ref[...], b_ref[...])

    in_proj(0)
    in_proj(half)
    gates(0)
    h_mid = recur(0, h_scr[...])
    gates(half)
    out_proj(0)
    h_scr[...] = recur(half, h_mid)
    out_proj(half)
    xbuf[0:CONV_HALO, :] = xbuf[tm:tm + CONV_HALO, :]


def _odd_layer(xf, posi, w_in, cw, cb, wax, ba, bx, lam, w_out, g, b, batch, seq):
    n = xf.shape[0]
    tm = ROW_TILE
    nt = seq // tm
    row = lambda bi, i: (bi * nt + i, 0)
    full = lambda bi, i: (0, 0)
    consts = [w_in, cw, cb]
    rest = [ba, bx, lam, w_out, g, b]
    return pl.pallas_call(
        _odd_kernel,
        grid=(batch, nt),
        in_specs=[pl.BlockSpec((tm, D_MODEL), row), pl.BlockSpec((tm, 1), row)]
                 + [pl.BlockSpec(a.shape, full) for a in consts]
                 + [pl.BlockSpec(wax.shape, lambda bi, i: (0, 0, 0))]
                 + [pl.BlockSpec(a.shape, full) for a in rest],
        out_specs=pl.BlockSpec((tm, D_MODEL), row),
        out_shape=jax.ShapeDtypeStruct((n, D_MODEL), F32),
        scratch_shapes=[pltpu.VMEM((CONV_HALO + tm, RNN_WIDTH), F32),
                        pltpu.VMEM((tm, RNN_WIDTH), F32),
                        pltpu.VMEM((tm, RNN_WIDTH), F32),
                        pltpu.VMEM((tm, RNN_WIDTH), F32),
                        pltpu.VMEM((SUBLANES, RNN_WIDTH), F32)],
        compiler_params=_params(2),
        name="odd_layer",
    )(xf, posi, *consts, wax, *rest)


def _even_mixer(xf, positions, w_in, pool_w, pool_scale, cmp_pos_k, cmp_pos_v, cmp_wk, cmp_wv,
                w_out, batch, seq):
    assert seq // SEL_BLOCK == LANES and seq % SEL_KEY_TILE == 0 and seq >= WINDOW + Q_BLOCK
    hd, g, r = NSA_HEAD_DIM, NSA_KV_HEADS, NSA_GROUP
    c0 = POOL_WIDTH
    c1 = c0 + Q_WIDTH
    c2 = c1 + GATE_WIDTH
    wp = w_in[:, :c0].astype(BF16)
    wq = (w_in[:, c0:c1].reshape(D_MODEL, g, r, hd).transpose(0, 2, 1, 3).reshape(D_MODEL, Q_WIDTH)
          * (hd ** -0.5 * LOG2E)).astype(BF16)
    wg = jnp.pad(w_in[:, c1:c2], ((0, 0), (0, LANES - GATE_WIDTH))).astype(BF16)
    wkv = w_in[:, c2:].astype(BF16)
    n = batch * seq
    nc = seq // CMP_STRIDE
    n_cmp = (seq - CMP_BLOCK) // CMP_STRIDE + 1
    posc = jnp.pad(positions[:, CMP_BLOCK - 1::CMP_STRIDE][:, :n_cmp], ((0, 0), (0, nc - n_cmp)))
    pos_all = jnp.concatenate([positions.reshape(1, n), posc.reshape(1, batch * nc)], axis=1).astype(F32)
    cos_all, sin_all = _rope_tables(pos_all)
    pool, q, kc_raw, vc_raw, ks, vs, kw, vw, gates = _proj_even(
        xf, cos_all, sin_all, wp, wq, wkv, wg, pool_w.astype(BF16), pool_scale[None, :], batch, seq)

    def halves(w):
        w3 = w.reshape(CMP_BLOCK, hd, hd).astype(BF16)
        w4 = jnp.zeros((CMP_BLOCK, g * hd, g * hd), BF16)
        for k in range(g):
            w4 = w4.at[:, k * hd:(k + 1) * hd, k * hd:(k + 1) * hd].set(w3)
        return (w4[:CMP_STRIDE].reshape(CMP_STRIDE * g * hd, g * hd),
                w4[CMP_STRIDE:].reshape(CMP_STRIDE * g * hd, g * hd))

    def pos_halves(p):
        p2 = jnp.tile(p[:, None, :], (1, g, 1)).reshape(CMP_BLOCK, g * hd)
        return p2[:CMP_STRIDE].reshape(1, -1), p2[CMP_STRIDE:].reshape(1, -1)

    pk1, pk2 = pos_halves(cmp_pos_k)
    pv1, pv2 = pos_halves(cmp_pos_v)
    wk1, wk2 = halves(cmp_wk)
    wv1, wv2 = halves(cmp_wv)
    seq3 = lambda a: a.reshape(batch, seq, KV_WIDTH)
    kc, vc = _compress(seq3(kc_raw), seq3(vc_raw), cos_all[n:].reshape(batch, nc, LANES),
                       sin_all[n:].reshape(batch, nc, LANES), (pk1, pk2, pv1, pv2, wk1, wk2, wv1, wv2))

    n_sb = seq // SEL_BLOCK
    starts = np.arange(nc) * CMP_STRIDE
    jb = np.arange(n_sb)
    ov = ((starts[:, None] < (jb[None, :] + 1) * SEL_BLOCK)
          & (starts[:, None] + CMP_BLOCK > jb[None, :] * SEL_BLOCK) & (np.arange(nc)[:, None] < n_cmp))
    ov = jnp.asarray(ov, BF16)
    pt = jnp.asarray(np.where(np.arange(seq)[:, None] // SEL_BLOCK == jb[None, :], NEG, 0.0), BF16)

    nsa = _nsa(q, gates, kc, vc, seq3(ks), seq3(vs), seq3(kw), seq3(vw), pt, ov, batch, seq)

    w_pool_out = w_out[:POOL_WIDTH].astype(BF16)
    w_nsa_out = (w_out[POOL_WIDTH:].reshape(g, r, hd, D_MODEL).transpose(1, 0, 2, 3)
                 .reshape(Q_WIDTH, D_MODEL).astype(BF16))
    return [pool, nsa], [w_pool_out, w_nsa_out]


def _odd_mixer(xf, posi, w_in, conv_w, conv_b, wa, ba, wx, bx, lam, w_out, ln_g, ln_b, batch, seq):
    wax = jnp.concatenate([wa, wx], axis=2).astype(BF16)
    return _odd_layer(xf, posi, w_in.astype(BF16), conv_w, conv_b[None, :], wax, ba[None, :], bx[None, :],
                      lam[None, :], w_out.astype(BF16), ln_g, ln_b, batch, seq)


def kernel(x, mem, positions, e_w_in, e_pool_w, e_pool_scale, e_cmp_pos_k, e_cmp_pos_v, e_cmp_wk, e_cmp_wv, e_w_out, o_w_in, o_conv_w, o_conv_b, o_wa, o_ba, o_wx, o_bx, o_lambda, o_w_out, x_wq, x_wkv, x_wo, f_w_up, f_w_down, ln_g, ln_b):
    batch, seq, d = x.shape
    n = batch * seq
    assert d == D_MODEL and seq % ROW_TILE == 0
    xf = x.reshape(n, d)
    posi = positions.reshape(n, 1)
    memf = mem.reshape(batch * mem.shape[1], d)
    wq_all = (x_wq * (X_HEAD_DIM ** -0.5 * LOG2E)).astype(BF16)
    wkv_all, wo_all = x_wkv.astype(BF16), x_wo.astype(BF16)
    wu_all, wd_all = f_w_up.astype(BF16), f_w_down.astype(BF16)
    for layer in range(DEPTH):
        j = layer // 2
        if layer % 2 == 0:
            parts, ws = _even_mixer(xf, positions, e_w_in[j], e_pool_w[j], e_pool_scale[j], e_cmp_pos_k[j],
                                    e_cmp_pos_v[j], e_cmp_wk[j], e_cmp_wv[j], e_w_out[j], batch, seq)
        else:
            xf = _odd_mixer(xf, posi, o_w_in[j], o_conv_w[j], o_conv_b[j], o_wa[j], o_ba[j], o_wx[j], o_bx[j],
                            o_lambda[j], o_w_out[j], ln_g[layer, 0][None, :], ln_b[layer, 0][None, :], batch, seq)
            parts, ws = [], []
        kv = _mem_kv(memf, wkv_all, layer).reshape(batch, mem.shape[1], 2 * d)
        xf = _tail(xf, parts, ws, kv, wq_all, wo_all, wu_all, wd_all, ln_g, ln_b, layer, batch, seq)
    return xf.reshape(batch, seq, d)
```

```python
import functools

import numpy as np
import jax
import jax.numpy as jnp
from jax import lax
from jax.experimental import pallas as pl
from jax.experimental.pallas import tpu as pltpu

F32 = jnp.float32
BF16 = jnp.bfloat16

D_MODEL = 1024
DEPTH = 2
ALPHA = (2.0 * DEPTH) ** 0.25
LN_EPS = 1e-5
NEG = -1e30
POOL_WIDTH = D_MODEL // 2
POOL_WINDOWS = (2, 4, 8, 16)
POOL_GROUP = POOL_WIDTH // len(POOL_WINDOWS)
POOL_HALO = 16
NSA_HEADS = 8
NSA_KV_HEADS = 2
NSA_HEAD_DIM = 64
NSA_GROUP = NSA_HEADS // NSA_KV_HEADS
CMP_BLOCK = 32
CMP_STRIDE = 16
SEL_BLOCK = 64
SEL_COUNT = 16
WINDOW = 512
Q_BLOCK = 256
N_BRANCH = 3
N_FORCED = 3
LOG2E = 1.4426950408889634
ROPE_THETA = 500000.0
ROT_DIM = NSA_HEAD_DIM // 4
ROT_HALF = ROT_DIM // 2
Q_WIDTH = NSA_HEADS * NSA_HEAD_DIM
KV_WIDTH = NSA_KV_HEADS * NSA_HEAD_DIM
GATE_WIDTH = NSA_HEADS * N_BRANCH
RNN_WIDTH = 1280
RNN_BLOCKS = 10
RNN_BLOCK_W = RNN_WIDTH // RNN_BLOCKS
CONV_WIDTH = 4
CONV_HALO = 8
LRU_C = 8.0
X_HEADS = 4
X_HEAD_DIM = D_MODEL // X_HEADS
FFN_HIDDEN = 2816
GELU_C0 = 0.7978845608028654
GELU_C1 = GELU_C0 * 0.044715

LANES = 128
SUBLANES = 8
BF16_ROWS = 16
VMEM_LIMIT = 56 * 1024 * 1024
ROW_TILE = 512
SEL_KEY_TILE = 512
FFN_CHUNKS = (1024, 1024, 768)


def _params(n_axes, vmem=VMEM_LIMIT):
    return pltpu.CompilerParams(dimension_semantics=("arbitrary",) * n_axes,
                                vmem_limit_bytes=vmem)


def _resident(shape, index_map):
    return pl.BlockSpec(shape, index_map, pipeline_mode=pl.Buffered(1))


def _dot(a, b):
    return jnp.dot(a, b, preferred_element_type=F32)


def _dot_nt(a, b):
    return lax.dot_general(a, b, (((1,), (1,)), ((), ())), preferred_element_type=F32)


def _rep_rows(a, k):
    return jnp.concatenate([a] * k, axis=0)


def _rep_lanes(a, k):
    return jnp.concatenate([a] * k, axis=1) if k > 1 else a


def _layer_norm(z, g, b):
    mu = jnp.mean(z, axis=-1, keepdims=True)
    d = z - mu
    var = jnp.mean(d * d, axis=-1, keepdims=True)
    return d * lax.rsqrt(var + LN_EPS) * g + b


def _rope(v, cos, sin):
    k = v.shape[1] // LANES
    up = pltpu.roll(v, v.shape[1] - ROT_HALF, 1)
    dn = pltpu.roll(v, ROT_HALF, 1)
    lane = lax.broadcasted_iota(jnp.int32, v.shape, 1)
    partner = jnp.where((lane & (NSA_HEAD_DIM - 1)) < ROT_HALF, up, dn)
    return v * _rep_lanes(cos, k) + partner * _rep_lanes(sin, k)


def _rope_angle_kernel(pos_ref, inv_ref, cos_o, sin_o):
    ang = inv_ref[...] * pos_ref[...]
    cos_o[...] = jnp.cos(ang)
    sin_o[...] = jnp.sin(ang)


def _rope_angles(pos_row):
    n = pos_row.shape[1]
    inv = (ROPE_THETA ** (-jnp.arange(ROT_HALF, dtype=F32) * 2.0 / ROT_DIM))[:, None]
    out = jax.ShapeDtypeStruct((ROT_HALF, n), F32)
    return pl.pallas_call(
        _rope_angle_kernel,
        out_shape=[out, out],
        name="rope_angles",
    )(pos_row, inv)


def _rope_spread():
    lane = np.arange(LANES) % NSA_HEAD_DIM
    f = np.arange(ROT_HALF)[:, None]
    lo = (lane[None, :] == f).astype(np.float32)
    hi = (lane[None, :] == f + ROT_HALF).astype(np.float32)
    return jnp.asarray(lo + hi, BF16), jnp.asarray(hi - lo, BF16)


def _rope_patterns(cos_t, sin_t, ec, es):
    def spread(t, e):
        hi = t.astype(BF16)
        r1 = t - hi.astype(F32)
        mid = r1.astype(BF16)
        lo = (r1 - mid.astype(F32)).astype(BF16)
        tn = (((0,), (0,)), ((), ()))
        return sum(lax.dot_general(p, e, tn, preferred_element_type=F32) for p in (hi, mid, lo))

    lane = lax.broadcasted_iota(jnp.int32, (1, LANES), 1)
    unrotated = jnp.where((lane & (NSA_HEAD_DIM - 1)) >= ROT_DIM, 1.0, 0.0)
    return spread(cos_t, ec) + unrotated, spread(sin_t, es)


def _proj_even_kernel(x_ref, cos_ref, sin_ref, ec_ref, es_ref, wp_ref, wq_ref, wkv_ref, wg_ref, poolw_ref, pscale_ref,
                      pool_o, q_o, kc_o, vc_o, ks_o, vs_o, kw_o, vw_o, gate_o, pbuf):
    i = pl.program_id(1)
    tm = x_ref.shape[0]
    half = tm // 2

    @pl.when(i == 0)
    def _():
        pbuf[0:POOL_HALO, :] = jnp.zeros((POOL_HALO, POOL_WIDTH), F32)

    def project(h0):
        rs = slice(h0, h0 + half)
        xb = x_ref[rs, :].astype(BF16)
        pbuf[POOL_HALO + h0:POOL_HALO + h0 + half, :] = _dot(xb, wp_ref[...])
        return _dot(xb, wq_ref[...]), _dot(xb, wkv_ref[...]), _dot(xb, wg_ref[...])

    def emit(h0, q, kv, gl):
        rs = slice(h0, h0 + half)
        cos, sin = _rope_patterns(cos_ref[:, rs], sin_ref[:, rs], ec_ref[...], es_ref[...])
        q_o[rs, :] = _rope(q, cos, sin).astype(BF16)
        kc_o[rs, :] = kv[:, 0 * LANES:1 * LANES]
        vc_o[rs, :] = kv[:, 1 * LANES:2 * LANES]
        ks_o[rs, :] = _rope(kv[:, 2 * LANES:3 * LANES], cos, sin).astype(BF16)
        vs_o[rs, :] = kv[:, 3 * LANES:4 * LANES].astype(BF16)
        kw_o[rs, :] = _rope(kv[:, 4 * LANES:5 * LANES], cos, sin).astype(BF16)
        vw_o[rs, :] = kv[:, 5 * LANES:6 * LANES].astype(BF16)
        gate_o[rs, :] = _sigmoid_tanh(gl)

    def pool(h0):
        rs = slice(h0, h0 + half)
        t1 = i * tm + h0 + lax.broadcasted_iota(jnp.int32, (half, 1), 0) + 1
        for g, w in enumerate(POOL_WINDOWS):
            sl = slice(g * POOL_GROUP, (g + 1) * POOL_GROUP)
            u = pbuf[POOL_HALO + h0:POOL_HALO + h0 + half, sl]
            tot = u
            for j in range(1, w):
                tot = tot + pbuf[pl.ds(POOL_HALO + h0 - j, half), sl]
            cnt = jnp.minimum(t1, w).astype(F32)
            pooled = tot / cnt - u
            mixed = _dot(pooled.astype(BF16), poolw_ref[g]) * pscale_ref[:, sl]
            pool_o[rs, sl] = mixed.astype(BF16)

    first = project(0)
    second = project(half)
    emit(0, *first)
    pool(0)
    emit(half, *second)
    pool(half)
    pbuf[0:POOL_HALO, :] = pbuf[tm:tm + POOL_HALO, :]


def _proj_even(xf, cos_t, sin_t, ec, es, wp, wq, wkv, wg, poolw, pscale, batch, seq):
    n = xf.shape[0]
    tm = ROW_TILE
    nt = seq // tm
    row = lambda b, i: (b * nt + i, 0)
    full = lambda b, i: (0, 0)
    outs = [
        jax.ShapeDtypeStruct((n, POOL_WIDTH), BF16),
        jax.ShapeDtypeStruct((n, Q_WIDTH), BF16),
        jax.ShapeDtypeStruct((n, KV_WIDTH), F32),
        jax.ShapeDtypeStruct((n, KV_WIDTH), F32),
        jax.ShapeDtypeStruct((n, KV_WIDTH), BF16),
        jax.ShapeDtypeStruct((n, KV_WIDTH), BF16),
        jax.ShapeDtypeStruct((n, KV_WIDTH), BF16),
        jax.ShapeDtypeStruct((n, KV_WIDTH), BF16),
        jax.ShapeDtypeStruct((n, LANES), F32),
    ]
    return pl.pallas_call(
        _proj_even_kernel,
        grid=(batch, nt),
        in_specs=[
            pl.BlockSpec((tm, D_MODEL), row),
            pl.BlockSpec((ROT_HALF, tm), lambda b, i: (0, b * nt + i)),
            pl.BlockSpec((ROT_HALF, tm), lambda b, i: (0, b * nt + i)),
            pl.BlockSpec(ec.shape, full),
            pl.BlockSpec(es.shape, full),
            pl.BlockSpec(wp.shape, full),
            pl.BlockSpec(wq.shape, full),
            pl.BlockSpec(wkv.shape, full),
            pl.BlockSpec(wg.shape, full),
            pl.BlockSpec(poolw.shape, lambda b, i: (0, 0, 0)),
            pl.BlockSpec(pscale.shape, full),
        ],
        out_specs=[pl.BlockSpec((tm, o.shape[1]), row) for o in outs],
        out_shape=outs,
        scratch_shapes=[pltpu.VMEM((POOL_HALO + tm, POOL_WIDTH), F32)],
        compiler_params=_params(2),
        name="proj_even",
    )(xf, cos_t, sin_t, ec, es, wp, wq, wkv, wg, poolw, pscale)


def _compress_kernel(rk_ref, rv_ref, cos_ref, sin_ref, ec_ref, es_ref, pk1_ref, pk2_ref, pv1_ref, pv2_ref,
                     wk1_ref, wk2_ref, wv1_ref, wv2_ref, kc_o, vc_o):
    nc = kc_o.shape[0]

    def compress(raw_ref, p1, p2, w1, w2):
        r = jnp.concatenate([raw_ref[pl.ds(j, nc, stride=CMP_STRIDE), :] for j in range(CMP_STRIDE)], axis=1)
        a = _dot((r + p1[...]).astype(BF16), w1[...])
        b = _dot((r + p2[...]).astype(BF16), w2[...])
        return a + pltpu.roll(b, nc - 1, 0)

    kc = compress(rk_ref, pk1_ref, pk2_ref, wk1_ref, wk2_ref)
    cos, sin = _rope_patterns(cos_ref[...], sin_ref[...], ec_ref[...], es_ref[...])
    kc_o[...] = _rope(kc, cos, sin).astype(BF16)
    vc_o[...] = compress(rv_ref, pv1_ref, pv2_ref, wv1_ref, wv2_ref).astype(BF16)


def _compress(rk, rv, cos_t, sin_t, first_col_block, consts):
    batch, seq, width = rk.shape
    nc = seq // CMP_STRIDE
    blk = lambda b: (b, 0, 0)
    full = lambda b: (0, 0)
    out = jax.ShapeDtypeStruct((batch, nc, KV_WIDTH), BF16)
    return pl.pallas_call(
        _compress_kernel,
        grid=(batch,),
        in_specs=[pl.BlockSpec((None, seq, width), blk), pl.BlockSpec((None, seq, width), blk),
                  pl.BlockSpec((ROT_HALF, nc), lambda b: (0, first_col_block + b)),
                  pl.BlockSpec((ROT_HALF, nc), lambda b: (0, first_col_block + b))]
                 + [pl.BlockSpec(a.shape, full) for a in consts],
        out_specs=[pl.BlockSpec((None, nc, KV_WIDTH), blk)] * 2,
        out_shape=[out, out],
        compiler_params=_params(1),
        name="compress",
    )(rk, rv, cos_t, sin_t, *consts)


def _softmax_rows(s):
    m = jnp.max(s, axis=1, keepdims=True)
    e = jnp.exp2(s - m)
    return e, jnp.sum(e, axis=1, keepdims=True)


def _dot_exact01(x, m01):
    hi = x.astype(BF16)
    r1 = x - hi.astype(F32)
    mid = r1.astype(BF16)
    lo = (r1 - mid.astype(F32)).astype(BF16)
    return _dot(hi, m01) + _dot(mid, m01) + _dot(lo, m01)


def _unselected_blocks(imp, q0, n_pick):
    nq, nb = imp.shape
    col = lax.broadcasted_iota(jnp.int32, (nq, nb), 1)
    cur = lax.shift_right_arithmetic(q0 + lax.broadcasted_iota(jnp.int32, (nq, nb), 0), 6)
    forced = (col == 0) | (col == cur) | (col == cur - 1)
    val = jnp.where(forced, -jnp.inf, jnp.where(col > cur, -1.0, imp))
    vt = val.T
    blk = lax.broadcasted_iota(jnp.int32, (nb, nq), 0).astype(F32)
    unsel = jnp.where(forced, 0.0, 1.0).T
    for _ in range(n_pick - N_FORCED):
        m = jnp.max(vt, axis=0, keepdims=True)
        first = jnp.min(jnp.where(vt == m, blk, float(nb)), axis=0, keepdims=True)
        hit = blk == first
        unsel = jnp.where(hit, 0.0, unsel)
        vt = jnp.where(hit, -jnp.inf, vt)
    return unsel.T


def _nsa_kernel(q_ref, gate_ref, kc_ref, vc_ref, ks_ref, vs_ref, kw_ref, vw_ref, pt_ref, ov_ref,
                o_ref, m_scr, l_scr, acc_scr, lhs_scr, oc_scr, ow_scr, s0_scr, s1_scr, w_scr, gx_scr, *, n_pick):
    qb = pl.program_id(1)
    q0 = qb * Q_BLOCK
    tk = SEL_KEY_TILE
    ncmp = kc_ref.shape[0]
    wk = WINDOW + Q_BLOCK
    rows = NSA_GROUP * Q_BLOCK
    lane = lax.broadcasted_iota(jnp.int32, (Q_BLOCK, LANES), 1)

    def trow(width):
        return q0 + lax.broadcasted_iota(jnp.int32, (Q_BLOCK, width), 0)

    def kcol(width):
        return lax.broadcasted_iota(jnp.int32, (Q_BLOCK, width), 1)

    bias_c = jnp.where(kcol(ncmp) * CMP_STRIDE + (CMP_BLOCK - 1) <= trow(ncmp), 0.0, NEG)
    has_cmp = (trow(1) >= CMP_BLOCK - 1).astype(F32)
    wstart = pl.multiple_of(jnp.maximum(q0 - WINDOW, 0), Q_BLOCK)
    kpos_w = wstart + kcol(wk)
    bias_w = jnp.where(kpos_w <= trow(wk), jnp.where(kpos_w > trow(wk) - WINDOW, 0.0, NEG), NEG)

    for g in range(NSA_KV_HEADS):
        mine = (lane >= NSA_HEAD_DIM) if g else (lane < NSA_HEAD_DIM)
        lhs_scr[g, :, 0:LANES] = jnp.concatenate(
            [jnp.where(mine, q_ref[:, r * LANES:(r + 1) * LANES], jnp.zeros((), BF16))
             for r in range(NSA_GROUP)], axis=0)
        m_scr[g] = jnp.full((rows, LANES), -jnp.inf, F32)
        l_scr[g] = jnp.zeros((rows, LANES), F32)
        acc_scr[g] = jnp.zeros((rows, LANES), F32)

    def cmp_scores(g):
        s0_scr[g] = _dot_nt(lhs_scr[g, :, 0:LANES], kc_ref[...])

    def cmp_attend(g):
        e, l = _softmax_rows(s0_scr[g] + _rep_rows(bias_c, NSA_GROUP))
        p = e * (_rep_rows(has_cmp, NSA_GROUP) / l)
        oc_scr[g] = _dot(p.astype(BF16), vc_ref[...])
        psum = p[0:Q_BLOCK]
        for r in range(1, NSA_GROUP):
            psum = psum + p[r * Q_BLOCK:(r + 1) * Q_BLOCK]
        return _dot_exact01(psum, ov_ref[...])

    def select(g, imp):
        unsel = _unselected_blocks(imp, q0, n_pick)
        lhs_scr[g, :, LANES:2 * LANES] = _rep_rows(unsel.astype(BF16), NSA_GROUP)

    def win_scores(g):
        w_scr[g] = _dot_nt(lhs_scr[g, :, 0:LANES], kw_ref[pl.ds(wstart, wk), :])

    def win_attend(g):
        e, l = _softmax_rows(w_scr[g] + _rep_rows(bias_w, NSA_GROUP))
        ow_scr[g] = _dot(e.astype(BF16), vw_ref[pl.ds(wstart, wk), :]) / l

    def expand_gates():
        low = lane < NSA_HEAD_DIM
        for r in range(NSA_GROUP):
            for br in range(N_BRANCH):
                c_lo = r * N_BRANCH + br
                c_hi = (NSA_GROUP + r) * N_BRANCH + br
                gx_scr[c_lo] = jnp.where(low, gate_ref[:, c_lo:c_lo + 1], gate_ref[:, c_hi:c_hi + 1])

    cmp_scores(0)
    cmp_scores(1)
    expand_gates()
    win_scores(0)
    imp0 = cmp_attend(0)
    win_scores(1)
    imp1 = cmp_attend(1)
    select(0, imp0)
    win_attend(0)
    select(1, imp1)
    win_attend(1)

    def scores(kt, buf, g):
        k0 = pl.multiple_of(kt * tk, tk)
        rhs = jnp.concatenate([ks_ref[pl.ds(k0, tk), :], pt_ref[pl.ds(k0, tk), :]], axis=1)
        buf[g] = _dot_nt(lhs_scr[g], rhs)

    def consume(kt, buf, g, causal):
        k0 = pl.multiple_of(kt * tk, tk)
        v = vs_ref[pl.ds(k0, tk), :]
        bias = jnp.where(k0 + kcol(tk) <= trow(tk), 0.0, NEG) if causal else None
        for r in range(NSA_GROUP):
            rs = slice(r * Q_BLOCK, (r + 1) * Q_BLOCK)
            s = buf[g, rs, :]
            if causal:
                s = s + bias
            m_prev = m_scr[g, rs, :]
            m_new = jnp.maximum(m_prev, jnp.max(s, axis=1, keepdims=True))
            alpha = jnp.exp2(m_prev - m_new)
            pe = jnp.exp2(s - _rep_lanes(m_new, tk // LANES))
            part = pe[:, 0:LANES]
            for c in range(1, tk // LANES):
                part = part + pe[:, c * LANES:(c + 1) * LANES]
            l_scr[g, rs, :] = alpha * l_scr[g, rs, :] + part
            acc_scr[g, rs, :] = alpha * acc_scr[g, rs, :] + _dot(pe.astype(BF16), v)
            m_scr[g, rs, :] = m_new

    def step(nxt, nxt_buf, cur, cur_buf, causal=False):
        for g in range(NSA_KV_HEADS):
            if nxt is not None:
                scores(nxt, nxt_buf, g)
        for g in range(NSA_KV_HEADS):
            consume(cur, cur_buf, g, causal)

    n_full = q0 // tk
    for g in range(NSA_KV_HEADS):
        scores(0, s0_scr, g)

    def pair(j, carry):
        step(2 * j + 1, s1_scr, 2 * j, s0_scr)
        step(2 * j + 2, s0_scr, 2 * j + 1, s1_scr)
        return carry

    lax.fori_loop(0, n_full // 2, pair, 0)

    @pl.when(n_full % 2 == 0)
    def _():
        step(None, None, n_full, s0_scr, causal=True)

    @pl.when(n_full % 2 == 1)
    def _():
        step(n_full, s1_scr, n_full - 1, s0_scr)
        step(None, None, n_full, s1_scr, causal=True)

    low = lane < NSA_HEAD_DIM
    for r in range(NSA_GROUP):
        rs = slice(r * Q_BLOCK, (r + 1) * Q_BLOCK)
        merged = lambda scr: jnp.where(low, scr[0, rs, :], scr[1, rs, :])
        l_sel = jnp.where(low, jnp.sum(l_scr[0, rs, :], axis=1, keepdims=True),
                          jnp.sum(l_scr[1, rs, :], axis=1, keepdims=True))
        out = (gx_scr[r * N_BRANCH] * merged(oc_scr)
               + gx_scr[r * N_BRANCH + 1] * (merged(acc_scr) / l_sel)
               + gx_scr[r * N_BRANCH + 2] * merged(ow_scr))
        o_ref[:, r * LANES:(r + 1) * LANES] = out.astype(BF16)


def _nsa(q, gates, kc, vc, ks, vs, kw, vw, pt, ov, batch, seq):
    n = q.shape[0]
    nqb = seq // Q_BLOCK
    rows = NSA_GROUP * Q_BLOCK
    row = lambda b, i: (b * nqb + i, 0)
    per_b = lambda b, i: (b, 0, 0)
    full = lambda b, i: (0, 0)
    seq_spec = _resident((None, seq, KV_WIDTH), per_b)
    cmp_spec = _resident((None, kc.shape[1], KV_WIDTH), per_b)
    n_pick = min(SEL_COUNT, seq // SEL_BLOCK)
    return pl.pallas_call(
        functools.partial(_nsa_kernel, n_pick=n_pick),
        grid=(batch, nqb),
        in_specs=[pl.BlockSpec((Q_BLOCK, Q_WIDTH), row), pl.BlockSpec((Q_BLOCK, LANES), row),
                  cmp_spec, cmp_spec, seq_spec, seq_spec, seq_spec, seq_spec,
                  _resident(pt.shape, full), _resident(ov.shape, full)],
        out_specs=pl.BlockSpec((Q_BLOCK, Q_WIDTH), row),
        out_shape=jax.ShapeDtypeStruct((n, Q_WIDTH), BF16),
        scratch_shapes=[pltpu.VMEM((NSA_KV_HEADS, rows, LANES), F32),
                        pltpu.VMEM((NSA_KV_HEADS, rows, LANES), F32),
                        pltpu.VMEM((NSA_KV_HEADS, rows, LANES), F32),
                        pltpu.VMEM((NSA_KV_HEADS, rows, 2 * LANES), BF16),
                        pltpu.VMEM((NSA_KV_HEADS, rows, LANES), F32),
                        pltpu.VMEM((NSA_KV_HEADS, rows, LANES), F32),
                        pltpu.VMEM((NSA_KV_HEADS, rows, SEL_KEY_TILE), F32),
                        pltpu.VMEM((NSA_KV_HEADS, rows, SEL_KEY_TILE), F32),
                        pltpu.VMEM((NSA_KV_HEADS, rows, WINDOW + Q_BLOCK), F32),
                        pltpu.VMEM((NSA_GROUP * N_BRANCH, Q_BLOCK, LANES), F32)],
        compiler_params=_params(2),
        name="nsa",
    )(q, gates, kc, vc, ks, vs, kw, vw, pt, ov)


def _matmul_kernel(a_ref, w_ref, o_ref):
    o_ref[...] = _dot(a_ref[...].astype(BF16), w_ref[...]).astype(o_ref.dtype)


def _mem_kv(memf, wkv_all, layer):
    m, k = memf.shape
    n = wkv_all.shape[2]
    return pl.pallas_call(
        _matmul_kernel,
        grid=(1,),
        in_specs=[pl.BlockSpec((m, k), lambda i: (0, 0)), pl.BlockSpec((None, k, n), lambda i: (layer, 0, 0))],
        out_specs=pl.BlockSpec((m, n), lambda i: (0, 0)),
        out_shape=jax.ShapeDtypeStruct((m, n), BF16),
        compiler_params=_params(1),
        name="mem_kv",
    )(memf, wkv_all)


def _tail_kernel(*refs, n_parts):
    x_ref = refs[0]
    parts = refs[1:1 + n_parts]
    ws = refs[1 + n_parts:1 + 2 * n_parts]
    wq_ref, k_ref, v_ref, wo_ref, wu_ref, wd_ref, g_ref, b_ref, o_ref, x1_scr, x2_scr = refs[1 + 2 * n_parts:]
    tm = x_ref.shape[0]
    half = tm // 2
    halves = (slice(0, half), slice(half, tm))
    x1 = x1_scr if n_parts else x_ref

    def norm(k, z):
        return _layer_norm(z, g_ref[k:k + 1, :], b_ref[k:k + 1, :])

    def mixer_out(rs):
        z = ALPHA * x_ref[rs, :]
        for a, w in zip(parts, ws):
            z = z + _dot(a[rs, :], w[...])
        x1_scr[rs, :] = norm(0, z)

    def project(rs):
        return _dot(x1[rs, :].astype(BF16), wq_ref[...]).astype(BF16)

    def attend(q):
        outs = []
        for h in range(X_HEADS):
            sl = slice(h * X_HEAD_DIM, (h + 1) * X_HEAD_DIM)
            e, l = _softmax_rows(_dot_nt(q[:, sl], k_ref[:, sl]))
            outs.append(_dot((e / l).astype(BF16), v_ref[:, sl]))
        return jnp.concatenate(outs, axis=1).astype(BF16)

    def finish(rs, o):
        x2_scr[rs, :] = norm(1, ALPHA * x1[rs, :] + _dot(o, wo_ref[...]))

    if n_parts:
        mixer_out(halves[0])
        mixer_out(halves[1])
    qa = project(halves[0])
    qb = project(halves[1])
    oa = attend(qa)
    ob = attend(qb)
    finish(halves[0], oa)
    finish(halves[1], ob)

    x2 = x2_scr[...]
    xb = x2.astype(BF16)
    z = ALPHA * x2
    c0 = 0
    for width in FFN_CHUNKS:
        h1 = _dot(xb, wu_ref[:, c0:c0 + width])
        h2 = _dot(xb, wu_ref[:, FFN_HIDDEN + c0:FFN_HIDDEN + c0 + width])
        act = (h1 * _sigmoid_tanh(h1) * h2).astype(BF16)
        z = z + _dot(act, wd_ref[c0:c0 + width, :])
        c0 += width
    o_ref[...] = norm(2, z)


def _tail(xf, parts, ws, kv, wq_all, wo_all, wu_all, wd_all, g_all, b_all, layer, batch, seq):
    n = xf.shape[0]
    tm = ROW_TILE
    nt = seq // tm
    n_mem = kv.shape[1]
    row = lambda bi, i: (bi * nt + i, 0)
    full = lambda bi, i: (0, 0)
    this_layer = lambda bi, i: (layer, 0, 0)
    stacked = lambda a: _resident((None,) + a.shape[1:], this_layer)
    return pl.pallas_call(
        functools.partial(_tail_kernel, n_parts=len(parts)),
        grid=(batch, nt),
        in_specs=[pl.BlockSpec((tm, D_MODEL), row)]
                 + [pl.BlockSpec((tm, a.shape[1]), row) for a in parts]
                 + [_resident(w.shape, full) for w in ws]
                 + [stacked(wq_all),
                    _resident((None, n_mem, D_MODEL), lambda bi, i: (bi, 0, 0)),
                    _resident((None, n_mem, D_MODEL), lambda bi, i: (bi, 0, 1)),
                    stacked(wo_all), stacked(wu_all), stacked(wd_all), stacked(g_all), stacked(b_all)],
        out_specs=pl.BlockSpec((tm, D_MODEL), row),
        out_shape=jax.ShapeDtypeStruct((n, D_MODEL), F32),
        scratch_shapes=[pltpu.VMEM((tm, D_MODEL), F32), pltpu.VMEM((tm, D_MODEL), F32)],
        compiler_params=_params(2),
        name="tail",
    )(xf, *parts, *ws, wq_all, kv, kv, wo_all, wu_all, wd_all, g_all, b_all)


def _gelu_tanh(x):
    inner = x * (GELU_C0 + GELU_C1 * (x * x))
    return x * (0.5 * jnp.tanh(inner) + 0.5)


def _sigmoid_tanh(z):
    return 0.5 * jnp.tanh(0.5 * z) + 0.5


def _odd_kernel(x_ref, pos_ref, win_ref, cw_ref, cb_ref, wax_ref, ba_ref, bx_ref, lam_ref, wout_ref,
                g_ref, b_ref, o_ref, xbuf, gate_scr, a_scr, b_scr, h_scr):
    i = pl.program_id(1)
    tm = x_ref.shape[0]
    half = tm // 2
    groups = half // SUBLANES

    @pl.when(i == 0)
    def _():
        xbuf[0:CONV_HALO, :] = jnp.zeros((CONV_HALO, RNN_WIDTH), F32)
        h_scr[...] = jnp.zeros((SUBLANES, RNN_WIDTH), F32)

    in_group = lax.broadcasted_iota(jnp.int32, (groups, SUBLANES, RNN_BLOCK_W), 1)
    lam = lam_ref[...]
    log_a_scale = -LRU_C * (jnp.maximum(-lam, 0.0) + jnp.log1p(jnp.exp(-jnp.abs(lam))))

    def in_proj(h0):
        xb = x_ref[h0:h0 + half, :].astype(BF16)
        gate_scr[h0:h0 + half, :] = _dot(xb, win_ref[:, :RNN_WIDTH])
        xbuf[CONV_HALO + h0:CONV_HALO + h0 + half, :] = _dot(xb, win_ref[:, RNN_WIDTH:])

    def gates(h0):
        reset = pos_ref[h0:h0 + half, :] == 0
        for h in range(RNN_BLOCKS):
            sl = slice(h * RNN_BLOCK_W, (h + 1) * RNN_BLOCK_W)
            xc = cb_ref[:, sl]
            for k in range(CONV_WIDTH):
                xc = xc + xbuf[pl.ds(CONV_HALO - (CONV_WIDTH - 1) + k + h0, half), sl] * cw_ref[k:k + 1, sl]
            ri = _dot(xc.astype(BF16), wax_ref[h])
            r = _sigmoid_tanh(ri[:, :RNN_BLOCK_W] + ba_ref[:, sl])
            ig = _sigmoid_tanh(ri[:, RNN_BLOCK_W:] + bx_ref[:, sl])
            log_a = r * log_a_scale[:, sl]
            a = jnp.where(reset, 0.0, jnp.exp(log_a))
            th = jnp.tanh(log_a)
            m2 = -2.0 * th / (1.0 - th)
            mult = jnp.where(reset, 1.0, jnp.where(m2 > 0.0, m2 * lax.rsqrt(m2), 0.0))
            b = mult * (ig * xc)
            a3 = a.reshape(groups, SUBLANES, RNN_BLOCK_W)
            b3 = b.reshape(groups, SUBLANES, RNN_BLOCK_W)
            for d in (1, 2, 4):
                a_prev = pltpu.roll(a3, d, 1)
                b_prev = pltpu.roll(b3, d, 1)
                ok = in_group >= d
                b3 = jnp.where(ok, a3 * b_prev + b3, b3)
                a3 = jnp.where(ok, a3 * a_prev, a3)
            a_scr[h0:h0 + half, sl] = a3.reshape(half, RNN_BLOCK_W)
            b_scr[h0:h0 + half, sl] = b3.reshape(half, RNN_BLOCK_W)

    def recur(h0, h_prev):
        for j in range(groups):
            rs = slice(h0 + j * SUBLANES, h0 + (j + 1) * SUBLANES)
            hh = a_scr[rs, :] * h_prev + b_scr[rs, :]
            b_scr[rs, :] = hh
            h_prev = jnp.broadcast_to(hh[SUBLANES - 1:SUBLANES, :], (SUBLANES, RNN_WIDTH))
        return h_prev

    def out_proj(h0):
        rs = slice(h0, h0 + half)
        y = (b_scr[rs, :] * _gelu_tanh(gate_scr[rs, :])).astype(BF16)
        z = ALPHA * x_ref[rs, :] + _dot(y, wout_ref[...])
        o_ref[rs, :] = _layer_norm(z, g_ref[...], b_ref[...])

    in_proj(0)
    in_proj(half)
    gates(0)
    h_mid = recur(0, h_scr[...])
    gates(half)
    out_proj(0)
    h_scr[...] = recur(half, h_mid)
    out_proj(half)
    xbuf[0:CONV_HALO, :] = xbuf[tm:tm + CONV_HALO, :]


def _odd_layer(xf, posi, w_in, cw, cb, wax, ba, bx, lam, w_out, g, b, batch, seq):
    n = xf.shape[0]
    tm = ROW_TILE
    nt = seq // tm
    row = lambda bi, i: (bi * nt + i, 0)
    full = lambda bi, i: (0, 0)
    consts = [w_in, cw, cb]
    rest = [ba, bx, lam, w_out, g, b]
    return pl.pallas_call(
        _odd_kernel,
        grid=(batch, nt),
        in_specs=[pl.BlockSpec((tm, D_MODEL), row), pl.BlockSpec((tm, 1), row)]
                 + [pl.BlockSpec(a.shape, full) for a in consts]
                 + [pl.BlockSpec(wax.shape, lambda bi, i: (0, 0, 0))]
                 + [pl.BlockSpec(a.shape, full) for a in rest],
        out_specs=pl.BlockSpec((tm, D_MODEL), row),
        out_shape=jax.ShapeDtypeStruct((n, D_MODEL), F32),
        scratch_shapes=[pltpu.VMEM((CONV_HALO + tm, RNN_WIDTH), F32),
                        pltpu.VMEM((tm, RNN_WIDTH), F32),
                        pltpu.VMEM((tm, RNN_WIDTH), F32),
                        pltpu.VMEM((tm, RNN_WIDTH), F32),
                        pltpu.VMEM((SUBLANES, RNN_WIDTH), F32)],
        compiler_params=_params(2),
        name="odd_layer",
    )(xf, posi, *consts, wax, *rest)


def _even_mixer(xf, positions, w_in, pool_w, pool_scale, cmp_pos_k, cmp_pos_v, cmp_wk, cmp_wv,
                w_out, batch, seq):
    assert seq // SEL_BLOCK == LANES and seq % SEL_KEY_TILE == 0 and seq >= WINDOW + Q_BLOCK
    hd, g, r = NSA_HEAD_DIM, NSA_KV_HEADS, NSA_GROUP
    c0 = POOL_WIDTH
    c1 = c0 + Q_WIDTH
    c2 = c1 + GATE_WIDTH
    wp = w_in[:, :c0].astype(BF16)
    wq = (w_in[:, c0:c1].reshape(D_MODEL, g, r, hd).transpose(0, 2, 1, 3).reshape(D_MODEL, Q_WIDTH)
          * (hd ** -0.5 * LOG2E)).astype(BF16)
    wg = jnp.pad(w_in[:, c1:c2], ((0, 0), (0, LANES - GATE_WIDTH))).astype(BF16)
    wkv = w_in[:, c2:].astype(BF16)
    n = batch * seq
    nc = seq // CMP_STRIDE
    n_cmp = (seq - CMP_BLOCK) // CMP_STRIDE + 1
    posc = jnp.pad(positions[:, CMP_BLOCK - 1::CMP_STRIDE][:, :n_cmp], ((0, 0), (0, nc - n_cmp)))
    pos_all = jnp.concatenate([positions.reshape(1, n), posc.reshape(1, batch * nc)], axis=1).astype(F32)
    cos_t, sin_t = _rope_angles(pos_all)
    ec, es = _rope_spread()
    pool, q, kc_raw, vc_raw, ks, vs, kw, vw, gates = _proj_even(
        xf, cos_t, sin_t, ec, es, wp, wq, wkv, wg, pool_w.astype(BF16), pool_scale[None, :], batch, seq)

    eye = jnp.eye(g, dtype=F32)

    def halves(w):
        w4 = jnp.einsum('lde,gh->lgdhe', w.reshape(CMP_BLOCK, hd, hd), eye)
        w4 = w4.reshape(CMP_BLOCK, g * hd, g * hd)
        return (w4[:CMP_STRIDE].reshape(CMP_STRIDE * g * hd, g * hd).astype(BF16),
                w4[CMP_STRIDE:].reshape(CMP_STRIDE * g * hd, g * hd).astype(BF16))

    def pos_halves(p):
        p2 = jnp.tile(p[:, None, :], (1, g, 1)).reshape(CMP_BLOCK, g * hd)
        return p2[:CMP_STRIDE].reshape(1, -1), p2[CMP_STRIDE:].reshape(1, -1)

    pk1, pk2 = pos_halves(cmp_pos_k)
    pv1, pv2 = pos_halves(cmp_pos_v)
    wk1, wk2 = halves(cmp_wk)
    wv1, wv2 = halves(cmp_wv)
    seq3 = lambda a: a.reshape(batch, seq, KV_WIDTH)
    kc, vc = _compress(seq3(kc_raw), seq3(vc_raw), cos_t, sin_t, n // nc,
                       (ec, es, pk1, pk2, pv1, pv2, wk1, wk2, wv1, wv2))

    n_sb = seq // SEL_BLOCK
    starts = np.arange(nc) * CMP_STRIDE
    jb = np.arange(n_sb)
    ov = ((starts[:, None] < (jb[None, :] + 1) * SEL_BLOCK)
          & (starts[:, None] + CMP_BLOCK > jb[None, :] * SEL_BLOCK) & (np.arange(nc)[:, None] < n_cmp))
    ov = jnp.asarray(ov, BF16)
    pt = jnp.asarray(np.where(np.arange(seq)[:, None] // SEL_BLOCK == jb[None, :], NEG, 0.0), BF16)

    nsa = _nsa(q, gates, kc, vc, seq3(ks), seq3(vs), seq3(kw), seq3(vw), pt, ov, batch, seq)

    w_pool_out = w_out[:POOL_WIDTH].astype(BF16)
    w_nsa_out = (w_out[POOL_WIDTH:].reshape(g, r, hd, D_MODEL).transpose(1, 0, 2, 3)
                 .reshape(Q_WIDTH, D_MODEL).astype(BF16))
    return [pool, nsa], [w_pool_out, w_nsa_out]


def _odd_mixer(xf, posi, w_in, conv_w, conv_b, wa, ba, wx, bx, lam, w_out, ln_g, ln_b, batch, seq):
    wax = jnp.concatenate([wa, wx], axis=2).astype(BF16)
    return _odd_layer(xf, posi, w_in.astype(BF16), conv_w, conv_b[None, :], wax, ba[None, :], bx[None, :],
                      lam[None, :], w_out.astype(BF16), ln_g, ln_b, batch, seq)


def kernel(x, mem, positions, e_w_in, e_pool_w, e_pool_scale, e_cmp_pos_k, e_cmp_pos_v, e_cmp_wk, e_cmp_wv, e_w_out, o_w_in, o_conv_w, o_conv_b, o_wa, o_ba, o_wx, o_bx, o_lambda, o_w_out, x_wq, x_wkv, x_wo, f_w_up, f_w_down, ln_g, ln_b):
    batch, seq, d = x.shape
    n = batch * seq
    assert d == D_MODEL and seq % ROW_TILE == 0
    xf = x.reshape(n, d)
    posi = positions.reshape(n, 1)
    memf = mem.reshape(batch * mem.shape[1], d)
    wq_all = (x_wq * (X_HEAD_DIM ** -0.5 * LOG2E)).astype(BF16)
    wkv_all, wo_all = x_wkv.astype(BF16), x_wo.astype(BF16)
    wu_all, wd_all = f_w_up.astype(BF16), f_w_down.astype(BF16)
    for layer in range(DEPTH):
        j = layer // 2
        if layer % 2 == 0:
            parts, ws = _even_mixer(xf, positions, e_w_in[j], e_pool_w[j], e_pool_scale[j], e_cmp_pos_k[j],
                                    e_cmp_pos_v[j], e_cmp_wk[j], e_cmp_wv[j], e_w_out[j], batch, seq)
        else:
            xf = _odd_mixer(xf, posi, o_w_in[j], o_conv_w[j], o_conv_b[j], o_wa[j], o_ba[j], o_wx[j], o_bx[j],
                            o_lambda[j], o_w_out[j], ln_g[layer, 0][None, :], ln_b[layer, 0][None, :], batch, seq)
            parts, ws = [], []
        kv = _mem_kv(memf, wkv_all, layer).reshape(batch, mem.shape[1], 2 * d)
        xf = _tail(xf, parts, ws, kv, wq_all, wo_all, wu_all, wd_all, ln_g, ln_b, layer, batch, seq)
    return xf.reshape(batch, seq, d)
```

```python
import functools

import numpy as np
import jax
import jax.numpy as jnp
from jax import lax
from jax.experimental import pallas as pl
from jax.experimental.pallas import tpu as pltpu

F32 = jnp.float32
BF16 = jnp.bfloat16

D_MODEL = 1024
DEPTH = 2
ALPHA = (2.0 * DEPTH) ** 0.25
LN_EPS = 1e-5
NEG = -1e30
POOL_WIDTH = D_MODEL // 2
POOL_WINDOWS = (2, 4, 8, 16)
POOL_GROUP = POOL_WIDTH // len(POOL_WINDOWS)
POOL_HALO = 16
NSA_HEADS = 8
NSA_KV_HEADS = 2
NSA_HEAD_DIM = 64
NSA_GROUP = NSA_HEADS // NSA_KV_HEADS
CMP_BLOCK = 32
CMP_STRIDE = 16
SEL_BLOCK = 64
SEL_COUNT = 16
WINDOW = 512
Q_BLOCK = 256
N_BRANCH = 3
N_FORCED = 3
LOG2E = 1.4426950408889634
ROPE_THETA = 500000.0
ROT_DIM = NSA_HEAD_DIM // 4
ROT_HALF = ROT_DIM // 2
Q_WIDTH = NSA_HEADS * NSA_HEAD_DIM
KV_WIDTH = NSA_KV_HEADS * NSA_HEAD_DIM
GATE_WIDTH = NSA_HEADS * N_BRANCH
RNN_WIDTH = 1280
RNN_BLOCKS = 10
RNN_BLOCK_W = RNN_WIDTH // RNN_BLOCKS
CONV_WIDTH = 4
CONV_HALO = 8
LRU_C = 8.0
X_HEADS = 4
X_HEAD_DIM = D_MODEL // X_HEADS
FFN_HIDDEN = 2816
GELU_C0 = 0.7978845608028654
GELU_C1 = GELU_C0 * 0.044715

LANES = 128
SUBLANES = 8
BF16_ROWS = 16
VMEM_LIMIT = 56 * 1024 * 1024
ROW_TILE = 512
SEL_KEY_TILE = 512
FFN_CHUNKS = (1024, 1024, 768)


def _params(n_axes, vmem=VMEM_LIMIT):
    return pltpu.CompilerParams(dimension_semantics=("arbitrary",) * n_axes,
                                vmem_limit_bytes=vmem)


def _resident(shape, index_map):
    return pl.BlockSpec(shape, index_map, pipeline_mode=pl.Buffered(1))


def _dot(a, b):
    return jnp.dot(a, b, preferred_element_type=F32)


def _dot_nt(a, b):
    return lax.dot_general(a, b, (((1,), (1,)), ((), ())), preferred_element_type=F32)


def _rep_rows(a, k):
    return jnp.concatenate([a] * k, axis=0)


def _rep_lanes(a, k):
    return jnp.concatenate([a] * k, axis=1) if k > 1 else a


def _layer_norm(z, g, b):
    mu = jnp.mean(z, axis=-1, keepdims=True)
    d = z - mu
    var = jnp.mean(d * d, axis=-1, keepdims=True)
    return d * lax.rsqrt(var + LN_EPS) * g + b


def _rope(v, cos, sin):
    k = v.shape[1] // LANES
    up = pltpu.roll(v, v.shape[1] - ROT_HALF, 1)
    dn = pltpu.roll(v, ROT_HALF, 1)
    lane = lax.broadcasted_iota(jnp.int32, v.shape, 1)
    partner = jnp.where((lane & (NSA_HEAD_DIM - 1)) < ROT_HALF, up, dn)
    return v * _rep_lanes(cos, k) + partner * _rep_lanes(sin, k)


def _rope_angle_kernel(pos_ref, inv_ref, cos_o, sin_o):
    ang = inv_ref[...] * pos_ref[...]
    cos_o[...] = jnp.cos(ang)
    sin_o[...] = jnp.sin(ang)


def _rope_angles(pos_row):
    n = pos_row.shape[1]
    inv = (ROPE_THETA ** (-jnp.arange(ROT_HALF, dtype=F32) * 2.0 / ROT_DIM))[:, None]
    out = jax.ShapeDtypeStruct((ROT_HALF, n), F32)
    return pl.pallas_call(
        _rope_angle_kernel,
        out_shape=[out, out],
        name="rope_angles",
    )(pos_row, inv)


def _rope_spread():
    lane = np.arange(LANES) % NSA_HEAD_DIM
    f = np.arange(ROT_HALF)[:, None]
    lo = (lane[None, :] == f).astype(np.float32)
    hi = (lane[None, :] == f + ROT_HALF).astype(np.float32)
    return jnp.asarray(lo + hi, BF16), jnp.asarray(hi - lo, BF16)


def _rope_patterns(cos_t, sin_t, ec, es):
    def spread(t, e):
        hi = t.astype(BF16)
        r1 = t - hi.astype(F32)
        mid = r1.astype(BF16)
        lo = (r1 - mid.astype(F32)).astype(BF16)
        tn = (((0,), (0,)), ((), ()))
        return sum(lax.dot_general(p, e, tn, preferred_element_type=F32) for p in (hi, mid, lo))

    lane = lax.broadcasted_iota(jnp.int32, (1, LANES), 1)
    unrotated = jnp.where((lane & (NSA_HEAD_DIM - 1)) >= ROT_DIM, 1.0, 0.0)
    return spread(cos_t, ec) + unrotated, spread(sin_t, es)


def _proj_even_kernel(x_ref, cos_ref, sin_ref, ec_ref, es_ref, wp_ref, wq_ref, wkv_ref, wg_ref, poolw_ref, pscale_ref,
                      pool_o, q_o, kc_o, vc_o, ks_o, vs_o, kw_o, vw_o, gate_o, pbuf):
    i = pl.program_id(1)
    tm = x_ref.shape[0]
    half = tm // 2

    @pl.when(i == 0)
    def _():
        pbuf[0:POOL_HALO, :] = jnp.zeros((POOL_HALO, POOL_WIDTH), F32)

    def project(h0):
        rs = slice(h0, h0 + half)
        xb = x_ref[rs, :].astype(BF16)
        pbuf[POOL_HALO + h0:POOL_HALO + h0 + half, :] = _dot(xb, wp_ref[...])
        return _dot(xb, wq_ref[...]), _dot(xb, wkv_ref[...]), _dot(xb, wg_ref[...])

    def emit(h0, q, kv, gl):
        rs = slice(h0, h0 + half)
        cos, sin = _rope_patterns(cos_ref[:, rs], sin_ref[:, rs], ec_ref[...], es_ref[...])
        q_o[rs, :] = _rope(q, cos, sin).astype(BF16)
        kc_o[rs, :] = kv[:, 0 * LANES:1 * LANES]
        vc_o[rs, :] = kv[:, 1 * LANES:2 * LANES]
        ks_o[rs, :] = _rope(kv[:, 2 * LANES:3 * LANES], cos, sin).astype(BF16)
        vs_o[rs, :] = kv[:, 3 * LANES:4 * LANES].astype(BF16)
        kw_o[rs, :] = _rope(kv[:, 4 * LANES:5 * LANES], cos, sin).astype(BF16)
        vw_o[rs, :] = kv[:, 5 * LANES:6 * LANES].astype(BF16)
        gate_o[rs, :] = _sigmoid_tanh(gl)

    def pool(h0):
        rs = slice(h0, h0 + half)
        t1 = i * tm + h0 + lax.broadcasted_iota(jnp.int32, (half, 1), 0) + 1
        for g, w in enumerate(POOL_WINDOWS):
            sl = slice(g * POOL_GROUP, (g + 1) * POOL_GROUP)
            u = pbuf[POOL_HALO + h0:POOL_HALO + h0 + half, sl]
            tot = u
            for j in range(1, w):
                tot = tot + pbuf[pl.ds(POOL_HALO + h0 - j, half), sl]
            cnt = jnp.minimum(t1, w).astype(F32)
            pooled = tot / cnt - u
            mixed = _dot(pooled.astype(BF16), poolw_ref[g]) * pscale_ref[:, sl]
            pool_o[rs, sl] = mixed.astype(BF16)

    first = project(0)
    second = project(half)
    emit(0, *first)
    pool(0)
    emit(half, *second)
    pool(half)
    pbuf[0:POOL_HALO, :] = pbuf[tm:tm + POOL_HALO, :]


def _proj_even(xf, cos_t, sin_t, ec, es, wp, wq, wkv, wg, poolw, pscale, batch, seq):
    n = xf.shape[0]
    tm = ROW_TILE
    nt = seq // tm
    row = lambda b, i: (b * nt + i, 0)
    full = lambda b, i: (0, 0)
    outs = [
        jax.ShapeDtypeStruct((n, POOL_WIDTH), BF16),
        jax.ShapeDtypeStruct((n, Q_WIDTH), BF16),
        jax.ShapeDtypeStruct((n, KV_WIDTH), F32),
        jax.ShapeDtypeStruct((n, KV_WIDTH), F32),
        jax.ShapeDtypeStruct((n, KV_WIDTH), BF16),
        jax.ShapeDtypeStruct((n, KV_WIDTH), BF16),
        jax.ShapeDtypeStruct((n, KV_WIDTH), BF16),
        jax.ShapeDtypeStruct((n, KV_WIDTH), BF16),
        jax.ShapeDtypeStruct((n, LANES), F32),
    ]
    return pl.pallas_call(
        _proj_even_kernel,
        grid=(batch, nt),
        in_specs=[
            pl.BlockSpec((tm, D_MODEL), row),
            pl.BlockSpec((ROT_HALF, tm), lambda b, i: (0, b * nt + i)),
            pl.BlockSpec((ROT_HALF, tm), lambda b, i: (0, b * nt + i)),
            pl.BlockSpec(ec.shape, full),
            pl.BlockSpec(es.shape, full),
            pl.BlockSpec(wp.shape, full),
            pl.BlockSpec(wq.shape, full),
            pl.BlockSpec(wkv.shape, full),
            pl.BlockSpec(wg.shape, full),
            pl.BlockSpec(poolw.shape, lambda b, i: (0, 0, 0)),
            pl.BlockSpec(pscale.shape, full),
        ],
        out_specs=[pl.BlockSpec((tm, o.shape[1]), row) for o in outs],
        out_shape=outs,
        scratch_shapes=[pltpu.VMEM((POOL_HALO + tm, POOL_WIDTH), F32)],
        compiler_params=_params(2),
        name="proj_even",
    )(xf, cos_t, sin_t, ec, es, wp, wq, wkv, wg, poolw, pscale)


def _compress_kernel(rk_ref, rv_ref, cos_ref, sin_ref, ec_ref, es_ref, pk1_ref, pk2_ref, pv1_ref, pv2_ref,
                     wk1_ref, wk2_ref, wv1_ref, wv2_ref, kc_o, vc_o):
    nc = kc_o.shape[0]

    def compress(raw_ref, p1, p2, w1, w2):
        r = jnp.concatenate([raw_ref[pl.ds(j, nc, stride=CMP_STRIDE), :] for j in range(CMP_STRIDE)], axis=1)
        a = _dot((r + p1[...]).astype(BF16), w1[...])
        b = _dot((r + p2[...]).astype(BF16), w2[...])
        return a + pltpu.roll(b, nc - 1, 0)

    kc = compress(rk_ref, pk1_ref, pk2_ref, wk1_ref, wk2_ref)
    cos, sin = _rope_patterns(cos_ref[...], sin_ref[...], ec_ref[...], es_ref[...])
    kc_o[...] = _rope(kc, cos, sin).astype(BF16)
    vc_o[...] = compress(rv_ref, pv1_ref, pv2_ref, wv1_ref, wv2_ref).astype(BF16)


def _compress(rk, rv, cos_t, sin_t, first_col_block, consts):
    batch, seq, width = rk.shape
    nc = seq // CMP_STRIDE
    blk = lambda b: (b, 0, 0)
    full = lambda b: (0, 0)
    out = jax.ShapeDtypeStruct((batch, nc, KV_WIDTH), BF16)
    return pl.pallas_call(
        _compress_kernel,
        grid=(batch,),
        in_specs=[pl.BlockSpec((None, seq, width), blk), pl.BlockSpec((None, seq, width), blk),
                  pl.BlockSpec((ROT_HALF, nc), lambda b: (0, first_col_block + b)),
                  pl.BlockSpec((ROT_HALF, nc), lambda b: (0, first_col_block + b))]
                 + [pl.BlockSpec(a.shape, full) for a in consts],
        out_specs=[pl.BlockSpec((None, nc, KV_WIDTH), blk)] * 2,
        out_shape=[out, out],
        compiler_params=_params(1),
        name="compress",
    )(rk, rv, cos_t, sin_t, *consts)


def _softmax_rows(s):
    m = jnp.max(s, axis=1, keepdims=True)
    e = jnp.exp2(s - m)
    return e, jnp.sum(e, axis=1, keepdims=True)


def _dot_exact01(x, m01):
    hi = x.astype(BF16)
    r1 = x - hi.astype(F32)
    mid = r1.astype(BF16)
    lo = (r1 - mid.astype(F32)).astype(BF16)
    return _dot(hi, m01) + _dot(mid, m01) + _dot(lo, m01)


def _unselected_blocks(imp, q0, n_pick):
    nq, nb = imp.shape
    col = lax.broadcasted_iota(jnp.int32, (nq, nb), 1)
    cur = lax.shift_right_arithmetic(q0 + lax.broadcasted_iota(jnp.int32, (nq, nb), 0), 6)
    forced = (col == 0) | (col == cur) | (col == cur - 1)
    val = jnp.where(forced, -jnp.inf, jnp.where(col > cur, -1.0, imp))
    vt = val.T
    blk = lax.broadcasted_iota(jnp.int32, (nb, nq), 0).astype(F32)
    unsel = jnp.where(forced, 0.0, 1.0).T
    for _ in range(n_pick - N_FORCED):
        m = jnp.max(vt, axis=0, keepdims=True)
        first = jnp.min(jnp.where(vt == m, blk, float(nb)), axis=0, keepdims=True)
        hit = blk == first
        unsel = jnp.where(hit, 0.0, unsel)
        vt = jnp.where(hit, -jnp.inf, vt)
    return unsel.T


def _nsa_kernel(q_ref, gate_ref, kc_ref, vc_ref, ks_ref, vs_ref, kw_ref, vw_ref, pt_ref, ov_ref,
                o_ref, m_scr, l_scr, acc_scr, lhs_scr, oc_scr, ow_scr, s0_scr, s1_scr, w_scr, gx_scr, *, n_pick):
    qb = pl.program_id(1)
    q0 = qb * Q_BLOCK
    tk = SEL_KEY_TILE
    ncmp = kc_ref.shape[0]
    wk = WINDOW + Q_BLOCK
    rows = NSA_GROUP * Q_BLOCK
    lane = lax.broadcasted_iota(jnp.int32, (Q_BLOCK, LANES), 1)

    def trow(width):
        return q0 + lax.broadcasted_iota(jnp.int32, (Q_BLOCK, width), 0)

    def kcol(width):
        return lax.broadcasted_iota(jnp.int32, (Q_BLOCK, width), 1)

    bias_c = jnp.where(kcol(ncmp) * CMP_STRIDE + (CMP_BLOCK - 1) <= trow(ncmp), 0.0, NEG)
    has_cmp = (trow(1) >= CMP_BLOCK - 1).astype(F32)
    wstart = pl.multiple_of(jnp.maximum(q0 - WINDOW, 0), Q_BLOCK)
    kpos_w = wstart + kcol(wk)
    bias_w = jnp.where(kpos_w <= trow(wk), jnp.where(kpos_w > trow(wk) - WINDOW, 0.0, NEG), NEG)

    for g in range(NSA_KV_HEADS):
        mine = (lane >= NSA_HEAD_DIM) if g else (lane < NSA_HEAD_DIM)
        lhs_scr[g, :, 0:LANES] = jnp.concatenate(
            [jnp.where(mine, q_ref[:, r * LANES:(r + 1) * LANES], jnp.zeros((), BF16))
             for r in range(NSA_GROUP)], axis=0)
        m_scr[g] = jnp.full((rows, LANES), -jnp.inf, F32)
        l_scr[g] = jnp.zeros((rows, LANES), F32)
        acc_scr[g] = jnp.zeros((rows, LANES), F32)

    def cmp_scores(g):
        s0_scr[g] = _dot_nt(lhs_scr[g, :, 0:LANES], kc_ref[...])

    def cmp_attend(g):
        e, l = _softmax_rows(s0_scr[g] + _rep_rows(bias_c, NSA_GROUP))
        p = e * (_rep_rows(has_cmp, NSA_GROUP) / l)
        oc_scr[g] = _dot(p.astype(BF16), vc_ref[...])
        psum = p[0:Q_BLOCK]
        for r in range(1, NSA_GROUP):
            psum = psum + p[r * Q_BLOCK:(r + 1) * Q_BLOCK]
        return _dot_exact01(psum, ov_ref[...])

    def select(g, imp):
        unsel = _unselected_blocks(imp, q0, n_pick)
        lhs_scr[g, :, LANES:2 * LANES] = _rep_rows(unsel.astype(BF16), NSA_GROUP)

    def win_scores(g):
        w_scr[g] = _dot_nt(lhs_scr[g, :, 0:LANES], kw_ref[pl.ds(wstart, wk), :])

    def win_attend(g):
        e, l = _softmax_rows(w_scr[g] + _rep_rows(bias_w, NSA_GROUP))
        ow_scr[g] = _dot(e.astype(BF16), vw_ref[pl.ds(wstart, wk), :]) / l

    def expand_gates():
        low = lane < NSA_HEAD_DIM
        for r in range(NSA_GROUP):
            for br in range(N_BRANCH):
                c_lo = r * N_BRANCH + br
                c_hi = (NSA_GROUP + r) * N_BRANCH + br
                gx_scr[c_lo] = jnp.where(low, gate_ref[:, c_lo:c_lo + 1], gate_ref[:, c_hi:c_hi + 1])

    cmp_scores(0)
    cmp_scores(1)
    expand_gates()
    win_scores(0)
    imp0 = cmp_attend(0)
    win_scores(1)
    imp1 = cmp_attend(1)
    select(0, imp0)
    win_attend(0)
    select(1, imp1)
    win_attend(1)

    def scores(kt, buf, g):
        k0 = pl.multiple_of(kt * tk, tk)
        rhs = jnp.concatenate([ks_ref[pl.ds(k0, tk), :], pt_ref[pl.ds(k0, tk), :]], axis=1)
        buf[g] = _dot_nt(lhs_scr[g], rhs)

    def consume(kt, buf, g, causal):
        k0 = pl.multiple_of(kt * tk, tk)
        v = vs_ref[pl.ds(k0, tk), :]
        bias = jnp.where(k0 + kcol(tk) <= trow(tk), 0.0, NEG) if causal else None
        for r in range(NSA_GROUP):
            rs = slice(r * Q_BLOCK, (r + 1) * Q_BLOCK)
            s = buf[g, rs, :]
            if causal:
                s = s + bias
            m_prev = m_scr[g, rs, :]
            m_new = jnp.maximum(m_prev, jnp.max(s, axis=1, keepdims=True))
            alpha = jnp.exp2(m_prev - m_new)
            pe = jnp.exp2(s - _rep_lanes(m_new, tk // LANES))
            part = pe[:, 0:LANES]
            for c in range(1, tk // LANES):
                part = part + pe[:, c * LANES:(c + 1) * LANES]
            l_scr[g, rs, :] = alpha * l_scr[g, rs, :] + part
            acc_scr[g, rs, :] = alpha * acc_scr[g, rs, :] + _dot(pe.astype(BF16), v)
            m_scr[g, rs, :] = m_new

    def step(nxt, nxt_buf, cur, cur_buf, causal=False):
        for g in range(NSA_KV_HEADS):
            if nxt is not None:
                scores(nxt, nxt_buf, g)
        for g in range(NSA_KV_HEADS):
            consume(cur, cur_buf, g, causal)

    n_full = q0 // tk
    for g in range(NSA_KV_HEADS):
        scores(0, s0_scr, g)

    def pair(j, carry):
        step(2 * j + 1, s1_scr, 2 * j, s0_scr)
        step(2 * j + 2, s0_scr, 2 * j + 1, s1_scr)
        return carry

    lax.fori_loop(0, n_full // 2, pair, 0)

    @pl.when(n_full % 2 == 0)
    def _():
        step(None, None, n_full, s0_scr, causal=True)

    @pl.when(n_full % 2 == 1)
    def _():
        step(n_full, s1_scr, n_full - 1, s0_scr)
        step(None, None, n_full, s1_scr, causal=True)

    low = lane < NSA_HEAD_DIM
    for r in range(NSA_GROUP):
        rs = slice(r * Q_BLOCK, (r + 1) * Q_BLOCK)
        merged = lambda scr: jnp.where(low, scr[0, rs, :], scr[1, rs, :])
        l_sel = jnp.where(low, jnp.sum(l_scr[0, rs, :], axis=1, keepdims=True),
                          jnp.sum(l_scr[1, rs, :], axis=1, keepdims=True))
        out = (gx_scr[r * N_BRANCH] * merged(oc_scr)
               + gx_scr[r * N_BRANCH + 1] * (merged(acc_scr) / l_sel)
               + gx_scr[r * N_BRANCH + 2] * merged(ow_scr))
        o_ref[:, r * LANES:(r + 1) * LANES] = out.astype(BF16)


def _nsa(q, gates, kc, vc, ks, vs, kw, vw, pt, ov, batch, seq):
    n = q.shape[0]
    nqb = seq // Q_BLOCK
    rows = NSA_GROUP * Q_BLOCK
    row = lambda b, i: (b * nqb + i, 0)
    per_b = lambda b, i: (b, 0, 0)
    full = lambda b, i: (0, 0)
    seq_spec = _resident((None, seq, KV_WIDTH), per_b)
    cmp_spec = _resident((None, kc.shape[1], KV_WIDTH), per_b)
    n_pick = min(SEL_COUNT, seq // SEL_BLOCK)
    return pl.pallas_call(
        functools.partial(_nsa_kernel, n_pick=n_pick),
        grid=(batch, nqb),
        in_specs=[pl.BlockSpec((Q_BLOCK, Q_WIDTH), row), pl.BlockSpec((Q_BLOCK, LANES), row),
                  cmp_spec, cmp_spec, seq_spec, seq_spec, seq_spec, seq_spec,
                  _resident(pt.shape, full), _resident(ov.shape, full)],
        out_specs=pl.BlockSpec((Q_BLOCK, Q_WIDTH), row),
        out_shape=jax.ShapeDtypeStruct((n, Q_WIDTH), BF16),
        scratch_shapes=[pltpu.VMEM((NSA_KV_HEADS, rows, LANES), F32),
                        pltpu.VMEM((NSA_KV_HEADS, rows, LANES), F32),
                        pltpu.VMEM((NSA_KV_HEADS, rows, LANES), F32),
                        pltpu.VMEM((NSA_KV_HEADS, rows, 2 * LANES), BF16),
                        pltpu.VMEM((NSA_KV_HEADS, rows, LANES), F32),
                        pltpu.VMEM((NSA_KV_HEADS, rows, LANES), F32),
                        pltpu.VMEM((NSA_KV_HEADS, rows, SEL_KEY_TILE), F32),
                        pltpu.VMEM((NSA_KV_HEADS, rows, SEL_KEY_TILE), F32),
                        pltpu.VMEM((NSA_KV_HEADS, rows, WINDOW + Q_BLOCK), F32),
                        pltpu.VMEM((NSA_GROUP * N_BRANCH, Q_BLOCK, LANES), F32)],
        compiler_params=_params(2),
        name="nsa",
    )(q, gates, kc, vc, ks, vs, kw, vw, pt, ov)


def _matmul_kernel(a_ref, w_ref, o_ref):
    o_ref[...] = _dot(a_ref[...].astype(BF16), w_ref[...].astype(BF16)).astype(o_ref.dtype)


def _mem_kv(memf, wkv_all, layer):
    m, k = memf.shape
    n = wkv_all.shape[2]
    return pl.pallas_call(
        _matmul_kernel,
        grid=(1,),
        in_specs=[pl.BlockSpec((m, k), lambda i: (0, 0)), pl.BlockSpec((None, k, n), lambda i: (layer, 0, 0))],
        out_specs=pl.BlockSpec((m, n), lambda i: (0, 0)),
        out_shape=jax.ShapeDtypeStruct((m, n), BF16),
        compiler_params=_params(1),
        name="mem_kv",
    )(memf, wkv_all)


def _tail_kernel(*refs, n_parts):
    x_ref = refs[0]
    parts = refs[1:1 + n_parts]
    ws = refs[1 + n_parts:1 + 2 * n_parts]
    wq_ref, k_ref, v_ref, wo_ref, wu_ref, wd_ref, g_ref, b_ref, o_ref, x1_scr, x2_scr, z_scr = refs[1 + 2 * n_parts:]
    t = pl.program_id(0)
    n_tiles = pl.num_programs(0) - 1
    tm = x_ref.shape[0]
    half = tm // 2
    halves = (slice(0, half), slice(half, tm))
    x1 = x1_scr if n_parts else x_ref

    def norm(k, z):
        return _layer_norm(z, g_ref[k:k + 1, :], b_ref[k:k + 1, :])

    def mixer_out(rs):
        z = ALPHA * x_ref[rs, :]
        for a, w in zip(parts, ws):
            z = z + _dot(a[rs, :], w[...])
        x1_scr[rs, :] = norm(0, z)

    def project(rs):
        return _dot(x1[rs, :].astype(BF16), wq_ref[...]).astype(BF16)

    def attend(q):
        outs = []
        for h in range(X_HEADS):
            sl = slice(h * X_HEAD_DIM, (h + 1) * X_HEAD_DIM)
            e, l = _softmax_rows(_dot_nt(q[:, sl], k_ref[:, sl]))
            outs.append(_dot((e / l).astype(BF16), v_ref[:, sl]))
        return jnp.concatenate(outs, axis=1).astype(BF16)

    def finish(rs, o):
        x2_scr[rs, :] = norm(1, ALPHA * x1[rs, :] + _dot(o, wo_ref[...]))

    @pl.when(t == 0)
    def _():
        z_scr[...] = jnp.zeros(z_scr.shape, F32)

    @pl.when(t < n_tiles)
    def _():
        if n_parts:
            mixer_out(halves[0])
            mixer_out(halves[1])
        qa = project(halves[0])
        qb = project(halves[1])
        oa = attend(qa)
        ob = attend(qb)
        finish(halves[0], oa)
        finish(halves[1], ob)

        x2 = x2_scr[...]
        xb = x2.astype(BF16)
        z = ALPHA * x2
        c0 = 0
        for width in FFN_CHUNKS:
            h1 = _dot(xb, wu_ref[:, c0:c0 + width])
            h2 = _dot(xb, wu_ref[:, FFN_HIDDEN + c0:FFN_HIDDEN + c0 + width])
            act = (h1 * _sigmoid_tanh(h1) * h2).astype(BF16)
            z = z + _dot(act, wd_ref[c0:c0 + width, :])
            if c0 == 0:
                o_ref[...] = norm(2, z_scr[...])
            c0 += width
        z_scr[...] = z

    @pl.when(t == n_tiles)
    def _():
        o_ref[...] = norm(2, z_scr[...])


def _tail(xf, parts, ws, kv, wq_all, wo_all, wu_all, wd_all, g_all, b_all, layer, batch, seq):
    n = xf.shape[0]
    tm = ROW_TILE
    nt = seq // tm
    n_tiles = n // tm
    n_mem = kv.shape[1]
    cur = lambda t: jnp.minimum(t, n_tiles - 1)
    row = lambda t: (cur(t), 0)
    lagged = lambda t: (jnp.maximum(t - 1, 0), 0)
    full = lambda t: (0, 0)
    this_layer = lambda t: (layer, 0, 0)
    stacked = lambda a: _resident((None,) + a.shape[1:], this_layer)
    return pl.pallas_call(
        functools.partial(_tail_kernel, n_parts=len(parts)),
        grid=(n_tiles + 1,),
        in_specs=[pl.BlockSpec((tm, D_MODEL), row)]
                 + [pl.BlockSpec((tm, a.shape[1]), row) for a in parts]
                 + [_resident(w.shape, full) for w in ws]
                 + [stacked(wq_all),
                    _resident((None, n_mem, D_MODEL), lambda t: (cur(t) // nt, 0, 0)),
                    _resident((None, n_mem, D_MODEL), lambda t: (cur(t) // nt, 0, 1)),
                    stacked(wo_all), stacked(wu_all), stacked(wd_all), stacked(g_all), stacked(b_all)],
        out_specs=pl.BlockSpec((tm, D_MODEL), lagged),
        out_shape=jax.ShapeDtypeStruct((n, D_MODEL), F32),
        scratch_shapes=[pltpu.VMEM((tm, D_MODEL), F32), pltpu.VMEM((tm, D_MODEL), F32),
                        pltpu.VMEM((tm, D_MODEL), F32)],
        compiler_params=_params(1),
        name="tail",
    )(xf, *parts, *ws, wq_all, kv, kv, wo_all, wu_all, wd_all, g_all, b_all)


def _gelu_tanh(x):
    inner = x * (GELU_C0 + GELU_C1 * (x * x))
    return x * (0.5 * jnp.tanh(inner) + 0.5)


def _sigmoid_tanh(z):
    return 0.5 * jnp.tanh(0.5 * z) + 0.5


def _odd_kernel(x_ref, pos_ref, win_ref, cw_ref, cb_ref, wax_ref, ba_ref, bx_ref, lam_ref, wout_ref,
                g_ref, b_ref, o_ref, xbuf, gate_scr, a_scr, b_scr, h_scr):
    i = pl.program_id(1)
    tm = x_ref.shape[0]
    half = tm // 2
    groups = half // SUBLANES

    @pl.when(i == 0)
    def _():
        xbuf[0:CONV_HALO, :] = jnp.zeros((CONV_HALO, RNN_WIDTH), F32)
        h_scr[...] = jnp.zeros((SUBLANES, RNN_WIDTH), F32)

    in_group = lax.broadcasted_iota(jnp.int32, (groups, SUBLANES, RNN_BLOCK_W), 1)
    lam = lam_ref[...]
    half_scale = (-0.5 * LRU_C) * (jnp.maximum(-lam, 0.0) + jnp.log1p(jnp.exp(-jnp.abs(lam))))

    def in_proj(h0):
        xb = x_ref[h0:h0 + half, :].astype(BF16)
        gate_scr[h0:h0 + half, :] = _dot(xb, win_ref[:, :RNN_WIDTH])
        xbuf[CONV_HALO + h0:CONV_HALO + h0 + half, :] = _dot(xb, win_ref[:, RNN_WIDTH:])

    def gates(h0):
        reset = pos_ref[h0:h0 + half, :] == 0
        for h in range(RNN_BLOCKS):
            sl = slice(h * RNN_BLOCK_W, (h + 1) * RNN_BLOCK_W)
            xc = cb_ref[:, sl]
            for k in range(CONV_WIDTH):
                xc = xc + xbuf[pl.ds(CONV_HALO - (CONV_WIDTH - 1) + k + h0, half), sl] * cw_ref[k:k + 1, sl]
            ri = _dot(xc.astype(BF16), wax_ref[h])
            t_r = jnp.tanh(ri[:, :RNN_BLOCK_W] + ba_ref[:, sl])
            t_i = jnp.tanh(ri[:, RNN_BLOCK_W:] + bx_ref[:, sl])
            log_a = t_r * half_scale[:, sl] + half_scale[:, sl]
            a = jnp.where(reset, 0.0, jnp.exp(log_a))
            th = jnp.tanh(log_a)
            q2 = -0.5 * th / (1.0 - th)
            half_mult = jnp.where(reset, 0.5, jnp.where(q2 > 0.0, q2 * lax.rsqrt(q2), 0.0))
            b = half_mult * (t_i * xc + xc)
            a3 = a.reshape(groups, SUBLANES, RNN_BLOCK_W)
            b3 = b.reshape(groups, SUBLANES, RNN_BLOCK_W)
            for d in (1, 2, 4):
                a_prev = pltpu.roll(a3, d, 1)
                b_prev = pltpu.roll(b3, d, 1)
                ok = in_group >= d
                b3 = jnp.where(ok, a3 * b_prev + b3, b3)
                a3 = jnp.where(ok, a3 * a_prev, a3)
            a_scr[h0:h0 + half, sl] = a3.reshape(half, RNN_BLOCK_W)
            b_scr[h0:h0 + half, sl] = b3.reshape(half, RNN_BLOCK_W)

    def recur(h0, h_prev):
        for j in range(groups):
            rs = slice(h0 + j * SUBLANES, h0 + (j + 1) * SUBLANES)
            hh = a_scr[rs, :] * h_prev + b_scr[rs, :]
            b_scr[rs, :] = hh
            h_prev = jnp.broadcast_to(hh[SUBLANES - 1:SUBLANES, :], (SUBLANES, RNN_WIDTH))
        return h_prev

    def out_proj(h0):
        rs = slice(h0, h0 + half)
        y = (b_scr[rs, :] * _gelu_tanh(gate_scr[rs, :])).astype(BF16)
        z = ALPHA * x_ref[rs, :] + _dot(y, wout_ref[...])
        o_ref[rs, :] = _layer_norm(z, g_ref[...], b_ref[...])

    in_proj(0)
    in_proj(half)
    gates(0)
    h_mid = recur(0, h_scr[...])
    gates(half)
    out_proj(0)
    h_scr[...] = recur(half, h_mid)
    out_proj(half)
    xbuf[0:CONV_HALO, :] = xbuf[tm:tm + CONV_HALO, :]


def _odd_layer(xf, posi, w_in, cw, cb, wax, ba, bx, lam, w_out, g, b, batch, seq):
    n = xf.shape[0]
    tm = ROW_TILE
    nt = seq // tm
    row = lambda bi, i: (bi * nt + i, 0)
    full = lambda bi, i: (0, 0)
    consts = [w_in, cw, cb]
    rest = [ba, bx, lam, w_out, g, b]
    return pl.pallas_call(
        _odd_kernel,
        grid=(batch, nt),
        in_specs=[pl.BlockSpec((tm, D_MODEL), row), pl.BlockSpec((tm, 1), row)]
                 + [pl.BlockSpec(a.shape, full) for a in consts]
                 + [pl.BlockSpec(wax.shape, lambda bi, i: (0, 0, 0))]
                 + [pl.BlockSpec(a.shape, full) for a in rest],
        out_specs=pl.BlockSpec((tm, D_MODEL), row),
        out_shape=jax.ShapeDtypeStruct((n, D_MODEL), F32),
        scratch_shapes=[pltpu.VMEM((CONV_HALO + tm, RNN_WIDTH), F32),
                        pltpu.VMEM((tm, RNN_WIDTH), F32),
                        pltpu.VMEM((tm, RNN_WIDTH), F32),
                        pltpu.VMEM((tm, RNN_WIDTH), F32),
                        pltpu.VMEM((SUBLANES, RNN_WIDTH), F32)],
        compiler_params=_params(2),
        name="odd_layer",
    )(xf, posi, *consts, wax, *rest)


def _even_mixer(xf, positions, w_in, pool_w, pool_scale, cmp_pos_k, cmp_pos_v, cmp_wk, cmp_wv,
                w_out, batch, seq):
    assert seq // SEL_BLOCK == LANES and seq % SEL_KEY_TILE == 0 and seq >= WINDOW + Q_BLOCK
    hd, g, r = NSA_HEAD_DIM, NSA_KV_HEADS, NSA_GROUP
    c0 = POOL_WIDTH
    c1 = c0 + Q_WIDTH
    c2 = c1 + GATE_WIDTH
    wp = w_in[:, :c0].astype(BF16)
    wq = (w_in[:, c0:c1].reshape(D_MODEL, g, r, hd).transpose(0, 2, 1, 3).reshape(D_MODEL, Q_WIDTH)
          * (hd ** -0.5 * LOG2E)).astype(BF16)
    wg = jnp.pad(w_in[:, c1:c2], ((0, 0), (0, LANES - GATE_WIDTH))).astype(BF16)
    wkv = w_in[:, c2:].astype(BF16)
    n = batch * seq
    nc = seq // CMP_STRIDE
    n_cmp = (seq - CMP_BLOCK) // CMP_STRIDE + 1
    posc = jnp.pad(positions[:, CMP_BLOCK - 1::CMP_STRIDE][:, :n_cmp], ((0, 0), (0, nc - n_cmp)))
    pos_all = jnp.concatenate([positions.reshape(1, n), posc.reshape(1, batch * nc)], axis=1).astype(F32)
    cos_t, sin_t = _rope_angles(pos_all)
    ec, es = _rope_spread()
    pool, q, kc_raw, vc_raw, ks, vs, kw, vw, gates = _proj_even(
        xf, cos_t, sin_t, ec, es, wp, wq, wkv, wg, pool_w.astype(BF16), pool_scale[None, :], batch, seq)

    eye = jnp.eye(g, dtype=F32)

    def halves(w):
        w4 = jnp.einsum('lde,gh->lgdhe', w.reshape(CMP_BLOCK, hd, hd), eye)
        w4 = w4.reshape(CMP_BLOCK, g * hd, g * hd)
        return (w4[:CMP_STRIDE].reshape(CMP_STRIDE * g * hd, g * hd).astype(BF16),
                w4[CMP_STRIDE:].reshape(CMP_STRIDE * g * hd, g * hd).astype(BF16))

    def pos_halves(p):
        p2 = jnp.tile(p[:, None, :], (1, g, 1)).reshape(CMP_BLOCK, g * hd)
        return p2[:CMP_STRIDE].reshape(1, -1), p2[CMP_STRIDE:].reshape(1, -1)

    pk1, pk2 = pos_halves(cmp_pos_k)
    pv1, pv2 = pos_halves(cmp_pos_v)
    wk1, wk2 = halves(cmp_wk)
    wv1, wv2 = halves(cmp_wv)
    seq3 = lambda a: a.reshape(batch, seq, KV_WIDTH)
    kc, vc = _compress(seq3(kc_raw), seq3(vc_raw), cos_t, sin_t, n // nc,
                       (ec, es, pk1, pk2, pv1, pv2, wk1, wk2, wv1, wv2))

    n_sb = seq // SEL_BLOCK
    starts = np.arange(nc) * CMP_STRIDE
    jb = np.arange(n_sb)
    ov = ((starts[:, None] < (jb[None, :] + 1) * SEL_BLOCK)
          & (starts[:, None] + CMP_BLOCK > jb[None, :] * SEL_BLOCK) & (np.arange(nc)[:, None] < n_cmp))
    ov = jnp.asarray(ov, BF16)
    pt = jnp.asarray(np.where(np.arange(seq)[:, None] // SEL_BLOCK == jb[None, :], NEG, 0.0), BF16)

    nsa = _nsa(q, gates, kc, vc, seq3(ks), seq3(vs), seq3(kw), seq3(vw), pt, ov, batch, seq)

    w_pool_out = w_out[:POOL_WIDTH].astype(BF16)
    w_nsa_out = (w_out[POOL_WIDTH:].reshape(g, r, hd, D_MODEL).transpose(1, 0, 2, 3)
                 .reshape(Q_WIDTH, D_MODEL).astype(BF16))
    return [pool, nsa], [w_pool_out, w_nsa_out]


def _odd_mixer(xf, posi, w_in, conv_w, conv_b, wa, ba, wx, bx, lam, w_out, ln_g, ln_b, batch, seq):
    wax = (0.5 * jnp.concatenate([wa, wx], axis=2)).astype(BF16)
    return _odd_layer(xf, posi, w_in.astype(BF16), conv_w, conv_b[None, :], wax, 0.5 * ba[None, :],
                      0.5 * bx[None, :], lam[None, :], w_out.astype(BF16), ln_g, ln_b, batch, seq)


def kernel(x, mem, positions, e_w_in, e_pool_w, e_pool_scale, e_cmp_pos_k, e_cmp_pos_v, e_cmp_wk, e_cmp_wv, e_w_out, o_w_in, o_conv_w, o_conv_b, o_wa, o_ba, o_wx, o_bx, o_lambda, o_w_out, x_wq, x_wkv, x_wo, f_w_up, f_w_down, ln_g, ln_b):
    batch, seq, d = x.shape
    n = batch * seq
    assert d == D_MODEL and seq % ROW_TILE == 0
    xf = x.reshape(n, d)
    posi = positions.reshape(n, 1)
    memf = mem.reshape(batch * mem.shape[1], d)
    wq_all = (x_wq * (X_HEAD_DIM ** -0.5 * LOG2E)).astype(BF16)
    wo_all = x_wo.astype(BF16)
    wu_all, wd_all = f_w_up.astype(BF16), f_w_down.astype(BF16)
    for layer in range(DEPTH):
        j = layer // 2
        if layer % 2 == 0:
            parts, ws = _even_mixer(xf, positions, e_w_in[j], e_pool_w[j], e_pool_scale[j], e_cmp_pos_k[j],
                                    e_cmp_pos_v[j], e_cmp_wk[j], e_cmp_wv[j], e_w_out[j], batch, seq)
        else:
            xf = _odd_mixer(xf, posi, o_w_in[j], o_conv_w[j], o_conv_b[j], o_wa[j], o_ba[j], o_wx[j], o_bx[j],
                            o_lambda[j], o_w_out[j], ln_g[layer, 0][None, :], ln_b[layer, 0][None, :], batch, seq)
            parts, ws = [], []
        kv = _mem_kv(memf, x_wkv, layer).reshape(batch, mem.shape[1], 2 * d)
        xf = _tail(xf, parts, ws, kv, wq_all, wo_all, wu_all, wd_all, ln_g, ln_b, layer, batch, seq)
    return xf.reshape(batch, seq, d)
```

```python
import functools

import numpy as np
import jax
import jax.numpy as jnp
from jax import lax
from jax.experimental import pallas as pl
from jax.experimental.pallas import tpu as pltpu

F32 = jnp.float32
BF16 = jnp.bfloat16

D_MODEL = 1024
DEPTH = 2
ALPHA = (2.0 * DEPTH) ** 0.25
LN_EPS = 1e-5
NEG = -1e30
POOL_WIDTH = D_MODEL // 2
POOL_WINDOWS = (2, 4, 8, 16)
POOL_GROUP = POOL_WIDTH // len(POOL_WINDOWS)
POOL_HALO = 16
NSA_HEADS = 8
NSA_KV_HEADS = 2
NSA_HEAD_DIM = 64
NSA_GROUP = NSA_HEADS // NSA_KV_HEADS
CMP_BLOCK = 32
CMP_STRIDE = 16
SEL_BLOCK = 64
SEL_COUNT = 16
WINDOW = 512
Q_BLOCK = 256
N_BRANCH = 3
N_FORCED = 3
LOG2E = 1.4426950408889634
ROPE_THETA = 500000.0
ROT_DIM = NSA_HEAD_DIM // 4
ROT_HALF = ROT_DIM // 2
Q_WIDTH = NSA_HEADS * NSA_HEAD_DIM
KV_WIDTH = NSA_KV_HEADS * NSA_HEAD_DIM
GATE_WIDTH = NSA_HEADS * N_BRANCH
RNN_WIDTH = 1280
RNN_BLOCKS = 10
RNN_BLOCK_W = RNN_WIDTH // RNN_BLOCKS
CONV_WIDTH = 4
LRU_C = 8.0
X_HEADS = 4
X_HEAD_DIM = D_MODEL // X_HEADS
FFN_HIDDEN = 2816
GELU_C0 = 0.7978845608028654
GELU_C1 = GELU_C0 * 0.044715

LANES = 128
SUBLANES = 8
BF16_ROWS = 16
VMEM_LIMIT = 56 * 1024 * 1024
ROW_TILE = 512
SEL_KEY_TILE = 512
FFN_CHUNKS = (1024, 1024, 768)


def _params(n_axes, vmem=VMEM_LIMIT):
    return pltpu.CompilerParams(dimension_semantics=("arbitrary",) * n_axes,
                                vmem_limit_bytes=vmem)


def _resident(shape, index_map):
    return pl.BlockSpec(shape, index_map, pipeline_mode=pl.Buffered(1))


def _dot(a, b):
    return jnp.dot(a, b, preferred_element_type=F32)


def _dot_nt(a, b):
    return lax.dot_general(a, b, (((1,), (1,)), ((), ())), preferred_element_type=F32)


def _rep_rows(a, k):
    return jnp.concatenate([a] * k, axis=0)


def _rep_lanes(a, k):
    return jnp.concatenate([a] * k, axis=1) if k > 1 else a


def _layer_norm(z, g, b):
    mu = jnp.mean(z, axis=-1, keepdims=True)
    d = z - mu
    var = jnp.mean(d * d, axis=-1, keepdims=True)
    return d * lax.rsqrt(var + LN_EPS) * g + b


def _rope(v, cos, sin):
    k = v.shape[1] // LANES
    up = pltpu.roll(v, v.shape[1] - ROT_HALF, 1)
    dn = pltpu.roll(v, ROT_HALF, 1)
    lane = lax.broadcasted_iota(jnp.int32, v.shape, 1)
    partner = jnp.where((lane & (NSA_HEAD_DIM - 1)) < ROT_HALF, up, dn)
    return v * _rep_lanes(cos, k) + partner * _rep_lanes(sin, k)


def _rope_angle_kernel(pos_ref, inv_ref, cos_o, sin_o):
    ang = inv_ref[...] * pos_ref[...]
    cos_o[...] = jnp.cos(ang)
    sin_o[...] = jnp.sin(ang)


def _rope_angles(pos_row):
    n = pos_row.shape[1]
    inv = (ROPE_THETA ** (-jnp.arange(ROT_HALF, dtype=F32) * 2.0 / ROT_DIM))[:, None]
    out = jax.ShapeDtypeStruct((ROT_HALF, n), F32)
    return pl.pallas_call(
        _rope_angle_kernel,
        out_shape=[out, out],
        name="rope_angles",
    )(pos_row, inv)


def _rope_spread():
    lane = np.arange(LANES) % NSA_HEAD_DIM
    f = np.arange(ROT_HALF)[:, None]
    lo = (lane[None, :] == f).astype(np.float32)
    hi = (lane[None, :] == f + ROT_HALF).astype(np.float32)
    return jnp.asarray(lo + hi, BF16), jnp.asarray(hi - lo, BF16)


def _rope_patterns(cos_t, sin_t, ec, es):
    def spread(t, e):
        hi = t.astype(BF16)
        r1 = t - hi.astype(F32)
        mid = r1.astype(BF16)
        lo = (r1 - mid.astype(F32)).astype(BF16)
        tn = (((0,), (0,)), ((), ()))
        return sum(lax.dot_general(p, e, tn, preferred_element_type=F32) for p in (hi, mid, lo))

    lane = lax.broadcasted_iota(jnp.int32, (1, LANES), 1)
    unrotated = jnp.where((lane & (NSA_HEAD_DIM - 1)) >= ROT_DIM, 1.0, 0.0)
    return spread(cos_t, ec) + unrotated, spread(sin_t, es)


def _proj_even_kernel(x_ref, cos_ref, sin_ref, ec_ref, es_ref, wp_ref, wq_ref, wkv_ref, wg_ref, poolw_ref, pscale_ref,
                      pool_o, q_o, kc_o, vc_o, ks_o, vs_o, kw_o, vw_o, gate_o, pbuf):
    i = pl.program_id(1)
    tm = x_ref.shape[0]
    half = tm // 2

    @pl.when(i == 0)
    def _():
        pbuf[0:POOL_HALO, :] = jnp.zeros((POOL_HALO, POOL_WIDTH), F32)

    def project(h0):
        rs = slice(h0, h0 + half)
        xb = x_ref[rs, :].astype(BF16)
        pbuf[POOL_HALO + h0:POOL_HALO + h0 + half, :] = _dot(xb, wp_ref[...])
        return _dot(xb, wq_ref[...]), _dot(xb, wkv_ref[...]), _dot(xb, wg_ref[...])

    def emit(h0, q, kv, gl):
        rs = slice(h0, h0 + half)
        cos, sin = _rope_patterns(cos_ref[:, rs], sin_ref[:, rs], ec_ref[...], es_ref[...])
        q_o[rs, :] = _rope(q, cos, sin).astype(BF16)
        kc_o[rs, :] = kv[:, 0 * LANES:1 * LANES]
        vc_o[rs, :] = kv[:, 1 * LANES:2 * LANES]
        ks_o[rs, :] = _rope(kv[:, 2 * LANES:3 * LANES], cos, sin).astype(BF16)
        vs_o[rs, :] = kv[:, 3 * LANES:4 * LANES].astype(BF16)
        kw_o[rs, :] = _rope(kv[:, 4 * LANES:5 * LANES], cos, sin).astype(BF16)
        vw_o[rs, :] = kv[:, 5 * LANES:6 * LANES].astype(BF16)
        gate_o[rs, :] = _sigmoid_tanh(gl)

    def pool(h0):
        rs = slice(h0, h0 + half)
        t1 = i * tm + h0 + lax.broadcasted_iota(jnp.int32, (half, 1), 0) + 1
        for g, w in enumerate(POOL_WINDOWS):
            sl = slice(g * POOL_GROUP, (g + 1) * POOL_GROUP)
            u = pbuf[POOL_HALO + h0:POOL_HALO + h0 + half, sl]
            tot = u
            for j in range(1, w):
                tot = tot + pbuf[pl.ds(POOL_HALO + h0 - j, half), sl]
            cnt = jnp.minimum(t1, w).astype(F32)
            pooled = tot / cnt - u
            mixed = _dot(pooled.astype(BF16), poolw_ref[g]) * pscale_ref[:, sl]
            pool_o[rs, sl] = mixed.astype(BF16)

    first = project(0)
    second = project(half)
    emit(0, *first)
    pool(0)
    emit(half, *second)
    pool(half)
    pbuf[0:POOL_HALO, :] = pbuf[tm:tm + POOL_HALO, :]


def _proj_even(xf, cos_t, sin_t, ec, es, wp, wq, wkv, wg, poolw, pscale, batch, seq):
    n = xf.shape[0]
    tm = ROW_TILE
    nt = seq // tm
    row = lambda b, i: (b * nt + i, 0)
    full = lambda b, i: (0, 0)
    outs = [
        jax.ShapeDtypeStruct((n, POOL_WIDTH), BF16),
        jax.ShapeDtypeStruct((n, Q_WIDTH), BF16),
        jax.ShapeDtypeStruct((n, KV_WIDTH), F32),
        jax.ShapeDtypeStruct((n, KV_WIDTH), F32),
        jax.ShapeDtypeStruct((n, KV_WIDTH), BF16),
        jax.ShapeDtypeStruct((n, KV_WIDTH), BF16),
        jax.ShapeDtypeStruct((n, KV_WIDTH), BF16),
        jax.ShapeDtypeStruct((n, KV_WIDTH), BF16),
        jax.ShapeDtypeStruct((n, LANES), F32),
    ]
    return pl.pallas_call(
        _proj_even_kernel,
        grid=(batch, nt),
        in_specs=[
            pl.BlockSpec((tm, D_MODEL), row),
            pl.BlockSpec((ROT_HALF, tm), lambda b, i: (0, b * nt + i)),
            pl.BlockSpec((ROT_HALF, tm), lambda b, i: (0, b * nt + i)),
            pl.BlockSpec(ec.shape, full),
            pl.BlockSpec(es.shape, full),
            pl.BlockSpec(wp.shape, full),
            pl.BlockSpec(wq.shape, full),
            pl.BlockSpec(wkv.shape, full),
            pl.BlockSpec(wg.shape, full),
            pl.BlockSpec(poolw.shape, lambda b, i: (0, 0, 0)),
            pl.BlockSpec(pscale.shape, full),
        ],
        out_specs=[pl.BlockSpec((tm, o.shape[1]), row) for o in outs],
        out_shape=outs,
        scratch_shapes=[pltpu.VMEM((POOL_HALO + tm, POOL_WIDTH), F32)],
        compiler_params=_params(2),
        name="proj_even",
    )(xf, cos_t, sin_t, ec, es, wp, wq, wkv, wg, poolw, pscale)


def _compress_kernel(rk_ref, rv_ref, cos_ref, sin_ref, ec_ref, es_ref, pk1_ref, pk2_ref, pv1_ref, pv2_ref,
                     wk1_ref, wk2_ref, wv1_ref, wv2_ref, kc_o, vc_o):
    nc = kc_o.shape[0]

    def compress(raw_ref, p1, p2, w1, w2):
        r = jnp.concatenate([raw_ref[pl.ds(j, nc, stride=CMP_STRIDE), :] for j in range(CMP_STRIDE)], axis=1)
        a = _dot((r + p1[...]).astype(BF16), w1[...])
        b = _dot((r + p2[...]).astype(BF16), w2[...])
        return a + pltpu.roll(b, nc - 1, 0)

    kc = compress(rk_ref, pk1_ref, pk2_ref, wk1_ref, wk2_ref)
    cos, sin = _rope_patterns(cos_ref[...], sin_ref[...], ec_ref[...], es_ref[...])
    kc_o[...] = _rope(kc, cos, sin).astype(BF16)
    vc_o[...] = compress(rv_ref, pv1_ref, pv2_ref, wv1_ref, wv2_ref).astype(BF16)


def _compress(rk, rv, cos_t, sin_t, first_col_block, consts):
    batch, seq, width = rk.shape
    nc = seq // CMP_STRIDE
    blk = lambda b: (b, 0, 0)
    full = lambda b: (0, 0)
    out = jax.ShapeDtypeStruct((batch, nc, KV_WIDTH), BF16)
    return pl.pallas_call(
        _compress_kernel,
        grid=(batch,),
        in_specs=[pl.BlockSpec((None, seq, width), blk), pl.BlockSpec((None, seq, width), blk),
                  pl.BlockSpec((ROT_HALF, nc), lambda b: (0, first_col_block + b)),
                  pl.BlockSpec((ROT_HALF, nc), lambda b: (0, first_col_block + b))]
                 + [pl.BlockSpec(a.shape, full) for a in consts],
        out_specs=[pl.BlockSpec((None, nc, KV_WIDTH), blk)] * 2,
        out_shape=[out, out],
        compiler_params=_params(1),
        name="compress",
    )(rk, rv, cos_t, sin_t, *consts)


def _softmax_rows(s):
    m = jnp.max(s, axis=1, keepdims=True)
    e = jnp.exp2(s - m)
    return e, jnp.sum(e, axis=1, keepdims=True)


def _dot_exact01(x, m01):
    hi = x.astype(BF16)
    r1 = x - hi.astype(F32)
    mid = r1.astype(BF16)
    lo = (r1 - mid.astype(F32)).astype(BF16)
    return _dot(hi, m01) + _dot(mid, m01) + _dot(lo, m01)


def _unselected_blocks(imp, q0, n_pick):
    nq, nb = imp.shape
    col = lax.broadcasted_iota(jnp.int32, (nq, nb), 1)
    cur = lax.shift_right_arithmetic(q0 + lax.broadcasted_iota(jnp.int32, (nq, nb), 0), 6)
    forced = (col == 0) | (col == cur) | (col == cur - 1)
    val = jnp.where(forced, -jnp.inf, jnp.where(col > cur, -1.0, imp))
    vt = val.T
    blk = lax.broadcasted_iota(jnp.int32, (nb, nq), 0).astype(F32)
    unsel = jnp.where(forced, 0.0, 1.0).T
    for _ in range(n_pick - N_FORCED):
        m = jnp.max(vt, axis=0, keepdims=True)
        first = jnp.min(jnp.where(vt == m, blk, float(nb)), axis=0, keepdims=True)
        hit = blk == first
        unsel = jnp.where(hit, 0.0, unsel)
        vt = jnp.where(hit, -jnp.inf, vt)
    return unsel.T


def _nsa_kernel(q_ref, gate_ref, kc_ref, vc_ref, ks_ref, vs_ref, kw_ref, vw_ref, pt_ref, ov_ref,
                o_ref, m_scr, l_scr, acc_scr, lhs_scr, oc_scr, ow_scr, s0_scr, s1_scr, w_scr, gx_scr, *, n_pick):
    qb = pl.program_id(1)
    q0 = qb * Q_BLOCK
    tk = SEL_KEY_TILE
    ncmp = kc_ref.shape[0]
    wk = WINDOW + Q_BLOCK
    rows = NSA_GROUP * Q_BLOCK
    lane = lax.broadcasted_iota(jnp.int32, (Q_BLOCK, LANES), 1)

    def trow(width):
        return q0 + lax.broadcasted_iota(jnp.int32, (Q_BLOCK, width), 0)

    def kcol(width):
        return lax.broadcasted_iota(jnp.int32, (Q_BLOCK, width), 1)

    bias_c = jnp.where(kcol(ncmp) * CMP_STRIDE + (CMP_BLOCK - 1) <= trow(ncmp), 0.0, NEG)
    has_cmp = (trow(1) >= CMP_BLOCK - 1).astype(F32)
    wstart = pl.multiple_of(jnp.maximum(q0 - WINDOW, 0), Q_BLOCK)
    kpos_w = wstart + kcol(wk)
    bias_w = jnp.where(kpos_w <= trow(wk), jnp.where(kpos_w > trow(wk) - WINDOW, 0.0, NEG), NEG)

    for g in range(NSA_KV_HEADS):
        mine = (lane >= NSA_HEAD_DIM) if g else (lane < NSA_HEAD_DIM)
        lhs_scr[g, :, 0:LANES] = jnp.concatenate(
            [jnp.where(mine, q_ref[:, r * LANES:(r + 1) * LANES], jnp.zeros((), BF16))
             for r in range(NSA_GROUP)], axis=0)
        m_scr[g] = jnp.full((rows, LANES), -jnp.inf, F32)
        l_scr[g] = jnp.zeros((rows, LANES), F32)
        acc_scr[g] = jnp.zeros((rows, LANES), F32)

    def cmp_scores(g):
        s0_scr[g] = _dot_nt(lhs_scr[g, :, 0:LANES], kc_ref[...])

    def cmp_attend(g):
        e, l = _softmax_rows(s0_scr[g] + _rep_rows(bias_c, NSA_GROUP))
        p = e * (_rep_rows(has_cmp, NSA_GROUP) / l)
        oc_scr[g] = _dot(p.astype(BF16), vc_ref[...])
        psum = p[0:Q_BLOCK]
        for r in range(1, NSA_GROUP):
            psum = psum + p[r * Q_BLOCK:(r + 1) * Q_BLOCK]
        return _dot_exact01(psum, ov_ref[...])

    def select(g, imp):
        unsel = _unselected_blocks(imp, q0, n_pick)
        lhs_scr[g, :, LANES:2 * LANES] = _rep_rows(unsel.astype(BF16), NSA_GROUP)

    def win_scores(g):
        w_scr[g] = _dot_nt(lhs_scr[g, :, 0:LANES], kw_ref[pl.ds(wstart, wk), :])

    def win_attend(g):
        e, l = _softmax_rows(w_scr[g] + _rep_rows(bias_w, NSA_GROUP))
        ow_scr[g] = _dot(e.astype(BF16), vw_ref[pl.ds(wstart, wk), :]) / l

    def expand_gates():
        low = lane < NSA_HEAD_DIM
        for r in range(NSA_GROUP):
            for br in range(N_BRANCH):
                c_lo = r * N_BRANCH + br
                c_hi = (NSA_GROUP + r) * N_BRANCH + br
                gx_scr[c_lo] = jnp.where(low, gate_ref[:, c_lo:c_lo + 1], gate_ref[:, c_hi:c_hi + 1])

    cmp_scores(0)
    cmp_scores(1)
    expand_gates()
    win_scores(0)
    imp0 = cmp_attend(0)
    win_scores(1)
    imp1 = cmp_attend(1)
    select(0, imp0)
    win_attend(0)
    select(1, imp1)
    win_attend(1)

    def scores(kt, buf, g):
        k0 = pl.multiple_of(kt * tk, tk)
        rhs = jnp.concatenate([ks_ref[pl.ds(k0, tk), :], pt_ref[pl.ds(k0, tk), :]], axis=1)
        buf[g] = _dot_nt(lhs_scr[g], rhs)

    def consume(kt, buf, g, causal):
        k0 = pl.multiple_of(kt * tk, tk)
        v = vs_ref[pl.ds(k0, tk), :]
        bias = jnp.where(k0 + kcol(tk) <= trow(tk), 0.0, NEG) if causal else None
        for r in range(NSA_GROUP):
            rs = slice(r * Q_BLOCK, (r + 1) * Q_BLOCK)
            s = buf[g, rs, :]
            if causal:
                s = s + bias
            m_prev = m_scr[g, rs, :]
            m_new = jnp.maximum(m_prev, jnp.max(s, axis=1, keepdims=True))
            alpha = jnp.exp2(m_prev - m_new)
            pe = jnp.exp2(s - _rep_lanes(m_new, tk // LANES))
            part = pe[:, 0:LANES]
            for c in range(1, tk // LANES):
                part = part + pe[:, c * LANES:(c + 1) * LANES]
            l_scr[g, rs, :] = alpha * l_scr[g, rs, :] + part
            acc_scr[g, rs, :] = alpha * acc_scr[g, rs, :] + _dot(pe.astype(BF16), v)
            m_scr[g, rs, :] = m_new

    def step(nxt, nxt_buf, cur, cur_buf, causal=False):
        for g in range(NSA_KV_HEADS):
            if nxt is not None:
                scores(nxt, nxt_buf, g)
        for g in range(NSA_KV_HEADS):
            consume(cur, cur_buf, g, causal)

    n_full = q0 // tk
    for g in range(NSA_KV_HEADS):
        scores(0, s0_scr, g)

    def pair(j, carry):
        step(2 * j + 1, s1_scr, 2 * j, s0_scr)
        step(2 * j + 2, s0_scr, 2 * j + 1, s1_scr)
        return carry

    lax.fori_loop(0, n_full // 2, pair, 0)

    @pl.when(n_full % 2 == 0)
    def _():
        step(None, None, n_full, s0_scr, causal=True)

    @pl.when(n_full % 2 == 1)
    def _():
        step(n_full, s1_scr, n_full - 1, s0_scr)
        step(None, None, n_full, s1_scr, causal=True)

    low = lane < NSA_HEAD_DIM
    for r in range(NSA_GROUP):
        rs = slice(r * Q_BLOCK, (r + 1) * Q_BLOCK)
        merged = lambda scr: jnp.where(low, scr[0, rs, :], scr[1, rs, :])
        l_sel = jnp.where(low, jnp.sum(l_scr[0, rs, :], axis=1, keepdims=True),
                          jnp.sum(l_scr[1, rs, :], axis=1, keepdims=True))
        out = (gx_scr[r * N_BRANCH] * merged(oc_scr)
               + gx_scr[r * N_BRANCH + 1] * (merged(acc_scr) / l_sel)
               + gx_scr[r * N_BRANCH + 2] * merged(ow_scr))
        o_ref[:, r * LANES:(r + 1) * LANES] = out.astype(BF16)


def _nsa(q, gates, kc, vc, ks, vs, kw, vw, pt, ov, batch, seq):
    n = q.shape[0]
    nqb = seq // Q_BLOCK
    rows = NSA_GROUP * Q_BLOCK
    row = lambda b, i: (b * nqb + i, 0)
    per_b = lambda b, i: (b, 0, 0)
    full = lambda b, i: (0, 0)
    seq_spec = _resident((None, seq, KV_WIDTH), per_b)
    cmp_spec = _resident((None, kc.shape[1], KV_WIDTH), per_b)
    n_pick = min(SEL_COUNT, seq // SEL_BLOCK)
    return pl.pallas_call(
        functools.partial(_nsa_kernel, n_pick=n_pick),
        grid=(batch, nqb),
        in_specs=[pl.BlockSpec((Q_BLOCK, Q_WIDTH), row), pl.BlockSpec((Q_BLOCK, LANES), row),
                  cmp_spec, cmp_spec, seq_spec, seq_spec, seq_spec, seq_spec,
                  _resident(pt.shape, full), _resident(ov.shape, full)],
        out_specs=pl.BlockSpec((Q_BLOCK, Q_WIDTH), row),
        out_shape=jax.ShapeDtypeStruct((n, Q_WIDTH), BF16),
        scratch_shapes=[pltpu.VMEM((NSA_KV_HEADS, rows, LANES), F32),
                        pltpu.VMEM((NSA_KV_HEADS, rows, LANES), F32),
                        pltpu.VMEM((NSA_KV_HEADS, rows, LANES), F32),
                        pltpu.VMEM((NSA_KV_HEADS, rows, 2 * LANES), BF16),
                        pltpu.VMEM((NSA_KV_HEADS, rows, LANES), F32),
                        pltpu.VMEM((NSA_KV_HEADS, rows, LANES), F32),
                        pltpu.VMEM((NSA_KV_HEADS, rows, SEL_KEY_TILE), F32),
                        pltpu.VMEM((NSA_KV_HEADS, rows, SEL_KEY_TILE), F32),
                        pltpu.VMEM((NSA_KV_HEADS, rows, WINDOW + Q_BLOCK), F32),
                        pltpu.VMEM((NSA_GROUP * N_BRANCH, Q_BLOCK, LANES), F32)],
        compiler_params=_params(2),
        name="nsa",
    )(q, gates, kc, vc, ks, vs, kw, vw, pt, ov)


def _matmul_kernel(a_ref, w_ref, o_ref):
    o_ref[...] = _dot(a_ref[...].astype(BF16), w_ref[...].astype(BF16)).astype(o_ref.dtype)


def _mem_kv(memf, wkv_all, layer):
    m, k = memf.shape
    n = wkv_all.shape[2]
    return pl.pallas_call(
        _matmul_kernel,
        grid=(1,),
        in_specs=[pl.BlockSpec((m, k), lambda i: (0, 0)), pl.BlockSpec((None, k, n), lambda i: (layer, 0, 0))],
        out_specs=pl.BlockSpec((m, n), lambda i: (0, 0)),
        out_shape=jax.ShapeDtypeStruct((m, n), BF16),
        compiler_params=_params(1),
        name="mem_kv",
    )(memf, wkv_all)


def _tail_kernel(*refs, n_parts):
    x_ref = refs[0]
    parts = refs[1:1 + n_parts]
    ws = refs[1 + n_parts:1 + 2 * n_parts]
    wq_ref, k_ref, v_ref, wo_ref, wu_ref, wd_ref, g_ref, b_ref, o_ref, x1_scr, x2_scr, z_scr = refs[1 + 2 * n_parts:]
    t = pl.program_id(0)
    n_tiles = pl.num_programs(0) - 1
    tm = x_ref.shape[0]
    half = tm // 2
    halves = (slice(0, half), slice(half, tm))
    x1 = x1_scr if n_parts else x_ref

    def norm(k, z):
        return _layer_norm(z, g_ref[k:k + 1, :], b_ref[k:k + 1, :])

    def mixer_out(rs):
        z = ALPHA * x_ref[rs, :]
        for a, w in zip(parts, ws):
            z = z + _dot(a[rs, :], w[...])
        x1_scr[rs, :] = norm(0, z)

    def project(rs):
        return _dot(x1[rs, :].astype(BF16), wq_ref[...]).astype(BF16)

    heads = [slice(h * X_HEAD_DIM, (h + 1) * X_HEAD_DIM) for h in range(X_HEADS)]

    def scores(q):
        return [_dot_nt(q[:, sl], k_ref[:, sl]) for sl in heads]

    def probabilities(ss):
        probs = []
        for s in ss:
            e, l = _softmax_rows(s)
            probs.append((e / l).astype(BF16))
        return probs

    def mix(probs):
        outs = [_dot(p, v_ref[:, sl]) for p, sl in zip(probs, heads)]
        return jnp.concatenate(outs, axis=1).astype(BF16)

    def finish(rs, o):
        x2_scr[rs, :] = norm(1, ALPHA * x1[rs, :] + _dot(o, wo_ref[...]))

    @pl.when(t == 0)
    def _():
        z_scr[...] = jnp.zeros(z_scr.shape, F32)

    @pl.when(t < n_tiles)
    def _():
        if n_parts:
            mixer_out(halves[0])
            mixer_out(halves[1])
        qa = project(halves[0])
        qb = project(halves[1])
        sa = scores(qa)
        sb = scores(qb)
        pa = probabilities(sa)
        pb = probabilities(sb)
        oa = mix(pa)
        ob = mix(pb)
        finish(halves[0], oa)
        finish(halves[1], ob)

        x2 = x2_scr[...]
        xb = x2.astype(BF16)
        z = ALPHA * x2
        c0 = 0
        for width in FFN_CHUNKS:
            h1 = _dot(xb, wu_ref[:, c0:c0 + width])
            h2 = _dot(xb, wu_ref[:, FFN_HIDDEN + c0:FFN_HIDDEN + c0 + width])
            act = (h1 * _sigmoid_tanh(h1) * h2).astype(BF16)
            z = z + _dot(act, wd_ref[c0:c0 + width, :])
            if c0 == 0:
                o_ref[...] = norm(2, z_scr[...])
            c0 += width
        z_scr[...] = z

    @pl.when(t == n_tiles)
    def _():
        o_ref[...] = norm(2, z_scr[...])


def _tail(xf, parts, ws, kv, wq_all, wo_all, wu_all, wd_all, g_all, b_all, layer, batch, seq):
    n = xf.shape[0]
    tm = ROW_TILE
    nt = seq // tm
    n_tiles = n // tm
    n_mem = kv.shape[1]
    cur = lambda t: jnp.minimum(t, n_tiles - 1)
    row = lambda t: (cur(t), 0)
    lagged = lambda t: (jnp.maximum(t - 1, 0), 0)
    full = lambda t: (0, 0)
    this_layer = lambda t: (layer, 0, 0)
    stacked = lambda a: _resident((None,) + a.shape[1:], this_layer)
    return pl.pallas_call(
        functools.partial(_tail_kernel, n_parts=len(parts)),
        grid=(n_tiles + 1,),
        in_specs=[pl.BlockSpec((tm, D_MODEL), row)]
                 + [pl.BlockSpec((tm, a.shape[1]), row) for a in parts]
                 + [_resident(w.shape, full) for w in ws]
                 + [stacked(wq_all),
                    _resident((None, n_mem, D_MODEL), lambda t: (cur(t) // nt, 0, 0)),
                    _resident((None, n_mem, D_MODEL), lambda t: (cur(t) // nt, 0, 1)),
                    stacked(wo_all), stacked(wu_all), stacked(wd_all), stacked(g_all), stacked(b_all)],
        out_specs=pl.BlockSpec((tm, D_MODEL), lagged),
        out_shape=jax.ShapeDtypeStruct((n, D_MODEL), F32),
        scratch_shapes=[pltpu.VMEM((tm, D_MODEL), F32), pltpu.VMEM((tm, D_MODEL), F32),
                        pltpu.VMEM((tm, D_MODEL), F32)],
        compiler_params=_params(1),
        name="tail",
    )(xf, *parts, *ws, wq_all, kv, kv, wo_all, wu_all, wd_all, g_all, b_all)


def _gelu_tanh(x):
    inner = x * (GELU_C0 + GELU_C1 * (x * x))
    return x * (0.5 * jnp.tanh(inner) + 0.5)


def _sigmoid_tanh(z):
    return 0.5 * jnp.tanh(0.5 * z) + 0.5


def _odd_kernel(x_ref, pos_ref, win_ref, cw_ref, cb_ref, wax_ref, ba_ref, bx_ref, lam_ref, wout_ref,
                g_ref, b_ref, o_ref, xbuf, xp_scr, gate_scr, a_scr, b_scr, tail_scr, h_scr):
    i = pl.program_id(1)
    tm = x_ref.shape[0]
    half = tm // 2
    nv = half // SUBLANES
    halo = (CONV_WIDTH - 1) * SUBLANES

    @pl.when(i == 0)
    def _():
        tail_scr[...] = jnp.zeros((SUBLANES, RNN_WIDTH), F32)
        h_scr[...] = jnp.zeros((SUBLANES, RNN_WIDTH), F32)

    def permute(v2d):
        return jnp.swapaxes(v2d.reshape(SUBLANES, nv, v2d.shape[1]), 0, 1).reshape(v2d.shape)

    def unpermute(v2d):
        return jnp.swapaxes(v2d.reshape(nv, SUBLANES, v2d.shape[1]), 0, 1).reshape(v2d.shape)

    sub = lax.broadcasted_iota(jnp.int32, (SUBLANES, RNN_WIDTH), 0)
    lam = lam_ref[...]
    half_scale = (-0.5 * LRU_C) * (jnp.maximum(-lam, 0.0) + jnp.log1p(jnp.exp(-jnp.abs(lam))))

    def in_proj(hi, h0):
        xp = permute(x_ref[h0:h0 + half, :])
        xp_scr[h0:h0 + half, :] = xp
        xb = xp.astype(BF16)
        gate_scr[h0:h0 + half, :] = _dot(xb, win_ref[:, :RNN_WIDTH])
        xbuf[hi, halo:halo + half, :] = _dot(xb, win_ref[:, RNN_WIDTH:])

    def gates(hi, h0):
        for j in range(1, CONV_WIDTH):
            last = xbuf[hi, halo + (nv - j) * SUBLANES:halo + (nv - j + 1) * SUBLANES, :]
            before = jnp.where(sub == 0, tail_scr[j:j + 1, :], pltpu.roll(last, 1, 0))
            xbuf[hi, halo - j * SUBLANES:halo - (j - 1) * SUBLANES, :] = before
            tail_scr[j:j + 1, :] = last[SUBLANES - 1:SUBLANES, :]
        is_reset = (pos_ref[h0:h0 + half, :] == 0).astype(F32)
        reset = permute(jnp.broadcast_to(is_reset, (half, RNN_BLOCK_W))) > 0.5
        for h in range(RNN_BLOCKS):
            sl = slice(h * RNN_BLOCK_W, (h + 1) * RNN_BLOCK_W)
            xc = cb_ref[:, sl]
            for k in range(CONV_WIDTH):
                xc = xc + xbuf[hi, k * SUBLANES:k * SUBLANES + half, sl] * cw_ref[k:k + 1, sl]
            ri = _dot(xc.astype(BF16), wax_ref[h])
            t_r = jnp.tanh(ri[:, :RNN_BLOCK_W] + ba_ref[:, sl])
            t_i = jnp.tanh(ri[:, RNN_BLOCK_W:] + bx_ref[:, sl])
            log_a = t_r * half_scale[:, sl] + half_scale[:, sl]
            a = jnp.where(reset, 0.0, jnp.exp(log_a))
            th = jnp.tanh(log_a)
            q2 = -0.5 * th / (1.0 - th)
            half_mult = jnp.where(reset, 0.5, jnp.where(q2 > 0.0, q2 * lax.rsqrt(q2), 0.0))
            b = half_mult * (t_i * xc + xc)
            a_scr[h0:h0 + half, sl] = a
            b_scr[h0:h0 + half, sl] = b

    def recur(h0, h_prev):
        decay = jnp.ones((SUBLANES, RNN_WIDTH), F32)
        resp = jnp.zeros((SUBLANES, RNN_WIDTH), F32)
        for v in range(nv):
            rs = slice(h0 + v * SUBLANES, h0 + (v + 1) * SUBLANES)
            av = a_scr[rs, :]
            decay = av * decay
            resp = av * resp + b_scr[rs, :]
            a_scr[rs, :] = decay
            b_scr[rs, :] = resp
        ca, cb_ = decay, resp
        for d in (1, 2, 4):
            ok = sub >= d
            cb_ = jnp.where(ok, ca * pltpu.roll(cb_, d, 0) + cb_, cb_)
            ca = jnp.where(ok, ca * pltpu.roll(ca, d, 0), ca)
        h_end = ca * h_prev + cb_
        h_in = jnp.where(sub == 0, h_prev, pltpu.roll(h_end, 1, 0))
        rs = slice(h0, h0 + half)
        hh = (a_scr[rs, :].reshape(nv, SUBLANES, RNN_WIDTH) * h_in[None]
              + b_scr[rs, :].reshape(nv, SUBLANES, RNN_WIDTH))
        b_scr[rs, :] = hh.reshape(half, RNN_WIDTH)
        return jnp.broadcast_to(h_end[SUBLANES - 1:SUBLANES, :], (SUBLANES, RNN_WIDTH))

    def out_proj(h0):
        rs = slice(h0, h0 + half)
        y = (b_scr[rs, :] * _gelu_tanh(gate_scr[rs, :])).astype(BF16)
        z = ALPHA * xp_scr[rs, :] + _dot(y, wout_ref[...])
        o_ref[rs, :] = unpermute(_layer_norm(z, g_ref[...], b_ref[...]))

    in_proj(0, 0)
    in_proj(1, half)
    gates(0, 0)
    h_mid = recur(0, h_scr[...])
    gates(1, half)
    out_proj(0)
    h_scr[...] = recur(half, h_mid)
    out_proj(half)


def _odd_layer(xf, posi, w_in, cw, cb, wax, ba, bx, lam, w_out, g, b, batch, seq):
    n = xf.shape[0]
    tm = ROW_TILE
    nt = seq // tm
    row = lambda bi, i: (bi * nt + i, 0)
    full = lambda bi, i: (0, 0)
    consts = [w_in, cw, cb]
    rest = [ba, bx, lam, w_out, g, b]
    return pl.pallas_call(
        _odd_kernel,
        grid=(batch, nt),
        in_specs=[pl.BlockSpec((tm, D_MODEL), row), pl.BlockSpec((tm, 1), row)]
                 + [pl.BlockSpec(a.shape, full) for a in consts]
                 + [pl.BlockSpec(wax.shape, lambda bi, i: (0, 0, 0))]
                 + [pl.BlockSpec(a.shape, full) for a in rest],
        out_specs=pl.BlockSpec((tm, D_MODEL), row),
        out_shape=jax.ShapeDtypeStruct((n, D_MODEL), F32),
        scratch_shapes=[pltpu.VMEM((2, (CONV_WIDTH - 1) * SUBLANES + tm // 2, RNN_WIDTH), F32),
                        pltpu.VMEM((tm, D_MODEL), F32),
                        pltpu.VMEM((tm, RNN_WIDTH), F32),
                        pltpu.VMEM((tm, RNN_WIDTH), F32),
                        pltpu.VMEM((tm, RNN_WIDTH), F32),
                        pltpu.VMEM((SUBLANES, RNN_WIDTH), F32),
                        pltpu.VMEM((SUBLANES, RNN_WIDTH), F32)],
        compiler_params=_params(2),
        name="odd_layer",
    )(xf, posi, *consts, wax, *rest)


def _even_mixer(xf, positions, w_in, pool_w, pool_scale, cmp_pos_k, cmp_pos_v, cmp_wk, cmp_wv,
                w_out, batch, seq):
    assert seq // SEL_BLOCK == LANES and seq % SEL_KEY_TILE == 0 and seq >= WINDOW + Q_BLOCK
    hd, g, r = NSA_HEAD_DIM, NSA_KV_HEADS, NSA_GROUP
    c0 = POOL_WIDTH
    c1 = c0 + Q_WIDTH
    c2 = c1 + GATE_WIDTH
    wp = w_in[:, :c0].astype(BF16)
    wq = (w_in[:, c0:c1].reshape(D_MODEL, g, r, hd).transpose(0, 2, 1, 3).reshape(D_MODEL, Q_WIDTH)
          * (hd ** -0.5 * LOG2E)).astype(BF16)
    wg = jnp.pad(w_in[:, c1:c2], ((0, 0), (0, LANES - GATE_WIDTH))).astype(BF16)
    wkv = w_in[:, c2:].astype(BF16)
    n = batch * seq
    nc = seq // CMP_STRIDE
    n_cmp = (seq - CMP_BLOCK) // CMP_STRIDE + 1
    posc = jnp.pad(positions[:, CMP_BLOCK - 1::CMP_STRIDE][:, :n_cmp], ((0, 0), (0, nc - n_cmp)))
    pos_all = jnp.concatenate([positions.reshape(1, n), posc.reshape(1, batch * nc)], axis=1).astype(F32)
    cos_t, sin_t = _rope_angles(pos_all)
    ec, es = _rope_spread()
    pool, q, kc_raw, vc_raw, ks, vs, kw, vw, gates = _proj_even(
        xf, cos_t, sin_t, ec, es, wp, wq, wkv, wg, pool_w.astype(BF16), pool_scale[None, :], batch, seq)

    eye = jnp.eye(g, dtype=F32)

    def halves(w):
        w4 = jnp.einsum('lde,gh->lgdhe', w.reshape(CMP_BLOCK, hd, hd), eye)
        w4 = w4.reshape(CMP_BLOCK, g * hd, g * hd)
        return (w4[:CMP_STRIDE].reshape(CMP_STRIDE * g * hd, g * hd).astype(BF16),
                w4[CMP_STRIDE:].reshape(CMP_STRIDE * g * hd, g * hd).astype(BF16))

    def pos_halves(p):
        p2 = jnp.tile(p[:, None, :], (1, g, 1)).reshape(CMP_BLOCK, g * hd)
        return p2[:CMP_STRIDE].reshape(1, -1), p2[CMP_STRIDE:].reshape(1, -1)

    pk1, pk2 = pos_halves(cmp_pos_k)
    pv1, pv2 = pos_halves(cmp_pos_v)
    wk1, wk2 = halves(cmp_wk)
    wv1, wv2 = halves(cmp_wv)
    seq3 = lambda a: a.reshape(batch, seq, KV_WIDTH)
    kc, vc = _compress(seq3(kc_raw), seq3(vc_raw), cos_t, sin_t, n // nc,
                       (ec, es, pk1, pk2, pv1, pv2, wk1, wk2, wv1, wv2))

    n_sb = seq // SEL_BLOCK
    starts = np.arange(nc) * CMP_STRIDE
    jb = np.arange(n_sb)
    ov = ((starts[:, None] < (jb[None, :] + 1) * SEL_BLOCK)
          & (starts[:, None] + CMP_BLOCK > jb[None, :] * SEL_BLOCK) & (np.arange(nc)[:, None] < n_cmp))
    ov = jnp.asarray(ov, BF16)
    pt = jnp.asarray(np.where(np.arange(seq)[:, None] // SEL_BLOCK == jb[None, :], NEG, 0.0), BF16)

    nsa = _nsa(q, gates, kc, vc, seq3(ks), seq3(vs), seq3(kw), seq3(vw), pt, ov, batch, seq)

    w_pool_out = w_out[:POOL_WIDTH].astype(BF16)
    w_nsa_out = (w_out[POOL_WIDTH:].reshape(g, r, hd, D_MODEL).transpose(1, 0, 2, 3)
                 .reshape(Q_WIDTH, D_MODEL).astype(BF16))
    return [pool, nsa], [w_pool_out, w_nsa_out]


def _odd_mixer(xf, posi, w_in, conv_w, conv_b, wa, ba, wx, bx, lam, w_out, ln_g, ln_b, batch, seq):
    wax = (0.5 * jnp.concatenate([wa, wx], axis=2)).astype(BF16)
    return _odd_layer(xf, posi, w_in.astype(BF16), conv_w, conv_b[None, :], wax, 0.5 * ba[None, :],
                      0.5 * bx[None, :], lam[None, :], w_out.astype(BF16), ln_g, ln_b, batch, seq)


def kernel(x, mem, positions, e_w_in, e_pool_w, e_pool_scale, e_cmp_pos_k, e_cmp_pos_v, e_cmp_wk, e_cmp_wv, e_w_out, o_w_in, o_conv_w, o_conv_b, o_wa, o_ba, o_wx, o_bx, o_lambda, o_w_out, x_wq, x_wkv, x_wo, f_w_up, f_w_down, ln_g, ln_b):
    batch, seq, d = x.shape
    n = batch * seq
    assert d == D_MODEL and seq % ROW_TILE == 0
    xf = x.reshape(n, d)
    posi = positions.reshape(n, 1)
    memf = mem.reshape(batch * mem.shape[1], d)
    wq_all = (x_wq * (X_HEAD_DIM ** -0.5 * LOG2E)).astype(BF16)
    wo_all = x_wo.astype(BF16)
    wu_all, wd_all = f_w_up.astype(BF16), f_w_down.astype(BF16)
    for layer in range(DEPTH):
        j = layer // 2
        if layer % 2 == 0:
            parts, ws = _even_mixer(xf, positions, e_w_in[j], e_pool_w[j], e_pool_scale[j], e_cmp_pos_k[j],
                                    e_cmp_pos_v[j], e_cmp_wk[j], e_cmp_wv[j], e_w_out[j], batch, seq)
        else:
            xf = _odd_mixer(xf, posi, o_w_in[j], o_conv_w[j], o_conv_b[j], o_wa[j], o_ba[j], o_wx[j], o_bx[j],
                            o_lambda[j], o_w_out[j], ln_g[layer, 0][None, :], ln_b[layer, 0][None, :], batch, seq)
            parts, ws = [], []
        kv = _mem_kv(memf, x_wkv, layer).reshape(batch, mem.shape[1], 2 * d)
        xf = _tail(xf, parts, ws, kv, wq_all, wo_all, wu_all, wd_all, ln_g, ln_b, layer, batch, seq)
    return xf.reshape(batch, seq, d)
```

```python
import functools

import numpy as np
import jax
import jax.numpy as jnp
from jax import lax
from jax.experimental import pallas as pl
from jax.experimental.pallas import tpu as pltpu

F32 = jnp.float32
BF16 = jnp.bfloat16

D_MODEL = 1024
DEPTH = 2
ALPHA = (2.0 * DEPTH) ** 0.25
LN_EPS = 1e-5
NEG = -1e30
POOL_WIDTH = D_MODEL // 2
POOL_WINDOWS = (2, 4, 8, 16)
POOL_GROUP = POOL_WIDTH // len(POOL_WINDOWS)
POOL_HALO = 16
NSA_HEADS = 8
NSA_KV_HEADS = 2
NSA_HEAD_DIM = 64
NSA_GROUP = NSA_HEADS // NSA_KV_HEADS
CMP_BLOCK = 32
CMP_STRIDE = 16
SEL_BLOCK = 64
SEL_COUNT = 16
WINDOW = 512
Q_BLOCK = 256
N_BRANCH = 3
N_FORCED = 3
LOG2E = 1.4426950408889634
ROPE_THETA = 500000.0
ROT_DIM = NSA_HEAD_DIM // 4
ROT_HALF = ROT_DIM // 2
Q_WIDTH = NSA_HEADS * NSA_HEAD_DIM
KV_WIDTH = NSA_KV_HEADS * NSA_HEAD_DIM
GATE_WIDTH = NSA_HEADS * N_BRANCH
RNN_WIDTH = 1280
RNN_BLOCKS = 10
RNN_BLOCK_W = RNN_WIDTH // RNN_BLOCKS
CONV_WIDTH = 4
LRU_C = 8.0
X_HEADS = 4
X_HEAD_DIM = D_MODEL // X_HEADS
FFN_HIDDEN = 2816
GELU_C0 = 0.7978845608028654
GELU_C1 = GELU_C0 * 0.044715

LANES = 128
SUBLANES = 8
BF16_ROWS = 16
VMEM_LIMIT = 56 * 1024 * 1024
ROW_TILE = 512
SEL_KEY_TILE = 512
FFN_CHUNKS = (1024, 1024, 768)


def _params(n_axes, vmem=VMEM_LIMIT):
    return pltpu.CompilerParams(dimension_semantics=("arbitrary",) * n_axes,
                                vmem_limit_bytes=vmem)


def _resident(shape, index_map):
    return pl.BlockSpec(shape, index_map, pipeline_mode=pl.Buffered(1))


def _dot(a, b):
    return jnp.dot(a, b, preferred_element_type=F32)


def _dot_nt(a, b):
    return lax.dot_general(a, b, (((1,), (1,)), ((), ())), preferred_element_type=F32)


def _rep_rows(a, k):
    return jnp.concatenate([a] * k, axis=0)


def _rep_lanes(a, k):
    return jnp.concatenate([a] * k, axis=1) if k > 1 else a


def _layer_norm(z, g, b):
    mu = jnp.mean(z, axis=-1, keepdims=True)
    d = z - mu
    var = jnp.mean(d * d, axis=-1, keepdims=True)
    return d * lax.rsqrt(var + LN_EPS) * g + b


def _rope(v, cos, sin):
    k = v.shape[1] // LANES
    up = pltpu.roll(v, v.shape[1] - ROT_HALF, 1)
    dn = pltpu.roll(v, ROT_HALF, 1)
    lane = lax.broadcasted_iota(jnp.int32, v.shape, 1)
    partner = jnp.where((lane & (NSA_HEAD_DIM - 1)) < ROT_HALF, up, dn)
    return v * _rep_lanes(cos, k) + partner * _rep_lanes(sin, k)


def _rope_angle_kernel(pos_ref, inv_ref, cos_o, sin_o):
    ang = inv_ref[...] * pos_ref[...]
    cos_o[...] = jnp.cos(ang)
    sin_o[...] = jnp.sin(ang)


def _rope_angles(pos_row):
    n = pos_row.shape[1]
    inv = (ROPE_THETA ** (-jnp.arange(ROT_HALF, dtype=F32) * 2.0 / ROT_DIM))[:, None]
    out = jax.ShapeDtypeStruct((ROT_HALF, n), F32)
    return pl.pallas_call(
        _rope_angle_kernel,
        out_shape=[out, out],
        name="rope_angles",
    )(pos_row, inv)


def _rope_spread():
    lane = np.arange(LANES) % NSA_HEAD_DIM
    f = np.arange(ROT_HALF)[:, None]
    lo = (lane[None, :] == f).astype(np.float32)
    hi = (lane[None, :] == f + ROT_HALF).astype(np.float32)
    return jnp.asarray(lo + hi, BF16), jnp.asarray(hi - lo, BF16)


def _rope_patterns(cos_t, sin_t, ec, es):
    def spread(t, e):
        hi = t.astype(BF16)
        r1 = t - hi.astype(F32)
        mid = r1.astype(BF16)
        lo = (r1 - mid.astype(F32)).astype(BF16)
        tn = (((0,), (0,)), ((), ()))
        return sum(lax.dot_general(p, e, tn, preferred_element_type=F32) for p in (hi, mid, lo))

    lane = lax.broadcasted_iota(jnp.int32, (1, LANES), 1)
    unrotated = jnp.where((lane & (NSA_HEAD_DIM - 1)) >= ROT_DIM, 1.0, 0.0)
    return spread(cos_t, ec) + unrotated, spread(sin_t, es)


def _proj_even_kernel(x_ref, cos_ref, sin_ref, ec_ref, es_ref, wp_ref, wq_ref, wkv_ref, wg_ref, poolw_ref, pscale_ref,
                      pool_o, q_o, kc_o, vc_o, ks_o, vs_o, kw_o, vw_o, gate_o, pbuf):
    i = pl.program_id(1)
    tm = x_ref.shape[0]
    half = tm // 2

    @pl.when(i == 0)
    def _():
        pbuf[0:POOL_HALO, :] = jnp.zeros((POOL_HALO, POOL_WIDTH), F32)

    def project(h0):
        rs = slice(h0, h0 + half)
        xb = x_ref[rs, :].astype(BF16)
        pbuf[POOL_HALO + h0:POOL_HALO + h0 + half, :] = _dot(xb, wp_ref[...])
        return _dot(xb, wq_ref[...]), _dot(xb, wkv_ref[...]), _dot(xb, wg_ref[...])

    def emit(h0, q, kv, gl):
        rs = slice(h0, h0 + half)
        cos, sin = _rope_patterns(cos_ref[:, rs], sin_ref[:, rs], ec_ref[...], es_ref[...])
        q_o[rs, :] = _rope(q, cos, sin).astype(BF16)
        kc_o[rs, :] = kv[:, 0 * LANES:1 * LANES]
        vc_o[rs, :] = kv[:, 1 * LANES:2 * LANES]
        ks_o[rs, :] = _rope(kv[:, 2 * LANES:3 * LANES], cos, sin).astype(BF16)
        vs_o[rs, :] = kv[:, 3 * LANES:4 * LANES].astype(BF16)
        kw_o[rs, :] = _rope(kv[:, 4 * LANES:5 * LANES], cos, sin).astype(BF16)
        vw_o[rs, :] = kv[:, 5 * LANES:6 * LANES].astype(BF16)
        gate_o[rs, :] = _sigmoid_tanh(gl)

    def pool(h0):
        rs = slice(h0, h0 + half)
        t1 = i * tm + h0 + lax.broadcasted_iota(jnp.int32, (half, 1), 0) + 1
        for g, w in enumerate(POOL_WINDOWS):
            sl = slice(g * POOL_GROUP, (g + 1) * POOL_GROUP)
            u = pbuf[POOL_HALO + h0:POOL_HALO + h0 + half, sl]
            tot = u
            for j in range(1, w):
                tot = tot + pbuf[pl.ds(POOL_HALO + h0 - j, half), sl]
            cnt = jnp.minimum(t1, w).astype(F32)
            pooled = tot / cnt - u
            mixed = _dot(pooled.astype(BF16), poolw_ref[g]) * pscale_ref[:, sl]
            pool_o[rs, sl] = mixed.astype(BF16)

    first = project(0)
    second = project(half)
    emit(0, *first)
    pool(0)
    emit(half, *second)
    pool(half)
    pbuf[0:POOL_HALO, :] = pbuf[tm:tm + POOL_HALO, :]


def _proj_even(xf, cos_t, sin_t, ec, es, wp, wq, wkv, wg, poolw, pscale, batch, seq):
    n = xf.shape[0]
    tm = ROW_TILE
    nt = seq // tm
    row = lambda b, i: (b * nt + i, 0)
    full = lambda b, i: (0, 0)
    outs = [
        jax.ShapeDtypeStruct((n, POOL_WIDTH), BF16),
        jax.ShapeDtypeStruct((n, Q_WIDTH), BF16),
        jax.ShapeDtypeStruct((n, KV_WIDTH), F32),
        jax.ShapeDtypeStruct((n, KV_WIDTH), F32),
        jax.ShapeDtypeStruct((n, KV_WIDTH), BF16),
        jax.ShapeDtypeStruct((n, KV_WIDTH), BF16),
        jax.ShapeDtypeStruct((n, KV_WIDTH), BF16),
        jax.ShapeDtypeStruct((n, KV_WIDTH), BF16),
        jax.ShapeDtypeStruct((n, LANES), F32),
    ]
    return pl.pallas_call(
        _proj_even_kernel,
        grid=(batch, nt),
        in_specs=[
            pl.BlockSpec((tm, D_MODEL), row),
            pl.BlockSpec((ROT_HALF, tm), lambda b, i: (0, b * nt + i)),
            pl.BlockSpec((ROT_HALF, tm), lambda b, i: (0, b * nt + i)),
            pl.BlockSpec(ec.shape, full),
            pl.BlockSpec(es.shape, full),
            pl.BlockSpec(wp.shape, full),
            pl.BlockSpec(wq.shape, full),
            pl.BlockSpec(wkv.shape, full),
            pl.BlockSpec(wg.shape, full),
            pl.BlockSpec(poolw.shape, lambda b, i: (0, 0, 0)),
            pl.BlockSpec(pscale.shape, full),
        ],
        out_specs=[pl.BlockSpec((tm, o.shape[1]), row) for o in outs],
        out_shape=outs,
        scratch_shapes=[pltpu.VMEM((POOL_HALO + tm, POOL_WIDTH), F32)],
        compiler_params=_params(2),
        name="proj_even",
    )(xf, cos_t, sin_t, ec, es, wp, wq, wkv, wg, poolw, pscale)


def _compress_kernel(rk_ref, rv_ref, cos_ref, sin_ref, ec_ref, es_ref, pk1_ref, pk2_ref, pv1_ref, pv2_ref,
                     wk1_ref, wk2_ref, wv1_ref, wv2_ref, kc_o, vc_o):
    nc = kc_o.shape[0]

    def compress(raw_ref, p1, p2, w1, w2):
        r = jnp.concatenate([raw_ref[pl.ds(j, nc, stride=CMP_STRIDE), :] for j in range(CMP_STRIDE)], axis=1)
        a = _dot((r + p1[...]).astype(BF16), w1[...])
        b = _dot((r + p2[...]).astype(BF16), w2[...])
        return a + pltpu.roll(b, nc - 1, 0)

    kc = compress(rk_ref, pk1_ref, pk2_ref, wk1_ref, wk2_ref)
    cos, sin = _rope_patterns(cos_ref[...], sin_ref[...], ec_ref[...], es_ref[...])
    kc_o[...] = _rope(kc, cos, sin).astype(BF16)
    vc_o[...] = compress(rv_ref, pv1_ref, pv2_ref, wv1_ref, wv2_ref).astype(BF16)


def _compress(rk, rv, cos_t, sin_t, first_col_block, consts):
    batch, seq, width = rk.shape
    nc = seq // CMP_STRIDE
    blk = lambda b: (b, 0, 0)
    full = lambda b: (0, 0)
    out = jax.ShapeDtypeStruct((batch, nc, KV_WIDTH), BF16)
    return pl.pallas_call(
        _compress_kernel,
        grid=(batch,),
        in_specs=[pl.BlockSpec((None, seq, width), blk), pl.BlockSpec((None, seq, width), blk),
                  pl.BlockSpec((ROT_HALF, nc), lambda b: (0, first_col_block + b)),
                  pl.BlockSpec((ROT_HALF, nc), lambda b: (0, first_col_block + b))]
                 + [pl.BlockSpec(a.shape, full) for a in consts],
        out_specs=[pl.BlockSpec((None, nc, KV_WIDTH), blk)] * 2,
        out_shape=[out, out],
        compiler_params=_params(1),
        name="compress",
    )(rk, rv, cos_t, sin_t, *consts)


def _softmax_rows(s):
    m = jnp.max(s, axis=1, keepdims=True)
    e = jnp.exp2(s - m)
    return e, jnp.sum(e, axis=1, keepdims=True)


def _dot_exact01(x, m01):
    hi = x.astype(BF16)
    r1 = x - hi.astype(F32)
    mid = r1.astype(BF16)
    lo = (r1 - mid.astype(F32)).astype(BF16)
    return _dot(hi, m01) + _dot(mid, m01) + _dot(lo, m01)


def _unselected_blocks(imp, q0, n_pick):
    nq, nb = imp.shape
    col = lax.broadcasted_iota(jnp.int32, (nq, nb), 1)
    cur = lax.shift_right_arithmetic(q0 + lax.broadcasted_iota(jnp.int32, (nq, nb), 0), 6)
    forced = (col == 0) | (col == cur) | (col == cur - 1)
    val = jnp.where(forced, -jnp.inf, jnp.where(col > cur, -1.0, imp))
    vt = val.T
    blk = lax.broadcasted_iota(jnp.int32, (nb, nq), 0).astype(F32)
    unsel = jnp.where(forced, 0.0, 1.0).T
    for _ in range(n_pick - N_FORCED):
        m = jnp.max(vt, axis=0, keepdims=True)
        first = jnp.min(jnp.where(vt == m, blk, float(nb)), axis=0, keepdims=True)
        hit = blk == first
        unsel = jnp.where(hit, 0.0, unsel)
        vt = jnp.where(hit, -jnp.inf, vt)
    return unsel.T


def _nsa_kernel(q_ref, gate_ref, kc_ref, vc_ref, ks_ref, vs_ref, kw_ref, vw_ref, pt_ref, ov_ref,
                o_ref, m_scr, l_scr, acc_scr, lhs_scr, oc_scr, ow_scr, s0_scr, s1_scr, w_scr, gx_scr, *, n_pick):
    qb = pl.program_id(1)
    q0 = qb * Q_BLOCK
    tk = SEL_KEY_TILE
    ncmp = kc_ref.shape[0]
    wk = WINDOW + Q_BLOCK
    rows = NSA_GROUP * Q_BLOCK
    lane = lax.broadcasted_iota(jnp.int32, (Q_BLOCK, LANES), 1)

    def trow(width):
        return q0 + lax.broadcasted_iota(jnp.int32, (Q_BLOCK, width), 0)

    def kcol(width):
        return lax.broadcasted_iota(jnp.int32, (Q_BLOCK, width), 1)

    bias_c = jnp.where(kcol(ncmp) * CMP_STRIDE + (CMP_BLOCK - 1) <= trow(ncmp), 0.0, NEG)
    has_cmp = (trow(1) >= CMP_BLOCK - 1).astype(F32)
    wstart = pl.multiple_of(jnp.maximum(q0 - WINDOW, 0), Q_BLOCK)
    kpos_w = wstart + kcol(wk)
    bias_w = jnp.where(kpos_w <= trow(wk), jnp.where(kpos_w > trow(wk) - WINDOW, 0.0, NEG), NEG)

    for g in range(NSA_KV_HEADS):
        mine = (lane >= NSA_HEAD_DIM) if g else (lane < NSA_HEAD_DIM)
        lhs_scr[g, :, 0:LANES] = jnp.concatenate(
            [jnp.where(mine, q_ref[:, r * LANES:(r + 1) * LANES], jnp.zeros((), BF16))
             for r in range(NSA_GROUP)], axis=0)
        m_scr[g] = jnp.full((rows, LANES), -jnp.inf, F32)
        l_scr[g] = jnp.zeros((rows, LANES), F32)
        acc_scr[g] = jnp.zeros((rows, LANES), F32)

    def cmp_scores(g):
        s0_scr[g] = _dot_nt(lhs_scr[g, :, 0:LANES], kc_ref[...])

    def cmp_attend(g):
        e, l = _softmax_rows(s0_scr[g] + _rep_rows(bias_c, NSA_GROUP))
        p = e * (_rep_rows(has_cmp, NSA_GROUP) / l)
        oc_scr[g] = _dot(p.astype(BF16), vc_ref[...])
        psum = p[0:Q_BLOCK]
        for r in range(1, NSA_GROUP):
            psum = psum + p[r * Q_BLOCK:(r + 1) * Q_BLOCK]
        return _dot_exact01(psum, ov_ref[...])

    def select(g, imp):
        unsel = _unselected_blocks(imp, q0, n_pick)
        lhs_scr[g, :, LANES:2 * LANES] = _rep_rows(unsel.astype(BF16), NSA_GROUP)

    def win_scores(g):
        w_scr[g] = _dot_nt(lhs_scr[g, :, 0:LANES], kw_ref[pl.ds(wstart, wk), :])

    def win_attend(g):
        e, l = _softmax_rows(w_scr[g] + _rep_rows(bias_w, NSA_GROUP))
        ow_scr[g] = _dot(e.astype(BF16), vw_ref[pl.ds(wstart, wk), :]) / l

    def expand_gates():
        low = lane < NSA_HEAD_DIM
        for r in range(NSA_GROUP):
            for br in range(N_BRANCH):
                c_lo = r * N_BRANCH + br
                c_hi = (NSA_GROUP + r) * N_BRANCH + br
                gx_scr[c_lo] = jnp.where(low, gate_ref[:, c_lo:c_lo + 1], gate_ref[:, c_hi:c_hi + 1])

    cmp_scores(0)
    cmp_scores(1)
    expand_gates()
    win_scores(0)
    imp0 = cmp_attend(0)
    win_scores(1)
    imp1 = cmp_attend(1)
    select(0, imp0)
    win_attend(0)
    select(1, imp1)
    win_attend(1)

    def scores(kt, buf, g):
        k0 = pl.multiple_of(kt * tk, tk)
        rhs = jnp.concatenate([ks_ref[pl.ds(k0, tk), :], pt_ref[pl.ds(k0, tk), :]], axis=1)
        buf[g] = _dot_nt(lhs_scr[g], rhs)

    def consume(kt, buf, g, causal):
        k0 = pl.multiple_of(kt * tk, tk)
        v = vs_ref[pl.ds(k0, tk), :]
        bias = jnp.where(k0 + kcol(tk) <= trow(tk), 0.0, NEG) if causal else None
        for r in range(NSA_GROUP):
            rs = slice(r * Q_BLOCK, (r + 1) * Q_BLOCK)
            s = buf[g, rs, :]
            if causal:
                s = s + bias
            m_prev = m_scr[g, rs, :]
            m_new = jnp.maximum(m_prev, jnp.max(s, axis=1, keepdims=True))
            alpha = jnp.exp2(m_prev - m_new)
            pe = jnp.exp2(s - _rep_lanes(m_new, tk // LANES))
            part = pe[:, 0:LANES]
            for c in range(1, tk // LANES):
                part = part + pe[:, c * LANES:(c + 1) * LANES]
            l_scr[g, rs, :] = alpha * l_scr[g, rs, :] + part
            acc_scr[g, rs, :] = alpha * acc_scr[g, rs, :] + _dot(pe.astype(BF16), v)
            m_scr[g, rs, :] = m_new

    def step(nxt, nxt_buf, cur, cur_buf, causal=False):
        for g in range(NSA_KV_HEADS):
            if nxt is not None:
                scores(nxt, nxt_buf, g)
        for g in range(NSA_KV_HEADS):
            consume(cur, cur_buf, g, causal)

    n_full = q0 // tk
    for g in range(NSA_KV_HEADS):
        scores(0, s0_scr, g)

    def pair(j, carry):
        step(2 * j + 1, s1_scr, 2 * j, s0_scr)
        step(2 * j + 2, s0_scr, 2 * j + 1, s1_scr)
        return carry

    lax.fori_loop(0, n_full // 2, pair, 0)

    @pl.when(n_full % 2 == 0)
    def _():
        step(None, None, n_full, s0_scr, causal=True)

    @pl.when(n_full % 2 == 1)
    def _():
        step(n_full, s1_scr, n_full - 1, s0_scr)
        step(None, None, n_full, s1_scr, causal=True)

    low = lane < NSA_HEAD_DIM
    for r in range(NSA_GROUP):
        rs = slice(r * Q_BLOCK, (r + 1) * Q_BLOCK)
        merged = lambda scr: jnp.where(low, scr[0, rs, :], scr[1, rs, :])
        l_sel = jnp.where(low, jnp.sum(l_scr[0, rs, :], axis=1, keepdims=True),
                          jnp.sum(l_scr[1, rs, :], axis=1, keepdims=True))
        out = (gx_scr[r * N_BRANCH] * merged(oc_scr)
               + gx_scr[r * N_BRANCH + 1] * (merged(acc_scr) / l_sel)
               + gx_scr[r * N_BRANCH + 2] * merged(ow_scr))
        o_ref[:, r * LANES:(r + 1) * LANES] = out.astype(BF16)


def _nsa(q, gates, kc, vc, ks, vs, kw, vw, pt, ov, batch, seq):
    n = q.shape[0]
    nqb = seq // Q_BLOCK
    rows = NSA_GROUP * Q_BLOCK
    row = lambda b, i: (b * nqb + i, 0)
    per_b = lambda b, i: (b, 0, 0)
    full = lambda b, i: (0, 0)
    seq_spec = _resident((None, seq, KV_WIDTH), per_b)
    cmp_spec = _resident((None, kc.shape[1], KV_WIDTH), per_b)
    n_pick = min(SEL_COUNT, seq // SEL_BLOCK)
    return pl.pallas_call(
        functools.partial(_nsa_kernel, n_pick=n_pick),
        grid=(batch, nqb),
        in_specs=[pl.BlockSpec((Q_BLOCK, Q_WIDTH), row), pl.BlockSpec((Q_BLOCK, LANES), row),
                  cmp_spec, cmp_spec, seq_spec, seq_spec, seq_spec, seq_spec,
                  _resident(pt.shape, full), _resident(ov.shape, full)],
        out_specs=pl.BlockSpec((Q_BLOCK, Q_WIDTH), row),
        out_shape=jax.ShapeDtypeStruct((n, Q_WIDTH), BF16),
        scratch_shapes=[pltpu.VMEM((NSA_KV_HEADS, rows, LANES), F32),
                        pltpu.VMEM((NSA_KV_HEADS, rows, LANES), F32),
                        pltpu.VMEM((NSA_KV_HEADS, rows, LANES), F32),
                        pltpu.VMEM((NSA_KV_HEADS, rows, 2 * LANES), BF16),
                        pltpu.VMEM((NSA_KV_HEADS, rows, LANES), F32),
                        pltpu.VMEM((NSA_KV_HEADS, rows, LANES), F32),
                        pltpu.VMEM((NSA_KV_HEADS, rows, SEL_KEY_TILE), F32),
                        pltpu.VMEM((NSA_KV_HEADS, rows, SEL_KEY_TILE), F32),
                        pltpu.VMEM((NSA_KV_HEADS, rows, WINDOW + Q_BLOCK), F32),
                        pltpu.VMEM((NSA_GROUP * N_BRANCH, Q_BLOCK, LANES), F32)],
        compiler_params=_params(2),
        name="nsa",
    )(q, gates, kc, vc, ks, vs, kw, vw, pt, ov)


def _matmul_kernel(a_ref, w_ref, o_ref):
    o_ref[...] = _dot(a_ref[...].astype(BF16), w_ref[...].astype(BF16)).astype(o_ref.dtype)


def _mem_kv(memf, wkv_all, layer):
    m, k = memf.shape
    n = wkv_all.shape[2]
    return pl.pallas_call(
        _matmul_kernel,
        grid=(1,),
        in_specs=[pl.BlockSpec((m, k), lambda i: (0, 0)), pl.BlockSpec((None, k, n), lambda i: (layer, 0, 0))],
        out_specs=pl.BlockSpec((m, n), lambda i: (0, 0)),
        out_shape=jax.ShapeDtypeStruct((m, n), BF16),
        compiler_params=_params(1),
        name="mem_kv",
    )(memf, wkv_all)


def _tail_kernel(*refs, n_parts):
    x_ref = refs[0]
    parts = refs[1:1 + n_parts]
    ws = refs[1 + n_parts:1 + 2 * n_parts]
    wq_ref, k_ref, v_ref, wo_ref, wu_ref, wd_ref, g_ref, b_ref, o_ref, x1_scr, x2_scr, z_scr = refs[1 + 2 * n_parts:]
    t = pl.program_id(0)
    n_tiles = pl.num_programs(0) - 1
    tm = x_ref.shape[0]
    half = tm // 2
    halves = (slice(0, half), slice(half, tm))
    x1 = x1_scr if n_parts else x_ref

    def norm(k, z):
        return _layer_norm(z, g_ref[k:k + 1, :], b_ref[k:k + 1, :])

    def mixer_out(rs):
        z = ALPHA * x_ref[rs, :]
        for a, w in zip(parts, ws):
            z = z + _dot(a[rs, :], w[...])
        x1_scr[rs, :] = norm(0, z)

    def project(rs):
        return _dot(x1[rs, :].astype(BF16), wq_ref[...]).astype(BF16)

    heads = [slice(h * X_HEAD_DIM, (h + 1) * X_HEAD_DIM) for h in range(X_HEADS)]

    def scores(q):
        return [_dot_nt(q[:, sl], k_ref[:, sl]) for sl in heads]

    def probabilities(ss):
        probs = []
        for s in ss:
            e, l = _softmax_rows(s)
            probs.append((e / l).astype(BF16))
        return probs

    def mix(probs):
        outs = [_dot(p, v_ref[:, sl]) for p, sl in zip(probs, heads)]
        return jnp.concatenate(outs, axis=1).astype(BF16)

    def finish(rs, o):
        x2_scr[rs, :] = norm(1, ALPHA * x1[rs, :] + _dot(o, wo_ref[...]))

    @pl.when(t == 0)
    def _():
        z_scr[...] = jnp.zeros(z_scr.shape, F32)

    @pl.when(t < n_tiles)
    def _():
        if n_parts:
            mixer_out(halves[0])
            mixer_out(halves[1])
        qa = project(halves[0])
        qb = project(halves[1])
        sa = scores(qa)
        sb = scores(qb)
        pa = probabilities(sa)
        pb = probabilities(sb)
        oa = mix(pa)
        ob = mix(pb)
        finish(halves[0], oa)
        finish(halves[1], ob)

        x2 = x2_scr[...]
        xb = x2.astype(BF16)
        z = ALPHA * x2
        c0 = 0
        for width in FFN_CHUNKS:
            h1 = _dot(xb, wu_ref[:, c0:c0 + width])
            h2 = _dot(xb, wu_ref[:, FFN_HIDDEN + c0:FFN_HIDDEN + c0 + width])
            act = (h1 * _sigmoid_tanh(h1) * h2).astype(BF16)
            z = z + _dot(act, wd_ref[c0:c0 + width, :])
            if c0 == 0:
                o_ref[...] = norm(2, z_scr[...])
            c0 += width
        z_scr[...] = z

    @pl.when(t == n_tiles)
    def _():
        o_ref[...] = norm(2, z_scr[...])


def _tail(xf, parts, ws, kv, wq_all, wo_all, wu_all, wd_all, g_all, b_all, layer, batch, seq):
    n = xf.shape[0]
    tm = ROW_TILE
    nt = seq // tm
    n_tiles = n // tm
    n_mem = kv.shape[1]
    cur = lambda t: jnp.minimum(t, n_tiles - 1)
    row = lambda t: (cur(t), 0)
    lagged = lambda t: (jnp.maximum(t - 1, 0), 0)
    full = lambda t: (0, 0)
    this_layer = lambda t: (layer, 0, 0)
    stacked = lambda a: _resident((None,) + a.shape[1:], this_layer)
    return pl.pallas_call(
        functools.partial(_tail_kernel, n_parts=len(parts)),
        grid=(n_tiles + 1,),
        in_specs=[pl.BlockSpec((tm, D_MODEL), row)]
                 + [pl.BlockSpec((tm, a.shape[1]), row) for a in parts]
                 + [_resident(w.shape, full) for w in ws]
                 + [stacked(wq_all),
                    _resident((None, n_mem, D_MODEL), lambda t: (cur(t) // nt, 0, 0)),
                    _resident((None, n_mem, D_MODEL), lambda t: (cur(t) // nt, 0, 1)),
                    stacked(wo_all), stacked(wu_all), stacked(wd_all), stacked(g_all), stacked(b_all)],
        out_specs=pl.BlockSpec((tm, D_MODEL), lagged),
        out_shape=jax.ShapeDtypeStruct((n, D_MODEL), F32),
        scratch_shapes=[pltpu.VMEM((tm, D_MODEL), F32), pltpu.VMEM((tm, D_MODEL), F32),
                        pltpu.VMEM((tm, D_MODEL), F32)],
        compiler_params=_params(1),
        name="tail",
    )(xf, *parts, *ws, wq_all, kv, kv, wo_all, wu_all, wd_all, g_all, b_all)


def _gelu_tanh(x):
    inner = x * (GELU_C0 + GELU_C1 * (x * x))
    return x * (0.5 * jnp.tanh(inner) + 0.5)


def _sigmoid_tanh(z):
    return 0.5 * jnp.tanh(0.5 * z) + 0.5


def _odd_kernel(x_ref, pos_ref, perm_ref, unperm_ref, win_ref, cw_ref, cb_ref, wax_ref, ba_ref, bx_ref, lam_ref,
                wout_ref, g_ref, b_ref, o_ref, xbuf, gate_scr, a_scr, b_scr, tail_scr, h_scr):
    i = pl.program_id(1)
    tm = x_ref.shape[0]
    half = tm // 2
    nv = half // SUBLANES
    halo = (CONV_WIDTH - 1) * SUBLANES

    @pl.when(i == 0)
    def _():
        tail_scr[...] = jnp.zeros((SUBLANES, RNN_WIDTH), F32)
        h_scr[...] = jnp.zeros((SUBLANES, RNN_WIDTH), F32)

    def permute(v2d):
        return jnp.swapaxes(v2d.reshape(SUBLANES, nv, v2d.shape[1]), 0, 1).reshape(v2d.shape)

    sub = lax.broadcasted_iota(jnp.int32, (SUBLANES, RNN_WIDTH), 0)
    lam = lam_ref[...]
    half_scale = (-0.5 * LRU_C) * (jnp.maximum(-lam, 0.0) + jnp.log1p(jnp.exp(-jnp.abs(lam))))

    def in_proj(hi, h0):
        xb = _dot(perm_ref[...], x_ref[h0:h0 + half, :].astype(BF16)).astype(BF16)
        gate_scr[h0:h0 + half, :] = _dot(xb, win_ref[:, :RNN_WIDTH])
        xbuf[hi, halo:halo + half, :] = _dot(xb, win_ref[:, RNN_WIDTH:])

    def gates(hi, h0):
        for j in range(1, CONV_WIDTH):
            last = xbuf[hi, halo + (nv - j) * SUBLANES:halo + (nv - j + 1) * SUBLANES, :]
            before = jnp.where(sub == 0, tail_scr[j:j + 1, :], pltpu.roll(last, 1, 0))
            xbuf[hi, halo - j * SUBLANES:halo - (j - 1) * SUBLANES, :] = before
            tail_scr[j:j + 1, :] = last[SUBLANES - 1:SUBLANES, :]
        is_reset = (pos_ref[h0:h0 + half, :] == 0).astype(F32)
        reset = permute(jnp.broadcast_to(is_reset, (half, RNN_BLOCK_W))) > 0.5
        for h in range(RNN_BLOCKS):
            sl = slice(h * RNN_BLOCK_W, (h + 1) * RNN_BLOCK_W)
            xc = cb_ref[:, sl]
            for k in range(CONV_WIDTH):
                xc = xc + xbuf[hi, k * SUBLANES:k * SUBLANES + half, sl] * cw_ref[k:k + 1, sl]
            ri = _dot(xc.astype(BF16), wax_ref[h])
            t_r = jnp.tanh(ri[:, :RNN_BLOCK_W] + ba_ref[:, sl])
            t_i = jnp.tanh(ri[:, RNN_BLOCK_W:] + bx_ref[:, sl])
            log_a = t_r * half_scale[:, sl] + half_scale[:, sl]
            a = jnp.where(reset, 0.0, jnp.exp(log_a))
            th = jnp.tanh(log_a)
            q2 = -0.5 * th / (1.0 - th)
            half_mult = jnp.where(reset, 0.5, jnp.where(q2 > 0.0, q2 * lax.rsqrt(q2), 0.0))
            b = half_mult * (t_i * xc + xc)
            a_scr[h0:h0 + half, sl] = a
            b_scr[h0:h0 + half, sl] = b

    def recur(h0, h_prev):
        decay = jnp.ones((SUBLANES, RNN_WIDTH), F32)
        resp = jnp.zeros((SUBLANES, RNN_WIDTH), F32)
        for v in range(nv):
            rs = slice(h0 + v * SUBLANES, h0 + (v + 1) * SUBLANES)
            av = a_scr[rs, :]
            decay = av * decay
            resp = av * resp + b_scr[rs, :]
            a_scr[rs, :] = decay
            b_scr[rs, :] = resp
        ca, cb_ = decay, resp
        for d in (1, 2, 4):
            ok = sub >= d
            cb_ = jnp.where(ok, ca * pltpu.roll(cb_, d, 0) + cb_, cb_)
            ca = jnp.where(ok, ca * pltpu.roll(ca, d, 0), ca)
        h_end = ca * h_prev + cb_
        h_in = jnp.where(sub == 0, h_prev, pltpu.roll(h_end, 1, 0))
        rs = slice(h0, h0 + half)
        hh = (a_scr[rs, :].reshape(nv, SUBLANES, RNN_WIDTH) * h_in[None]
              + b_scr[rs, :].reshape(nv, SUBLANES, RNN_WIDTH))
        b_scr[rs, :] = hh.reshape(half, RNN_WIDTH)
        return jnp.broadcast_to(h_end[SUBLANES - 1:SUBLANES, :], (SUBLANES, RNN_WIDTH))

    def out_proj(h0):
        rs = slice(h0, h0 + half)
        yp = (b_scr[rs, :] * _gelu_tanh(gate_scr[rs, :])).astype(BF16)
        y = _dot(unperm_ref[...], yp).astype(BF16)
        z = ALPHA * x_ref[rs, :] + _dot(y, wout_ref[...])
        o_ref[rs, :] = _layer_norm(z, g_ref[...], b_ref[...])

    in_proj(0, 0)
    in_proj(1, half)
    gates(0, 0)
    h_mid = recur(0, h_scr[...])
    gates(1, half)
    out_proj(0)
    h_scr[...] = recur(half, h_mid)
    out_proj(half)


def _odd_layer(xf, posi, w_in, cw, cb, wax, ba, bx, lam, w_out, g, b, batch, seq):
    n = xf.shape[0]
    tm = ROW_TILE
    nt = seq // tm
    row = lambda bi, i: (bi * nt + i, 0)
    full = lambda bi, i: (0, 0)
    half = tm // 2
    rho = np.arange(half)
    src_time = (rho % SUBLANES) * (half // SUBLANES) + rho // SUBLANES
    perm = np.zeros((half, half), np.float32)
    perm[rho, src_time] = 1.0
    consts = [jnp.asarray(perm, BF16), jnp.asarray(perm.T, BF16), w_in, cw, cb]
    rest = [ba, bx, lam, w_out, g, b]
    return pl.pallas_call(
        _odd_kernel,
        grid=(batch, nt),
        in_specs=[pl.BlockSpec((tm, D_MODEL), row), pl.BlockSpec((tm, 1), row)]
                 + [pl.BlockSpec(a.shape, full) for a in consts]
                 + [pl.BlockSpec(wax.shape, lambda bi, i: (0, 0, 0))]
                 + [pl.BlockSpec(a.shape, full) for a in rest],
        out_specs=pl.BlockSpec((tm, D_MODEL), row),
        out_shape=jax.ShapeDtypeStruct((n, D_MODEL), F32),
        scratch_shapes=[pltpu.VMEM((2, (CONV_WIDTH - 1) * SUBLANES + tm // 2, RNN_WIDTH), F32),
                        pltpu.VMEM((tm, RNN_WIDTH), F32),
                        pltpu.VMEM((tm, RNN_WIDTH), F32),
                        pltpu.VMEM((tm, RNN_WIDTH), F32),
                        pltpu.VMEM((SUBLANES, RNN_WIDTH), F32),
                        pltpu.VMEM((SUBLANES, RNN_WIDTH), F32)],
        compiler_params=_params(2),
        name="odd_layer",
    )(xf, posi, *consts, wax, *rest)


def _even_mixer(xf, positions, w_in, pool_w, pool_scale, cmp_pos_k, cmp_pos_v, cmp_wk, cmp_wv,
                w_out, batch, seq):
    assert seq // SEL_BLOCK == LANES and seq % SEL_KEY_TILE == 0 and seq >= WINDOW + Q_BLOCK
    hd, g, r = NSA_HEAD_DIM, NSA_KV_HEADS, NSA_GROUP
    c0 = POOL_WIDTH
    c1 = c0 + Q_WIDTH
    c2 = c1 + GATE_WIDTH
    wp = w_in[:, :c0].astype(BF16)
    wq = (w_in[:, c0:c1].reshape(D_MODEL, g, r, hd).transpose(0, 2, 1, 3).reshape(D_MODEL, Q_WIDTH)
          * (hd ** -0.5 * LOG2E)).astype(BF16)
    wg = jnp.pad(w_in[:, c1:c2], ((0, 0), (0, LANES - GATE_WIDTH))).astype(BF16)
    wkv = w_in[:, c2:].astype(BF16)
    n = batch * seq
    nc = seq // CMP_STRIDE
    n_cmp = (seq - CMP_BLOCK) // CMP_STRIDE + 1
    posc = jnp.pad(positions[:, CMP_BLOCK - 1::CMP_STRIDE][:, :n_cmp], ((0, 0), (0, nc - n_cmp)))
    pos_all = jnp.concatenate([positions.reshape(1, n), posc.reshape(1, batch * nc)], axis=1).astype(F32)
    cos_t, sin_t = _rope_angles(pos_all)
    ec, es = _rope_spread()
    pool, q, kc_raw, vc_raw, ks, vs, kw, vw, gates = _proj_even(
        xf, cos_t, sin_t, ec, es, wp, wq, wkv, wg, pool_w.astype(BF16), pool_scale[None, :], batch, seq)

    def halves(w):
        w3 = w.reshape(CMP_BLOCK, hd, hd).astype(BF16)
        zero = jnp.zeros_like(w3)
        w4 = jnp.concatenate([jnp.concatenate([w3 if k == j else zero for k in range(g)], axis=2)
                              for j in range(g)], axis=1)
        return (w4[:CMP_STRIDE].reshape(CMP_STRIDE * g * hd, g * hd),
                w4[CMP_STRIDE:].reshape(CMP_STRIDE * g * hd, g * hd))

    def pos_halves(p):
        p2 = jnp.tile(p[:, None, :], (1, g, 1)).reshape(CMP_BLOCK, g * hd)
        return p2[:CMP_STRIDE].reshape(1, -1), p2[CMP_STRIDE:].reshape(1, -1)

    pk1, pk2 = pos_halves(cmp_pos_k)
    pv1, pv2 = pos_halves(cmp_pos_v)
    wk1, wk2 = halves(cmp_wk)
    wv1, wv2 = halves(cmp_wv)
    seq3 = lambda a: a.reshape(batch, seq, KV_WIDTH)
    kc, vc = _compress(seq3(kc_raw), seq3(vc_raw), cos_t, sin_t, n // nc,
                       (ec, es, pk1, pk2, pv1, pv2, wk1, wk2, wv1, wv2))

    n_sb = seq // SEL_BLOCK
    starts = np.arange(nc) * CMP_STRIDE
    jb = np.arange(n_sb)
    ov = ((starts[:, None] < (jb[None, :] + 1) * SEL_BLOCK)
          & (starts[:, None] + CMP_BLOCK > jb[None, :] * SEL_BLOCK) & (np.arange(nc)[:, None] < n_cmp))
    ov = jnp.asarray(ov, BF16)
    pt = jnp.asarray(np.where(np.arange(seq)[:, None] // SEL_BLOCK == jb[None, :], NEG, 0.0), BF16)

    nsa = _nsa(q, gates, kc, vc, seq3(ks), seq3(vs), seq3(kw), seq3(vw), pt, ov, batch, seq)

    w_pool_out = w_out[:POOL_WIDTH].astype(BF16)
    w_nsa_out = (w_out[POOL_WIDTH:].reshape(g, r, hd, D_MODEL).transpose(1, 0, 2, 3)
                 .reshape(Q_WIDTH, D_MODEL).astype(BF16))
    return [pool, nsa], [w_pool_out, w_nsa_out]


def _odd_mixer(xf, posi, w_in, conv_w, conv_b, wa, ba, wx, bx, lam, w_out, ln_g, ln_b, batch, seq):
    wax = (0.5 * jnp.concatenate([wa, wx], axis=2)).astype(BF16)
    return _odd_layer(xf, posi, w_in.astype(BF16), conv_w, conv_b[None, :], wax, 0.5 * ba[None, :],
                      0.5 * bx[None, :], lam[None, :], w_out.astype(BF16), ln_g, ln_b, batch, seq)


def kernel(x, mem, positions, e_w_in, e_pool_w, e_pool_scale, e_cmp_pos_k, e_cmp_pos_v, e_cmp_wk, e_cmp_wv, e_w_out, o_w_in, o_conv_w, o_conv_b, o_wa, o_ba, o_wx, o_bx, o_lambda, o_w_out, x_wq, x_wkv, x_wo, f_w_up, f_w_down, ln_g, ln_b):
    batch, seq, d = x.shape
    n = batch * seq
    assert d == D_MODEL and seq % ROW_TILE == 0
    xf = x.reshape(n, d)
    posi = positions.reshape(n, 1)
    memf = mem.reshape(batch * mem.shape[1], d)
    wq_all = (x_wq * (X_HEAD_DIM ** -0.5 * LOG2E)).astype(BF16)
    wo_all = x_wo.astype(BF16)
    wu_all, wd_all = f_w_up.astype(BF16), f_w_down.astype(BF16)
    for layer in range(DEPTH):
        j = layer // 2
        if layer % 2 == 0:
            parts, ws = _even_mixer(xf, positions, e_w_in[j], e_pool_w[j], e_pool_scale[j], e_cmp_pos_k[j],
                                    e_cmp_pos_v[j], e_cmp_wk[j], e_cmp_wv[j], e_w_out[j], batch, seq)
        else:
            xf = _odd_mixer(xf, posi, o_w_in[j], o_conv_w[j], o_conv_b[j], o_wa[j], o_ba[j], o_wx[j], o_bx[j],
                            o_lambda[j], o_w_out[j], ln_g[layer, 0][None, :], ln_b[layer, 0][None, :], batch, seq)
            parts, ws = [], []
        kv = _mem_kv(memf, x_wkv, layer).reshape(batch, mem.shape[1], 2 * d)
        xf = _tail(xf, parts, ws, kv, wq_all, wo_all, wu_all, wd_all, ln_g, ln_b, layer, batch, seq)
    return xf.reshape(batch, seq, d)
```

```python
import functools

import numpy as np
import jax
import jax.numpy as jnp
from jax import lax
from jax.experimental import pallas as pl
from jax.experimental.pallas import tpu as pltpu

F32 = jnp.float32
BF16 = jnp.bfloat16

D_MODEL = 1024
DEPTH = 2
ALPHA = (2.0 * DEPTH) ** 0.25
LN_EPS = 1e-5
NEG = -1e30
POOL_WIDTH = D_MODEL // 2
POOL_WINDOWS = (2, 4, 8, 16)
POOL_GROUP = POOL_WIDTH // len(POOL_WINDOWS)
POOL_HALO = 16
NSA_HEADS = 8
NSA_KV_HEADS = 2
NSA_HEAD_DIM = 64
NSA_GROUP = NSA_HEADS // NSA_KV_HEADS
CMP_BLOCK = 32
CMP_STRIDE = 16
SEL_BLOCK = 64
SEL_COUNT = 16
WINDOW = 512
Q_BLOCK = 256
N_BRANCH = 3
N_FORCED = 3
LOG2E = 1.4426950408889634
ROPE_THETA = 500000.0
ROT_DIM = NSA_HEAD_DIM // 4
ROT_HALF = ROT_DIM // 2
Q_WIDTH = NSA_HEADS * NSA_HEAD_DIM
KV_WIDTH = NSA_KV_HEADS * NSA_HEAD_DIM
GATE_WIDTH = NSA_HEADS * N_BRANCH
RNN_WIDTH = 1280
RNN_BLOCKS = 10
RNN_BLOCK_W = RNN_WIDTH // RNN_BLOCKS
CONV_WIDTH = 4
LRU_C = 8.0
X_HEADS = 4
X_HEAD_DIM = D_MODEL // X_HEADS
FFN_HIDDEN = 2816
GELU_C0 = 0.7978845608028654
GELU_C1 = GELU_C0 * 0.044715

LANES = 128
SUBLANES = 8
VMEM_LIMIT = 56 * 1024 * 1024
ROW_TILE = 512
SEL_KEY_TILE = 512
FFN_CHUNKS = (1024, 1024, 768)


def _params(n_axes, vmem=VMEM_LIMIT):
    return pltpu.CompilerParams(dimension_semantics=("arbitrary",) * n_axes,
                                vmem_limit_bytes=vmem)


def _resident(shape, index_map):
    return pl.BlockSpec(shape, index_map, pipeline_mode=pl.Buffered(1))


def _dot(a, b):
    return jnp.dot(a, b, preferred_element_type=F32)


def _dot_nt(a, b):
    return lax.dot_general(a, b, (((1,), (1,)), ((), ())), preferred_element_type=F32)


def _rep_rows(a, k):
    return jnp.concatenate([a] * k, axis=0)


def _rep_lanes(a, k):
    return jnp.concatenate([a] * k, axis=1) if k > 1 else a


def _layer_norm(z, g, b):
    mu = jnp.mean(z, axis=-1, keepdims=True)
    d = z - mu
    var = jnp.mean(d * d, axis=-1, keepdims=True)
    return d * lax.rsqrt(var + LN_EPS) * g + b


def _rope(v, cos, sin):
    k = v.shape[1] // LANES
    up = pltpu.roll(v, v.shape[1] - ROT_HALF, 1)
    dn = pltpu.roll(v, ROT_HALF, 1)
    lane = lax.broadcasted_iota(jnp.int32, v.shape, 1)
    partner = jnp.where((lane & (NSA_HEAD_DIM - 1)) < ROT_HALF, up, dn)
    return v * _rep_lanes(cos, k) + partner * _rep_lanes(sin, k)


def _rope_angle_kernel(pos_ref, inv_ref, cos_o, sin_o):
    ang = inv_ref[...] * pos_ref[...]
    cos_o[...] = jnp.cos(ang)
    sin_o[...] = jnp.sin(ang)


def _rope_angles(pos_row):
    n = pos_row.shape[1]
    inv = (ROPE_THETA ** (-jnp.arange(ROT_HALF, dtype=F32) * 2.0 / ROT_DIM))[:, None]
    out = jax.ShapeDtypeStruct((ROT_HALF, n), F32)
    return pl.pallas_call(
        _rope_angle_kernel,
        out_shape=[out, out],
        name="rope_angles",
    )(pos_row, inv)


def _rope_spread():
    lane = np.arange(LANES) % NSA_HEAD_DIM
    f = np.arange(ROT_HALF)[:, None]
    lo = (lane[None, :] == f).astype(np.float32)
    hi = (lane[None, :] == f + ROT_HALF).astype(np.float32)
    return jnp.asarray(lo + hi, BF16), jnp.asarray(hi - lo, BF16)


def _rope_patterns(cos_t, sin_t, ec, es):
    def spread(t, e):
        hi = t.astype(BF16)
        r1 = t - hi.astype(F32)
        mid = r1.astype(BF16)
        lo = (r1 - mid.astype(F32)).astype(BF16)
        tn = (((0,), (0,)), ((), ()))
        return sum(lax.dot_general(p, e, tn, preferred_element_type=F32) for p in (hi, mid, lo))

    lane = lax.broadcasted_iota(jnp.int32, (1, LANES), 1)
    unrotated = jnp.where((lane & (NSA_HEAD_DIM - 1)) >= ROT_DIM, 1.0, 0.0)
    return spread(cos_t, ec) + unrotated, spread(sin_t, es)


def _proj_even_kernel(x_ref, cos_ref, sin_ref, ec_ref, es_ref, wp_ref, wq_ref, wkv_ref, wg_ref, poolw_ref, pscale_ref,
                      pool_o, q_o, kc_o, vc_o, ks_o, vs_o, kw_o, vw_o, gate_o, pbuf):
    i = pl.program_id(1)
    tm = x_ref.shape[0]
    half = tm // 2

    @pl.when(i == 0)
    def _():
        pbuf[0:POOL_HALO, :] = jnp.zeros((POOL_HALO, POOL_WIDTH), F32)

    def project(h0):
        rs = slice(h0, h0 + half)
        xb = x_ref[rs, :].astype(BF16)
        pbuf[POOL_HALO + h0:POOL_HALO + h0 + half, :] = _dot(xb, wp_ref[...])
        return _dot(xb, wq_ref[...]), _dot(xb, wkv_ref[...]), _dot(xb, wg_ref[...])

    def emit(h0, q, kv, gl):
        rs = slice(h0, h0 + half)
        cos, sin = _rope_patterns(cos_ref[:, rs], sin_ref[:, rs], ec_ref[...], es_ref[...])
        q_o[rs, :] = _rope(q, cos, sin).astype(BF16)
        kc_o[rs, :] = kv[:, 0 * LANES:1 * LANES]
        vc_o[rs, :] = kv[:, 1 * LANES:2 * LANES]
        ks_o[rs, :] = _rope(kv[:, 2 * LANES:3 * LANES], cos, sin).astype(BF16)
        vs_o[rs, :] = kv[:, 3 * LANES:4 * LANES].astype(BF16)
        kw_o[rs, :] = _rope(kv[:, 4 * LANES:5 * LANES], cos, sin).astype(BF16)
        vw_o[rs, :] = kv[:, 5 * LANES:6 * LANES].astype(BF16)
        gate_o[rs, :] = _sigmoid_tanh(gl)

    def pool(h0):
        rs = slice(h0, h0 + half)
        t1 = i * tm + h0 + lax.broadcasted_iota(jnp.int32, (half, 1), 0) + 1
        for g, w in enumerate(POOL_WINDOWS):
            sl = slice(g * POOL_GROUP, (g + 1) * POOL_GROUP)
            u = pbuf[POOL_HALO + h0:POOL_HALO + h0 + half, sl]
            tot = u
            for j in range(1, w):
                tot = tot + pbuf[pl.ds(POOL_HALO + h0 - j, half), sl]
            cnt = jnp.minimum(t1, w).astype(F32)
            pooled = tot / cnt - u
            mixed = _dot(pooled.astype(BF16), poolw_ref[g]) * pscale_ref[:, sl]
            pool_o[rs, sl] = mixed.astype(BF16)

    first = project(0)
    second = project(half)
    emit(0, *first)
    pool(0)
    emit(half, *second)
    pool(half)
    pbuf[0:POOL_HALO, :] = pbuf[tm:tm + POOL_HALO, :]


def _proj_even(xf, cos_t, sin_t, ec, es, wp, wq, wkv, wg, poolw, pscale, batch, seq):
    n = xf.shape[0]
    tm = ROW_TILE
    nt = seq // tm
    row = lambda b, i: (b * nt + i, 0)
    full = lambda b, i: (0, 0)
    outs = [
        jax.ShapeDtypeStruct((n, POOL_WIDTH), BF16),
        jax.ShapeDtypeStruct((n, Q_WIDTH), BF16),
        jax.ShapeDtypeStruct((n, KV_WIDTH), F32),
        jax.ShapeDtypeStruct((n, KV_WIDTH), F32),
        jax.ShapeDtypeStruct((n, KV_WIDTH), BF16),
        jax.ShapeDtypeStruct((n, KV_WIDTH), BF16),
        jax.ShapeDtypeStruct((n, KV_WIDTH), BF16),
        jax.ShapeDtypeStruct((n, KV_WIDTH), BF16),
        jax.ShapeDtypeStruct((n, LANES), F32),
    ]
    return pl.pallas_call(
        _proj_even_kernel,
        grid=(batch, nt),
        in_specs=[
            pl.BlockSpec((tm, D_MODEL), row),
            pl.BlockSpec((ROT_HALF, tm), lambda b, i: (0, b * nt + i)),
            pl.BlockSpec((ROT_HALF, tm), lambda b, i: (0, b * nt + i)),
            pl.BlockSpec(ec.shape, full),
            pl.BlockSpec(es.shape, full),
            pl.BlockSpec(wp.shape, full),
            pl.BlockSpec(wq.shape, full),
            pl.BlockSpec(wkv.shape, full),
            pl.BlockSpec(wg.shape, full),
            pl.BlockSpec(poolw.shape, lambda b, i: (0, 0, 0)),
            pl.BlockSpec(pscale.shape, full),
        ],
        out_specs=[pl.BlockSpec((tm, o.shape[1]), row) for o in outs],
        out_shape=outs,
        scratch_shapes=[pltpu.VMEM((POOL_HALO + tm, POOL_WIDTH), F32)],
        compiler_params=_params(2),
        name="proj_even",
    )(xf, cos_t, sin_t, ec, es, wp, wq, wkv, wg, poolw, pscale)


def _compress_kernel(rk_ref, rv_ref, cos_ref, sin_ref, ec_ref, es_ref, pk1_ref, pk2_ref, pv1_ref, pv2_ref,
                     wk1_ref, wk2_ref, wv1_ref, wv2_ref, kc_o, vc_o):
    nc = kc_o.shape[0]

    def compress(raw_ref, p1, p2, w1, w2):
        r = jnp.concatenate([raw_ref[pl.ds(j, nc, stride=CMP_STRIDE), :] for j in range(CMP_STRIDE)], axis=1)
        a = _dot((r + p1[...]).astype(BF16), w1[...])
        b = _dot((r + p2[...]).astype(BF16), w2[...])
        return a + pltpu.roll(b, nc - 1, 0)

    kc = compress(rk_ref, pk1_ref, pk2_ref, wk1_ref, wk2_ref)
    cos, sin = _rope_patterns(cos_ref[...], sin_ref[...], ec_ref[...], es_ref[...])
    kc_o[...] = _rope(kc, cos, sin).astype(BF16)
    vc_o[...] = compress(rv_ref, pv1_ref, pv2_ref, wv1_ref, wv2_ref).astype(BF16)


def _compress(rk, rv, cos_t, sin_t, first_col_block, consts):
    batch, seq, width = rk.shape
    nc = seq // CMP_STRIDE
    blk = lambda b: (b, 0, 0)
    full = lambda b: (0, 0)
    out = jax.ShapeDtypeStruct((batch, nc, KV_WIDTH), BF16)
    return pl.pallas_call(
        _compress_kernel,
        grid=(batch,),
        in_specs=[pl.BlockSpec((None, seq, width), blk), pl.BlockSpec((None, seq, width), blk),
                  pl.BlockSpec((ROT_HALF, nc), lambda b: (0, first_col_block + b)),
                  pl.BlockSpec((ROT_HALF, nc), lambda b: (0, first_col_block + b))]
                 + [pl.BlockSpec(a.shape, full) for a in consts],
        out_specs=[pl.BlockSpec((None, nc, KV_WIDTH), blk)] * 2,
        out_shape=[out, out],
        compiler_params=_params(1),
        name="compress",
    )(rk, rv, cos_t, sin_t, *consts)


def _softmax_rows(s):
    m = jnp.max(s, axis=1, keepdims=True)
    e = jnp.exp2(s - m)
    return e, jnp.sum(e, axis=1, keepdims=True)


def _dot_exact01(x, m01):
    hi = x.astype(BF16)
    r1 = x - hi.astype(F32)
    mid = r1.astype(BF16)
    lo = (r1 - mid.astype(F32)).astype(BF16)
    return _dot(hi, m01) + _dot(mid, m01) + _dot(lo, m01)


def _unselected_blocks(imp, q0, n_pick):
    nq, nb = imp.shape
    col = lax.broadcasted_iota(jnp.int32, (nq, nb), 1)
    cur = lax.shift_right_arithmetic(q0 + lax.broadcasted_iota(jnp.int32, (nq, nb), 0), 6)
    forced = (col == 0) | (col == cur) | (col == cur - 1)
    val = jnp.where(forced, -jnp.inf, jnp.where(col > cur, -1.0, imp))
    vt = val.T
    blk = lax.broadcasted_iota(jnp.int32, (nb, nq), 0).astype(F32)
    unsel = jnp.where(forced, 0.0, 1.0).T
    for _ in range(n_pick - N_FORCED):
        m = jnp.max(vt, axis=0, keepdims=True)
        first = jnp.min(jnp.where(vt == m, blk, float(nb)), axis=0, keepdims=True)
        hit = blk == first
        unsel = jnp.where(hit, 0.0, unsel)
        vt = jnp.where(hit, -jnp.inf, vt)
    return unsel.T


def _nsa_kernel(*refs, n_pick, cast_scales):
    n_cast = len(cast_scales)
    q_ref, gate_ref, kc_ref, vc_ref, ks_ref, vs_ref, kw_ref, vw_ref, pt_ref, ov_ref = refs[:10]
    cast_in = refs[10:10 + n_cast]
    o_ref = refs[10 + n_cast]
    cast_out = refs[11 + n_cast:11 + 2 * n_cast]
    m_scr, l_scr, acc_scr, lhs_scr, oc_scr, ow_scr, s0_scr, s1_scr, w_scr, gx_scr = refs[11 + 2 * n_cast:]

    for src, dst, scale in zip(cast_in, cast_out, cast_scales):
        w = src[...] if scale == 1.0 else src[...] * scale
        dst[...] = w.astype(BF16)

    qb = pl.program_id(1)
    q0 = qb * Q_BLOCK
    tk = SEL_KEY_TILE
    ncmp = kc_ref.shape[0]
    wk = WINDOW + Q_BLOCK
    rows = NSA_GROUP * Q_BLOCK
    lane = lax.broadcasted_iota(jnp.int32, (Q_BLOCK, LANES), 1)

    def trow(width):
        return q0 + lax.broadcasted_iota(jnp.int32, (Q_BLOCK, width), 0)

    def kcol(width):
        return lax.broadcasted_iota(jnp.int32, (Q_BLOCK, width), 1)

    bias_c = jnp.where(kcol(ncmp) * CMP_STRIDE + (CMP_BLOCK - 1) <= trow(ncmp), 0.0, NEG)
    has_cmp = (trow(1) >= CMP_BLOCK - 1).astype(F32)
    wstart = pl.multiple_of(jnp.maximum(q0 - WINDOW, 0), Q_BLOCK)
    kpos_w = wstart + kcol(wk)
    bias_w = jnp.where(kpos_w <= trow(wk), jnp.where(kpos_w > trow(wk) - WINDOW, 0.0, NEG), NEG)

    for g in range(NSA_KV_HEADS):
        mine = (lane >= NSA_HEAD_DIM) if g else (lane < NSA_HEAD_DIM)
        lhs_scr[g, :, 0:LANES] = jnp.concatenate(
            [jnp.where(mine, q_ref[:, r * LANES:(r + 1) * LANES], jnp.zeros((), BF16))
             for r in range(NSA_GROUP)], axis=0)
        m_scr[g] = jnp.full((rows, LANES), -jnp.inf, F32)
        l_scr[g] = jnp.zeros((rows, LANES), F32)
        acc_scr[g] = jnp.zeros((rows, LANES), F32)

    def cmp_scores(g):
        s0_scr[g] = _dot_nt(lhs_scr[g, :, 0:LANES], kc_ref[...])

    def cmp_attend(g):
        e, l = _softmax_rows(s0_scr[g] + _rep_rows(bias_c, NSA_GROUP))
        p = e * (_rep_rows(has_cmp, NSA_GROUP) / l)
        oc_scr[g] = _dot(p.astype(BF16), vc_ref[...])
        psum = p[0:Q_BLOCK]
        for r in range(1, NSA_GROUP):
            psum = psum + p[r * Q_BLOCK:(r + 1) * Q_BLOCK]
        return _dot_exact01(psum, ov_ref[...])

    def select(g, imp):
        unsel = _unselected_blocks(imp, q0, n_pick)
        lhs_scr[g, :, LANES:2 * LANES] = _rep_rows(unsel.astype(BF16), NSA_GROUP)

    def win_scores(g):
        w_scr[g] = _dot_nt(lhs_scr[g, :, 0:LANES], kw_ref[pl.ds(wstart, wk), :])

    def win_attend(g):
        e, l = _softmax_rows(w_scr[g] + _rep_rows(bias_w, NSA_GROUP))
        ow_scr[g] = _dot(e.astype(BF16), vw_ref[pl.ds(wstart, wk), :]) / l

    def expand_gates():
        low = lane < NSA_HEAD_DIM
        for r in range(NSA_GROUP):
            for br in range(N_BRANCH):
                c_lo = r * N_BRANCH + br
                c_hi = (NSA_GROUP + r) * N_BRANCH + br
                gx_scr[c_lo] = jnp.where(low, gate_ref[:, c_lo:c_lo + 1], gate_ref[:, c_hi:c_hi + 1])

    cmp_scores(0)
    cmp_scores(1)
    expand_gates()
    win_scores(0)
    imp0 = cmp_attend(0)
    win_scores(1)
    imp1 = cmp_attend(1)
    select(0, imp0)
    win_attend(0)
    select(1, imp1)
    win_attend(1)

    def scores(kt, buf, g):
        k0 = pl.multiple_of(kt * tk, tk)
        rhs = jnp.concatenate([ks_ref[pl.ds(k0, tk), :], pt_ref[pl.ds(k0, tk), :]], axis=1)
        buf[g] = _dot_nt(lhs_scr[g], rhs)

    def consume(kt, buf, g, causal):
        k0 = pl.multiple_of(kt * tk, tk)
        v = vs_ref[pl.ds(k0, tk), :]
        bias = jnp.where(k0 + kcol(tk) <= trow(tk), 0.0, NEG) if causal else None
        for r in range(NSA_GROUP):
            rs = slice(r * Q_BLOCK, (r + 1) * Q_BLOCK)
            s = buf[g, rs, :]
            if causal:
                s = s + bias
            m_prev = m_scr[g, rs, :]
            m_new = jnp.maximum(m_prev, jnp.max(s, axis=1, keepdims=True))
            alpha = jnp.exp2(m_prev - m_new)
            pe = jnp.exp2(s - _rep_lanes(m_new, tk // LANES))
            part = pe[:, 0:LANES]
            for c in range(1, tk // LANES):
                part = part + pe[:, c * LANES:(c + 1) * LANES]
            l_scr[g, rs, :] = alpha * l_scr[g, rs, :] + part
            acc_scr[g, rs, :] = alpha * acc_scr[g, rs, :] + _dot(pe.astype(BF16), v)
            m_scr[g, rs, :] = m_new

    def step(nxt, nxt_buf, cur, cur_buf, causal=False):
        for g in range(NSA_KV_HEADS):
            if nxt is not None:
                scores(nxt, nxt_buf, g)
        for g in range(NSA_KV_HEADS):
            consume(cur, cur_buf, g, causal)

    n_full = q0 // tk
    for g in range(NSA_KV_HEADS):
        scores(0, s0_scr, g)

    def pair(j, carry):
        step(2 * j + 1, s1_scr, 2 * j, s0_scr)
        step(2 * j + 2, s0_scr, 2 * j + 1, s1_scr)
        return carry

    lax.fori_loop(0, n_full // 2, pair, 0)

    @pl.when(n_full % 2 == 0)
    def _():
        step(None, None, n_full, s0_scr, causal=True)

    @pl.when(n_full % 2 == 1)
    def _():
        step(n_full, s1_scr, n_full - 1, s0_scr)
        step(None, None, n_full, s1_scr, causal=True)

    low = lane < NSA_HEAD_DIM
    for r in range(NSA_GROUP):
        rs = slice(r * Q_BLOCK, (r + 1) * Q_BLOCK)
        merged = lambda scr: jnp.where(low, scr[0, rs, :], scr[1, rs, :])
        l_sel = jnp.where(low, jnp.sum(l_scr[0, rs, :], axis=1, keepdims=True),
                          jnp.sum(l_scr[1, rs, :], axis=1, keepdims=True))
        out = (gx_scr[r * N_BRANCH] * merged(oc_scr)
               + gx_scr[r * N_BRANCH + 1] * (merged(acc_scr) / l_sel)
               + gx_scr[r * N_BRANCH + 2] * merged(ow_scr))
        o_ref[:, r * LANES:(r + 1) * LANES] = out.astype(BF16)


def _nsa(q, gates, kc, vc, ks, vs, kw, vw, pt, ov, casts, batch, seq):
    n = q.shape[0]
    nqb = seq // Q_BLOCK
    steps = batch * nqb
    rows = NSA_GROUP * Q_BLOCK
    row = lambda b, i: (b * nqb + i, 0)
    per_b = lambda b, i: (b, 0, 0)
    full = lambda b, i: (0, 0)
    seq_spec = _resident((None, seq, KV_WIDTH), per_b)
    cmp_spec = _resident((None, kc.shape[1], KV_WIDTH), per_b)
    n_pick = min(SEL_COUNT, seq // SEL_BLOCK)
    sliced = [w.reshape(steps, -1, w.shape[-1]) for w, _ in casts]
    slice_specs = [pl.BlockSpec((None,) + w.shape[1:], lambda b, i: (b * nqb + i, 0, 0)) for w in sliced]
    outs = pl.pallas_call(
        functools.partial(_nsa_kernel, n_pick=n_pick, cast_scales=tuple(s for _, s in casts)),
        grid=(batch, nqb),
        in_specs=[pl.BlockSpec((Q_BLOCK, Q_WIDTH), row), pl.BlockSpec((Q_BLOCK, LANES), row),
                  cmp_spec, cmp_spec, seq_spec, seq_spec, seq_spec, seq_spec,
                  _resident(pt.shape, full), _resident(ov.shape, full)] + slice_specs,
        out_specs=[pl.BlockSpec((Q_BLOCK, Q_WIDTH), row)] + slice_specs,
        out_shape=[jax.ShapeDtypeStruct((n, Q_WIDTH), BF16)]
                  + [jax.ShapeDtypeStruct(w.shape, BF16) for w in sliced],
        scratch_shapes=[pltpu.VMEM((NSA_KV_HEADS, rows, LANES), F32),
                        pltpu.VMEM((NSA_KV_HEADS, rows, LANES), F32),
                        pltpu.VMEM((NSA_KV_HEADS, rows, LANES), F32),
                        pltpu.VMEM((NSA_KV_HEADS, rows, 2 * LANES), BF16),
                        pltpu.VMEM((NSA_KV_HEADS, rows, LANES), F32),
                        pltpu.VMEM((NSA_KV_HEADS, rows, LANES), F32),
                        pltpu.VMEM((NSA_KV_HEADS, rows, SEL_KEY_TILE), F32),
                        pltpu.VMEM((NSA_KV_HEADS, rows, SEL_KEY_TILE), F32),
                        pltpu.VMEM((NSA_KV_HEADS, rows, WINDOW + Q_BLOCK), F32),
                        pltpu.VMEM((NSA_GROUP * N_BRANCH, Q_BLOCK, LANES), F32)],
        compiler_params=_params(2),
        name="nsa",
    )(q, gates, kc, vc, ks, vs, kw, vw, pt, ov, *sliced)
    return outs[0], [o.reshape(w.shape) for o, (w, _) in zip(outs[1:], casts)]


def _matmul_kernel(a_ref, w_ref, o_ref):
    o_ref[...] = _dot(a_ref[...].astype(BF16), w_ref[...].astype(BF16)).astype(o_ref.dtype)


def _mem_kv(memf, wkv_all, layer):
    m, k = memf.shape
    n = wkv_all.shape[2]
    return pl.pallas_call(
        _matmul_kernel,
        grid=(1,),
        in_specs=[pl.BlockSpec((m, k), lambda i: (0, 0)), pl.BlockSpec((None, k, n), lambda i: (layer, 0, 0))],
        out_specs=pl.BlockSpec((m, n), lambda i: (0, 0)),
        out_shape=jax.ShapeDtypeStruct((m, n), BF16),
        compiler_params=_params(1),
        name="mem_kv",
    )(memf, wkv_all)


def _tail_kernel(*refs, n_parts):
    x_ref = refs[0]
    parts = refs[1:1 + n_parts]
    ws = refs[1 + n_parts:1 + 2 * n_parts]
    wq_ref, k_ref, v_ref, wo_ref, wu_ref, wd_ref, g_ref, b_ref, o_ref, x1_scr, x2_scr, z_scr = refs[1 + 2 * n_parts:]
    t = pl.program_id(0)
    n_tiles = pl.num_programs(0) - 1
    tm = x_ref.shape[0]
    half = tm // 2
    halves = (slice(0, half), slice(half, tm))
    x1 = x1_scr if n_parts else x_ref

    def norm(k, z):
        return _layer_norm(z, g_ref[k:k + 1, :], b_ref[k:k + 1, :])

    def mixer_out(rs):
        z = ALPHA * x_ref[rs, :]
        for a, w in zip(parts, ws):
            z = z + _dot(a[rs, :], w[...])
        x1_scr[rs, :] = norm(0, z)

    def project(rs):
        return _dot(x1[rs, :].astype(BF16), wq_ref[...]).astype(BF16)

    heads = [slice(h * X_HEAD_DIM, (h + 1) * X_HEAD_DIM) for h in range(X_HEADS)]

    def scores(q):
        return [_dot_nt(q[:, sl], k_ref[:, sl]) for sl in heads]

    def probabilities(ss):
        probs = []
        for s in ss:
            e, l = _softmax_rows(s)
            probs.append((e / l).astype(BF16))
        return probs

    def mix(probs):
        outs = [_dot(p, v_ref[:, sl]) for p, sl in zip(probs, heads)]
        return jnp.concatenate(outs, axis=1).astype(BF16)

    def finish(rs, o):
        x2_scr[rs, :] = norm(1, ALPHA * x1[rs, :] + _dot(o, wo_ref[...]))

    @pl.when(t == 0)
    def _():
        z_scr[...] = jnp.zeros(z_scr.shape, F32)

    @pl.when(t < n_tiles)
    def _():
        if n_parts:
            mixer_out(halves[0])
            mixer_out(halves[1])
        qa = project(halves[0])
        qb = project(halves[1])
        sa = scores(qa)
        sb = scores(qb)
        pa = probabilities(sa)
        pb = probabilities(sb)
        oa = mix(pa)
        ob = mix(pb)
        finish(halves[0], oa)
        finish(halves[1], ob)

        x2 = x2_scr[...]
        xb = x2.astype(BF16)
        z = ALPHA * x2
        c0 = 0
        for width in FFN_CHUNKS:
            h1 = _dot(xb, wu_ref[:, c0:c0 + width])
            h2 = _dot(xb, wu_ref[:, FFN_HIDDEN + c0:FFN_HIDDEN + c0 + width])
            act = (h1 * _sigmoid_tanh(h1) * h2).astype(BF16)
            z = z + _dot(act, wd_ref[c0:c0 + width, :])
            if c0 == 0:
                o_ref[...] = norm(2, z_scr[...])
            c0 += width
        z_scr[...] = z

    @pl.when(t == n_tiles)
    def _():
        o_ref[...] = norm(2, z_scr[...])


def _tail(xf, parts, ws, kv, wq_all, wo_all, wu_all, wd_all, g_all, b_all, layer, batch, seq):
    n = xf.shape[0]
    tm = ROW_TILE
    nt = seq // tm
    n_tiles = n // tm
    n_mem = kv.shape[1]
    cur = lambda t: jnp.minimum(t, n_tiles - 1)
    row = lambda t: (cur(t), 0)
    lagged = lambda t: (jnp.maximum(t - 1, 0), 0)
    full = lambda t: (0, 0)
    this_layer = lambda t: (layer, 0, 0)
    stacked = lambda a: _resident((None,) + a.shape[1:], this_layer)
    return pl.pallas_call(
        functools.partial(_tail_kernel, n_parts=len(parts)),
        grid=(n_tiles + 1,),
        in_specs=[pl.BlockSpec((tm, D_MODEL), row)]
                 + [pl.BlockSpec((tm, a.shape[1]), row) for a in parts]
                 + [_resident(w.shape, full) for w in ws]
                 + [stacked(wq_all),
                    _resident((None, n_mem, D_MODEL), lambda t: (cur(t) // nt, 0, 0)),
                    _resident((None, n_mem, D_MODEL), lambda t: (cur(t) // nt, 0, 1)),
                    stacked(wo_all), stacked(wu_all), stacked(wd_all), stacked(g_all), stacked(b_all)],
        out_specs=pl.BlockSpec((tm, D_MODEL), lagged),
        out_shape=jax.ShapeDtypeStruct((n, D_MODEL), F32),
        scratch_shapes=[pltpu.VMEM((tm, D_MODEL), F32), pltpu.VMEM((tm, D_MODEL), F32),
                        pltpu.VMEM((tm, D_MODEL), F32)],
        compiler_params=_params(1),
        name="tail",
    )(xf, *parts, *ws, wq_all, kv, kv, wo_all, wu_all, wd_all, g_all, b_all)


def _gelu_tanh(x):
    inner = x * (GELU_C0 + GELU_C1 * (x * x))
    return x * (0.5 * jnp.tanh(inner) + 0.5)


def _sigmoid_tanh(z):
    return 0.5 * jnp.tanh(0.5 * z) + 0.5


def _odd_kernel(x_ref, pos_ref, perm_ref, unperm_ref, win_ref, cw_ref, cb_ref, wax_ref, ba_ref, bx_ref, lam_ref,
                wout_ref, g_ref, b_ref, o_ref, xbuf, gate_scr, a_scr, b_scr, tail_scr, h_scr):
    i = pl.program_id(1)
    tm = x_ref.shape[0]
    half = tm // 2
    nv = half // SUBLANES
    halo = (CONV_WIDTH - 1) * SUBLANES

    @pl.when(i == 0)
    def _():
        tail_scr[...] = jnp.zeros((SUBLANES, RNN_WIDTH), F32)
        h_scr[...] = jnp.zeros((SUBLANES, RNN_WIDTH), F32)

    def permute(v2d):
        return jnp.swapaxes(v2d.reshape(SUBLANES, nv, v2d.shape[1]), 0, 1).reshape(v2d.shape)

    sub = lax.broadcasted_iota(jnp.int32, (SUBLANES, RNN_WIDTH), 0)
    lam = lam_ref[...]
    half_scale = (-0.5 * LRU_C) * (jnp.maximum(-lam, 0.0) + jnp.log1p(jnp.exp(-jnp.abs(lam))))

    def in_proj(hi, h0):
        xb = _dot(perm_ref[...], x_ref[h0:h0 + half, :].astype(BF16)).astype(BF16)
        gate_scr[h0:h0 + half, :] = _dot(xb, win_ref[:, :RNN_WIDTH])
        xbuf[hi, halo:halo + half, :] = _dot(xb, win_ref[:, RNN_WIDTH:])

    def gates(hi, h0):
        for j in range(1, CONV_WIDTH):
            last = xbuf[hi, halo + (nv - j) * SUBLANES:halo + (nv - j + 1) * SUBLANES, :]
            before = jnp.where(sub == 0, tail_scr[j:j + 1, :], pltpu.roll(last, 1, 0))
            xbuf[hi, halo - j * SUBLANES:halo - (j - 1) * SUBLANES, :] = before
            tail_scr[j:j + 1, :] = last[SUBLANES - 1:SUBLANES, :]
        is_reset = (pos_ref[h0:h0 + half, :] == 0).astype(F32)
        reset = permute(jnp.broadcast_to(is_reset, (half, RNN_BLOCK_W))) > 0.5
        for h in range(RNN_BLOCKS):
            sl = slice(h * RNN_BLOCK_W, (h + 1) * RNN_BLOCK_W)
            xc = cb_ref[:, sl]
            for k in range(CONV_WIDTH):
                xc = xc + xbuf[hi, k * SUBLANES:k * SUBLANES + half, sl] * cw_ref[k:k + 1, sl]
            ri = _dot(xc.astype(BF16), wax_ref[h])
            t_r = jnp.tanh(ri[:, :RNN_BLOCK_W] + ba_ref[:, sl])
            t_i = jnp.tanh(ri[:, RNN_BLOCK_W:] + bx_ref[:, sl])
            log_a = t_r * half_scale[:, sl] + half_scale[:, sl]
            a = jnp.where(reset, 0.0, jnp.exp(log_a))
            th = jnp.tanh(log_a)
            q2 = -0.5 * th / (1.0 - th)
            half_mult = jnp.where(reset, 0.5, jnp.where(q2 > 0.0, q2 * lax.rsqrt(q2), 0.0))
            b = half_mult * (t_i * xc + xc)
            a_scr[h0:h0 + half, sl] = a
            b_scr[h0:h0 + half, sl] = b

    def recur(h0, h_prev):
        decay = jnp.ones((SUBLANES, RNN_WIDTH), F32)
        resp = jnp.zeros((SUBLANES, RNN_WIDTH), F32)
        for v in range(nv):
            rs = slice(h0 + v * SUBLANES, h0 + (v + 1) * SUBLANES)
            av = a_scr[rs, :]
            decay = av * decay
            resp = av * resp + b_scr[rs, :]
            a_scr[rs, :] = decay
            b_scr[rs, :] = resp
        ca, cb_ = decay, resp
        for d in (1, 2, 4):
            ok = sub >= d
            cb_ = jnp.where(ok, ca * pltpu.roll(cb_, d, 0) + cb_, cb_)
            ca = jnp.where(ok, ca * pltpu.roll(ca, d, 0), ca)
        h_end = ca * h_prev + cb_
        h_in = jnp.where(sub == 0, h_prev, pltpu.roll(h_end, 1, 0))
        rs = slice(h0, h0 + half)
        hh = (a_scr[rs, :].reshape(nv, SUBLANES, RNN_WIDTH) * h_in[None]
              + b_scr[rs, :].reshape(nv, SUBLANES, RNN_WIDTH))
        b_scr[rs, :] = hh.reshape(half, RNN_WIDTH)
        return jnp.broadcast_to(h_end[SUBLANES - 1:SUBLANES, :], (SUBLANES, RNN_WIDTH))

    def out_proj(h0):
        rs = slice(h0, h0 + half)
        yp = (b_scr[rs, :] * _gelu_tanh(gate_scr[rs, :])).astype(BF16)
        y = _dot(unperm_ref[...], yp).astype(BF16)
        z = ALPHA * x_ref[rs, :] + _dot(y, wout_ref[...])
        o_ref[rs, :] = _layer_norm(z, g_ref[...], b_ref[...])

    in_proj(0, 0)
    in_proj(1, half)
    gates(0, 0)
    h_mid = recur(0, h_scr[...])
    gates(1, half)
    out_proj(0)
    h_scr[...] = recur(half, h_mid)
    out_proj(half)


def _odd_layer(xf, posi, w_in, cw, cb, wax, ba, bx, lam, w_out, g, b, batch, seq):
    n = xf.shape[0]
    tm = ROW_TILE
    nt = seq // tm
    row = lambda bi, i: (bi * nt + i, 0)
    full = lambda bi, i: (0, 0)
    half = tm // 2
    rho = np.arange(half)
    src_time = (rho % SUBLANES) * (half // SUBLANES) + rho // SUBLANES
    perm = np.zeros((half, half), np.float32)
    perm[rho, src_time] = 1.0
    consts = [jnp.asarray(perm, BF16), jnp.asarray(perm.T, BF16), w_in, cw, cb]
    rest = [ba, bx, lam, w_out, g, b]
    return pl.pallas_call(
        _odd_kernel,
        grid=(batch, nt),
        in_specs=[pl.BlockSpec((tm, D_MODEL), row), pl.BlockSpec((tm, 1), row)]
                 + [pl.BlockSpec(a.shape, full) for a in consts]
                 + [pl.BlockSpec(wax.shape, lambda bi, i: (0, 0, 0))]
                 + [pl.BlockSpec(a.shape, full) for a in rest],
        out_specs=pl.BlockSpec((tm, D_MODEL), row),
        out_shape=jax.ShapeDtypeStruct((n, D_MODEL), F32),
        scratch_shapes=[pltpu.VMEM((2, (CONV_WIDTH - 1) * SUBLANES + tm // 2, RNN_WIDTH), F32),
                        pltpu.VMEM((tm, RNN_WIDTH), F32),
                        pltpu.VMEM((tm, RNN_WIDTH), F32),
                        pltpu.VMEM((tm, RNN_WIDTH), F32),
                        pltpu.VMEM((SUBLANES, RNN_WIDTH), F32),
                        pltpu.VMEM((SUBLANES, RNN_WIDTH), F32)],
        compiler_params=_params(2),
        name="odd_layer",
    )(xf, posi, *consts, wax, *rest)


def _even_mixer(xf, positions, w_in, pool_w, pool_scale, cmp_pos_k, cmp_pos_v, cmp_wk, cmp_wv,
                w_out, casts, batch, seq):
    assert seq // SEL_BLOCK == LANES and seq % SEL_KEY_TILE == 0 and seq >= WINDOW + Q_BLOCK
    hd, g, r = NSA_HEAD_DIM, NSA_KV_HEADS, NSA_GROUP
    c0 = POOL_WIDTH
    c1 = c0 + Q_WIDTH
    c2 = c1 + GATE_WIDTH
    wp = w_in[:, :c0].astype(BF16)
    wq = (w_in[:, c0:c1].reshape(D_MODEL, g, r, hd).transpose(0, 2, 1, 3).reshape(D_MODEL, Q_WIDTH)
          * (hd ** -0.5 * LOG2E)).astype(BF16)
    wg = jnp.pad(w_in[:, c1:c2], ((0, 0), (0, LANES - GATE_WIDTH))).astype(BF16)
    wkv = w_in[:, c2:].astype(BF16)
    n = batch * seq
    nc = seq // CMP_STRIDE
    n_cmp = (seq - CMP_BLOCK) // CMP_STRIDE + 1
    posc = jnp.pad(positions[:, CMP_BLOCK - 1::CMP_STRIDE][:, :n_cmp], ((0, 0), (0, nc - n_cmp)))
    pos_all = jnp.concatenate([positions.reshape(1, n), posc.reshape(1, batch * nc)], axis=1).astype(F32)
    cos_t, sin_t = _rope_angles(pos_all)
    ec, es = _rope_spread()
    pool, q, kc_raw, vc_raw, ks, vs, kw, vw, gates = _proj_even(
        xf, cos_t, sin_t, ec, es, wp, wq, wkv, wg, pool_w.astype(BF16), pool_scale[None, :], batch, seq)

    def halves(w):
        w3 = w.reshape(CMP_BLOCK, hd, hd).astype(BF16)
        zero = jnp.zeros_like(w3)
        w4 = jnp.concatenate([jnp.concatenate([w3 if k == j else zero for k in range(g)], axis=2)
                              for j in range(g)], axis=1)
        return (w4[:CMP_STRIDE].reshape(CMP_STRIDE * g * hd, g * hd),
                w4[CMP_STRIDE:].reshape(CMP_STRIDE * g * hd, g * hd))

    def pos_halves(p):
        p2 = jnp.tile(p[:, None, :], (1, g, 1)).reshape(CMP_BLOCK, g * hd)
        return p2[:CMP_STRIDE].reshape(1, -1), p2[CMP_STRIDE:].reshape(1, -1)

    pk1, pk2 = pos_halves(cmp_pos_k)
    pv1, pv2 = pos_halves(cmp_pos_v)
    wk1, wk2 = halves(cmp_wk)
    wv1, wv2 = halves(cmp_wv)
    seq3 = lambda a: a.reshape(batch, seq, KV_WIDTH)
    kc, vc = _compress(seq3(kc_raw), seq3(vc_raw), cos_t, sin_t, n // nc,
                       (ec, es, pk1, pk2, pv1, pv2, wk1, wk2, wv1, wv2))

    n_sb = seq // SEL_BLOCK
    starts = np.arange(nc) * CMP_STRIDE
    jb = np.arange(n_sb)
    ov = ((starts[:, None] < (jb[None, :] + 1) * SEL_BLOCK)
          & (starts[:, None] + CMP_BLOCK > jb[None, :] * SEL_BLOCK) & (np.arange(nc)[:, None] < n_cmp))
    ov = jnp.asarray(ov, BF16)
    pt = jnp.asarray(np.where(np.arange(seq)[:, None] // SEL_BLOCK == jb[None, :], NEG, 0.0), BF16)

    nsa, cast_weights = _nsa(q, gates, kc, vc, seq3(ks), seq3(vs), seq3(kw), seq3(vw), pt, ov, casts, batch, seq)

    w_pool_out = w_out[:POOL_WIDTH].astype(BF16)
    w_nsa_out = (w_out[POOL_WIDTH:].reshape(g, r, hd, D_MODEL).transpose(1, 0, 2, 3)
                 .reshape(Q_WIDTH, D_MODEL).astype(BF16))
    return [pool, nsa], [w_pool_out, w_nsa_out], cast_weights


def _odd_mixer(xf, posi, w_in, conv_w, conv_b, wa, ba, wx, bx, lam, w_out, ln_g, ln_b, batch, seq):
    wax = (0.5 * jnp.concatenate([wa, wx], axis=2)).astype(BF16)
    return _odd_layer(xf, posi, w_in, conv_w, conv_b[None, :], wax, 0.5 * ba[None, :],
                      0.5 * bx[None, :], lam[None, :], w_out, ln_g, ln_b, batch, seq)


def kernel(x, mem, positions, e_w_in, e_pool_w, e_pool_scale, e_cmp_pos_k, e_cmp_pos_v, e_cmp_wk, e_cmp_wv, e_w_out, o_w_in, o_conv_w, o_conv_b, o_wa, o_ba, o_wx, o_bx, o_lambda, o_w_out, x_wq, x_wkv, x_wo, f_w_up, f_w_down, ln_g, ln_b):
    batch, seq, d = x.shape
    n = batch * seq
    assert d == D_MODEL and seq % ROW_TILE == 0
    xf = x.reshape(n, d)
    posi = positions.reshape(n, 1)
    memf = mem.reshape(batch * mem.shape[1], d)
    casts = [(x_wq, X_HEAD_DIM ** -0.5 * LOG2E), (x_wo, 1.0), (f_w_up, 1.0), (f_w_down, 1.0),
             (o_w_in, 1.0), (o_w_out, 1.0)]
    for layer in range(DEPTH):
        j = layer // 2
        if layer % 2 == 0:
            parts, ws, cast_weights = _even_mixer(
                xf, positions, e_w_in[j], e_pool_w[j], e_pool_scale[j], e_cmp_pos_k[j], e_cmp_pos_v[j],
                e_cmp_wk[j], e_cmp_wv[j], e_w_out[j], casts if layer == 0 else [], batch, seq)
            if layer == 0:
                wq_all, wo_all, wu_all, wd_all, o_w_in16, o_w_out16 = cast_weights
        else:
            xf = _odd_mixer(xf, posi, o_w_in16[j], o_conv_w[j], o_conv_b[j], o_wa[j], o_ba[j], o_wx[j], o_bx[j],
                            o_lambda[j], o_w_out16[j], ln_g[layer, 0][None, :], ln_b[layer, 0][None, :],
                            batch, seq)
            parts, ws = [], []
        kv = _mem_kv(memf, x_wkv, layer).reshape(batch, mem.shape[1], 2 * d)
        xf = _tail(xf, parts, ws, kv, wq_all, wo_all, wu_all, wd_all, ln_g, ln_b, layer, batch, seq)
    return xf.reshape(batch, seq, d)
```

```python
import functools

import numpy as np
import jax
import jax.numpy as jnp
from jax import lax
from jax.experimental import pallas as pl
from jax.experimental.pallas import tpu as pltpu

F32 = jnp.float32
BF16 = jnp.bfloat16

D_MODEL = 1024
DEPTH = 2
ALPHA = (2.0 * DEPTH) ** 0.25
LN_EPS = 1e-5
NEG = -1e30
POOL_WIDTH = D_MODEL // 2
POOL_WINDOWS = (2, 4, 8, 16)
POOL_GROUP = POOL_WIDTH // len(POOL_WINDOWS)
POOL_HALO = 16
NSA_HEADS = 8
NSA_KV_HEADS = 2
NSA_HEAD_DIM = 64
NSA_GROUP = NSA_HEADS // NSA_KV_HEADS
CMP_BLOCK = 32
CMP_STRIDE = 16
SEL_BLOCK = 64
SEL_COUNT = 16
WINDOW = 512
Q_BLOCK = 256
N_BRANCH = 3
N_FORCED = 3
LOG2E = 1.4426950408889634
ROPE_THETA = 500000.0
ROT_DIM = NSA_HEAD_DIM // 4
ROT_HALF = ROT_DIM // 2
Q_WIDTH = NSA_HEADS * NSA_HEAD_DIM
KV_WIDTH = NSA_KV_HEADS * NSA_HEAD_DIM
GATE_WIDTH = NSA_HEADS * N_BRANCH
RNN_WIDTH = 1280
RNN_BLOCKS = 10
RNN_BLOCK_W = RNN_WIDTH // RNN_BLOCKS
CONV_WIDTH = 4
LRU_C = 8.0
X_HEADS = 4
X_HEAD_DIM = D_MODEL // X_HEADS
FFN_HIDDEN = 2816
GELU_C0 = 0.7978845608028654
GELU_C1 = GELU_C0 * 0.044715

LANES = 128
SUBLANES = 8
VMEM_LIMIT = 56 * 1024 * 1024
ROW_TILE = 512
SEL_KEY_TILE = 512
FFN_CHUNKS = (256,) * 11


def _params(n_axes, vmem=VMEM_LIMIT):
    return pltpu.CompilerParams(dimension_semantics=("arbitrary",) * n_axes,
                                vmem_limit_bytes=vmem)


def _resident(shape, index_map):
    return pl.BlockSpec(shape, index_map, pipeline_mode=pl.Buffered(1))


def _dot(a, b):
    return jnp.dot(a, b, preferred_element_type=F32)


def _dot_nt(a, b):
    return lax.dot_general(a, b, (((1,), (1,)), ((), ())), preferred_element_type=F32)


def _rep_rows(a, k):
    return jnp.concatenate([a] * k, axis=0)


def _rep_lanes(a, k):
    return jnp.concatenate([a] * k, axis=1) if k > 1 else a


def _layer_norm(z, g, b):
    mu = jnp.mean(z, axis=-1, keepdims=True)
    d = z - mu
    var = jnp.mean(d * d, axis=-1, keepdims=True)
    return d * lax.rsqrt(var + LN_EPS) * g + b


def _rope(v, cos, sin):
    k = v.shape[1] // LANES
    up = pltpu.roll(v, v.shape[1] - ROT_HALF, 1)
    dn = pltpu.roll(v, ROT_HALF, 1)
    lane = lax.broadcasted_iota(jnp.int32, v.shape, 1)
    partner = jnp.where((lane & (NSA_HEAD_DIM - 1)) < ROT_HALF, up, dn)
    return v * _rep_lanes(cos, k) + partner * _rep_lanes(sin, k)


def _rope_angle_kernel(pos_ref, inv_ref, cos_o, sin_o):
    ang = inv_ref[...] * pos_ref[...]
    cos_o[...] = jnp.cos(ang)
    sin_o[...] = jnp.sin(ang)


def _rope_angles(pos_row):
    n = pos_row.shape[1]
    inv = (ROPE_THETA ** (-jnp.arange(ROT_HALF, dtype=F32) * 2.0 / ROT_DIM))[:, None]
    out = jax.ShapeDtypeStruct((ROT_HALF, n), F32)
    return pl.pallas_call(
        _rope_angle_kernel,
        out_shape=[out, out],
        name="rope_angles",
    )(pos_row, inv)


def _rope_spread():
    lane = np.arange(LANES) % NSA_HEAD_DIM
    f = np.arange(ROT_HALF)[:, None]
    lo = (lane[None, :] == f).astype(np.float32)
    hi = (lane[None, :] == f + ROT_HALF).astype(np.float32)
    return jnp.asarray(lo + hi, BF16), jnp.asarray(hi - lo, BF16)


def _rope_patterns(cos_t, sin_t, ec, es):
    def spread(t, e):
        hi = t.astype(BF16)
        r1 = t - hi.astype(F32)
        mid = r1.astype(BF16)
        lo = (r1 - mid.astype(F32)).astype(BF16)
        tn = (((0,), (0,)), ((), ()))
        return sum(lax.dot_general(p, e, tn, preferred_element_type=F32) for p in (hi, mid, lo))

    lane = lax.broadcasted_iota(jnp.int32, (1, LANES), 1)
    unrotated = jnp.where((lane & (NSA_HEAD_DIM - 1)) >= ROT_DIM, 1.0, 0.0)
    return spread(cos_t, ec) + unrotated, spread(sin_t, es)


def _proj_even_kernel(x_ref, cos_ref, sin_ref, ec_ref, es_ref, wp_ref, wq_ref, wkv_ref, wg_ref, poolw_ref, pscale_ref,
                      pool_o, q_o, kc_o, vc_o, ks_o, vs_o, kw_o, vw_o, gate_o, pbuf):
    i = pl.program_id(1)
    tm = x_ref.shape[0]
    half = tm // 2

    @pl.when(i == 0)
    def _():
        pbuf[0:POOL_HALO, :] = jnp.zeros((POOL_HALO, POOL_WIDTH), F32)

    def project(h0):
        rs = slice(h0, h0 + half)
        xb = x_ref[rs, :].astype(BF16)
        pbuf[POOL_HALO + h0:POOL_HALO + h0 + half, :] = _dot(xb, wp_ref[...])
        return _dot(xb, wq_ref[...]), _dot(xb, wkv_ref[...]), _dot(xb, wg_ref[...])

    def emit(h0, q, kv, gl):
        rs = slice(h0, h0 + half)
        cos, sin = _rope_patterns(cos_ref[:, rs], sin_ref[:, rs], ec_ref[...], es_ref[...])
        q_o[rs, :] = _rope(q, cos, sin).astype(BF16)
        kc_o[rs, :] = kv[:, 0 * LANES:1 * LANES]
        vc_o[rs, :] = kv[:, 1 * LANES:2 * LANES]
        ks_o[rs, :] = _rope(kv[:, 2 * LANES:3 * LANES], cos, sin).astype(BF16)
        vs_o[rs, :] = kv[:, 3 * LANES:4 * LANES].astype(BF16)
        kw_o[rs, :] = _rope(kv[:, 4 * LANES:5 * LANES], cos, sin).astype(BF16)
        vw_o[rs, :] = kv[:, 5 * LANES:6 * LANES].astype(BF16)
        gate_o[rs, :] = _sigmoid_tanh(gl)

    def pool(h0):
        rs = slice(h0, h0 + half)
        t1 = i * tm + h0 + lax.broadcasted_iota(jnp.int32, (half, 1), 0) + 1
        for g, w in enumerate(POOL_WINDOWS):
            sl = slice(g * POOL_GROUP, (g + 1) * POOL_GROUP)
            u = pbuf[POOL_HALO + h0:POOL_HALO + h0 + half, sl]
            tot = u
            for j in range(1, w):
                tot = tot + pbuf[pl.ds(POOL_HALO + h0 - j, half), sl]
            cnt = jnp.minimum(t1, w).astype(F32)
            pooled = tot / cnt - u
            mixed = _dot(pooled.astype(BF16), poolw_ref[g]) * pscale_ref[:, sl]
            pool_o[rs, sl] = mixed.astype(BF16)

    first = project(0)
    second = project(half)
    emit(0, *first)
    pool(0)
    emit(half, *second)
    pool(half)
    pbuf[0:POOL_HALO, :] = pbuf[tm:tm + POOL_HALO, :]


def _proj_even(xf, cos_t, sin_t, ec, es, wp, wq, wkv, wg, poolw, pscale, batch, seq):
    n = xf.shape[0]
    tm = ROW_TILE
    nt = seq // tm
    row = lambda b, i: (b * nt + i, 0)
    full = lambda b, i: (0, 0)
    outs = [
        jax.ShapeDtypeStruct((n, POOL_WIDTH), BF16),
        jax.ShapeDtypeStruct((n, Q_WIDTH), BF16),
        jax.ShapeDtypeStruct((n, KV_WIDTH), F32),
        jax.ShapeDtypeStruct((n, KV_WIDTH), F32),
        jax.ShapeDtypeStruct((n, KV_WIDTH), BF16),
        jax.ShapeDtypeStruct((n, KV_WIDTH), BF16),
        jax.ShapeDtypeStruct((n, KV_WIDTH), BF16),
        jax.ShapeDtypeStruct((n, KV_WIDTH), BF16),
        jax.ShapeDtypeStruct((n, LANES), F32),
    ]
    return pl.pallas_call(
        _proj_even_kernel,
        grid=(batch, nt),
        in_specs=[
            pl.BlockSpec((tm, D_MODEL), row),
            pl.BlockSpec((ROT_HALF, tm), lambda b, i: (0, b * nt + i)),
            pl.BlockSpec((ROT_HALF, tm), lambda b, i: (0, b * nt + i)),
            pl.BlockSpec(ec.shape, full),
            pl.BlockSpec(es.shape, full),
            pl.BlockSpec(wp.shape, full),
            pl.BlockSpec(wq.shape, full),
            pl.BlockSpec(wkv.shape, full),
            pl.BlockSpec(wg.shape, full),
            pl.BlockSpec(poolw.shape, lambda b, i: (0, 0, 0)),
            pl.BlockSpec(pscale.shape, full),
        ],
        out_specs=[pl.BlockSpec((tm, o.shape[1]), row) for o in outs],
        out_shape=outs,
        scratch_shapes=[pltpu.VMEM((POOL_HALO + tm, POOL_WIDTH), F32)],
        compiler_params=_params(2),
        name="proj_even",
    )(xf, cos_t, sin_t, ec, es, wp, wq, wkv, wg, poolw, pscale)


def _compress_kernel(rk_ref, rv_ref, cos_ref, sin_ref, ec_ref, es_ref, pk1_ref, pk2_ref, pv1_ref, pv2_ref,
                     wk1_ref, wk2_ref, wv1_ref, wv2_ref, kc_o, vc_o):
    nc = kc_o.shape[0]

    def compress(raw_ref, p1, p2, w1, w2):
        r = jnp.concatenate([raw_ref[pl.ds(j, nc, stride=CMP_STRIDE), :] for j in range(CMP_STRIDE)], axis=1)
        a = _dot((r + p1[...]).astype(BF16), w1[...])
        b = _dot((r + p2[...]).astype(BF16), w2[...])
        return a + pltpu.roll(b, nc - 1, 0)

    kc = compress(rk_ref, pk1_ref, pk2_ref, wk1_ref, wk2_ref)
    cos, sin = _rope_patterns(cos_ref[...], sin_ref[...], ec_ref[...], es_ref[...])
    kc_o[...] = _rope(kc, cos, sin).astype(BF16)
    vc_o[...] = compress(rv_ref, pv1_ref, pv2_ref, wv1_ref, wv2_ref).astype(BF16)


def _compress(rk, rv, cos_t, sin_t, first_col_block, consts):
    batch, seq, width = rk.shape
    nc = seq // CMP_STRIDE
    blk = lambda b: (b, 0, 0)
    full = lambda b: (0, 0)
    out = jax.ShapeDtypeStruct((batch, nc, KV_WIDTH), BF16)
    return pl.pallas_call(
        _compress_kernel,
        grid=(batch,),
        in_specs=[pl.BlockSpec((None, seq, width), blk), pl.BlockSpec((None, seq, width), blk),
                  pl.BlockSpec((ROT_HALF, nc), lambda b: (0, first_col_block + b)),
                  pl.BlockSpec((ROT_HALF, nc), lambda b: (0, first_col_block + b))]
                 + [pl.BlockSpec(a.shape, full) for a in consts],
        out_specs=[pl.BlockSpec((None, nc, KV_WIDTH), blk)] * 2,
        out_shape=[out, out],
        compiler_params=_params(1),
        name="compress",
    )(rk, rv, cos_t, sin_t, *consts)


def _softmax_rows(s):
    m = jnp.max(s, axis=1, keepdims=True)
    e = jnp.exp2(s - m)
    return e, jnp.sum(e, axis=1, keepdims=True)


def _dot_exact01(x, m01):
    hi = x.astype(BF16)
    r1 = x - hi.astype(F32)
    mid = r1.astype(BF16)
    lo = (r1 - mid.astype(F32)).astype(BF16)
    return _dot(hi, m01) + _dot(mid, m01) + _dot(lo, m01)


def _unselected_blocks(imp, q0, n_pick):
    nq, nb = imp.shape
    col = lax.broadcasted_iota(jnp.int32, (nq, nb), 1)
    cur = lax.shift_right_arithmetic(q0 + lax.broadcasted_iota(jnp.int32, (nq, nb), 0), 6)
    forced = (col == 0) | (col == cur) | (col == cur - 1)
    val = jnp.where(forced, -jnp.inf, jnp.where(col > cur, -1.0, imp))
    vt = val.T
    blk = lax.broadcasted_iota(jnp.int32, (nb, nq), 0).astype(F32)
    for _ in range(n_pick - N_FORCED):
        m = jnp.max(vt, axis=0, keepdims=True)
        first = jnp.min(jnp.where(vt == m, blk, float(nb)), axis=0, keepdims=True)
        vt = jnp.where(blk == first, -jnp.inf, vt)
    return jnp.where(vt == -jnp.inf, 0.0, 1.0).T


def _nsa_kernel(*refs, n_pick, cast_scales):
    n_cast = len(cast_scales)
    q_ref, gate_ref, kc_ref, vc_ref, ks_ref, vs_ref, kw_ref, vw_ref, pt_ref, ov_ref = refs[:10]
    cast_in = refs[10:10 + n_cast]
    o_ref = refs[10 + n_cast]
    cast_out = refs[11 + n_cast:11 + 2 * n_cast]
    m_scr, l_scr, acc_scr, lhs_scr, oc_scr, ow_scr, s0_scr, s1_scr, w_scr, gx_scr = refs[11 + 2 * n_cast:]

    for src, dst, scale in zip(cast_in, cast_out, cast_scales):
        w = src[...] if scale == 1.0 else src[...] * scale
        dst[...] = w.astype(BF16)

    qb = pl.program_id(1)
    q0 = qb * Q_BLOCK
    tk = SEL_KEY_TILE
    ncmp = kc_ref.shape[0]
    wk = WINDOW + Q_BLOCK
    rows = NSA_GROUP * Q_BLOCK
    lane = lax.broadcasted_iota(jnp.int32, (Q_BLOCK, LANES), 1)

    def trow(width):
        return q0 + lax.broadcasted_iota(jnp.int32, (Q_BLOCK, width), 0)

    def kcol(width):
        return lax.broadcasted_iota(jnp.int32, (Q_BLOCK, width), 1)

    bias_c = jnp.where(kcol(ncmp) * CMP_STRIDE + (CMP_BLOCK - 1) <= trow(ncmp), 0.0, NEG)
    has_cmp = (trow(1) >= CMP_BLOCK - 1).astype(F32)
    wstart = pl.multiple_of(jnp.maximum(q0 - WINDOW, 0), Q_BLOCK)
    kpos_w = wstart + kcol(wk)
    bias_w = jnp.where(kpos_w <= trow(wk), jnp.where(kpos_w > trow(wk) - WINDOW, 0.0, NEG), NEG)

    for g in range(NSA_KV_HEADS):
        mine = (lane >= NSA_HEAD_DIM) if g else (lane < NSA_HEAD_DIM)
        lhs_scr[g, :, 0:LANES] = jnp.concatenate(
            [jnp.where(mine, q_ref[:, r * LANES:(r + 1) * LANES], jnp.zeros((), BF16))
             for r in range(NSA_GROUP)], axis=0)
        m_scr[g] = jnp.full((rows, LANES), -jnp.inf, F32)
        l_scr[g] = jnp.zeros((rows, LANES), F32)
        acc_scr[g] = jnp.zeros((rows, LANES), F32)

    def cmp_scores(g):
        s0_scr[g] = _dot_nt(lhs_scr[g, :, 0:LANES], kc_ref[...])

    def cmp_attend(g):
        e, l = _softmax_rows(s0_scr[g] + _rep_rows(bias_c, NSA_GROUP))
        p = e * (_rep_rows(has_cmp, NSA_GROUP) / l)
        oc_scr[g] = _dot(p.astype(BF16), vc_ref[...])
        psum = p[0:Q_BLOCK]
        for r in range(1, NSA_GROUP):
            psum = psum + p[r * Q_BLOCK:(r + 1) * Q_BLOCK]
        return _dot_exact01(psum, ov_ref[...])

    def select(g, imp):
        unsel = _unselected_blocks(imp, q0, n_pick)
        lhs_scr[g, :, LANES:2 * LANES] = _rep_rows(unsel.astype(BF16), NSA_GROUP)

    def win_scores(g):
        w_scr[g] = _dot_nt(lhs_scr[g, :, 0:LANES], kw_ref[pl.ds(wstart, wk), :])

    def win_attend(g):
        e, l = _softmax_rows(w_scr[g] + _rep_rows(bias_w, NSA_GROUP))
        ow_scr[g] = _dot(e.astype(BF16), vw_ref[pl.ds(wstart, wk), :]) / l

    def expand_gates():
        low = lane < NSA_HEAD_DIM
        for r in range(NSA_GROUP):
            for br in range(N_BRANCH):
                c_lo = r * N_BRANCH + br
                c_hi = (NSA_GROUP + r) * N_BRANCH + br
                gx_scr[c_lo] = jnp.where(low, gate_ref[:, c_lo:c_lo + 1], gate_ref[:, c_hi:c_hi + 1])

    cmp_scores(0)
    cmp_scores(1)
    expand_gates()
    win_scores(0)
    imp0 = cmp_attend(0)
    win_scores(1)
    imp1 = cmp_attend(1)
    select(0, imp0)
    win_attend(0)
    select(1, imp1)
    win_attend(1)

    def scores(kt, buf, g):
        k0 = pl.multiple_of(kt * tk, tk)
        rhs = jnp.concatenate([ks_ref[pl.ds(k0, tk), :], pt_ref[pl.ds(k0, tk), :]], axis=1)
        buf[g] = _dot_nt(lhs_scr[g], rhs)

    def consume(kt, buf, g, causal):
        k0 = pl.multiple_of(kt * tk, tk)
        v = vs_ref[pl.ds(k0, tk), :]
        bias = jnp.where(k0 + kcol(tk) <= trow(tk), 0.0, NEG) if causal else None
        for r in range(NSA_GROUP):
            rs = slice(r * Q_BLOCK, (r + 1) * Q_BLOCK)
            s = buf[g, rs, :]
            if causal:
                s = s + bias
            m_prev = m_scr[g, rs, :]
            m_new = jnp.maximum(m_prev, jnp.max(s, axis=1, keepdims=True))
            alpha = jnp.exp2(m_prev - m_new)
            pe = jnp.exp2(s - _rep_lanes(m_new, tk // LANES))
            part = pe[:, 0:LANES]
            for c in range(1, tk // LANES):
                part = part + pe[:, c * LANES:(c + 1) * LANES]
            l_scr[g, rs, :] = alpha * l_scr[g, rs, :] + part
            acc_scr[g, rs, :] = alpha * acc_scr[g, rs, :] + _dot(pe.astype(BF16), v)
            m_scr[g, rs, :] = m_new

    def step(nxt, nxt_buf, cur, cur_buf, causal=False):
        for g in range(NSA_KV_HEADS):
            if nxt is not None:
                scores(nxt, nxt_buf, g)
        for g in range(NSA_KV_HEADS):
            consume(cur, cur_buf, g, causal)

    n_full = q0 // tk
    for g in range(NSA_KV_HEADS):
        scores(0, s0_scr, g)

    def pair(j, carry):
        step(2 * j + 1, s1_scr, 2 * j, s0_scr)
        step(2 * j + 2, s0_scr, 2 * j + 1, s1_scr)
        return carry

    lax.fori_loop(0, n_full // 2, pair, 0)

    @pl.when(n_full % 2 == 0)
    def _():
        step(None, None, n_full, s0_scr, causal=True)

    @pl.when(n_full % 2 == 1)
    def _():
        step(n_full, s1_scr, n_full - 1, s0_scr)
        step(None, None, n_full, s1_scr, causal=True)

    low = lane < NSA_HEAD_DIM
    for r in range(NSA_GROUP):
        rs = slice(r * Q_BLOCK, (r + 1) * Q_BLOCK)
        merged = lambda scr: jnp.where(low, scr[0, rs, :], scr[1, rs, :])
        l_sel = jnp.where(low, jnp.sum(l_scr[0, rs, :], axis=1, keepdims=True),
                          jnp.sum(l_scr[1, rs, :], axis=1, keepdims=True))
        out = (gx_scr[r * N_BRANCH] * merged(oc_scr)
               + gx_scr[r * N_BRANCH + 1] * (merged(acc_scr) / l_sel)
               + gx_scr[r * N_BRANCH + 2] * merged(ow_scr))
        o_ref[:, r * LANES:(r + 1) * LANES] = out.astype(BF16)


def _nsa(q, gates, kc, vc, ks, vs, kw, vw, pt, ov, casts, batch, seq):
    n = q.shape[0]
    nqb = seq // Q_BLOCK
    steps = batch * nqb
    rows = NSA_GROUP * Q_BLOCK
    row = lambda b, i: (b * nqb + i, 0)
    per_b = lambda b, i: (b, 0, 0)
    full = lambda b, i: (0, 0)
    seq_spec = _resident((None, seq, KV_WIDTH), per_b)
    cmp_spec = _resident((None, kc.shape[1], KV_WIDTH), per_b)
    n_pick = min(SEL_COUNT, seq // SEL_BLOCK)
    sliced = [w.reshape(steps, -1, w.shape[-1]) for w, _ in casts]
    slice_specs = [pl.BlockSpec((None,) + w.shape[1:], lambda b, i: (b * nqb + i, 0, 0)) for w in sliced]
    outs = pl.pallas_call(
        functools.partial(_nsa_kernel, n_pick=n_pick, cast_scales=tuple(s for _, s in casts)),
        grid=(batch, nqb),
        in_specs=[pl.BlockSpec((Q_BLOCK, Q_WIDTH), row), pl.BlockSpec((Q_BLOCK, LANES), row),
                  cmp_spec, cmp_spec, seq_spec, seq_spec, seq_spec, seq_spec,
                  _resident(pt.shape, full), _resident(ov.shape, full)] + slice_specs,
        out_specs=[pl.BlockSpec((Q_BLOCK, Q_WIDTH), row)] + slice_specs,
        out_shape=[jax.ShapeDtypeStruct((n, Q_WIDTH), BF16)]
                  + [jax.ShapeDtypeStruct(w.shape, BF16) for w in sliced],
        scratch_shapes=[pltpu.VMEM((NSA_KV_HEADS, rows, LANES), F32),
                        pltpu.VMEM((NSA_KV_HEADS, rows, LANES), F32),
                        pltpu.VMEM((NSA_KV_HEADS, rows, LANES), F32),
                        pltpu.VMEM((NSA_KV_HEADS, rows, 2 * LANES), BF16),
                        pltpu.VMEM((NSA_KV_HEADS, rows, LANES), F32),
                        pltpu.VMEM((NSA_KV_HEADS, rows, LANES), F32),
                        pltpu.VMEM((NSA_KV_HEADS, rows, SEL_KEY_TILE), F32),
                        pltpu.VMEM((NSA_KV_HEADS, rows, SEL_KEY_TILE), F32),
                        pltpu.VMEM((NSA_KV_HEADS, rows, WINDOW + Q_BLOCK), F32),
                        pltpu.VMEM((NSA_GROUP * N_BRANCH, Q_BLOCK, LANES), F32)],
        compiler_params=_params(2),
        name="nsa",
    )(q, gates, kc, vc, ks, vs, kw, vw, pt, ov, *sliced)
    return outs[0], [o.reshape(w.shape) for o, (w, _) in zip(outs[1:], casts)]


def _matmul_kernel(a_ref, w_ref, o_ref):
    o_ref[...] = _dot(a_ref[...].astype(BF16), w_ref[...].astype(BF16)).astype(o_ref.dtype)


def _mem_kv(memf, wkv_all, layer):
    m, k = memf.shape
    n = wkv_all.shape[2]
    return pl.pallas_call(
        _matmul_kernel,
        grid=(1,),
        in_specs=[pl.BlockSpec((m, k), lambda i: (0, 0)), pl.BlockSpec((None, k, n), lambda i: (layer, 0, 0))],
        out_specs=pl.BlockSpec((m, n), lambda i: (0, 0)),
        out_shape=jax.ShapeDtypeStruct((m, n), BF16),
        compiler_params=_params(1),
        name="mem_kv",
    )(memf, wkv_all)


def _tail_kernel(*refs, n_parts):
    x_ref = refs[0]
    parts = refs[1:1 + n_parts]
    ws = refs[1 + n_parts:1 + 2 * n_parts]
    wq_ref, k_ref, v_ref, wo_ref, wu_ref, wd_ref, g_ref, b_ref, o_ref, x1_scr, x2_scr, x2_prev = refs[1 + 2 * n_parts:]
    t = pl.program_id(0)
    n_tiles = pl.num_programs(0) - 1
    tm = x_ref.shape[0]
    half = tm // 2
    halves = (slice(0, half), slice(half, tm))
    x1 = x1_scr if n_parts else x_ref

    def norm(k, z):
        return _layer_norm(z, g_ref[k:k + 1, :], b_ref[k:k + 1, :])

    def mixer_out(rs):
        z = ALPHA * x_ref[rs, :]
        for a, w in zip(parts, ws):
            z = z + _dot(a[rs, :], w[...])
        x1_scr[rs, :] = norm(0, z)

    def project(rs):
        return _dot(x1[rs, :].astype(BF16), wq_ref[...]).astype(BF16)

    heads = [slice(h * X_HEAD_DIM, (h + 1) * X_HEAD_DIM) for h in range(X_HEADS)]

    def scores(q):
        return [_dot_nt(q[:, sl], k_ref[:, sl]) for sl in heads]

    def probabilities(ss):
        probs = []
        for s in ss:
            e, l = _softmax_rows(s)
            probs.append((e / l).astype(BF16))
        return probs

    def mix(probs):
        outs = [_dot(p, v_ref[:, sl]) for p, sl in zip(probs, heads)]
        return jnp.concatenate(outs, axis=1).astype(BF16)

    def finish(rs, o):
        x2_scr[rs, :] = norm(1, ALPHA * x1[rs, :] + _dot(o, wo_ref[...]))

    def attention_stages():
        st = {}
        stages = []
        both = range(len(halves))
        if n_parts:
            stages += [functools.partial(mixer_out, h) for h in halves]
        stages += [lambda i=i: st.__setitem__(("q", i), project(halves[i])) for i in both]
        stages += [lambda i=i: st.__setitem__(("s", i), scores(st["q", i])) for i in both]
        stages += [lambda i=i: st.__setitem__(("p", i), probabilities(st["s", i])) for i in both]
        stages += [lambda i=i: st.__setitem__(("o", i), mix(st["p", i])) for i in both]
        stages += [lambda i=i: finish(halves[i], st["o", i]) for i in both]
        return stages

    def ffn_stages(src, out):
        st = {}

        def start():
            x2 = src[...]
            st.update(xb=x2.astype(BF16), z=ALPHA * x2)

        def chunk(c0, width):
            h1 = _dot(st["xb"], wu_ref[:, c0:c0 + width])
            h2 = _dot(st["xb"], wu_ref[:, FFN_HIDDEN + c0:FFN_HIDDEN + c0 + width])
            act = (h1 * _sigmoid_tanh(h1) * h2).astype(BF16)
            st["z"] = st["z"] + _dot(act, wd_ref[c0:c0 + width, :])

        stages = [start]
        c0 = 0
        for width in FFN_CHUNKS:
            stages.append(functools.partial(chunk, c0, width))
            c0 += width
        stages.append(lambda: out.__setitem__(Ellipsis, norm(2, st["z"])))
        return stages

    @pl.when(t == 0)
    def _():
        for stage in attention_stages():
            stage()
        x2_prev[...] = x2_scr[...]

    @pl.when((t > 0) & (t < n_tiles))
    def _():
        att = attention_stages()
        ffn = ffn_stages(x2_prev, o_ref)
        order = []
        while att or ffn:
            if ffn:
                order.append(ffn.pop(0))
            if att:
                order.append(att.pop(0))
        for stage in order:
            stage()
        x2_prev[...] = x2_scr[...]

    @pl.when(t == n_tiles)
    def _():
        for stage in ffn_stages(x2_prev, o_ref):
            stage()


def _tail(xf, parts, ws, kv, wq_all, wo_all, wu_all, wd_all, g_all, b_all, layer, batch, seq):
    n = xf.shape[0]
    tm = ROW_TILE
    nt = seq // tm
    n_tiles = n // tm
    n_mem = kv.shape[1]
    cur = lambda t: jnp.minimum(t, n_tiles - 1)
    row = lambda t: (cur(t), 0)
    lagged = lambda t: (jnp.maximum(t - 1, 0), 0)
    full = lambda t: (0, 0)
    this_layer = lambda t: (layer, 0, 0)
    stacked = lambda a: _resident((None,) + a.shape[1:], this_layer)
    return pl.pallas_call(
        functools.partial(_tail_kernel, n_parts=len(parts)),
        grid=(n_tiles + 1,),
        in_specs=[pl.BlockSpec((tm, D_MODEL), row)]
                 + [pl.BlockSpec((tm, a.shape[1]), row) for a in parts]
                 + [_resident(w.shape, full) for w in ws]
                 + [stacked(wq_all),
                    _resident((None, n_mem, D_MODEL), lambda t: (cur(t) // nt, 0, 0)),
                    _resident((None, n_mem, D_MODEL), lambda t: (cur(t) // nt, 0, 1)),
                    stacked(wo_all), stacked(wu_all), stacked(wd_all), stacked(g_all), stacked(b_all)],
        out_specs=pl.BlockSpec((tm, D_MODEL), lagged),
        out_shape=jax.ShapeDtypeStruct((n, D_MODEL), F32),
        scratch_shapes=[pltpu.VMEM((tm, D_MODEL), F32), pltpu.VMEM((tm, D_MODEL), F32),
                        pltpu.VMEM((tm, D_MODEL), F32)],
        compiler_params=_params(1),
        name="tail",
    )(xf, *parts, *ws, wq_all, kv, kv, wo_all, wu_all, wd_all, g_all, b_all)


def _gelu_tanh(x):
    inner = x * (GELU_C0 + GELU_C1 * (x * x))
    return x * (0.5 * jnp.tanh(inner) + 0.5)


def _sigmoid_tanh(z):
    return 0.5 * jnp.tanh(0.5 * z) + 0.5


def _odd_kernel(x_ref, pos_ref, perm_ref, unperm_ref, win_ref, cw_ref, cb_ref, wax_ref, ba_ref, bx_ref, lam_ref,
                wout_ref, g_ref, b_ref, o_ref, xbuf, gate_scr, a_scr, b_scr, tail_scr, h_scr):
    i = pl.program_id(1)
    tm = x_ref.shape[0]
    half = tm // 2
    nv = half // SUBLANES
    halo = (CONV_WIDTH - 1) * SUBLANES

    @pl.when(i == 0)
    def _():
        tail_scr[...] = jnp.zeros((SUBLANES, RNN_WIDTH), F32)
        h_scr[...] = jnp.zeros((SUBLANES, RNN_WIDTH), F32)

    def permute(v2d):
        return jnp.swapaxes(v2d.reshape(SUBLANES, nv, v2d.shape[1]), 0, 1).reshape(v2d.shape)

    sub = lax.broadcasted_iota(jnp.int32, (SUBLANES, RNN_WIDTH), 0)
    lam = lam_ref[...]
    half_scale = (-0.5 * LRU_C) * (jnp.maximum(-lam, 0.0) + jnp.log1p(jnp.exp(-jnp.abs(lam))))

    def in_proj(hi, h0):
        xb = _dot(perm_ref[...], x_ref[h0:h0 + half, :].astype(BF16)).astype(BF16)
        gate_scr[h0:h0 + half, :] = _dot(xb, win_ref[:, :RNN_WIDTH])
        xbuf[hi, halo:halo + half, :] = _dot(xb, win_ref[:, RNN_WIDTH:])

    def gates(hi, h0):
        for j in range(1, CONV_WIDTH):
            last = xbuf[hi, halo + (nv - j) * SUBLANES:halo + (nv - j + 1) * SUBLANES, :]
            before = jnp.where(sub == 0, tail_scr[j:j + 1, :], pltpu.roll(last, 1, 0))
            xbuf[hi, halo - j * SUBLANES:halo - (j - 1) * SUBLANES, :] = before
            tail_scr[j:j + 1, :] = last[SUBLANES - 1:SUBLANES, :]
        is_reset = (pos_ref[h0:h0 + half, :] == 0).astype(F32)
        reset = permute(jnp.broadcast_to(is_reset, (half, RNN_BLOCK_W))) > 0.5
        for h in range(RNN_BLOCKS):
            sl = slice(h * RNN_BLOCK_W, (h + 1) * RNN_BLOCK_W)
            xc = cb_ref[:, sl]
            for k in range(CONV_WIDTH):
                xc = xc + xbuf[hi, k * SUBLANES:k * SUBLANES + half, sl] * cw_ref[k:k + 1, sl]
            ri = _dot(xc.astype(BF16), wax_ref[h])
            t_r = jnp.tanh(ri[:, :RNN_BLOCK_W] + ba_ref[:, sl])
            t_i = jnp.tanh(ri[:, RNN_BLOCK_W:] + bx_ref[:, sl])
            log_a = t_r * half_scale[:, sl] + half_scale[:, sl]
            a = jnp.where(reset, 0.0, jnp.exp(log_a))
            th = jnp.tanh(log_a)
            q2 = -0.5 * th / (1.0 - th)
            half_mult = jnp.where(reset, 0.5, jnp.where(q2 > 0.0, q2 * lax.rsqrt(q2), 0.0))
            b = half_mult * (t_i * xc + xc)
            a_scr[h0:h0 + half, sl] = a
            b_scr[h0:h0 + half, sl] = b

    def recur(h0, h_prev):
        decay = jnp.ones((SUBLANES, RNN_WIDTH), F32)
        resp = jnp.zeros((SUBLANES, RNN_WIDTH), F32)
        for v in range(nv):
            rs = slice(h0 + v * SUBLANES, h0 + (v + 1) * SUBLANES)
            av = a_scr[rs, :]
            decay = av * decay
            resp = av * resp + b_scr[rs, :]
            a_scr[rs, :] = decay
            b_scr[rs, :] = resp
        ca, cb_ = decay, resp
        for d in (1, 2, 4):
            ok = sub >= d
            cb_ = jnp.where(ok, ca * pltpu.roll(cb_, d, 0) + cb_, cb_)
            ca = jnp.where(ok, ca * pltpu.roll(ca, d, 0), ca)
        h_end = ca * h_prev + cb_
        h_in = jnp.where(sub == 0, h_prev, pltpu.roll(h_end, 1, 0))
        rs = slice(h0, h0 + half)
        hh = (a_scr[rs, :].reshape(nv, SUBLANES, RNN_WIDTH) * h_in[None]
              + b_scr[rs, :].reshape(nv, SUBLANES, RNN_WIDTH))
        b_scr[rs, :] = hh.reshape(half, RNN_WIDTH)
        return jnp.broadcast_to(h_end[SUBLANES - 1:SUBLANES, :], (SUBLANES, RNN_WIDTH))

    def out_proj(h0):
        rs = slice(h0, h0 + half)
        yp = (b_scr[rs, :] * _gelu_tanh(gate_scr[rs, :])).astype(BF16)
        y = _dot(unperm_ref[...], yp).astype(BF16)
        z = ALPHA * x_ref[rs, :] + _dot(y, wout_ref[...])
        o_ref[rs, :] = _layer_norm(z, g_ref[...], b_ref[...])

    in_proj(0, 0)
    in_proj(1, half)
    gates(0, 0)
    h_mid = recur(0, h_scr[...])
    gates(1, half)
    out_proj(0)
    h_scr[...] = recur(half, h_mid)
    out_proj(half)


def _odd_layer(xf, posi, w_in, cw, cb, wax, ba, bx, lam, w_out, g, b, batch, seq):
    n = xf.shape[0]
    tm = ROW_TILE
    nt = seq // tm
    row = lambda bi, i: (bi * nt + i, 0)
    full = lambda bi, i: (0, 0)
    half = tm // 2
    rho = np.arange(half)
    src_time = (rho % SUBLANES) * (half // SUBLANES) + rho // SUBLANES
    perm = np.zeros((half, half), np.float32)
    perm[rho, src_time] = 1.0
    consts = [jnp.asarray(perm, BF16), jnp.asarray(perm.T, BF16), w_in, cw, cb]
    rest = [ba, bx, lam, w_out, g, b]
    return pl.pallas_call(
        _odd_kernel,
        grid=(batch, nt),
        in_specs=[pl.BlockSpec((tm, D_MODEL), row), pl.BlockSpec((tm, 1), row)]
                 + [pl.BlockSpec(a.shape, full) for a in consts]
                 + [pl.BlockSpec(wax.shape, lambda bi, i: (0, 0, 0))]
                 + [pl.BlockSpec(a.shape, full) for a in rest],
        out_specs=pl.BlockSpec((tm, D_MODEL), row),
        out_shape=jax.ShapeDtypeStruct((n, D_MODEL), F32),
        scratch_shapes=[pltpu.VMEM((2, (CONV_WIDTH - 1) * SUBLANES + tm // 2, RNN_WIDTH), F32),
                        pltpu.VMEM((tm, RNN_WIDTH), F32),
                        pltpu.VMEM((tm, RNN_WIDTH), F32),
                        pltpu.VMEM((tm, RNN_WIDTH), F32),
                        pltpu.VMEM((SUBLANES, RNN_WIDTH), F32),
                        pltpu.VMEM((SUBLANES, RNN_WIDTH), F32)],
        compiler_params=_params(2),
        name="odd_layer",
    )(xf, posi, *consts, wax, *rest)


def _even_mixer(xf, positions, w_in, pool_w, pool_scale, cmp_pos_k, cmp_pos_v, cmp_wk, cmp_wv,
                w_out, casts, batch, seq):
    assert seq // SEL_BLOCK == LANES and seq % SEL_KEY_TILE == 0 and seq >= WINDOW + Q_BLOCK
    hd, g, r = NSA_HEAD_DIM, NSA_KV_HEADS, NSA_GROUP
    c0 = POOL_WIDTH
    c1 = c0 + Q_WIDTH
    c2 = c1 + GATE_WIDTH
    wp = w_in[:, :c0].astype(BF16)
    wq = (w_in[:, c0:c1].reshape(D_MODEL, g, r, hd).transpose(0, 2, 1, 3).reshape(D_MODEL, Q_WIDTH)
          * (hd ** -0.5 * LOG2E)).astype(BF16)
    wg = jnp.pad(w_in[:, c1:c2], ((0, 0), (0, LANES - GATE_WIDTH))).astype(BF16)
    wkv = w_in[:, c2:].astype(BF16)
    n = batch * seq
    nc = seq // CMP_STRIDE
    n_cmp = (seq - CMP_BLOCK) // CMP_STRIDE + 1
    posc = jnp.pad(positions[:, CMP_BLOCK - 1::CMP_STRIDE][:, :n_cmp], ((0, 0), (0, nc - n_cmp)))
    pos_all = jnp.concatenate([positions.reshape(1, n), posc.reshape(1, batch * nc)], axis=1).astype(F32)
    cos_t, sin_t = _rope_angles(pos_all)
    ec, es = _rope_spread()
    pool, q, kc_raw, vc_raw, ks, vs, kw, vw, gates = _proj_even(
        xf, cos_t, sin_t, ec, es, wp, wq, wkv, wg, pool_w.astype(BF16), pool_scale[None, :], batch, seq)

    def halves(w):
        w3 = w.reshape(CMP_BLOCK, hd, hd).astype(BF16)
        zero = jnp.zeros_like(w3)
        w4 = jnp.concatenate([jnp.concatenate([w3 if k == j else zero for k in range(g)], axis=2)
                              for j in range(g)], axis=1)
        return (w4[:CMP_STRIDE].reshape(CMP_STRIDE * g * hd, g * hd),
                w4[CMP_STRIDE:].reshape(CMP_STRIDE * g * hd, g * hd))

    def pos_halves(p):
        p2 = jnp.tile(p[:, None, :], (1, g, 1)).reshape(CMP_BLOCK, g * hd)
        return p2[:CMP_STRIDE].reshape(1, -1), p2[CMP_STRIDE:].reshape(1, -1)

    pk1, pk2 = pos_halves(cmp_pos_k)
    pv1, pv2 = pos_halves(cmp_pos_v)
    wk1, wk2 = halves(cmp_wk)
    wv1, wv2 = halves(cmp_wv)
    seq3 = lambda a: a.reshape(batch, seq, KV_WIDTH)
    kc, vc = _compress(seq3(kc_raw), seq3(vc_raw), cos_t, sin_t, n // nc,
                       (ec, es, pk1, pk2, pv1, pv2, wk1, wk2, wv1, wv2))

    n_sb = seq // SEL_BLOCK
    starts = np.arange(nc) * CMP_STRIDE
    jb = np.arange(n_sb)
    ov = ((starts[:, None] < (jb[None, :] + 1) * SEL_BLOCK)
          & (starts[:, None] + CMP_BLOCK > jb[None, :] * SEL_BLOCK) & (np.arange(nc)[:, None] < n_cmp))
    ov = jnp.asarray(ov, BF16)
    pt = jnp.asarray(np.where(np.arange(seq)[:, None] // SEL_BLOCK == jb[None, :], NEG, 0.0), BF16)

    nsa, cast_weights = _nsa(q, gates, kc, vc, seq3(ks), seq3(vs), seq3(kw), seq3(vw), pt, ov, casts, batch, seq)

    w_pool_out = w_out[:POOL_WIDTH].astype(BF16)
    w_nsa_out = (w_out[POOL_WIDTH:].reshape(g, r, hd, D_MODEL).transpose(1, 0, 2, 3)
                 .reshape(Q_WIDTH, D_MODEL).astype(BF16))
    return [pool, nsa], [w_pool_out, w_nsa_out], cast_weights


def _odd_mixer(xf, posi, w_in, conv_w, conv_b, wa, ba, wx, bx, lam, w_out, ln_g, ln_b, batch, seq):
    wax = (0.5 * jnp.concatenate([wa, wx], axis=2)).astype(BF16)
    return _odd_layer(xf, posi, w_in, conv_w, conv_b[None, :], wax, 0.5 * ba[None, :],
                      0.5 * bx[None, :], lam[None, :], w_out, ln_g, ln_b, batch, seq)


def kernel(x, mem, positions, e_w_in, e_pool_w, e_pool_scale, e_cmp_pos_k, e_cmp_pos_v, e_cmp_wk, e_cmp_wv, e_w_out, o_w_in, o_conv_w, o_conv_b, o_wa, o_ba, o_wx, o_bx, o_lambda, o_w_out, x_wq, x_wkv, x_wo, f_w_up, f_w_down, ln_g, ln_b):
    batch, seq, d = x.shape
    n = batch * seq
    assert d == D_MODEL and seq % ROW_TILE == 0
    xf = x.reshape(n, d)
    posi = positions.reshape(n, 1)
    memf = mem.reshape(batch * mem.shape[1], d)
    casts = [(x_wq, X_HEAD_DIM ** -0.5 * LOG2E), (x_wo, 1.0), (f_w_up, 1.0), (f_w_down, 1.0),
             (o_w_in, 1.0), (o_w_out, 1.0)]
    for layer in range(DEPTH):
        j = layer // 2
        if layer % 2 == 0:
            parts, ws, cast_weights = _even_mixer(
                xf, positions, e_w_in[j], e_pool_w[j], e_pool_scale[j], e_cmp_pos_k[j], e_cmp_pos_v[j],
                e_cmp_wk[j], e_cmp_wv[j], e_w_out[j], casts if layer == 0 else [], batch, seq)
            if layer == 0:
                wq_all, wo_all, wu_all, wd_all, o_w_in16, o_w_out16 = cast_weights
        else:
            xf = _odd_mixer(xf, posi, o_w_in16[j], o_conv_w[j], o_conv_b[j], o_wa[j], o_ba[j], o_wx[j], o_bx[j],
                            o_lambda[j], o_w_out16[j], ln_g[layer, 0][None, :], ln_b[layer, 0][None, :],
                            batch, seq)
            parts, ws = [], []
        kv = _mem_kv(memf, x_wkv, layer).reshape(batch, mem.shape[1], 2 * d)
        xf = _tail(xf, parts, ws, kv, wq_all, wo_all, wu_all, wd_all, ln_g, ln_b, layer, batch, seq)
    return xf.reshape(batch, seq, d)
```

```python
import functools

import numpy as np
import jax
import jax.numpy as jnp
from jax import lax
from jax.experimental import pallas as pl
from jax.experimental.pallas import tpu as pltpu

F32 = jnp.float32
BF16 = jnp.bfloat16

D_MODEL = 1024
DEPTH = 2
ALPHA = (2.0 * DEPTH) ** 0.25
LN_EPS = 1e-5
NEG = -1e30
POOL_WIDTH = D_MODEL // 2
POOL_WINDOWS = (2, 4, 8, 16)
POOL_GROUP = POOL_WIDTH // len(POOL_WINDOWS)
POOL_HALO = 16
NSA_HEADS = 8
NSA_KV_HEADS = 2
NSA_HEAD_DIM = 64
NSA_GROUP = NSA_HEADS // NSA_KV_HEADS
CMP_BLOCK = 32
CMP_STRIDE = 16
SEL_BLOCK = 64
SEL_COUNT = 16
WINDOW = 512
Q_BLOCK = 256
N_BRANCH = 3
N_FORCED = 3
LOG2E = 1.4426950408889634
ROPE_THETA = 500000.0
ROT_DIM = NSA_HEAD_DIM // 4
ROT_HALF = ROT_DIM // 2
Q_WIDTH = NSA_HEADS * NSA_HEAD_DIM
KV_WIDTH = NSA_KV_HEADS * NSA_HEAD_DIM
GATE_WIDTH = NSA_HEADS * N_BRANCH
RNN_WIDTH = 1280
RNN_BLOCKS = 10
RNN_BLOCK_W = RNN_WIDTH // RNN_BLOCKS
CONV_WIDTH = 4
LRU_C = 8.0
X_HEADS = 4
X_HEAD_DIM = D_MODEL // X_HEADS
FFN_HIDDEN = 2816
GELU_C0 = 0.7978845608028654
GELU_C1 = GELU_C0 * 0.044715

LANES = 128
SUBLANES = 8
VMEM_LIMIT = 56 * 1024 * 1024
ROW_TILE = 512
SEL_KEY_TILE = 512
FFN_CHUNKS = (256,) * 11


def _params(n_axes, vmem=VMEM_LIMIT):
    return pltpu.CompilerParams(dimension_semantics=("arbitrary",) * n_axes,
                                vmem_limit_bytes=vmem)


def _resident(shape, index_map):
    return pl.BlockSpec(shape, index_map, pipeline_mode=pl.Buffered(1))


def _dot(a, b):
    return jnp.dot(a, b, preferred_element_type=F32)


def _dot_nt(a, b):
    return lax.dot_general(a, b, (((1,), (1,)), ((), ())), preferred_element_type=F32)


def _rep_rows(a, k):
    return jnp.concatenate([a] * k, axis=0)


def _rep_lanes(a, k):
    return jnp.concatenate([a] * k, axis=1) if k > 1 else a


def _layer_norm(z, g, b):
    mu = jnp.mean(z, axis=-1, keepdims=True)
    d = z - mu
    var = jnp.mean(d * d, axis=-1, keepdims=True)
    return d * lax.rsqrt(var + LN_EPS) * g + b


def _rope(v, cos, sin):
    k = v.shape[1] // LANES
    up = pltpu.roll(v, v.shape[1] - ROT_HALF, 1)
    dn = pltpu.roll(v, ROT_HALF, 1)
    lane = lax.broadcasted_iota(jnp.int32, v.shape, 1)
    partner = jnp.where((lane & (NSA_HEAD_DIM - 1)) < ROT_HALF, up, dn)
    return v * _rep_lanes(cos, k) + partner * _rep_lanes(sin, k)


def _rope_angle_kernel(pos_ref, inv_ref, cos_o, sin_o):
    ang = inv_ref[...] * pos_ref[...]
    cos_o[...] = jnp.cos(ang)
    sin_o[...] = jnp.sin(ang)


def _rope_angles(pos_row):
    n = pos_row.shape[1]
    inv = (ROPE_THETA ** (-jnp.arange(ROT_HALF, dtype=F32) * 2.0 / ROT_DIM))[:, None]
    out = jax.ShapeDtypeStruct((ROT_HALF, n), F32)
    return pl.pallas_call(
        _rope_angle_kernel,
        out_shape=[out, out],
        name="rope_angles",
    )(pos_row, inv)


def _rope_spread():
    lane = np.arange(LANES) % NSA_HEAD_DIM
    f = np.arange(ROT_HALF)[:, None]
    lo = (lane[None, :] == f).astype(np.float32)
    hi = (lane[None, :] == f + ROT_HALF).astype(np.float32)
    return jnp.asarray(lo + hi, BF16), jnp.asarray(hi - lo, BF16)


def _rope_patterns(cos_t, sin_t, ec, es):
    def spread(t, e):
        hi = t.astype(BF16)
        r1 = t - hi.astype(F32)
        mid = r1.astype(BF16)
        lo = (r1 - mid.astype(F32)).astype(BF16)
        tn = (((0,), (0,)), ((), ()))
        return sum(lax.dot_general(p, e, tn, preferred_element_type=F32) for p in (hi, mid, lo))

    lane = lax.broadcasted_iota(jnp.int32, (1, LANES), 1)
    unrotated = jnp.where((lane & (NSA_HEAD_DIM - 1)) >= ROT_DIM, 1.0, 0.0)
    return spread(cos_t, ec) + unrotated, spread(sin_t, es)


def _proj_even_kernel(x_ref, cos_ref, sin_ref, ec_ref, es_ref, wp_ref, wq_ref, wkv_ref, wg_ref, poolw_ref, pscale_ref,
                      pool_o, q_o, kc_o, vc_o, ks_o, vs_o, kw_o, vw_o, gate_o, pbuf):
    i = pl.program_id(1)
    tm = x_ref.shape[0]
    half = tm // 2

    @pl.when(i == 0)
    def _():
        pbuf[0:POOL_HALO, :] = jnp.zeros((POOL_HALO, POOL_WIDTH), F32)

    def project(h0):
        rs = slice(h0, h0 + half)
        xb = x_ref[rs, :].astype(BF16)
        pbuf[POOL_HALO + h0:POOL_HALO + h0 + half, :] = _dot(xb, wp_ref[...])
        return _dot(xb, wq_ref[...]), _dot(xb, wkv_ref[...]), _dot(xb, wg_ref[...])

    def emit(h0, q, kv, gl):
        rs = slice(h0, h0 + half)
        cos, sin = _rope_patterns(cos_ref[:, rs], sin_ref[:, rs], ec_ref[...], es_ref[...])
        q_o[rs, :] = _rope(q, cos, sin).astype(BF16)
        kc_o[rs, :] = kv[:, 0 * LANES:1 * LANES]
        vc_o[rs, :] = kv[:, 1 * LANES:2 * LANES]
        ks_o[rs, :] = _rope(kv[:, 2 * LANES:3 * LANES], cos, sin).astype(BF16)
        vs_o[rs, :] = kv[:, 3 * LANES:4 * LANES].astype(BF16)
        kw_o[rs, :] = _rope(kv[:, 4 * LANES:5 * LANES], cos, sin).astype(BF16)
        vw_o[rs, :] = kv[:, 5 * LANES:6 * LANES].astype(BF16)
        gate_o[rs, :] = _sigmoid_tanh(gl)

    def pool(h0):
        rs = slice(h0, h0 + half)
        t1 = i * tm + h0 + lax.broadcasted_iota(jnp.int32, (half, 1), 0) + 1
        for g, w in enumerate(POOL_WINDOWS):
            sl = slice(g * POOL_GROUP, (g + 1) * POOL_GROUP)
            u = pbuf[POOL_HALO + h0:POOL_HALO + h0 + half, sl]
            tot = u
            for j in range(1, w):
                tot = tot + pbuf[pl.ds(POOL_HALO + h0 - j, half), sl]
            cnt = jnp.minimum(t1, w).astype(F32)
            pooled = tot / cnt - u
            mixed = _dot(pooled.astype(BF16), poolw_ref[g]) * pscale_ref[:, sl]
            pool_o[rs, sl] = mixed.astype(BF16)

    first = project(0)
    second = project(half)
    emit(0, *first)
    pool(0)
    emit(half, *second)
    pool(half)
    pbuf[0:POOL_HALO, :] = pbuf[tm:tm + POOL_HALO, :]


def _proj_even(xf, cos_t, sin_t, ec, es, wp, wq, wkv, wg, poolw, pscale, batch, seq):
    n = xf.shape[0]
    tm = ROW_TILE
    nt = seq // tm
    row = lambda b, i: (b * nt + i, 0)
    full = lambda b, i: (0, 0)
    outs = [
        jax.ShapeDtypeStruct((n, POOL_WIDTH), BF16),
        jax.ShapeDtypeStruct((n, Q_WIDTH), BF16),
        jax.ShapeDtypeStruct((n, KV_WIDTH), F32),
        jax.ShapeDtypeStruct((n, KV_WIDTH), F32),
        jax.ShapeDtypeStruct((n, KV_WIDTH), BF16),
        jax.ShapeDtypeStruct((n, KV_WIDTH), BF16),
        jax.ShapeDtypeStruct((n, KV_WIDTH), BF16),
        jax.ShapeDtypeStruct((n, KV_WIDTH), BF16),
        jax.ShapeDtypeStruct((n, LANES), F32),
    ]
    return pl.pallas_call(
        _proj_even_kernel,
        grid=(batch, nt),
        in_specs=[
            pl.BlockSpec((tm, D_MODEL), row),
            pl.BlockSpec((ROT_HALF, tm), lambda b, i: (0, b * nt + i)),
            pl.BlockSpec((ROT_HALF, tm), lambda b, i: (0, b * nt + i)),
            pl.BlockSpec(ec.shape, full),
            pl.BlockSpec(es.shape, full),
            pl.BlockSpec(wp.shape, full),
            pl.BlockSpec(wq.shape, full),
            pl.BlockSpec(wkv.shape, full),
            pl.BlockSpec(wg.shape, full),
            pl.BlockSpec(poolw.shape, lambda b, i: (0, 0, 0)),
            pl.BlockSpec(pscale.shape, full),
        ],
        out_specs=[pl.BlockSpec((tm, o.shape[1]), row) for o in outs],
        out_shape=outs,
        scratch_shapes=[pltpu.VMEM((POOL_HALO + tm, POOL_WIDTH), F32)],
        compiler_params=_params(2),
        name="proj_even",
    )(xf, cos_t, sin_t, ec, es, wp, wq, wkv, wg, poolw, pscale)


def _compress_kernel(rk_ref, rv_ref, cos_ref, sin_ref, ec_ref, es_ref, pk1_ref, pk2_ref, pv1_ref, pv2_ref,
                     wk1_ref, wk2_ref, wv1_ref, wv2_ref, kc_o, vc_o):
    nc = kc_o.shape[0]

    def compress(raw_ref, p1, p2, w1, w2):
        r = jnp.concatenate([raw_ref[pl.ds(j, nc, stride=CMP_STRIDE), :] for j in range(CMP_STRIDE)], axis=1)
        a = _dot((r + p1[...]).astype(BF16), w1[...])
        b = _dot((r + p2[...]).astype(BF16), w2[...])
        return a + pltpu.roll(b, nc - 1, 0)

    kc = compress(rk_ref, pk1_ref, pk2_ref, wk1_ref, wk2_ref)
    cos, sin = _rope_patterns(cos_ref[...], sin_ref[...], ec_ref[...], es_ref[...])
    kc_o[...] = _rope(kc, cos, sin).astype(BF16)
    vc_o[...] = compress(rv_ref, pv1_ref, pv2_ref, wv1_ref, wv2_ref).astype(BF16)


def _compress(rk, rv, cos_t, sin_t, first_col_block, consts):
    batch, seq, width = rk.shape
    nc = seq // CMP_STRIDE
    blk = lambda b: (b, 0, 0)
    full = lambda b: (0, 0)
    out = jax.ShapeDtypeStruct((batch, nc, KV_WIDTH), BF16)
    return pl.pallas_call(
        _compress_kernel,
        grid=(batch,),
        in_specs=[pl.BlockSpec((None, seq, width), blk), pl.BlockSpec((None, seq, width), blk),
                  pl.BlockSpec((ROT_HALF, nc), lambda b: (0, first_col_block + b)),
                  pl.BlockSpec((ROT_HALF, nc), lambda b: (0, first_col_block + b))]
                 + [pl.BlockSpec(a.shape, full) for a in consts],
        out_specs=[pl.BlockSpec((None, nc, KV_WIDTH), blk)] * 2,
        out_shape=[out, out],
        compiler_params=_params(1),
        name="compress",
    )(rk, rv, cos_t, sin_t, *consts)


def _softmax_rows(s):
    m = jnp.max(s, axis=1, keepdims=True)
    e = jnp.exp2(s - m)
    return e, jnp.sum(e, axis=1, keepdims=True)


def _dot_exact01(x, m01):
    hi = x.astype(BF16)
    r1 = x - hi.astype(F32)
    mid = r1.astype(BF16)
    lo = (r1 - mid.astype(F32)).astype(BF16)
    return _dot(hi, m01) + _dot(mid, m01) + _dot(lo, m01)


def _unselected_blocks(imp, q0, n_pick):
    nq, nb = imp.shape
    col = lax.broadcasted_iota(jnp.int32, (nq, nb), 1)
    cur = lax.shift_right_arithmetic(q0 + lax.broadcasted_iota(jnp.int32, (nq, nb), 0), 6)
    forced = (col == 0) | (col == cur) | (col == cur - 1)
    val = jnp.where(forced, -jnp.inf, jnp.where(col > cur, -1.0, imp))
    vt = val.T
    blk = lax.broadcasted_iota(jnp.int32, (nb, nq), 0).astype(F32)
    for _ in range(n_pick - N_FORCED):
        m = jnp.max(vt, axis=0, keepdims=True)
        first = jnp.min(jnp.where(vt == m, blk, float(nb)), axis=0, keepdims=True)
        vt = jnp.where(blk == first, -jnp.inf, vt)
    return jnp.where(vt == -jnp.inf, 0.0, 1.0).T


def _nsa_kernel(*refs, n_pick, cast_scales):
    n_cast = len(cast_scales)
    q_ref, gate_ref, kc_ref, vc_ref, ks_ref, vs_ref, kw_ref, vw_ref, pt_ref, ov_ref = refs[:10]
    cast_in = refs[10:10 + n_cast]
    o_ref = refs[10 + n_cast]
    cast_out = refs[11 + n_cast:11 + 2 * n_cast]
    m_scr, l_scr, acc_scr, lhs_scr, oc_scr, ow_scr, s0_scr, s1_scr, w_scr, gx_scr = refs[11 + 2 * n_cast:]

    for src, dst, scale in zip(cast_in, cast_out, cast_scales):
        w = src[...] if scale == 1.0 else src[...] * scale
        dst[...] = w.astype(BF16)

    qb = pl.program_id(1)
    q0 = qb * Q_BLOCK
    tk = SEL_KEY_TILE
    ncmp = kc_ref.shape[0]
    wk = WINDOW + Q_BLOCK
    rows = NSA_GROUP * Q_BLOCK
    lane = lax.broadcasted_iota(jnp.int32, (Q_BLOCK, LANES), 1)

    def trow(width):
        return q0 + lax.broadcasted_iota(jnp.int32, (Q_BLOCK, width), 0)

    def kcol(width):
        return lax.broadcasted_iota(jnp.int32, (Q_BLOCK, width), 1)

    bias_c = jnp.where(kcol(ncmp) * CMP_STRIDE + (CMP_BLOCK - 1) <= trow(ncmp), 0.0, NEG)
    has_cmp = (trow(1) >= CMP_BLOCK - 1).astype(F32)
    wstart = pl.multiple_of(jnp.maximum(q0 - WINDOW, 0), Q_BLOCK)
    kpos_w = wstart + kcol(wk)
    bias_w = jnp.where(kpos_w <= trow(wk), jnp.where(kpos_w > trow(wk) - WINDOW, 0.0, NEG), NEG)

    for g in range(NSA_KV_HEADS):
        mine = (lane >= NSA_HEAD_DIM) if g else (lane < NSA_HEAD_DIM)
        lhs_scr[g, :, 0:LANES] = jnp.concatenate(
            [jnp.where(mine, q_ref[:, r * LANES:(r + 1) * LANES], jnp.zeros((), BF16))
             for r in range(NSA_GROUP)], axis=0)
        m_scr[g] = jnp.full((rows, LANES), -jnp.inf, F32)
        l_scr[g] = jnp.zeros((rows, LANES), F32)
        acc_scr[g] = jnp.zeros((rows, LANES), F32)

    def cmp_scores(g):
        s0_scr[g] = _dot_nt(lhs_scr[g, :, 0:LANES], kc_ref[...])

    def cmp_attend(g):
        e, l = _softmax_rows(s0_scr[g] + _rep_rows(bias_c, NSA_GROUP))
        p = e * (_rep_rows(has_cmp, NSA_GROUP) / l)
        oc_scr[g] = _dot(p.astype(BF16), vc_ref[...])
        psum = p[0:Q_BLOCK]
        for r in range(1, NSA_GROUP):
            psum = psum + p[r * Q_BLOCK:(r + 1) * Q_BLOCK]
        return _dot_exact01(psum, ov_ref[...])

    def select(g, imp):
        unsel = _unselected_blocks(imp, q0, n_pick)
        lhs_scr[g, :, LANES:2 * LANES] = _rep_rows(unsel.astype(BF16), NSA_GROUP)

    def win_scores(g):
        w_scr[g] = _dot_nt(lhs_scr[g, :, 0:LANES], kw_ref[pl.ds(wstart, wk), :])

    def win_attend(g):
        e, l = _softmax_rows(w_scr[g] + _rep_rows(bias_w, NSA_GROUP))
        ow_scr[g] = _dot(e.astype(BF16), vw_ref[pl.ds(wstart, wk), :]) / l

    def expand_gates():
        low = lane < NSA_HEAD_DIM
        for r in range(NSA_GROUP):
            for br in range(N_BRANCH):
                c_lo = r * N_BRANCH + br
                c_hi = (NSA_GROUP + r) * N_BRANCH + br
                gx_scr[c_lo] = jnp.where(low, gate_ref[:, c_lo:c_lo + 1], gate_ref[:, c_hi:c_hi + 1])

    cmp_scores(0)
    cmp_scores(1)
    expand_gates()
    win_scores(0)
    imp0 = cmp_attend(0)
    win_scores(1)
    imp1 = cmp_attend(1)
    select(0, imp0)
    win_attend(0)
    select(1, imp1)
    win_attend(1)

    def scores(kt, buf, g):
        k0 = pl.multiple_of(kt * tk, tk)
        rhs = jnp.concatenate([ks_ref[pl.ds(k0, tk), :], pt_ref[pl.ds(k0, tk), :]], axis=1)
        buf[g] = _dot_nt(lhs_scr[g], rhs)

    def consume(kt, buf, g, causal):
        k0 = pl.multiple_of(kt * tk, tk)
        v = vs_ref[pl.ds(k0, tk), :]
        bias = jnp.where(k0 + kcol(tk) <= trow(tk), 0.0, NEG) if causal else None
        for r in range(NSA_GROUP):
            rs = slice(r * Q_BLOCK, (r + 1) * Q_BLOCK)
            s = buf[g, rs, :]
            if causal:
                s = s + bias
            m_prev = m_scr[g, rs, :]
            m_new = jnp.maximum(m_prev, jnp.max(s, axis=1, keepdims=True))
            alpha = jnp.exp2(m_prev - m_new)
            pe = jnp.exp2(s - _rep_lanes(m_new, tk // LANES))
            part = pe[:, 0:LANES]
            for c in range(1, tk // LANES):
                part = part + pe[:, c * LANES:(c + 1) * LANES]
            l_scr[g, rs, :] = alpha * l_scr[g, rs, :] + part
            acc_scr[g, rs, :] = alpha * acc_scr[g, rs, :] + _dot(pe.astype(BF16), v)
            m_scr[g, rs, :] = m_new

    def step(nxt, nxt_buf, cur, cur_buf, causal=False):
        for g in range(NSA_KV_HEADS):
            if nxt is not None:
                scores(nxt, nxt_buf, g)
        for g in range(NSA_KV_HEADS):
            consume(cur, cur_buf, g, causal)

    n_full = q0 // tk
    for g in range(NSA_KV_HEADS):
        scores(0, s0_scr, g)

    def pair(j, carry):
        step(2 * j + 1, s1_scr, 2 * j, s0_scr)
        step(2 * j + 2, s0_scr, 2 * j + 1, s1_scr)
        return carry

    lax.fori_loop(0, n_full // 2, pair, 0)

    @pl.when(n_full % 2 == 0)
    def _():
        step(None, None, n_full, s0_scr, causal=True)

    @pl.when(n_full % 2 == 1)
    def _():
        step(n_full, s1_scr, n_full - 1, s0_scr)
        step(None, None, n_full, s1_scr, causal=True)

    low = lane < NSA_HEAD_DIM
    for r in range(NSA_GROUP):
        rs = slice(r * Q_BLOCK, (r + 1) * Q_BLOCK)
        merged = lambda scr: jnp.where(low, scr[0, rs, :], scr[1, rs, :])
        l_sel = jnp.where(low, jnp.sum(l_scr[0, rs, :], axis=1, keepdims=True),
                          jnp.sum(l_scr[1, rs, :], axis=1, keepdims=True))
        out = (gx_scr[r * N_BRANCH] * merged(oc_scr)
               + gx_scr[r * N_BRANCH + 1] * (merged(acc_scr) / l_sel)
               + gx_scr[r * N_BRANCH + 2] * merged(ow_scr))
        o_ref[:, r * LANES:(r + 1) * LANES] = out.astype(BF16)


def _nsa(q, gates, kc, vc, ks, vs, kw, vw, pt, ov, casts, batch, seq):
    n = q.shape[0]
    nqb = seq // Q_BLOCK
    steps = batch * nqb
    rows = NSA_GROUP * Q_BLOCK
    row = lambda b, i: (b * nqb + i, 0)
    per_b = lambda b, i: (b, 0, 0)
    full = lambda b, i: (0, 0)
    seq_spec = _resident((None, seq, KV_WIDTH), per_b)
    cmp_spec = _resident((None, kc.shape[1], KV_WIDTH), per_b)
    n_pick = min(SEL_COUNT, seq // SEL_BLOCK)
    sliced = [w.reshape(steps, -1, w.shape[-1]) for w, _ in casts]
    slice_specs = [pl.BlockSpec((None,) + w.shape[1:], lambda b, i: (b * nqb + i, 0, 0)) for w in sliced]
    outs = pl.pallas_call(
        functools.partial(_nsa_kernel, n_pick=n_pick, cast_scales=tuple(s for _, s in casts)),
        grid=(batch, nqb),
        in_specs=[pl.BlockSpec((Q_BLOCK, Q_WIDTH), row), pl.BlockSpec((Q_BLOCK, LANES), row),
                  cmp_spec, cmp_spec, seq_spec, seq_spec, seq_spec, seq_spec,
                  _resident(pt.shape, full), _resident(ov.shape, full)] + slice_specs,
        out_specs=[pl.BlockSpec((Q_BLOCK, Q_WIDTH), row)] + slice_specs,
        out_shape=[jax.ShapeDtypeStruct((n, Q_WIDTH), BF16)]
                  + [jax.ShapeDtypeStruct(w.shape, BF16) for w in sliced],
        scratch_shapes=[pltpu.VMEM((NSA_KV_HEADS, rows, LANES), F32),
                        pltpu.VMEM((NSA_KV_HEADS, rows, LANES), F32),
                        pltpu.VMEM((NSA_KV_HEADS, rows, LANES), F32),
                        pltpu.VMEM((NSA_KV_HEADS, rows, 2 * LANES), BF16),
                        pltpu.VMEM((NSA_KV_HEADS, rows, LANES), F32),
                        pltpu.VMEM((NSA_KV_HEADS, rows, LANES), F32),
                        pltpu.VMEM((NSA_KV_HEADS, rows, SEL_KEY_TILE), F32),
                        pltpu.VMEM((NSA_KV_HEADS, rows, SEL_KEY_TILE), F32),
                        pltpu.VMEM((NSA_KV_HEADS, rows, WINDOW + Q_BLOCK), F32),
                        pltpu.VMEM((NSA_GROUP * N_BRANCH, Q_BLOCK, LANES), F32)],
        compiler_params=_params(2),
        name="nsa",
    )(q, gates, kc, vc, ks, vs, kw, vw, pt, ov, *sliced)
    return outs[0], [o.reshape(w.shape) for o, (w, _) in zip(outs[1:], casts)]


def _matmul_kernel(a_ref, w_ref, o_ref):
    o_ref[...] = _dot(a_ref[...].astype(BF16), w_ref[...].astype(BF16)).astype(o_ref.dtype)


def _mem_kv(memf, wkv_all, layer):
    m, k = memf.shape
    n = wkv_all.shape[2]
    return pl.pallas_call(
        _matmul_kernel,
        grid=(1,),
        in_specs=[pl.BlockSpec((m, k), lambda i: (0, 0)), pl.BlockSpec((None, k, n), lambda i: (layer, 0, 0))],
        out_specs=pl.BlockSpec((m, n), lambda i: (0, 0)),
        out_shape=jax.ShapeDtypeStruct((m, n), BF16),
        compiler_params=_params(1),
        name="mem_kv",
    )(memf, wkv_all)


def _tail_kernel(*refs, n_parts):
    x_ref = refs[0]
    parts = refs[1:1 + n_parts]
    ws = refs[1 + n_parts:1 + 2 * n_parts]
    wq_ref, k_ref, v_ref, wo_ref, wu_ref, wd_ref, g_ref, b_ref, o_ref, x1_scr, x2_scr, x2_prev = refs[1 + 2 * n_parts:]
    t = pl.program_id(0)
    n_tiles = pl.num_programs(0) - 1
    tm = x_ref.shape[0]
    half = tm // 2
    halves = (slice(0, half), slice(half, tm))
    x1 = x1_scr if n_parts else x_ref

    def norm(k, z):
        return _layer_norm(z, g_ref[k:k + 1, :], b_ref[k:k + 1, :])

    def mixer_out(rs):
        z = ALPHA * x_ref[rs, :]
        for a, w in zip(parts, ws):
            z = z + _dot(a[rs, :], w[...])
        x1_scr[rs, :] = norm(0, z)

    def project(rs):
        return _dot(x1[rs, :].astype(BF16), wq_ref[...]).astype(BF16)

    heads = [slice(h * X_HEAD_DIM, (h + 1) * X_HEAD_DIM) for h in range(X_HEADS)]

    def scores(q):
        return [_dot_nt(q[:, sl], k_ref[:, sl]) for sl in heads]

    def probabilities(ss):
        probs = []
        for s in ss:
            e, l = _softmax_rows(s)
            probs.append((e / l).astype(BF16))
        return probs

    def mix(probs):
        outs = [_dot(p, v_ref[:, sl]) for p, sl in zip(probs, heads)]
        return jnp.concatenate(outs, axis=1).astype(BF16)

    def finish(rs, o):
        x2_scr[rs, :] = norm(1, ALPHA * x1[rs, :] + _dot(o, wo_ref[...]))

    def attention_stages():
        st = {}
        stages = []
        both = range(len(halves))
        if n_parts:
            stages += [functools.partial(mixer_out, h) for h in halves]
        stages += [lambda i=i: st.__setitem__(("q", i), project(halves[i])) for i in both]
        stages += [lambda i=i: st.__setitem__(("s", i), scores(st["q", i])) for i in both]
        stages += [lambda i=i: st.__setitem__(("p", i), probabilities(st["s", i])) for i in both]
        stages += [lambda i=i: st.__setitem__(("o", i), mix(st["p", i])) for i in both]
        stages += [lambda i=i: finish(halves[i], st["o", i]) for i in both]
        return stages

    def ffn_stages(src, out):
        st = {}

        def start():
            x2 = src[...]
            st.update(xb=x2.astype(BF16), z=ALPHA * x2)

        def chunk(c0, width):
            h1 = _dot(st["xb"], wu_ref[:, c0:c0 + width])
            h2 = _dot(st["xb"], wu_ref[:, FFN_HIDDEN + c0:FFN_HIDDEN + c0 + width])
            act = (h1 * _sigmoid_tanh(h1) * h2).astype(BF16)
            st["z"] = st["z"] + _dot(act, wd_ref[c0:c0 + width, :])

        stages = [start]
        c0 = 0
        for width in FFN_CHUNKS:
            stages.append(functools.partial(chunk, c0, width))
            c0 += width
        stages.append(lambda: out.__setitem__(Ellipsis, norm(2, st["z"])))
        return stages

    @pl.when(t == 0)
    def _():
        for stage in attention_stages():
            stage()
        x2_prev[...] = x2_scr[...]

    @pl.when((t > 0) & (t < n_tiles))
    def _():
        att = attention_stages()
        ffn = ffn_stages(x2_prev, o_ref)
        order = []
        while att or ffn:
            if ffn:
                order.append(ffn.pop(0))
            if att:
                order.append(att.pop(0))
        for stage in order:
            stage()
        x2_prev[...] = x2_scr[...]

    @pl.when(t == n_tiles)
    def _():
        for stage in ffn_stages(x2_prev, o_ref):
            stage()


def _tail(xf, parts, ws, kv, wq_all, wo_all, wu_all, wd_all, g_all, b_all, layer, batch, seq):
    n = xf.shape[0]
    tm = ROW_TILE
    nt = seq // tm
    n_tiles = n // tm
    n_mem = kv.shape[1]
    cur = lambda t: jnp.minimum(t, n_tiles - 1)
    row = lambda t: (cur(t), 0)
    lagged = lambda t: (jnp.maximum(t - 1, 0), 0)
    full = lambda t: (0, 0)
    this_layer = lambda t: (layer, 0, 0)
    stacked = lambda a: _resident((None,) + a.shape[1:], this_layer)
    return pl.pallas_call(
        functools.partial(_tail_kernel, n_parts=len(parts)),
        grid=(n_tiles + 1,),
        in_specs=[pl.BlockSpec((tm, D_MODEL), row)]
                 + [pl.BlockSpec((tm, a.shape[1]), row) for a in parts]
                 + [_resident(w.shape, full) for w in ws]
                 + [stacked(wq_all),
                    _resident((None, n_mem, D_MODEL), lambda t: (cur(t) // nt, 0, 0)),
                    _resident((None, n_mem, D_MODEL), lambda t: (cur(t) // nt, 0, 1)),
                    stacked(wo_all), stacked(wu_all), stacked(wd_all), stacked(g_all), stacked(b_all)],
        out_specs=pl.BlockSpec((tm, D_MODEL), lagged),
        out_shape=jax.ShapeDtypeStruct((n, D_MODEL), F32),
        scratch_shapes=[pltpu.VMEM((tm, D_MODEL), F32), pltpu.VMEM((tm, D_MODEL), F32),
                        pltpu.VMEM((tm, D_MODEL), F32)],
        compiler_params=_params(1),
        name="tail",
    )(xf, *parts, *ws, wq_all, kv, kv, wo_all, wu_all, wd_all, g_all, b_all)


def _gelu_tanh(x):
    inner = x * (GELU_C0 + GELU_C1 * (x * x))
    return x * (0.5 * jnp.tanh(inner) + 0.5)


def _sigmoid_tanh(z):
    return 0.5 * jnp.tanh(0.5 * z) + 0.5


def _odd_kernel(x_ref, pos_ref, perm_ref, unperm_ref, win_ref, cw_ref, cb_ref, wax_ref, ba_ref, bx_ref, lam_ref,
                wout_ref, g_ref, b_ref, o_ref, xbuf, gate_scr, a_scr, b_scr, tail_scr, h_scr):
    i = pl.program_id(1)
    tm = x_ref.shape[0]
    half = tm // 2
    nv = half // SUBLANES
    halo = (CONV_WIDTH - 1) * SUBLANES

    @pl.when(i == 0)
    def _():
        tail_scr[...] = jnp.zeros((SUBLANES, RNN_WIDTH), F32)
        h_scr[...] = jnp.zeros((SUBLANES, RNN_WIDTH), F32)

    def permute(v2d):
        return jnp.swapaxes(v2d.reshape(SUBLANES, nv, v2d.shape[1]), 0, 1).reshape(v2d.shape)

    sub = lax.broadcasted_iota(jnp.int32, (SUBLANES, RNN_WIDTH), 0)
    lam = lam_ref[...]
    half_scale = (-0.5 * LRU_C) * (jnp.maximum(-lam, 0.0) + jnp.log1p(jnp.exp(-jnp.abs(lam))))

    col_chunk = 2 * LANES

    def in_proj(hi, h0):
        st = {}

        def permute_rows():
            st["xb"] = _dot(perm_ref[...], x_ref[h0:h0 + half, :].astype(BF16)).astype(BF16)

        def gate_cols(c):
            gate_scr[h0:h0 + half, c:c + col_chunk] = _dot(st["xb"], win_ref[:, c:c + col_chunk])

        def input_cols(c):
            xbuf[hi, halo:halo + half, c:c + col_chunk] = _dot(
                st["xb"], win_ref[:, RNN_WIDTH + c:RNN_WIDTH + c + col_chunk])

        cols = range(0, RNN_WIDTH, col_chunk)
        return ([permute_rows] + [functools.partial(gate_cols, c) for c in cols]
                + [functools.partial(input_cols, c) for c in cols])

    def gates(hi, h0):
        st = {}

        def prepare():
            for j in range(1, CONV_WIDTH):
                last = xbuf[hi, halo + (nv - j) * SUBLANES:halo + (nv - j + 1) * SUBLANES, :]
                before = jnp.where(sub == 0, tail_scr[j:j + 1, :], pltpu.roll(last, 1, 0))
                xbuf[hi, halo - j * SUBLANES:halo - (j - 1) * SUBLANES, :] = before
                tail_scr[j:j + 1, :] = last[SUBLANES - 1:SUBLANES, :]
            is_reset = (pos_ref[h0:h0 + half, :] == 0).astype(F32)
            st["reset"] = permute(jnp.broadcast_to(is_reset, (half, RNN_BLOCK_W))) > 0.5

        def block(h):
            reset = st["reset"]
            sl = slice(h * RNN_BLOCK_W, (h + 1) * RNN_BLOCK_W)
            xc = cb_ref[:, sl]
            for k in range(CONV_WIDTH):
                xc = xc + xbuf[hi, k * SUBLANES:k * SUBLANES + half, sl] * cw_ref[k:k + 1, sl]
            ri = _dot(xc.astype(BF16), wax_ref[h])
            t_r = jnp.tanh(ri[:, :RNN_BLOCK_W] + ba_ref[:, sl])
            t_i = jnp.tanh(ri[:, RNN_BLOCK_W:] + bx_ref[:, sl])
            log_a = t_r * half_scale[:, sl] + half_scale[:, sl]
            a = jnp.where(reset, 0.0, jnp.exp(log_a))
            th = jnp.tanh(log_a)
            q2 = -0.5 * th / (1.0 - th)
            half_mult = jnp.where(reset, 0.5, jnp.where(q2 > 0.0, q2 * lax.rsqrt(q2), 0.0))
            b = half_mult * (t_i * xc + xc)
            a_scr[h0:h0 + half, sl] = a
            b_scr[h0:h0 + half, sl] = b

        return [prepare] + [functools.partial(block, h) for h in range(RNN_BLOCKS)]

    def recur(h0, h_prev):
        decay = jnp.ones((SUBLANES, RNN_WIDTH), F32)
        resp = jnp.zeros((SUBLANES, RNN_WIDTH), F32)
        for v in range(nv):
            rs = slice(h0 + v * SUBLANES, h0 + (v + 1) * SUBLANES)
            av = a_scr[rs, :]
            decay = av * decay
            resp = av * resp + b_scr[rs, :]
            a_scr[rs, :] = decay
            b_scr[rs, :] = resp
        ca, cb_ = decay, resp
        for d in (1, 2, 4):
            ok = sub >= d
            cb_ = jnp.where(ok, ca * pltpu.roll(cb_, d, 0) + cb_, cb_)
            ca = jnp.where(ok, ca * pltpu.roll(ca, d, 0), ca)
        h_end = ca * h_prev + cb_
        h_in = jnp.where(sub == 0, h_prev, pltpu.roll(h_end, 1, 0))
        rs = slice(h0, h0 + half)
        hh = (a_scr[rs, :].reshape(nv, SUBLANES, RNN_WIDTH) * h_in[None]
              + b_scr[rs, :].reshape(nv, SUBLANES, RNN_WIDTH))
        b_scr[rs, :] = hh.reshape(half, RNN_WIDTH)
        return jnp.broadcast_to(h_end[SUBLANES - 1:SUBLANES, :], (SUBLANES, RNN_WIDTH))

    def out_proj(h0):
        rs = slice(h0, h0 + half)
        st = {}

        def gate_output():
            st["yp"] = (b_scr[rs, :] * _gelu_tanh(gate_scr[rs, :])).astype(BF16)

        def unpermute_rows():
            st["y"] = _dot(unperm_ref[...], st["yp"]).astype(BF16)

        def project():
            z = ALPHA * x_ref[rs, :] + _dot(st["y"], wout_ref[...])
            o_ref[rs, :] = _layer_norm(z, g_ref[...], b_ref[...])

        return [gate_output, unpermute_rows, project]

    def run(*stage_lists):
        lists = [list(s) for s in stage_lists]
        while any(lists):
            for s in lists:
                if s:
                    s.pop(0)()

    run(in_proj(0, 0))
    run(gates(0, 0), in_proj(1, half))
    h_mid = recur(0, h_scr[...])
    run(gates(1, half), out_proj(0))
    h_scr[...] = recur(half, h_mid)
    run(out_proj(half))


def _odd_layer(xf, posi, w_in, cw, cb, wax, ba, bx, lam, w_out, g, b, batch, seq):
    n = xf.shape[0]
    tm = ROW_TILE
    nt = seq // tm
    row = lambda bi, i: (bi * nt + i, 0)
    full = lambda bi, i: (0, 0)
    half = tm // 2
    rho = np.arange(half)
    src_time = (rho % SUBLANES) * (half // SUBLANES) + rho // SUBLANES
    perm = np.zeros((half, half), np.float32)
    perm[rho, src_time] = 1.0
    consts = [jnp.asarray(perm, BF16), jnp.asarray(perm.T, BF16), w_in, cw, cb]
    rest = [ba, bx, lam, w_out, g, b]
    return pl.pallas_call(
        _odd_kernel,
        grid=(batch, nt),
        in_specs=[pl.BlockSpec((tm, D_MODEL), row), pl.BlockSpec((tm, 1), row)]
                 + [pl.BlockSpec(a.shape, full) for a in consts]
                 + [pl.BlockSpec(wax.shape, lambda bi, i: (0, 0, 0))]
                 + [pl.BlockSpec(a.shape, full) for a in rest],
        out_specs=pl.BlockSpec((tm, D_MODEL), row),
        out_shape=jax.ShapeDtypeStruct((n, D_MODEL), F32),
        scratch_shapes=[pltpu.VMEM((2, (CONV_WIDTH - 1) * SUBLANES + tm // 2, RNN_WIDTH), F32),
                        pltpu.VMEM((tm, RNN_WIDTH), F32),
                        pltpu.VMEM((tm, RNN_WIDTH), F32),
                        pltpu.VMEM((tm, RNN_WIDTH), F32),
                        pltpu.VMEM((SUBLANES, RNN_WIDTH), F32),
                        pltpu.VMEM((SUBLANES, RNN_WIDTH), F32)],
        compiler_params=_params(2),
        name="odd_layer",
    )(xf, posi, *consts, wax, *rest)


def _even_mixer(xf, positions, w_in, pool_w, pool_scale, cmp_pos_k, cmp_pos_v, cmp_wk, cmp_wv,
                w_out, casts, batch, seq):
    assert seq // SEL_BLOCK == LANES and seq % SEL_KEY_TILE == 0 and seq >= WINDOW + Q_BLOCK
    hd, g, r = NSA_HEAD_DIM, NSA_KV_HEADS, NSA_GROUP
    c0 = POOL_WIDTH
    c1 = c0 + Q_WIDTH
    c2 = c1 + GATE_WIDTH
    wp = w_in[:, :c0].astype(BF16)
    wq = (w_in[:, c0:c1].reshape(D_MODEL, g, r, hd).transpose(0, 2, 1, 3).reshape(D_MODEL, Q_WIDTH)
          * (hd ** -0.5 * LOG2E)).astype(BF16)
    wg = jnp.pad(w_in[:, c1:c2], ((0, 0), (0, LANES - GATE_WIDTH))).astype(BF16)
    wkv = w_in[:, c2:].astype(BF16)
    n = batch * seq
    nc = seq // CMP_STRIDE
    n_cmp = (seq - CMP_BLOCK) // CMP_STRIDE + 1
    posc = jnp.pad(positions[:, CMP_BLOCK - 1::CMP_STRIDE][:, :n_cmp], ((0, 0), (0, nc - n_cmp)))
    pos_all = jnp.concatenate([positions.reshape(1, n), posc.reshape(1, batch * nc)], axis=1).astype(F32)
    cos_t, sin_t = _rope_angles(pos_all)
    ec, es = _rope_spread()
    pool, q, kc_raw, vc_raw, ks, vs, kw, vw, gates = _proj_even(
        xf, cos_t, sin_t, ec, es, wp, wq, wkv, wg, pool_w.astype(BF16), pool_scale[None, :], batch, seq)

    def halves(w):
        w3 = w.reshape(CMP_BLOCK, hd, hd).astype(BF16)
        zero = jnp.zeros_like(w3)
        w4 = jnp.concatenate([jnp.concatenate([w3 if k == j else zero for k in range(g)], axis=2)
                              for j in range(g)], axis=1)
        return (w4[:CMP_STRIDE].reshape(CMP_STRIDE * g * hd, g * hd),
                w4[CMP_STRIDE:].reshape(CMP_STRIDE * g * hd, g * hd))

    def pos_halves(p):
        p2 = jnp.tile(p[:, None, :], (1, g, 1)).reshape(CMP_BLOCK, g * hd)
        return p2[:CMP_STRIDE].reshape(1, -1), p2[CMP_STRIDE:].reshape(1, -1)

    pk1, pk2 = pos_halves(cmp_pos_k)
    pv1, pv2 = pos_halves(cmp_pos_v)
    wk1, wk2 = halves(cmp_wk)
    wv1, wv2 = halves(cmp_wv)
    seq3 = lambda a: a.reshape(batch, seq, KV_WIDTH)
    kc, vc = _compress(seq3(kc_raw), seq3(vc_raw), cos_t, sin_t, n // nc,
                       (ec, es, pk1, pk2, pv1, pv2, wk1, wk2, wv1, wv2))

    n_sb = seq // SEL_BLOCK
    starts = np.arange(nc) * CMP_STRIDE
    jb = np.arange(n_sb)
    ov = ((starts[:, None] < (jb[None, :] + 1) * SEL_BLOCK)
          & (starts[:, None] + CMP_BLOCK > jb[None, :] * SEL_BLOCK) & (np.arange(nc)[:, None] < n_cmp))
    ov = jnp.asarray(ov, BF16)
    pt = jnp.asarray(np.where(np.arange(seq)[:, None] // SEL_BLOCK == jb[None, :], NEG, 0.0), BF16)

    nsa, cast_weights = _nsa(q, gates, kc, vc, seq3(ks), seq3(vs), seq3(kw), seq3(vw), pt, ov, casts, batch, seq)

    w_pool_out = w_out[:POOL_WIDTH].astype(BF16)
    w_nsa_out = (w_out[POOL_WIDTH:].reshape(g, r, hd, D_MODEL).transpose(1, 0, 2, 3)
                 .reshape(Q_WIDTH, D_MODEL).astype(BF16))
    return [pool, nsa], [w_pool_out, w_nsa_out], cast_weights


def _odd_mixer(xf, posi, w_in, conv_w, conv_b, wa, ba, wx, bx, lam, w_out, ln_g, ln_b, batch, seq):
    wax = (0.5 * jnp.concatenate([wa, wx], axis=2)).astype(BF16)
    return _odd_layer(xf, posi, w_in, conv_w, conv_b[None, :], wax, 0.5 * ba[None, :],
                      0.5 * bx[None, :], lam[None, :], w_out, ln_g, ln_b, batch, seq)


def kernel(x, mem, positions, e_w_in, e_pool_w, e_pool_scale, e_cmp_pos_k, e_cmp_pos_v, e_cmp_wk, e_cmp_wv, e_w_out, o_w_in, o_conv_w, o_conv_b, o_wa, o_ba, o_wx, o_bx, o_lambda, o_w_out, x_wq, x_wkv, x_wo, f_w_up, f_w_down, ln_g, ln_b):
    batch, seq, d = x.shape
    n = batch * seq
    assert d == D_MODEL and seq % ROW_TILE == 0
    xf = x.reshape(n, d)
    posi = positions.reshape(n, 1)
    memf = mem.reshape(batch * mem.shape[1], d)
    casts = [(x_wq, X_HEAD_DIM ** -0.5 * LOG2E), (x_wo, 1.0), (f_w_up, 1.0), (f_w_down, 1.0),
             (o_w_in, 1.0), (o_w_out, 1.0)]
    for layer in range(DEPTH):
        j = layer // 2
        if layer % 2 == 0:
            parts, ws, cast_weights = _even_mixer(
                xf, positions, e_w_in[j], e_pool_w[j], e_pool_scale[j], e_cmp_pos_k[j], e_cmp_pos_v[j],
                e_cmp_wk[j], e_cmp_wv[j], e_w_out[j], casts if layer == 0 else [], batch, seq)
            if layer == 0:
                wq_all, wo_all, wu_all, wd_all, o_w_in16, o_w_out16 = cast_weights
        else:
            xf = _odd_mixer(xf, posi, o_w_in16[j], o_conv_w[j], o_conv_b[j], o_wa[j], o_ba[j], o_wx[j], o_bx[j],
                            o_lambda[j], o_w_out16[j], ln_g[layer, 0][None, :], ln_b[layer, 0][None, :],
                            batch, seq)
            parts, ws = [], []
        kv = _mem_kv(memf, x_wkv, layer).reshape(batch, mem.shape[1], 2 * d)
        xf = _tail(xf, parts, ws, kv, wq_all, wo_all, wu_all, wd_all, ln_g, ln_b, layer, batch, seq)
    return xf.reshape(batch, seq, d)
```

```python
import functools

import numpy as np
import jax
import jax.numpy as jnp
from jax import lax
from jax.experimental import pallas as pl
from jax.experimental.pallas import tpu as pltpu

F32 = jnp.float32
BF16 = jnp.bfloat16

D_MODEL = 1024
DEPTH = 2
ALPHA = (2.0 * DEPTH) ** 0.25
LN_EPS = 1e-5
NEG = -1e30
POOL_WIDTH = D_MODEL // 2
POOL_WINDOWS = (2, 4, 8, 16)
POOL_GROUP = POOL_WIDTH // len(POOL_WINDOWS)
POOL_HALO = 16
NSA_HEADS = 8
NSA_KV_HEADS = 2
NSA_HEAD_DIM = 64
NSA_GROUP = NSA_HEADS // NSA_KV_HEADS
CMP_BLOCK = 32
CMP_STRIDE = 16
SEL_BLOCK = 64
SEL_COUNT = 16
WINDOW = 512
Q_BLOCK = 256
N_BRANCH = 3
N_FORCED = 3
LOG2E = 1.4426950408889634
ROPE_THETA = 500000.0
ROT_DIM = NSA_HEAD_DIM // 4
ROT_HALF = ROT_DIM // 2
Q_WIDTH = NSA_HEADS * NSA_HEAD_DIM
KV_WIDTH = NSA_KV_HEADS * NSA_HEAD_DIM
GATE_WIDTH = NSA_HEADS * N_BRANCH
RNN_WIDTH = 1280
RNN_BLOCKS = 10
RNN_BLOCK_W = RNN_WIDTH // RNN_BLOCKS
CONV_WIDTH = 4
LRU_C = 8.0
X_HEADS = 4
X_HEAD_DIM = D_MODEL // X_HEADS
FFN_HIDDEN = 2816
GELU_C0 = 0.7978845608028654
GELU_C1 = GELU_C0 * 0.044715

LANES = 128
SUBLANES = 8
VMEM_LIMIT = 56 * 1024 * 1024
ROW_TILE = 512
SEL_KEY_TILE = 512
FFN_CHUNKS = (256,) * 11


def _params(n_axes, vmem=VMEM_LIMIT):
    return pltpu.CompilerParams(dimension_semantics=("arbitrary",) * n_axes,
                                vmem_limit_bytes=vmem)


def _resident(shape, index_map):
    return pl.BlockSpec(shape, index_map, pipeline_mode=pl.Buffered(1))


def _emit_round_robin(*stage_lists):
    lists = [list(s) for s in stage_lists]
    while any(lists):
        for s in lists:
            if s:
                s.pop(0)()


def _dot(a, b):
    return jnp.dot(a, b, preferred_element_type=F32)


def _dot_nt(a, b):
    return lax.dot_general(a, b, (((1,), (1,)), ((), ())), preferred_element_type=F32)


def _rep_rows(a, k):
    return jnp.concatenate([a] * k, axis=0)


def _rep_lanes(a, k):
    return jnp.concatenate([a] * k, axis=1) if k > 1 else a


def _layer_norm(z, g, b):
    mu = jnp.mean(z, axis=-1, keepdims=True)
    d = z - mu
    var = jnp.mean(d * d, axis=-1, keepdims=True)
    return d * lax.rsqrt(var + LN_EPS) * g + b


def _rope(v, cos, sin):
    k = v.shape[1] // LANES
    up = pltpu.roll(v, v.shape[1] - ROT_HALF, 1)
    dn = pltpu.roll(v, ROT_HALF, 1)
    lane = lax.broadcasted_iota(jnp.int32, v.shape, 1)
    partner = jnp.where((lane & (NSA_HEAD_DIM - 1)) < ROT_HALF, up, dn)
    return v * _rep_lanes(cos, k) + partner * _rep_lanes(sin, k)


def _rope_angle_kernel(pos_ref, inv_ref, cos_o, sin_o):
    ang = inv_ref[...] * pos_ref[...]
    cos_o[...] = jnp.cos(ang)
    sin_o[...] = jnp.sin(ang)


def _rope_angles(pos_row):
    n = pos_row.shape[1]
    inv = (ROPE_THETA ** (-jnp.arange(ROT_HALF, dtype=F32) * 2.0 / ROT_DIM))[:, None]
    out = jax.ShapeDtypeStruct((ROT_HALF, n), F32)
    return pl.pallas_call(
        _rope_angle_kernel,
        out_shape=[out, out],
        name="rope_angles",
    )(pos_row, inv)


def _rope_spread():
    lane = np.arange(LANES) % NSA_HEAD_DIM
    f = np.arange(ROT_HALF)[:, None]
    lo = (lane[None, :] == f).astype(np.float32)
    hi = (lane[None, :] == f + ROT_HALF).astype(np.float32)
    return jnp.asarray(lo + hi, BF16), jnp.asarray(hi - lo, BF16)


def _rope_patterns(cos_t, sin_t, ec, es):
    def spread(t, e):
        hi = t.astype(BF16)
        r1 = t - hi.astype(F32)
        mid = r1.astype(BF16)
        lo = (r1 - mid.astype(F32)).astype(BF16)
        tn = (((0,), (0,)), ((), ()))
        return sum(lax.dot_general(p, e, tn, preferred_element_type=F32) for p in (hi, mid, lo))

    lane = lax.broadcasted_iota(jnp.int32, (1, LANES), 1)
    unrotated = jnp.where((lane & (NSA_HEAD_DIM - 1)) >= ROT_DIM, 1.0, 0.0)
    return spread(cos_t, ec) + unrotated, spread(sin_t, es)


def _proj_even_kernel(x_ref, cos_ref, sin_ref, ec_ref, es_ref, wp_ref, wq_ref, wkv_ref, wg_ref, poolw_ref, pscale_ref,
                      pool_o, q_o, kc_o, vc_o, ks_o, vs_o, kw_o, vw_o, gate_o, pbuf):
    i = pl.program_id(1)
    tm = x_ref.shape[0]
    half = tm // 2

    @pl.when(i == 0)
    def _():
        pbuf[0:POOL_HALO, :] = jnp.zeros((POOL_HALO, POOL_WIDTH), F32)

    def project(h0, st):
        rs = slice(h0, h0 + half)

        def cast():
            st["xb"] = x_ref[rs, :].astype(BF16)

        def pool_in():
            pbuf[POOL_HALO + h0:POOL_HALO + h0 + half, :] = _dot(st["xb"], wp_ref[...])

        return [cast, pool_in,
                lambda: st.__setitem__("q", _dot(st["xb"], wq_ref[...])),
                lambda: st.__setitem__("kv", _dot(st["xb"], wkv_ref[...])),
                lambda: st.__setitem__("gl", _dot(st["xb"], wg_ref[...]))]

    def emit(h0, st):
        rs = slice(h0, h0 + half)

        def patterns():
            st["cos"], st["sin"] = _rope_patterns(cos_ref[:, rs], sin_ref[:, rs], ec_ref[...], es_ref[...])

        def queries():
            q_o[rs, :] = _rope(st["q"], st["cos"], st["sin"]).astype(BF16)

        def keys_values():
            kv, cos, sin = st["kv"], st["cos"], st["sin"]
            kc_o[rs, :] = kv[:, 0 * LANES:1 * LANES]
            vc_o[rs, :] = kv[:, 1 * LANES:2 * LANES]
            ks_o[rs, :] = _rope(kv[:, 2 * LANES:3 * LANES], cos, sin).astype(BF16)
            vs_o[rs, :] = kv[:, 3 * LANES:4 * LANES].astype(BF16)
            kw_o[rs, :] = _rope(kv[:, 4 * LANES:5 * LANES], cos, sin).astype(BF16)
            vw_o[rs, :] = kv[:, 5 * LANES:6 * LANES].astype(BF16)
            gate_o[rs, :] = _sigmoid_tanh(st["gl"])

        return [patterns, queries, keys_values]

    def pool(h0):
        rs = slice(h0, h0 + half)
        t1 = i * tm + h0 + lax.broadcasted_iota(jnp.int32, (half, 1), 0) + 1

        def group(g, w):
            sl = slice(g * POOL_GROUP, (g + 1) * POOL_GROUP)
            u = pbuf[POOL_HALO + h0:POOL_HALO + h0 + half, sl]
            tot = u
            for j in range(1, w):
                tot = tot + pbuf[pl.ds(POOL_HALO + h0 - j, half), sl]
            cnt = jnp.minimum(t1, w).astype(F32)
            pooled = tot / cnt - u
            mixed = _dot(pooled.astype(BF16), poolw_ref[g]) * pscale_ref[:, sl]
            pool_o[rs, sl] = mixed.astype(BF16)

        return [functools.partial(group, g, w) for g, w in enumerate(POOL_WINDOWS)]

    first, second = {}, {}
    _emit_round_robin(project(0, first))
    _emit_round_robin(emit(0, first) + pool(0), project(half, second))
    _emit_round_robin(emit(half, second) + pool(half))
    pbuf[0:POOL_HALO, :] = pbuf[tm:tm + POOL_HALO, :]


def _proj_even(xf, cos_t, sin_t, ec, es, wp, wq, wkv, wg, poolw, pscale, batch, seq):
    n = xf.shape[0]
    tm = ROW_TILE
    nt = seq // tm
    row = lambda b, i: (b * nt + i, 0)
    full = lambda b, i: (0, 0)
    outs = [
        jax.ShapeDtypeStruct((n, POOL_WIDTH), BF16),
        jax.ShapeDtypeStruct((n, Q_WIDTH), BF16),
        jax.ShapeDtypeStruct((n, KV_WIDTH), F32),
        jax.ShapeDtypeStruct((n, KV_WIDTH), F32),
        jax.ShapeDtypeStruct((n, KV_WIDTH), BF16),
        jax.ShapeDtypeStruct((n, KV_WIDTH), BF16),
        jax.ShapeDtypeStruct((n, KV_WIDTH), BF16),
        jax.ShapeDtypeStruct((n, KV_WIDTH), BF16),
        jax.ShapeDtypeStruct((n, LANES), F32),
    ]
    return pl.pallas_call(
        _proj_even_kernel,
        grid=(batch, nt),
        in_specs=[
            pl.BlockSpec((tm, D_MODEL), row),
            pl.BlockSpec((ROT_HALF, tm), lambda b, i: (0, b * nt + i)),
            pl.BlockSpec((ROT_HALF, tm), lambda b, i: (0, b * nt + i)),
            pl.BlockSpec(ec.shape, full),
            pl.BlockSpec(es.shape, full),
            pl.BlockSpec(wp.shape, full),
            pl.BlockSpec(wq.shape, full),
            pl.BlockSpec(wkv.shape, full),
            pl.BlockSpec(wg.shape, full),
            pl.BlockSpec(poolw.shape, lambda b, i: (0, 0, 0)),
            pl.BlockSpec(pscale.shape, full),
        ],
        out_specs=[pl.BlockSpec((tm, o.shape[1]), row) for o in outs],
        out_shape=outs,
        scratch_shapes=[pltpu.VMEM((POOL_HALO + tm, POOL_WIDTH), F32)],
        compiler_params=_params(2),
        name="proj_even",
    )(xf, cos_t, sin_t, ec, es, wp, wq, wkv, wg, poolw, pscale)


def _compress_kernel(rk_ref, rv_ref, cos_ref, sin_ref, ec_ref, es_ref, pk1_ref, pk2_ref, pv1_ref, pv2_ref,
                     wk1_ref, wk2_ref, wv1_ref, wv2_ref, kc_o, vc_o):
    nc = kc_o.shape[0]

    def compress(raw_ref, p1, p2, w1, w2):
        r = jnp.concatenate([raw_ref[pl.ds(j, nc, stride=CMP_STRIDE), :] for j in range(CMP_STRIDE)], axis=1)
        a = _dot((r + p1[...]).astype(BF16), w1[...])
        b = _dot((r + p2[...]).astype(BF16), w2[...])
        return a + pltpu.roll(b, nc - 1, 0)

    kc = compress(rk_ref, pk1_ref, pk2_ref, wk1_ref, wk2_ref)
    cos, sin = _rope_patterns(cos_ref[...], sin_ref[...], ec_ref[...], es_ref[...])
    kc_o[...] = _rope(kc, cos, sin).astype(BF16)
    vc_o[...] = compress(rv_ref, pv1_ref, pv2_ref, wv1_ref, wv2_ref).astype(BF16)


def _compress(rk, rv, cos_t, sin_t, first_col_block, consts):
    batch, seq, width = rk.shape
    nc = seq // CMP_STRIDE
    blk = lambda b: (b, 0, 0)
    full = lambda b: (0, 0)
    out = jax.ShapeDtypeStruct((batch, nc, KV_WIDTH), BF16)
    return pl.pallas_call(
        _compress_kernel,
        grid=(batch,),
        in_specs=[pl.BlockSpec((None, seq, width), blk), pl.BlockSpec((None, seq, width), blk),
                  pl.BlockSpec((ROT_HALF, nc), lambda b: (0, first_col_block + b)),
                  pl.BlockSpec((ROT_HALF, nc), lambda b: (0, first_col_block + b))]
                 + [pl.BlockSpec(a.shape, full) for a in consts],
        out_specs=[pl.BlockSpec((None, nc, KV_WIDTH), blk)] * 2,
        out_shape=[out, out],
        compiler_params=_params(1),
        name="compress",
    )(rk, rv, cos_t, sin_t, *consts)


def _softmax_rows(s):
    m = jnp.max(s, axis=1, keepdims=True)
    e = jnp.exp2(s - m)
    return e, jnp.sum(e, axis=1, keepdims=True)


def _dot_exact01(x, m01):
    hi = x.astype(BF16)
    r1 = x - hi.astype(F32)
    mid = r1.astype(BF16)
    lo = (r1 - mid.astype(F32)).astype(BF16)
    return _dot(hi, m01) + _dot(mid, m01) + _dot(lo, m01)


def _unselected_blocks(imp, q0, n_pick):
    nq, nb = imp.shape
    col = lax.broadcasted_iota(jnp.int32, (nq, nb), 1)
    t_row = q0 + lax.broadcasted_iota(jnp.int32, (nq, nb), 0)
    cur = lax.shift_right_arithmetic(t_row, SEL_BLOCK.bit_length() - 1)
    forced = (col == 0) | (col == cur) | (col == cur - 1)
    val = jnp.where(forced, -jnp.inf, jnp.where(col > cur, -1.0, imp))
    vt = val.T
    blk = lax.broadcasted_iota(jnp.int32, (nb, nq), 0).astype(F32)
    for _ in range(n_pick - N_FORCED):
        m = jnp.max(vt, axis=0, keepdims=True)
        first = jnp.min(jnp.where(vt == m, blk, float(nb)), axis=0, keepdims=True)
        vt = jnp.where(blk == first, -jnp.inf, vt)
    return jnp.where(vt == -jnp.inf, 0.0, 1.0).T


def _nsa_kernel(*refs, n_pick, cast_scales):
    n_cast = len(cast_scales)
    q_ref, gate_ref, kc_ref, vc_ref, ks_ref, vs_ref, kw_ref, vw_ref, pt_ref, ov_ref = refs[:10]
    cast_in = refs[10:10 + n_cast]
    o_ref = refs[10 + n_cast]
    cast_out = refs[11 + n_cast:11 + 2 * n_cast]
    m_scr, l_scr, acc_scr, lhs_scr, oc_scr, ow_scr, s0_scr, s1_scr, w_scr, gx_scr = refs[11 + 2 * n_cast:]

    for src, dst, scale in zip(cast_in, cast_out, cast_scales):
        w = src[...] if scale == 1.0 else src[...] * scale
        dst[...] = w.astype(BF16)

    qb = pl.program_id(1)
    q0 = qb * Q_BLOCK
    tk = SEL_KEY_TILE
    ncmp = kc_ref.shape[0]
    wk = WINDOW + Q_BLOCK
    rows = NSA_GROUP * Q_BLOCK
    lane = lax.broadcasted_iota(jnp.int32, (Q_BLOCK, LANES), 1)

    def trow(width):
        return q0 + lax.broadcasted_iota(jnp.int32, (Q_BLOCK, width), 0)

    def kcol(width):
        return lax.broadcasted_iota(jnp.int32, (Q_BLOCK, width), 1)

    bias_c = jnp.where(kcol(ncmp) * CMP_STRIDE + (CMP_BLOCK - 1) <= trow(ncmp), 0.0, NEG)
    has_cmp = (trow(1) >= CMP_BLOCK - 1).astype(F32)
    wstart = pl.multiple_of(jnp.maximum(q0 - WINDOW, 0), Q_BLOCK)
    kpos_w = wstart + kcol(wk)
    bias_w = jnp.where(kpos_w <= trow(wk), jnp.where(kpos_w > trow(wk) - WINDOW, 0.0, NEG), NEG)

    for g in range(NSA_KV_HEADS):
        mine = (lane >= NSA_HEAD_DIM) if g else (lane < NSA_HEAD_DIM)
        lhs_scr[g, :, 0:LANES] = jnp.concatenate(
            [jnp.where(mine, q_ref[:, r * LANES:(r + 1) * LANES], jnp.zeros((), BF16))
             for r in range(NSA_GROUP)], axis=0)
        m_scr[g] = jnp.full((rows, LANES), -jnp.inf, F32)
        l_scr[g] = jnp.zeros((rows, LANES), F32)
        acc_scr[g] = jnp.zeros((rows, LANES), F32)

    def cmp_scores(g):
        s0_scr[g] = _dot_nt(lhs_scr[g, :, 0:LANES], kc_ref[...])

    def cmp_attend(g):
        e, l = _softmax_rows(s0_scr[g] + _rep_rows(bias_c, NSA_GROUP))
        p = e * (_rep_rows(has_cmp, NSA_GROUP) / l)
        oc_scr[g] = _dot(p.astype(BF16), vc_ref[...])
        psum = p[0:Q_BLOCK]
        for r in range(1, NSA_GROUP):
            psum = psum + p[r * Q_BLOCK:(r + 1) * Q_BLOCK]
        return _dot_exact01(psum, ov_ref[...])

    def select(g, imp):
        unsel = _unselected_blocks(imp, q0, n_pick)
        lhs_scr[g, :, LANES:2 * LANES] = _rep_rows(unsel.astype(BF16), NSA_GROUP)

    def win_scores(g):
        w_scr[g] = _dot_nt(lhs_scr[g, :, 0:LANES], kw_ref[pl.ds(wstart, wk), :])

    def win_attend(g):
        e, l = _softmax_rows(w_scr[g] + _rep_rows(bias_w, NSA_GROUP))
        ow_scr[g] = _dot(e.astype(BF16), vw_ref[pl.ds(wstart, wk), :]) / l

    def expand_gates():
        low = lane < NSA_HEAD_DIM
        for r in range(NSA_GROUP):
            for br in range(N_BRANCH):
                c_lo = r * N_BRANCH + br
                c_hi = (NSA_GROUP + r) * N_BRANCH + br
                gx_scr[c_lo] = jnp.where(low, gate_ref[:, c_lo:c_lo + 1], gate_ref[:, c_hi:c_hi + 1])

    cmp_scores(0)
    cmp_scores(1)
    expand_gates()
    win_scores(0)
    imp0 = cmp_attend(0)
    win_scores(1)
    imp1 = cmp_attend(1)
    select(0, imp0)
    win_attend(0)
    select(1, imp1)
    win_attend(1)

    def scores(kt, buf, g):
        k0 = pl.multiple_of(kt * tk, tk)
        rhs = jnp.concatenate([ks_ref[pl.ds(k0, tk), :], pt_ref[pl.ds(k0, tk), :]], axis=1)
        buf[g] = _dot_nt(lhs_scr[g], rhs)

    def consume(kt, buf, g, causal):
        k0 = pl.multiple_of(kt * tk, tk)
        v = vs_ref[pl.ds(k0, tk), :]
        bias = jnp.where(k0 + kcol(tk) <= trow(tk), 0.0, NEG) if causal else None
        for r in range(NSA_GROUP):
            rs = slice(r * Q_BLOCK, (r + 1) * Q_BLOCK)
            s = buf[g, rs, :]
            if causal:
                s = s + bias
            m_prev = m_scr[g, rs, :]
            m_new = jnp.maximum(m_prev, jnp.max(s, axis=1, keepdims=True))
            alpha = jnp.exp2(m_prev - m_new)
            pe = jnp.exp2(s - _rep_lanes(m_new, tk // LANES))
            part = pe[:, 0:LANES]
            for c in range(1, tk // LANES):
                part = part + pe[:, c * LANES:(c + 1) * LANES]
            l_scr[g, rs, :] = alpha * l_scr[g, rs, :] + part
            acc_scr[g, rs, :] = alpha * acc_scr[g, rs, :] + _dot(pe.astype(BF16), v)
            m_scr[g, rs, :] = m_new

    def step(nxt, nxt_buf, cur, cur_buf, causal=False):
        for g in range(NSA_KV_HEADS):
            if nxt is not None:
                scores(nxt, nxt_buf, g)
        for g in range(NSA_KV_HEADS):
            consume(cur, cur_buf, g, causal)

    n_full = q0 // tk
    for g in range(NSA_KV_HEADS):
        scores(0, s0_scr, g)

    def pair(j, carry):
        step(2 * j + 1, s1_scr, 2 * j, s0_scr)
        step(2 * j + 2, s0_scr, 2 * j + 1, s1_scr)
        return carry

    lax.fori_loop(0, n_full // 2, pair, 0)

    @pl.when(n_full % 2 == 0)
    def _():
        step(None, None, n_full, s0_scr, causal=True)

    @pl.when(n_full % 2 == 1)
    def _():
        step(n_full, s1_scr, n_full - 1, s0_scr)
        step(None, None, n_full, s1_scr, causal=True)

    low = lane < NSA_HEAD_DIM
    for r in range(NSA_GROUP):
        rs = slice(r * Q_BLOCK, (r + 1) * Q_BLOCK)
        merged = lambda scr: jnp.where(low, scr[0, rs, :], scr[1, rs, :])
        l_sel = jnp.where(low, jnp.sum(l_scr[0, rs, :], axis=1, keepdims=True),
                          jnp.sum(l_scr[1, rs, :], axis=1, keepdims=True))
        out = (gx_scr[r * N_BRANCH] * merged(oc_scr)
               + gx_scr[r * N_BRANCH + 1] * (merged(acc_scr) / l_sel)
               + gx_scr[r * N_BRANCH + 2] * merged(ow_scr))
        o_ref[:, r * LANES:(r + 1) * LANES] = out.astype(BF16)


def _nsa(q, gates, kc, vc, ks, vs, kw, vw, pt, ov, casts, batch, seq):
    n = q.shape[0]
    nqb = seq // Q_BLOCK
    steps = batch * nqb
    rows = NSA_GROUP * Q_BLOCK
    row = lambda b, i: (b * nqb + i, 0)
    per_b = lambda b, i: (b, 0, 0)
    full = lambda b, i: (0, 0)
    seq_spec = _resident((None, seq, KV_WIDTH), per_b)
    cmp_spec = _resident((None, kc.shape[1], KV_WIDTH), per_b)
    n_pick = min(SEL_COUNT, seq // SEL_BLOCK)
    sliced = [w.reshape(steps, -1, w.shape[-1]) for w, _ in casts]
    slice_specs = [pl.BlockSpec((None,) + w.shape[1:], lambda b, i: (b * nqb + i, 0, 0)) for w in sliced]
    outs = pl.pallas_call(
        functools.partial(_nsa_kernel, n_pick=n_pick, cast_scales=tuple(s for _, s in casts)),
        grid=(batch, nqb),
        in_specs=[pl.BlockSpec((Q_BLOCK, Q_WIDTH), row), pl.BlockSpec((Q_BLOCK, LANES), row),
                  cmp_spec, cmp_spec, seq_spec, seq_spec, seq_spec, seq_spec,
                  _resident(pt.shape, full), _resident(ov.shape, full)] + slice_specs,
        out_specs=[pl.BlockSpec((Q_BLOCK, Q_WIDTH), row)] + slice_specs,
        out_shape=[jax.ShapeDtypeStruct((n, Q_WIDTH), BF16)]
                  + [jax.ShapeDtypeStruct(w.shape, BF16) for w in sliced],
        scratch_shapes=[pltpu.VMEM((NSA_KV_HEADS, rows, LANES), F32),
                        pltpu.VMEM((NSA_KV_HEADS, rows, LANES), F32),
                        pltpu.VMEM((NSA_KV_HEADS, rows, LANES), F32),
                        pltpu.VMEM((NSA_KV_HEADS, rows, 2 * LANES), BF16),
                        pltpu.VMEM((NSA_KV_HEADS, rows, LANES), F32),
                        pltpu.VMEM((NSA_KV_HEADS, rows, LANES), F32),
                        pltpu.VMEM((NSA_KV_HEADS, rows, SEL_KEY_TILE), F32),
                        pltpu.VMEM((NSA_KV_HEADS, rows, SEL_KEY_TILE), F32),
                        pltpu.VMEM((NSA_KV_HEADS, rows, WINDOW + Q_BLOCK), F32),
                        pltpu.VMEM((NSA_GROUP * N_BRANCH, Q_BLOCK, LANES), F32)],
        compiler_params=_params(2),
        name="nsa",
    )(q, gates, kc, vc, ks, vs, kw, vw, pt, ov, *sliced)
    return outs[0], [o.reshape(w.shape) for o, (w, _) in zip(outs[1:], casts)]


def _matmul_kernel(a_ref, w_ref, o_ref):
    o_ref[...] = _dot(a_ref[...].astype(BF16), w_ref[...].astype(BF16)).astype(o_ref.dtype)


def _mem_kv(memf, wkv_all, layer):
    m, k = memf.shape
    n = wkv_all.shape[2]
    return pl.pallas_call(
        _matmul_kernel,
        grid=(1,),
        in_specs=[pl.BlockSpec((m, k), lambda i: (0, 0)), pl.BlockSpec((None, k, n), lambda i: (layer, 0, 0))],
        out_specs=pl.BlockSpec((m, n), lambda i: (0, 0)),
        out_shape=jax.ShapeDtypeStruct((m, n), BF16),
        compiler_params=_params(1),
        name="mem_kv",
    )(memf, wkv_all)


def _tail_kernel(*refs, n_parts):
    x_ref = refs[0]
    parts = refs[1:1 + n_parts]
    ws = refs[1 + n_parts:1 + 2 * n_parts]
    wq_ref, k_ref, v_ref, wo_ref, wu_ref, wd_ref, g_ref, b_ref, o_ref, x1_scr, x2_scr, x2_prev = refs[1 + 2 * n_parts:]
    t = pl.program_id(0)
    n_tiles = pl.num_programs(0) - 1
    tm = x_ref.shape[0]
    half = tm // 2
    halves = (slice(0, half), slice(half, tm))
    x1 = x1_scr if n_parts else x_ref

    def norm(k, z):
        return _layer_norm(z, g_ref[k:k + 1, :], b_ref[k:k + 1, :])

    def mixer_out(rs):
        z = ALPHA * x_ref[rs, :]
        for a, w in zip(parts, ws):
            z = z + _dot(a[rs, :], w[...])
        x1_scr[rs, :] = norm(0, z)

    def project(rs):
        return _dot(x1[rs, :].astype(BF16), wq_ref[...]).astype(BF16)

    heads = [slice(h * X_HEAD_DIM, (h + 1) * X_HEAD_DIM) for h in range(X_HEADS)]

    def scores(q):
        return [_dot_nt(q[:, sl], k_ref[:, sl]) for sl in heads]

    def probabilities(ss):
        probs = []
        for s in ss:
            e, l = _softmax_rows(s)
            probs.append((e / l).astype(BF16))
        return probs

    def mix(probs):
        outs = [_dot(p, v_ref[:, sl]) for p, sl in zip(probs, heads)]
        return jnp.concatenate(outs, axis=1).astype(BF16)

    def finish(rs, o):
        x2_scr[rs, :] = norm(1, ALPHA * x1[rs, :] + _dot(o, wo_ref[...]))

    def attention_stages():
        st = {}
        stages = []
        both = range(len(halves))
        if n_parts:
            stages += [functools.partial(mixer_out, h) for h in halves]
        stages += [lambda i=i: st.__setitem__(("q", i), project(halves[i])) for i in both]
        stages += [lambda i=i: st.__setitem__(("s", i), scores(st["q", i])) for i in both]
        stages += [lambda i=i: st.__setitem__(("p", i), probabilities(st["s", i])) for i in both]
        stages += [lambda i=i: st.__setitem__(("o", i), mix(st["p", i])) for i in both]
        stages += [lambda i=i: finish(halves[i], st["o", i]) for i in both]
        return stages

    def ffn_stages(src, out):
        st = {}

        def start():
            x2 = src[...]
            st.update(xb=x2.astype(BF16), z=ALPHA * x2)

        def chunk(c0, width):
            h1 = _dot(st["xb"], wu_ref[:, c0:c0 + width])
            h2 = _dot(st["xb"], wu_ref[:, FFN_HIDDEN + c0:FFN_HIDDEN + c0 + width])
            act = (h1 * _sigmoid_tanh(h1) * h2).astype(BF16)
            st["z"] = st["z"] + _dot(act, wd_ref[c0:c0 + width, :])

        stages = [start]
        c0 = 0
        for width in FFN_CHUNKS:
            stages.append(functools.partial(chunk, c0, width))
            c0 += width
        stages.append(lambda: out.__setitem__(Ellipsis, norm(2, st["z"])))
        return stages

    @pl.when(t == 0)
    def _():
        for stage in attention_stages():
            stage()
        x2_prev[...] = x2_scr[...]

    @pl.when((t > 0) & (t < n_tiles))
    def _():
        _emit_round_robin(ffn_stages(x2_prev, o_ref), attention_stages())
        x2_prev[...] = x2_scr[...]

    @pl.when(t == n_tiles)
    def _():
        for stage in ffn_stages(x2_prev, o_ref):
            stage()


def _tail(xf, parts, ws, kv, wq_all, wo_all, wu_all, wd_all, g_all, b_all, layer, batch, seq):
    n = xf.shape[0]
    tm = ROW_TILE
    nt = seq // tm
    n_tiles = n // tm
    n_mem = kv.shape[1]
    cur = lambda t: jnp.minimum(t, n_tiles - 1)
    row = lambda t: (cur(t), 0)
    lagged = lambda t: (jnp.maximum(t - 1, 0), 0)
    full = lambda t: (0, 0)
    this_layer = lambda t: (layer, 0, 0)
    stacked = lambda a: _resident((None,) + a.shape[1:], this_layer)
    return pl.pallas_call(
        functools.partial(_tail_kernel, n_parts=len(parts)),
        grid=(n_tiles + 1,),
        in_specs=[pl.BlockSpec((tm, D_MODEL), row)]
                 + [pl.BlockSpec((tm, a.shape[1]), row) for a in parts]
                 + [_resident(w.shape, full) for w in ws]
                 + [stacked(wq_all),
                    _resident((None, n_mem, D_MODEL), lambda t: (cur(t) // nt, 0, 0)),
                    _resident((None, n_mem, D_MODEL), lambda t: (cur(t) // nt, 0, 1)),
                    stacked(wo_all), stacked(wu_all), stacked(wd_all), stacked(g_all), stacked(b_all)],
        out_specs=pl.BlockSpec((tm, D_MODEL), lagged),
        out_shape=jax.ShapeDtypeStruct((n, D_MODEL), F32),
        scratch_shapes=[pltpu.VMEM((tm, D_MODEL), F32), pltpu.VMEM((tm, D_MODEL), F32),
                        pltpu.VMEM((tm, D_MODEL), F32)],
        compiler_params=_params(1),
        name="tail",
    )(xf, *parts, *ws, wq_all, kv, kv, wo_all, wu_all, wd_all, g_all, b_all)


def _gelu_tanh(x):
    inner = x * (GELU_C0 + GELU_C1 * (x * x))
    return x * (0.5 * jnp.tanh(inner) + 0.5)


def _sigmoid_tanh(z):
    return 0.5 * jnp.tanh(0.5 * z) + 0.5


def _odd_kernel(x_ref, pos_ref, perm_ref, unperm_ref, win_ref, cw_ref, cb_ref, wax_ref, ba_ref, bx_ref, lam_ref,
                wout_ref, g_ref, b_ref, o_ref, xbuf, gate_scr, a_scr, b_scr, tail_scr, h_scr):
    i = pl.program_id(1)
    tm = x_ref.shape[0]
    half = tm // 2
    nv = half // SUBLANES
    halo = (CONV_WIDTH - 1) * SUBLANES

    @pl.when(i == 0)
    def _():
        tail_scr[...] = jnp.zeros((SUBLANES, RNN_WIDTH), F32)
        h_scr[...] = jnp.zeros((SUBLANES, RNN_WIDTH), F32)

    def permute(v2d):
        return jnp.swapaxes(v2d.reshape(SUBLANES, nv, v2d.shape[1]), 0, 1).reshape(v2d.shape)

    sub = lax.broadcasted_iota(jnp.int32, (SUBLANES, RNN_WIDTH), 0)
    lam = lam_ref[...]
    half_scale = (-0.5 * LRU_C) * (jnp.maximum(-lam, 0.0) + jnp.log1p(jnp.exp(-jnp.abs(lam))))

    col_chunk = 2 * LANES

    def in_proj(hi, h0):
        st = {}

        def permute_rows():
            st["xb"] = _dot(perm_ref[...], x_ref[h0:h0 + half, :].astype(BF16)).astype(BF16)

        def gate_cols(c):
            gate_scr[h0:h0 + half, c:c + col_chunk] = _dot(st["xb"], win_ref[:, c:c + col_chunk])

        def input_cols(c):
            xbuf[hi, halo:halo + half, c:c + col_chunk] = _dot(
                st["xb"], win_ref[:, RNN_WIDTH + c:RNN_WIDTH + c + col_chunk])

        cols = range(0, RNN_WIDTH, col_chunk)
        return ([permute_rows] + [functools.partial(gate_cols, c) for c in cols]
                + [functools.partial(input_cols, c) for c in cols])

    def gates(hi, h0):
        st = {}

        def prepare():
            for j in range(1, CONV_WIDTH):
                last = xbuf[hi, halo + (nv - j) * SUBLANES:halo + (nv - j + 1) * SUBLANES, :]
                before = jnp.where(sub == 0, tail_scr[j:j + 1, :], pltpu.roll(last, 1, 0))
                xbuf[hi, halo - j * SUBLANES:halo - (j - 1) * SUBLANES, :] = before
                tail_scr[j:j + 1, :] = last[SUBLANES - 1:SUBLANES, :]
            is_reset = (pos_ref[h0:h0 + half, :] == 0).astype(F32)
            st["reset"] = permute(jnp.broadcast_to(is_reset, (half, RNN_BLOCK_W))) > 0.5

        def block(h):
            reset = st["reset"]
            sl = slice(h * RNN_BLOCK_W, (h + 1) * RNN_BLOCK_W)
            xc = cb_ref[:, sl]
            for k in range(CONV_WIDTH):
                xc = xc + xbuf[hi, k * SUBLANES:k * SUBLANES + half, sl] * cw_ref[k:k + 1, sl]
            ri = _dot(xc.astype(BF16), wax_ref[h])
            t_r = jnp.tanh(ri[:, :RNN_BLOCK_W] + ba_ref[:, sl])
            t_i = jnp.tanh(ri[:, RNN_BLOCK_W:] + bx_ref[:, sl])
            log_a = t_r * half_scale[:, sl] + half_scale[:, sl]
            a = jnp.where(reset, 0.0, jnp.exp(log_a))
            th = jnp.tanh(log_a)
            q2 = -0.5 * th / (1.0 - th)
            half_mult = jnp.where(reset, 0.5, jnp.where(q2 > 0.0, q2 * lax.rsqrt(q2), 0.0))
            b = half_mult * (t_i * xc + xc)
            a_scr[h0:h0 + half, sl] = a
            b_scr[h0:h0 + half, sl] = b

        return [prepare] + [functools.partial(block, h) for h in range(RNN_BLOCKS)]

    def recur(h0, h_prev):
        decay = jnp.ones((SUBLANES, RNN_WIDTH), F32)
        resp = jnp.zeros((SUBLANES, RNN_WIDTH), F32)
        for v in range(nv):
            rs = slice(h0 + v * SUBLANES, h0 + (v + 1) * SUBLANES)
            av = a_scr[rs, :]
            decay = av * decay
            resp = av * resp + b_scr[rs, :]
            a_scr[rs, :] = decay
            b_scr[rs, :] = resp
        ca, cb_ = decay, resp
        for d in (1, 2, 4):
            ok = sub >= d
            cb_ = jnp.where(ok, ca * pltpu.roll(cb_, d, 0) + cb_, cb_)
            ca = jnp.where(ok, ca * pltpu.roll(ca, d, 0), ca)
        h_end = ca * h_prev + cb_
        h_in = jnp.where(sub == 0, h_prev, pltpu.roll(h_end, 1, 0))
        rs = slice(h0, h0 + half)
        hh = (a_scr[rs, :].reshape(nv, SUBLANES, RNN_WIDTH) * h_in[None]
              + b_scr[rs, :].reshape(nv, SUBLANES, RNN_WIDTH))
        b_scr[rs, :] = hh.reshape(half, RNN_WIDTH)
        return jnp.broadcast_to(h_end[SUBLANES - 1:SUBLANES, :], (SUBLANES, RNN_WIDTH))

    def out_proj(h0):
        rs = slice(h0, h0 + half)
        st = {}

        def gate_output():
            st["yp"] = (b_scr[rs, :] * _gelu_tanh(gate_scr[rs, :])).astype(BF16)

        def unpermute_rows():
            st["y"] = _dot(unperm_ref[...], st["yp"]).astype(BF16)

        def project():
            z = ALPHA * x_ref[rs, :] + _dot(st["y"], wout_ref[...])
            o_ref[rs, :] = _layer_norm(z, g_ref[...], b_ref[...])

        return [gate_output, unpermute_rows, project]

    _emit_round_robin(in_proj(0, 0))
    _emit_round_robin(gates(0, 0), in_proj(1, half))
    h_mid = recur(0, h_scr[...])
    _emit_round_robin(gates(1, half), out_proj(0))
    h_scr[...] = recur(half, h_mid)
    _emit_round_robin(out_proj(half))


def _odd_layer(xf, posi, w_in, cw, cb, wax, ba, bx, lam, w_out, g, b, batch, seq):
    n = xf.shape[0]
    tm = ROW_TILE
    nt = seq // tm
    row = lambda bi, i: (bi * nt + i, 0)
    full = lambda bi, i: (0, 0)
    half = tm // 2
    rho = np.arange(half)
    src_time = (rho % SUBLANES) * (half // SUBLANES) + rho // SUBLANES
    perm = np.zeros((half, half), np.float32)
    perm[rho, src_time] = 1.0
    consts = [jnp.asarray(perm, BF16), jnp.asarray(perm.T, BF16), w_in, cw, cb]
    rest = [ba, bx, lam, w_out, g, b]
    return pl.pallas_call(
        _odd_kernel,
        grid=(batch, nt),
        in_specs=[pl.BlockSpec((tm, D_MODEL), row), pl.BlockSpec((tm, 1), row)]
                 + [pl.BlockSpec(a.shape, full) for a in consts]
                 + [pl.BlockSpec(wax.shape, lambda bi, i: (0, 0, 0))]
                 + [pl.BlockSpec(a.shape, full) for a in rest],
        out_specs=pl.BlockSpec((tm, D_MODEL), row),
        out_shape=jax.ShapeDtypeStruct((n, D_MODEL), F32),
        scratch_shapes=[pltpu.VMEM((2, (CONV_WIDTH - 1) * SUBLANES + tm // 2, RNN_WIDTH), F32),
                        pltpu.VMEM((tm, RNN_WIDTH), F32),
                        pltpu.VMEM((tm, RNN_WIDTH), F32),
                        pltpu.VMEM((tm, RNN_WIDTH), F32),
                        pltpu.VMEM((SUBLANES, RNN_WIDTH), F32),
                        pltpu.VMEM((SUBLANES, RNN_WIDTH), F32)],
        compiler_params=_params(2),
        name="odd_layer",
    )(xf, posi, *consts, wax, *rest)


def _even_mixer(xf, positions, w_in, pool_w, pool_scale, cmp_pos_k, cmp_pos_v, cmp_wk, cmp_wv,
                w_out, casts, batch, seq):
    assert seq // SEL_BLOCK == LANES and seq % SEL_KEY_TILE == 0 and seq >= WINDOW + Q_BLOCK
    hd, g, r = NSA_HEAD_DIM, NSA_KV_HEADS, NSA_GROUP
    c0 = POOL_WIDTH
    c1 = c0 + Q_WIDTH
    c2 = c1 + GATE_WIDTH
    wp = w_in[:, :c0].astype(BF16)
    wq = (w_in[:, c0:c1].reshape(D_MODEL, g, r, hd).transpose(0, 2, 1, 3).reshape(D_MODEL, Q_WIDTH)
          * (hd ** -0.5 * LOG2E)).astype(BF16)
    wg = jnp.pad(w_in[:, c1:c2], ((0, 0), (0, LANES - GATE_WIDTH))).astype(BF16)
    wkv = w_in[:, c2:].astype(BF16)
    n = batch * seq
    nc = seq // CMP_STRIDE
    n_cmp = (seq - CMP_BLOCK) // CMP_STRIDE + 1
    posc = jnp.pad(positions[:, CMP_BLOCK - 1::CMP_STRIDE][:, :n_cmp], ((0, 0), (0, nc - n_cmp)))
    pos_all = jnp.concatenate([positions.reshape(1, n), posc.reshape(1, batch * nc)], axis=1).astype(F32)
    cos_t, sin_t = _rope_angles(pos_all)
    ec, es = _rope_spread()
    pool, q, kc_raw, vc_raw, ks, vs, kw, vw, gates = _proj_even(
        xf, cos_t, sin_t, ec, es, wp, wq, wkv, wg, pool_w.astype(BF16), pool_scale[None, :], batch, seq)

    def halves(w):
        w3 = w.reshape(CMP_BLOCK, hd, hd).astype(BF16)
        zero = jnp.zeros_like(w3)
        w4 = jnp.concatenate([jnp.concatenate([w3 if k == j else zero for k in range(g)], axis=2)
                              for j in range(g)], axis=1)
        return (w4[:CMP_STRIDE].reshape(CMP_STRIDE * g * hd, g * hd),
                w4[CMP_STRIDE:].reshape(CMP_STRIDE * g * hd, g * hd))

    def pos_halves(p):
        p2 = jnp.tile(p[:, None, :], (1, g, 1)).reshape(CMP_BLOCK, g * hd)
        return p2[:CMP_STRIDE].reshape(1, -1), p2[CMP_STRIDE:].reshape(1, -1)

    pk1, pk2 = pos_halves(cmp_pos_k)
    pv1, pv2 = pos_halves(cmp_pos_v)
    wk1, wk2 = halves(cmp_wk)
    wv1, wv2 = halves(cmp_wv)
    seq3 = lambda a: a.reshape(batch, seq, KV_WIDTH)
    kc, vc = _compress(seq3(kc_raw), seq3(vc_raw), cos_t, sin_t, n // nc,
                       (ec, es, pk1, pk2, pv1, pv2, wk1, wk2, wv1, wv2))

    n_sb = seq // SEL_BLOCK
    starts = np.arange(nc) * CMP_STRIDE
    jb = np.arange(n_sb)
    ov = ((starts[:, None] < (jb[None, :] + 1) * SEL_BLOCK)
          & (starts[:, None] + CMP_BLOCK > jb[None, :] * SEL_BLOCK) & (np.arange(nc)[:, None] < n_cmp))
    ov = jnp.asarray(ov, BF16)
    pt = jnp.asarray(np.where(np.arange(seq)[:, None] // SEL_BLOCK == jb[None, :], NEG, 0.0), BF16)

    nsa, cast_weights = _nsa(q, gates, kc, vc, seq3(ks), seq3(vs), seq3(kw), seq3(vw), pt, ov, casts, batch, seq)

    w_pool_out = w_out[:POOL_WIDTH].astype(BF16)
    w_nsa_out = (w_out[POOL_WIDTH:].reshape(g, r, hd, D_MODEL).transpose(1, 0, 2, 3)
                 .reshape(Q_WIDTH, D_MODEL).astype(BF16))
    return [pool, nsa], [w_pool_out, w_nsa_out], cast_weights


def _odd_mixer(xf, posi, w_in, conv_w, conv_b, wa, ba, wx, bx, lam, w_out, ln_g, ln_b, batch, seq):
    wax = (0.5 * jnp.concatenate([wa, wx], axis=2)).astype(BF16)
    return _odd_layer(xf, posi, w_in, conv_w, conv_b[None, :], wax, 0.5 * ba[None, :],
                      0.5 * bx[None, :], lam[None, :], w_out, ln_g, ln_b, batch, seq)


def kernel(x, mem, positions, e_w_in, e_pool_w, e_pool_scale, e_cmp_pos_k, e_cmp_pos_v, e_cmp_wk, e_cmp_wv, e_w_out, o_w_in, o_conv_w, o_conv_b, o_wa, o_ba, o_wx, o_bx, o_lambda, o_w_out, x_wq, x_wkv, x_wo, f_w_up, f_w_down, ln_g, ln_b):
    batch, seq, d = x.shape
    n = batch * seq
    assert d == D_MODEL and seq % ROW_TILE == 0
    xf = x.reshape(n, d)
    posi = positions.reshape(n, 1)
    memf = mem.reshape(batch * mem.shape[1], d)
    casts = [(x_wq, X_HEAD_DIM ** -0.5 * LOG2E), (x_wo, 1.0), (f_w_up, 1.0), (f_w_down, 1.0),
             (o_w_in, 1.0), (o_w_out, 1.0)]
    for layer in range(DEPTH):
        j = layer // 2
        if layer % 2 == 0:
            parts, ws, cast_weights = _even_mixer(
                xf, positions, e_w_in[j], e_pool_w[j], e_pool_scale[j], e_cmp_pos_k[j], e_cmp_pos_v[j],
                e_cmp_wk[j], e_cmp_wv[j], e_w_out[j], casts if layer == 0 else [], batch, seq)
            if layer == 0:
                wq_all, wo_all, wu_all, wd_all, o_w_in16, o_w_out16 = cast_weights
        else:
            xf = _odd_mixer(xf, posi, o_w_in16[j], o_conv_w[j], o_conv_b[j], o_wa[j], o_ba[j], o_wx[j], o_bx[j],
                            o_lambda[j], o_w_out16[j], ln_g[layer, 0][None, :], ln_b[layer, 0][None, :],
                            batch, seq)
            parts, ws = [], []
        kv = _mem_kv(memf, x_wkv, layer).reshape(batch, mem.shape[1], 2 * d)
        xf = _tail(xf, parts, ws, kv, wq_all, wo_all, wu_all, wd_all, ln_g, ln_b, layer, batch, seq)
    return xf.reshape(batch, seq, d)
```

```python
import functools

import numpy as np
import jax
import jax.numpy as jnp
from jax import lax
from jax.experimental import pallas as pl
from jax.experimental.pallas import tpu as pltpu

F32 = jnp.float32
BF16 = jnp.bfloat16

D_MODEL = 1024
DEPTH = 2
ALPHA = (2.0 * DEPTH) ** 0.25
LN_EPS = 1e-5
NEG = -1e30
POOL_WIDTH = D_MODEL // 2
POOL_WINDOWS = (2, 4, 8, 16)
POOL_GROUP = POOL_WIDTH // len(POOL_WINDOWS)
POOL_HALO = 16
NSA_HEADS = 8
NSA_KV_HEADS = 2
NSA_HEAD_DIM = 64
NSA_GROUP = NSA_HEADS // NSA_KV_HEADS
CMP_BLOCK = 32
CMP_STRIDE = 16
SEL_BLOCK = 64
SEL_COUNT = 16
WINDOW = 512
Q_BLOCK = 256
N_BRANCH = 3
N_FORCED = 3
LOG2E = 1.4426950408889634
ROPE_THETA = 500000.0
ROT_DIM = NSA_HEAD_DIM // 4
ROT_HALF = ROT_DIM // 2
Q_WIDTH = NSA_HEADS * NSA_HEAD_DIM
KV_WIDTH = NSA_KV_HEADS * NSA_HEAD_DIM
GATE_WIDTH = NSA_HEADS * N_BRANCH
RNN_WIDTH = 1280
RNN_BLOCKS = 10
RNN_BLOCK_W = RNN_WIDTH // RNN_BLOCKS
CONV_WIDTH = 4
LRU_C = 8.0
X_HEADS = 4
X_HEAD_DIM = D_MODEL // X_HEADS
FFN_HIDDEN = 2816
GELU_C0 = 0.7978845608028654
GELU_C1 = GELU_C0 * 0.044715

LANES = 128
SUBLANES = 8
VMEM_LIMIT = 56 * 1024 * 1024
ROW_TILE = 512
SEL_KEY_TILE = 512
FFN_CHUNKS = (256,) * 11


def _params(n_axes, vmem=VMEM_LIMIT):
    return pltpu.CompilerParams(dimension_semantics=("arbitrary",) * n_axes,
                                vmem_limit_bytes=vmem)


def _resident(shape, index_map):
    return pl.BlockSpec(shape, index_map, pipeline_mode=pl.Buffered(1))


def _emit_round_robin(*stage_lists):
    lists = [list(s) for s in stage_lists]
    while any(lists):
        for s in lists:
            if s:
                s.pop(0)()


def _dot(a, b):
    return jnp.dot(a, b, preferred_element_type=F32)


def _dot_nt(a, b):
    return lax.dot_general(a, b, (((1,), (1,)), ((), ())), preferred_element_type=F32)


def _rep_rows(a, k):
    return jnp.concatenate([a] * k, axis=0)


def _rep_lanes(a, k):
    return jnp.concatenate([a] * k, axis=1) if k > 1 else a


def _layer_norm(z, g, b):
    mu = jnp.mean(z, axis=-1, keepdims=True)
    d = z - mu
    var = jnp.mean(d * d, axis=-1, keepdims=True)
    return d * lax.rsqrt(var + LN_EPS) * g + b


def _rope(v, cos, sin):
    k = v.shape[1] // LANES
    up = pltpu.roll(v, v.shape[1] - ROT_HALF, 1)
    dn = pltpu.roll(v, ROT_HALF, 1)
    lane = lax.broadcasted_iota(jnp.int32, v.shape, 1)
    partner = jnp.where((lane & (NSA_HEAD_DIM - 1)) < ROT_HALF, up, dn)
    return v * _rep_lanes(cos, k) + partner * _rep_lanes(sin, k)


def _rope_angle_kernel(pos_ref, inv_ref, cos_o, sin_o):
    ang = inv_ref[...] * pos_ref[...]
    cos_o[...] = jnp.cos(ang)
    sin_o[...] = jnp.sin(ang)


def _rope_angles(pos_row):
    n = pos_row.shape[1]
    inv = (ROPE_THETA ** (-jnp.arange(ROT_HALF, dtype=F32) * 2.0 / ROT_DIM))[:, None]
    out = jax.ShapeDtypeStruct((ROT_HALF, n), F32)
    return pl.pallas_call(
        _rope_angle_kernel,
        out_shape=[out, out],
        name="rope_angles",
    )(pos_row, inv)


def _rope_spread():
    lane = np.arange(LANES) % NSA_HEAD_DIM
    f = np.arange(ROT_HALF)[:, None]
    lo = (lane[None, :] == f).astype(np.float32)
    hi = (lane[None, :] == f + ROT_HALF).astype(np.float32)
    return jnp.asarray(lo + hi, BF16), jnp.asarray(hi - lo, BF16)


def _rope_patterns(cos_t, sin_t, ec, es):
    def spread(t, e):
        hi = t.astype(BF16)
        r1 = t - hi.astype(F32)
        mid = r1.astype(BF16)
        lo = (r1 - mid.astype(F32)).astype(BF16)
        tn = (((0,), (0,)), ((), ()))
        return sum(lax.dot_general(p, e, tn, preferred_element_type=F32) for p in (hi, mid, lo))

    lane = lax.broadcasted_iota(jnp.int32, (1, LANES), 1)
    unrotated = jnp.where((lane & (NSA_HEAD_DIM - 1)) >= ROT_DIM, 1.0, 0.0)
    return spread(cos_t, ec) + unrotated, spread(sin_t, es)


def _proj_even_kernel(x_ref, cos_ref, sin_ref, ec_ref, es_ref, wp_ref, wq_ref, wkv_ref, wg_ref, poolw_ref, pscale_ref,
                      pool_o, q_o, kc_o, vc_o, ks_o, vs_o, kw_o, vw_o, gate_o, pbuf):
    i = pl.program_id(1)
    tm = x_ref.shape[0]
    half = tm // 2

    @pl.when(i == 0)
    def _():
        pbuf[0:POOL_HALO, :] = jnp.zeros((POOL_HALO, POOL_WIDTH), F32)

    def project(h0, st):
        rs = slice(h0, h0 + half)

        def cast():
            st["xb"] = x_ref[rs, :].astype(BF16)

        def pool_in():
            pbuf[POOL_HALO + h0:POOL_HALO + h0 + half, :] = _dot(st["xb"], wp_ref[...])

        return [cast, pool_in,
                lambda: st.__setitem__("q", _dot(st["xb"], wq_ref[...])),
                lambda: st.__setitem__("kv", _dot(st["xb"], wkv_ref[...])),
                lambda: st.__setitem__("gl", _dot(st["xb"], wg_ref[...]))]

    def emit(h0, st):
        rs = slice(h0, h0 + half)

        def patterns():
            st["cos"], st["sin"] = _rope_patterns(cos_ref[:, rs], sin_ref[:, rs], ec_ref[...], es_ref[...])

        def queries():
            q_o[rs, :] = _rope(st["q"], st["cos"], st["sin"]).astype(BF16)

        def keys_values():
            kv, cos, sin = st["kv"], st["cos"], st["sin"]
            kc_o[rs, :] = kv[:, 0 * LANES:1 * LANES]
            vc_o[rs, :] = kv[:, 1 * LANES:2 * LANES]
            ks_o[rs, :] = _rope(kv[:, 2 * LANES:3 * LANES], cos, sin).astype(BF16)
            vs_o[rs, :] = kv[:, 3 * LANES:4 * LANES].astype(BF16)
            kw_o[rs, :] = _rope(kv[:, 4 * LANES:5 * LANES], cos, sin).astype(BF16)
            vw_o[rs, :] = kv[:, 5 * LANES:6 * LANES].astype(BF16)
            gate_o[rs, :] = _sigmoid_tanh(st["gl"])

        return [patterns, queries, keys_values]

    def pool(h0):
        rs = slice(h0, h0 + half)
        t1 = i * tm + h0 + lax.broadcasted_iota(jnp.int32, (half, 1), 0) + 1

        def group(g, w):
            sl = slice(g * POOL_GROUP, (g + 1) * POOL_GROUP)
            u = pbuf[POOL_HALO + h0:POOL_HALO + h0 + half, sl]
            tot = u
            for j in range(1, w):
                tot = tot + pbuf[pl.ds(POOL_HALO + h0 - j, half), sl]
            cnt = jnp.minimum(t1, w).astype(F32)
            pooled = tot / cnt - u
            mixed = _dot(pooled.astype(BF16), poolw_ref[g]) * pscale_ref[:, sl]
            pool_o[rs, sl] = mixed.astype(BF16)

        return [functools.partial(group, g, w) for g, w in enumerate(POOL_WINDOWS)]

    first, second = {}, {}
    _emit_round_robin(project(0, first))
    _emit_round_robin(emit(0, first) + pool(0), project(half, second))
    _emit_round_robin(emit(half, second) + pool(half))
    pbuf[0:POOL_HALO, :] = pbuf[tm:tm + POOL_HALO, :]


def _proj_even(xf, cos_t, sin_t, ec, es, wp, wq, wkv, wg, poolw, pscale, batch, seq):
    n = xf.shape[0]
    tm = ROW_TILE
    nt = seq // tm
    row = lambda b, i: (b * nt + i, 0)
    full = lambda b, i: (0, 0)
    outs = [
        jax.ShapeDtypeStruct((n, POOL_WIDTH), BF16),
        jax.ShapeDtypeStruct((n, Q_WIDTH), BF16),
        jax.ShapeDtypeStruct((n, KV_WIDTH), F32),
        jax.ShapeDtypeStruct((n, KV_WIDTH), F32),
        jax.ShapeDtypeStruct((n, KV_WIDTH), BF16),
        jax.ShapeDtypeStruct((n, KV_WIDTH), BF16),
        jax.ShapeDtypeStruct((n, KV_WIDTH), BF16),
        jax.ShapeDtypeStruct((n, KV_WIDTH), BF16),
        jax.ShapeDtypeStruct((n, LANES), F32),
    ]
    return pl.pallas_call(
        _proj_even_kernel,
        grid=(batch, nt),
        in_specs=[
            pl.BlockSpec((tm, D_MODEL), row),
            pl.BlockSpec((ROT_HALF, tm), lambda b, i: (0, b * nt + i)),
            pl.BlockSpec((ROT_HALF, tm), lambda b, i: (0, b * nt + i)),
            pl.BlockSpec(ec.shape, full),
            pl.BlockSpec(es.shape, full),
            pl.BlockSpec(wp.shape, full),
            pl.BlockSpec(wq.shape, full),
            pl.BlockSpec(wkv.shape, full),
            pl.BlockSpec(wg.shape, full),
            pl.BlockSpec(poolw.shape, lambda b, i: (0, 0, 0)),
            pl.BlockSpec(pscale.shape, full),
        ],
        out_specs=[pl.BlockSpec((tm, o.shape[1]), row) for o in outs],
        out_shape=outs,
        scratch_shapes=[pltpu.VMEM((POOL_HALO + tm, POOL_WIDTH), F32)],
        compiler_params=_params(2),
        name="proj_even",
    )(xf, cos_t, sin_t, ec, es, wp, wq, wkv, wg, poolw, pscale)


def _compress_kernel(rk_ref, rv_ref, cos_ref, sin_ref, ec_ref, es_ref, pk1_ref, pk2_ref, pv1_ref, pv2_ref,
                     wk1_ref, wk2_ref, wv1_ref, wv2_ref, kc_o, vc_o):
    nc = kc_o.shape[0]

    def compress(raw_ref, p1, p2, w1, w2):
        r = jnp.concatenate([raw_ref[pl.ds(j, nc, stride=CMP_STRIDE), :] for j in range(CMP_STRIDE)], axis=1)
        a = _dot((r + p1[...]).astype(BF16), w1[...])
        b = _dot((r + p2[...]).astype(BF16), w2[...])
        return a + pltpu.roll(b, nc - 1, 0)

    kc = compress(rk_ref, pk1_ref, pk2_ref, wk1_ref, wk2_ref)
    cos, sin = _rope_patterns(cos_ref[...], sin_ref[...], ec_ref[...], es_ref[...])
    kc_o[...] = _rope(kc, cos, sin).astype(BF16)
    vc_o[...] = compress(rv_ref, pv1_ref, pv2_ref, wv1_ref, wv2_ref).astype(BF16)


def _compress(rk, rv, cos_t, sin_t, first_col_block, consts):
    batch, seq, width = rk.shape
    nc = seq // CMP_STRIDE
    blk = lambda b: (b, 0, 0)
    full = lambda b: (0, 0)
    out = jax.ShapeDtypeStruct((batch, nc, KV_WIDTH), BF16)
    return pl.pallas_call(
        _compress_kernel,
        grid=(batch,),
        in_specs=[pl.BlockSpec((None, seq, width), blk), pl.BlockSpec((None, seq, width), blk),
                  pl.BlockSpec((ROT_HALF, nc), lambda b: (0, first_col_block + b)),
                  pl.BlockSpec((ROT_HALF, nc), lambda b: (0, first_col_block + b))]
                 + [pl.BlockSpec(a.shape, full) for a in consts],
        out_specs=[pl.BlockSpec((None, nc, KV_WIDTH), blk)] * 2,
        out_shape=[out, out],
        compiler_params=_params(1),
        name="compress",
    )(rk, rv, cos_t, sin_t, *consts)


def _softmax_rows(s):
    m = jnp.max(s, axis=1, keepdims=True)
    e = jnp.exp2(s - m)
    return e, jnp.sum(e, axis=1, keepdims=True)


def _dot_exact01(x, m01):
    hi = x.astype(BF16)
    r1 = x - hi.astype(F32)
    mid = r1.astype(BF16)
    lo = (r1 - mid.astype(F32)).astype(BF16)
    return _dot(hi, m01) + _dot(mid, m01) + _dot(lo, m01)


def _unselected_blocks(imp, q0, n_pick):
    nq, nb = imp.shape
    col = lax.broadcasted_iota(jnp.int32, (nq, nb), 1)
    t_row = q0 + lax.broadcasted_iota(jnp.int32, (nq, nb), 0)
    cur = lax.shift_right_arithmetic(t_row, SEL_BLOCK.bit_length() - 1)
    forced = (col == 0) | (col == cur) | (col == cur - 1)
    val = jnp.where(forced, -jnp.inf, jnp.where(col > cur, -1.0, imp))
    vt = val.T
    blk = lax.broadcasted_iota(jnp.int32, (nb, nq), 0).astype(F32)
    for _ in range(n_pick - N_FORCED):
        m = jnp.max(vt, axis=0, keepdims=True)
        first = jnp.min(jnp.where(vt == m, blk, float(nb)), axis=0, keepdims=True)
        vt = jnp.where(blk == first, -jnp.inf, vt)
    return jnp.where(vt == -jnp.inf, 0.0, 1.0).T


def _nsa_kernel(*refs, n_pick, cast_scales):
    n_cast = len(cast_scales)
    q_ref, gate_ref, kc_ref, vc_ref, ks_ref, vs_ref, kw_ref, vw_ref, pt_ref, ov_ref = refs[:10]
    cast_in = refs[10:10 + n_cast]
    o_ref = refs[10 + n_cast]
    cast_out = refs[11 + n_cast:11 + 2 * n_cast]
    m_scr, l_scr, acc_scr, lhs_scr, oc_scr, ow_scr, s0_scr, s1_scr, w_scr, gx_scr = refs[11 + 2 * n_cast:]

    qb = pl.program_id(1)
    q0 = qb * Q_BLOCK
    tk = SEL_KEY_TILE
    ncmp = kc_ref.shape[0]
    wk = WINDOW + Q_BLOCK
    rows = NSA_GROUP * Q_BLOCK
    lane = lax.broadcasted_iota(jnp.int32, (Q_BLOCK, LANES), 1)

    def trow(width):
        return q0 + lax.broadcasted_iota(jnp.int32, (Q_BLOCK, width), 0)

    def kcol(width):
        return lax.broadcasted_iota(jnp.int32, (Q_BLOCK, width), 1)

    bias_c = jnp.where(kcol(ncmp) * CMP_STRIDE + (CMP_BLOCK - 1) <= trow(ncmp), 0.0, NEG)
    has_cmp = (trow(1) >= CMP_BLOCK - 1).astype(F32)
    wstart = pl.multiple_of(jnp.maximum(q0 - WINDOW, 0), Q_BLOCK)
    kpos_w = wstart + kcol(wk)
    bias_w = jnp.where(kpos_w <= trow(wk), jnp.where(kpos_w > trow(wk) - WINDOW, 0.0, NEG), NEG)

    for g in range(NSA_KV_HEADS):
        mine = (lane >= NSA_HEAD_DIM) if g else (lane < NSA_HEAD_DIM)
        lhs_scr[g, :, 0:LANES] = jnp.concatenate(
            [jnp.where(mine, q_ref[:, r * LANES:(r + 1) * LANES], jnp.zeros((), BF16))
             for r in range(NSA_GROUP)], axis=0)
        m_scr[g] = jnp.full((rows, LANES), -jnp.inf, F32)
        l_scr[g] = jnp.zeros((rows, LANES), F32)
        acc_scr[g] = jnp.zeros((rows, LANES), F32)

    def cmp_scores(g):
        s0_scr[g] = _dot_nt(lhs_scr[g, :, 0:LANES], kc_ref[...])

    def cmp_attend(g):
        e, l = _softmax_rows(s0_scr[g] + _rep_rows(bias_c, NSA_GROUP))
        p = e * (_rep_rows(has_cmp, NSA_GROUP) / l)
        oc_scr[g] = _dot(p.astype(BF16), vc_ref[...])
        psum = p[0:Q_BLOCK]
        for r in range(1, NSA_GROUP):
            psum = psum + p[r * Q_BLOCK:(r + 1) * Q_BLOCK]
        return _dot_exact01(psum, ov_ref[...])

    def select(g, imp):
        unsel = _unselected_blocks(imp, q0, n_pick)
        lhs_scr[g, :, LANES:2 * LANES] = _rep_rows(unsel.astype(BF16), NSA_GROUP)

    def win_scores(g):
        w_scr[g] = _dot_nt(lhs_scr[g, :, 0:LANES], kw_ref[pl.ds(wstart, wk), :])

    def win_attend(g):
        e, l = _softmax_rows(w_scr[g] + _rep_rows(bias_w, NSA_GROUP))
        ow_scr[g] = _dot(e.astype(BF16), vw_ref[pl.ds(wstart, wk), :]) / l

    def expand_gates():
        low = lane < NSA_HEAD_DIM
        for r in range(NSA_GROUP):
            for br in range(N_BRANCH):
                c_lo = r * N_BRANCH + br
                c_hi = (NSA_GROUP + r) * N_BRANCH + br
                gx_scr[c_lo] = jnp.where(low, gate_ref[:, c_lo:c_lo + 1], gate_ref[:, c_hi:c_hi + 1])

    cmp_scores(0)
    cmp_scores(1)
    expand_gates()
    win_scores(0)
    imp0 = cmp_attend(0)
    win_scores(1)
    imp1 = cmp_attend(1)
    select(0, imp0)
    win_attend(0)
    select(1, imp1)
    win_attend(1)

    def scores(kt, buf, g):
        k0 = pl.multiple_of(kt * tk, tk)
        rhs = jnp.concatenate([ks_ref[pl.ds(k0, tk), :], pt_ref[pl.ds(k0, tk), :]], axis=1)
        buf[g] = _dot_nt(lhs_scr[g], rhs)

    def consume(kt, buf, g, causal):
        k0 = pl.multiple_of(kt * tk, tk)
        v = vs_ref[pl.ds(k0, tk), :]
        bias = jnp.where(k0 + kcol(tk) <= trow(tk), 0.0, NEG) if causal else None
        for r in range(NSA_GROUP):
            rs = slice(r * Q_BLOCK, (r + 1) * Q_BLOCK)
            s = buf[g, rs, :]
            if causal:
                s = s + bias
            m_prev = m_scr[g, rs, :]
            m_new = jnp.maximum(m_prev, jnp.max(s, axis=1, keepdims=True))
            alpha = jnp.exp2(m_prev - m_new)
            pe = jnp.exp2(s - _rep_lanes(m_new, tk // LANES))
            part = pe[:, 0:LANES]
            for c in range(1, tk // LANES):
                part = part + pe[:, c * LANES:(c + 1) * LANES]
            l_scr[g, rs, :] = alpha * l_scr[g, rs, :] + part
            acc_scr[g, rs, :] = alpha * acc_scr[g, rs, :] + _dot(pe.astype(BF16), v)
            m_scr[g, rs, :] = m_new

    def step(nxt, nxt_buf, cur, cur_buf, causal=False):
        for g in range(NSA_KV_HEADS):
            if nxt is not None:
                scores(nxt, nxt_buf, g)
        for g in range(NSA_KV_HEADS):
            consume(cur, cur_buf, g, causal)

    n_full = q0 // tk
    for g in range(NSA_KV_HEADS):
        scores(0, s0_scr, g)

    def pair(j, carry):
        step(2 * j + 1, s1_scr, 2 * j, s0_scr)
        step(2 * j + 2, s0_scr, 2 * j + 1, s1_scr)
        return carry

    lax.fori_loop(0, n_full // 2, pair, 0)

    @pl.when(n_full % 2 == 0)
    def _():
        step(None, None, n_full, s0_scr, causal=True)

    @pl.when(n_full % 2 == 1)
    def _():
        step(n_full, s1_scr, n_full - 1, s0_scr)
        step(None, None, n_full, s1_scr, causal=True)

    low = lane < NSA_HEAD_DIM
    for r in range(NSA_GROUP):
        rs = slice(r * Q_BLOCK, (r + 1) * Q_BLOCK)
        merged = lambda scr: jnp.where(low, scr[0, rs, :], scr[1, rs, :])
        l_sel = jnp.where(low, jnp.sum(l_scr[0, rs, :], axis=1, keepdims=True),
                          jnp.sum(l_scr[1, rs, :], axis=1, keepdims=True))
        out = (gx_scr[r * N_BRANCH] * merged(oc_scr)
               + gx_scr[r * N_BRANCH + 1] * (merged(acc_scr) / l_sel)
               + gx_scr[r * N_BRANCH + 2] * merged(ow_scr))
        o_ref[:, r * LANES:(r + 1) * LANES] = out.astype(BF16)

    for src, dst, scale in zip(cast_in, cast_out, cast_scales):
        w = src[...] if scale == 1.0 else src[...] * scale
        dst[...] = w.astype(BF16)


def _nsa(q, gates, kc, vc, ks, vs, kw, vw, pt, ov, casts, batch, seq):
    n = q.shape[0]
    nqb = seq // Q_BLOCK
    steps = batch * nqb
    rows = NSA_GROUP * Q_BLOCK
    row = lambda b, i: (b * nqb + i, 0)
    per_b = lambda b, i: (b, 0, 0)
    full = lambda b, i: (0, 0)
    seq_spec = _resident((None, seq, KV_WIDTH), per_b)
    cmp_spec = _resident((None, kc.shape[1], KV_WIDTH), per_b)
    n_pick = min(SEL_COUNT, seq // SEL_BLOCK)
    sliced = [w.reshape(steps, -1, w.shape[-1]) for w, _ in casts]
    slice_specs = [pl.BlockSpec((None,) + w.shape[1:], lambda b, i: (b * nqb + i, 0, 0)) for w in sliced]
    outs = pl.pallas_call(
        functools.partial(_nsa_kernel, n_pick=n_pick, cast_scales=tuple(s for _, s in casts)),
        grid=(batch, nqb),
        in_specs=[pl.BlockSpec((Q_BLOCK, Q_WIDTH), row), pl.BlockSpec((Q_BLOCK, LANES), row),
                  cmp_spec, cmp_spec, seq_spec, seq_spec, seq_spec, seq_spec,
                  _resident(pt.shape, full), _resident(ov.shape, full)] + slice_specs,
        out_specs=[pl.BlockSpec((Q_BLOCK, Q_WIDTH), row)] + slice_specs,
        out_shape=[jax.ShapeDtypeStruct((n, Q_WIDTH), BF16)]
                  + [jax.ShapeDtypeStruct(w.shape, BF16) for w in sliced],
        scratch_shapes=[pltpu.VMEM((NSA_KV_HEADS, rows, LANES), F32),
                        pltpu.VMEM((NSA_KV_HEADS, rows, LANES), F32),
                        pltpu.VMEM((NSA_KV_HEADS, rows, LANES), F32),
                        pltpu.VMEM((NSA_KV_HEADS, rows, 2 * LANES), BF16),
                        pltpu.VMEM((NSA_KV_HEADS, rows, LANES), F32),
                        pltpu.VMEM((NSA_KV_HEADS, rows, LANES), F32),
                        pltpu.VMEM((NSA_KV_HEADS, rows, SEL_KEY_TILE), F32),
                        pltpu.VMEM((NSA_KV_HEADS, rows, SEL_KEY_TILE), F32),
                        pltpu.VMEM((NSA_KV_HEADS, rows, WINDOW + Q_BLOCK), F32),
                        pltpu.VMEM((NSA_GROUP * N_BRANCH, Q_BLOCK, LANES), F32)],
        compiler_params=_params(2),
        name="nsa",
    )(q, gates, kc, vc, ks, vs, kw, vw, pt, ov, *sliced)
    return outs[0], [o.reshape(w.shape) for o, (w, _) in zip(outs[1:], casts)]


def _matmul_kernel(a_ref, w_ref, o_ref):
    o_ref[...] = _dot(a_ref[...].astype(BF16), w_ref[...].astype(BF16)).astype(o_ref.dtype)


def _mem_kv(memf, wkv_all):
    m, k = memf.shape
    layers, _, n = wkv_all.shape
    return pl.pallas_call(
        _matmul_kernel,
        grid=(layers,),
        in_specs=[pl.BlockSpec((m, k), lambda l: (0, 0)), pl.BlockSpec((None, k, n), lambda l: (l, 0, 0))],
        out_specs=pl.BlockSpec((None, m, n), lambda l: (l, 0, 0)),
        out_shape=jax.ShapeDtypeStruct((layers, m, n), BF16),
        compiler_params=_params(1),
        name="mem_kv",
    )(memf, wkv_all)


def _tail_kernel(*refs, n_parts):
    x_ref = refs[0]
    parts = refs[1:1 + n_parts]
    ws = refs[1 + n_parts:1 + 2 * n_parts]
    wq_ref, k_ref, v_ref, wo_ref, wu_ref, wd_ref, g_ref, b_ref, o_ref, x1_scr, x2_scr, x2_prev = refs[1 + 2 * n_parts:]
    t = pl.program_id(0)
    n_tiles = pl.num_programs(0) - 1
    tm = x_ref.shape[0]
    half = tm // 2
    halves = (slice(0, half), slice(half, tm))
    x1 = x1_scr if n_parts else x_ref

    def norm(k, z):
        return _layer_norm(z, g_ref[k:k + 1, :], b_ref[k:k + 1, :])

    def mixer_out(rs):
        z = ALPHA * x_ref[rs, :]
        for a, w in zip(parts, ws):
            z = z + _dot(a[rs, :], w[...])
        x1_scr[rs, :] = norm(0, z)

    def project(rs):
        return _dot(x1[rs, :].astype(BF16), wq_ref[...]).astype(BF16)

    heads = [slice(h * X_HEAD_DIM, (h + 1) * X_HEAD_DIM) for h in range(X_HEADS)]

    def scores(q):
        return [_dot_nt(q[:, sl], k_ref[:, sl]) for sl in heads]

    def probabilities(ss):
        probs = []
        for s in ss:
            e, l = _softmax_rows(s)
            probs.append((e / l).astype(BF16))
        return probs

    def mix(probs):
        outs = [_dot(p, v_ref[:, sl]) for p, sl in zip(probs, heads)]
        return jnp.concatenate(outs, axis=1).astype(BF16)

    def finish(rs, o):
        x2_scr[rs, :] = norm(1, ALPHA * x1[rs, :] + _dot(o, wo_ref[...]))

    def attention_stages():
        st = {}
        stages = []
        both = range(len(halves))
        if n_parts:
            stages += [functools.partial(mixer_out, h) for h in halves]
        stages += [lambda i=i: st.__setitem__(("q", i), project(halves[i])) for i in both]
        stages += [lambda i=i: st.__setitem__(("s", i), scores(st["q", i])) for i in both]
        stages += [lambda i=i: st.__setitem__(("p", i), probabilities(st["s", i])) for i in both]
        stages += [lambda i=i: st.__setitem__(("o", i), mix(st["p", i])) for i in both]
        stages += [lambda i=i: finish(halves[i], st["o", i]) for i in both]
        return stages

    def ffn_stages(src, out):
        st = {}

        def start():
            x2 = src[...]
            st.update(xb=x2.astype(BF16), z=ALPHA * x2)

        def chunk(c0, width):
            h1 = _dot(st["xb"], wu_ref[:, c0:c0 + width])
            h2 = _dot(st["xb"], wu_ref[:, FFN_HIDDEN + c0:FFN_HIDDEN + c0 + width])
            act = (h1 * _sigmoid_tanh(h1) * h2).astype(BF16)
            st["z"] = st["z"] + _dot(act, wd_ref[c0:c0 + width, :])

        stages = [start]
        c0 = 0
        for width in FFN_CHUNKS:
            stages.append(functools.partial(chunk, c0, width))
            c0 += width
        stages.append(lambda: out.__setitem__(Ellipsis, norm(2, st["z"])))
        return stages

    @pl.when(t == 0)
    def _():
        for stage in attention_stages():
            stage()
        x2_prev[...] = x2_scr[...]

    @pl.when((t > 0) & (t < n_tiles))
    def _():
        _emit_round_robin(ffn_stages(x2_prev, o_ref), attention_stages())
        x2_prev[...] = x2_scr[...]

    @pl.when(t == n_tiles)
    def _():
        for stage in ffn_stages(x2_prev, o_ref):
            stage()


def _tail(xf, parts, ws, kv, wq_all, wo_all, wu_all, wd_all, g_all, b_all, layer, batch, seq):
    n = xf.shape[0]
    tm = ROW_TILE
    nt = seq // tm
    n_tiles = n // tm
    n_mem = kv.shape[2]
    cur = lambda t: jnp.minimum(t, n_tiles - 1)
    row = lambda t: (cur(t), 0)
    lagged = lambda t: (jnp.maximum(t - 1, 0), 0)
    full = lambda t: (0, 0)
    this_layer = lambda t: (layer, 0, 0)
    stacked = lambda a: _resident((None,) + a.shape[1:], this_layer)
    return pl.pallas_call(
        functools.partial(_tail_kernel, n_parts=len(parts)),
        grid=(n_tiles + 1,),
        in_specs=[pl.BlockSpec((tm, D_MODEL), row)]
                 + [pl.BlockSpec((tm, a.shape[1]), row) for a in parts]
                 + [_resident(w.shape, full) for w in ws]
                 + [stacked(wq_all),
                    _resident((None, None, n_mem, D_MODEL), lambda t: (layer, cur(t) // nt, 0, 0)),
                    _resident((None, None, n_mem, D_MODEL), lambda t: (layer, cur(t) // nt, 0, 1)),
                    stacked(wo_all), stacked(wu_all), stacked(wd_all), stacked(g_all), stacked(b_all)],
        out_specs=pl.BlockSpec((tm, D_MODEL), lagged),
        out_shape=jax.ShapeDtypeStruct((n, D_MODEL), F32),
        scratch_shapes=[pltpu.VMEM((tm, D_MODEL), F32), pltpu.VMEM((tm, D_MODEL), F32),
                        pltpu.VMEM((tm, D_MODEL), F32)],
        compiler_params=_params(1),
        name="tail",
    )(xf, *parts, *ws, wq_all, kv, kv, wo_all, wu_all, wd_all, g_all, b_all)


def _gelu_tanh(x):
    inner = x * (GELU_C0 + GELU_C1 * (x * x))
    return x * (0.5 * jnp.tanh(inner) + 0.5)


def _sigmoid_tanh(z):
    return 0.5 * jnp.tanh(0.5 * z) + 0.5


def _odd_kernel(x_ref, pos_ref, perm_ref, unperm_ref, win_ref, cw_ref, cb_ref, wax_ref, ba_ref, bx_ref, lam_ref,
                wout_ref, g_ref, b_ref, o_ref, xbuf, gate_scr, a_scr, b_scr, tail_scr, h_scr):
    i = pl.program_id(1)
    tm = x_ref.shape[0]
    half = tm // 2
    nv = half // SUBLANES
    halo = (CONV_WIDTH - 1) * SUBLANES

    @pl.when(i == 0)
    def _():
        tail_scr[...] = jnp.zeros((SUBLANES, RNN_WIDTH), F32)
        h_scr[...] = jnp.zeros((SUBLANES, RNN_WIDTH), F32)

    def permute(v2d):
        return jnp.swapaxes(v2d.reshape(SUBLANES, nv, v2d.shape[1]), 0, 1).reshape(v2d.shape)

    sub = lax.broadcasted_iota(jnp.int32, (SUBLANES, RNN_WIDTH), 0)
    lam = lam_ref[...]
    half_scale = (-0.5 * LRU_C) * (jnp.maximum(-lam, 0.0) + jnp.log1p(jnp.exp(-jnp.abs(lam))))

    col_chunk = 2 * LANES

    def in_proj(hi, h0):
        st = {}

        def permute_rows():
            st["xb"] = _dot(perm_ref[...], x_ref[h0:h0 + half, :].astype(BF16)).astype(BF16)

        def gate_cols(c):
            gate_scr[h0:h0 + half, c:c + col_chunk] = _dot(st["xb"], win_ref[:, c:c + col_chunk])

        def input_cols(c):
            xbuf[hi, halo:halo + half, c:c + col_chunk] = _dot(
                st["xb"], win_ref[:, RNN_WIDTH + c:RNN_WIDTH + c + col_chunk])

        cols = range(0, RNN_WIDTH, col_chunk)
        return ([permute_rows] + [functools.partial(gate_cols, c) for c in cols]
                + [functools.partial(input_cols, c) for c in cols])

    def gates(hi, h0):
        st = {}

        def prepare():
            for j in range(1, CONV_WIDTH):
                last = xbuf[hi, halo + (nv - j) * SUBLANES:halo + (nv - j + 1) * SUBLANES, :]
                before = jnp.where(sub == 0, tail_scr[j:j + 1, :], pltpu.roll(last, 1, 0))
                xbuf[hi, halo - j * SUBLANES:halo - (j - 1) * SUBLANES, :] = before
                tail_scr[j:j + 1, :] = last[SUBLANES - 1:SUBLANES, :]
            is_reset = (pos_ref[h0:h0 + half, :] == 0).astype(F32)
            st["reset"] = permute(jnp.broadcast_to(is_reset, (half, RNN_BLOCK_W))) > 0.5

        def block(h):
            reset = st["reset"]
            sl = slice(h * RNN_BLOCK_W, (h + 1) * RNN_BLOCK_W)
            xc = cb_ref[:, sl]
            for k in range(CONV_WIDTH):
                xc = xc + xbuf[hi, k * SUBLANES:k * SUBLANES + half, sl] * cw_ref[k:k + 1, sl]
            ri = _dot(xc.astype(BF16), wax_ref[h])
            t_r = jnp.tanh(ri[:, :RNN_BLOCK_W] + ba_ref[:, sl])
            t_i = jnp.tanh(ri[:, RNN_BLOCK_W:] + bx_ref[:, sl])
            log_a = t_r * half_scale[:, sl] + half_scale[:, sl]
            a = jnp.where(reset, 0.0, jnp.exp(log_a))
            th = jnp.tanh(log_a)
            q2 = -0.5 * th / (1.0 - th)
            half_mult = jnp.where(reset, 0.5, jnp.where(q2 > 0.0, q2 * lax.rsqrt(q2), 0.0))
            b = half_mult * (t_i * xc + xc)
            a_scr[h0:h0 + half, sl] = a
            b_scr[h0:h0 + half, sl] = b

        return [prepare] + [functools.partial(block, h) for h in range(RNN_BLOCKS)]

    def recur(h0, h_prev):
        decay = jnp.ones((SUBLANES, RNN_WIDTH), F32)
        resp = jnp.zeros((SUBLANES, RNN_WIDTH), F32)
        for v in range(nv):
            rs = slice(h0 + v * SUBLANES, h0 + (v + 1) * SUBLANES)
            av = a_scr[rs, :]
            decay = av * decay
            resp = av * resp + b_scr[rs, :]
            a_scr[rs, :] = decay
            b_scr[rs, :] = resp
        ca, cb_ = decay, resp
        for d in (1, 2, 4):
            ok = sub >= d
            cb_ = jnp.where(ok, ca * pltpu.roll(cb_, d, 0) + cb_, cb_)
            ca = jnp.where(ok, ca * pltpu.roll(ca, d, 0), ca)
        h_end = ca * h_prev + cb_
        h_in = jnp.where(sub == 0, h_prev, pltpu.roll(h_end, 1, 0))
        rs = slice(h0, h0 + half)
        hh = (a_scr[rs, :].reshape(nv, SUBLANES, RNN_WIDTH) * h_in[None]
              + b_scr[rs, :].reshape(nv, SUBLANES, RNN_WIDTH))
        b_scr[rs, :] = hh.reshape(half, RNN_WIDTH)
        return jnp.broadcast_to(h_end[SUBLANES - 1:SUBLANES, :], (SUBLANES, RNN_WIDTH))

    def out_proj(h0):
        rs = slice(h0, h0 + half)
        st = {}

        def gate_output():
            st["yp"] = (b_scr[rs, :] * _gelu_tanh(gate_scr[rs, :])).astype(BF16)

        def unpermute_rows():
            st["y"] = _dot(unperm_ref[...], st["yp"]).astype(BF16)

        def project():
            z = ALPHA * x_ref[rs, :] + _dot(st["y"], wout_ref[...])
            o_ref[rs, :] = _layer_norm(z, g_ref[...], b_ref[...])

        return [gate_output, unpermute_rows, project]

    _emit_round_robin(in_proj(0, 0))
    _emit_round_robin(gates(0, 0), in_proj(1, half))
    h_mid = recur(0, h_scr[...])
    _emit_round_robin(gates(1, half), out_proj(0))
    h_scr[...] = recur(half, h_mid)
    _emit_round_robin(out_proj(half))


def _odd_layer(xf, posi, w_in, cw, cb, wax, ba, bx, lam, w_out, g, b, batch, seq):
    n = xf.shape[0]
    tm = ROW_TILE
    nt = seq // tm
    row = lambda bi, i: (bi * nt + i, 0)
    full = lambda bi, i: (0, 0)
    half = tm // 2
    rho = np.arange(half)
    src_time = (rho % SUBLANES) * (half // SUBLANES) + rho // SUBLANES
    perm = np.zeros((half, half), np.float32)
    perm[rho, src_time] = 1.0
    consts = [jnp.asarray(perm, BF16), jnp.asarray(perm.T, BF16), w_in, cw, cb]
    rest = [ba, bx, lam, w_out, g, b]
    return pl.pallas_call(
        _odd_kernel,
        grid=(batch, nt),
        in_specs=[pl.BlockSpec((tm, D_MODEL), row), pl.BlockSpec((tm, 1), row)]
                 + [pl.BlockSpec(a.shape, full) for a in consts]
                 + [pl.BlockSpec(wax.shape, lambda bi, i: (0, 0, 0))]
                 + [pl.BlockSpec(a.shape, full) for a in rest],
        out_specs=pl.BlockSpec((tm, D_MODEL), row),
        out_shape=jax.ShapeDtypeStruct((n, D_MODEL), F32),
        scratch_shapes=[pltpu.VMEM((2, (CONV_WIDTH - 1) * SUBLANES + tm // 2, RNN_WIDTH), F32),
                        pltpu.VMEM((tm, RNN_WIDTH), F32),
                        pltpu.VMEM((tm, RNN_WIDTH), F32),
                        pltpu.VMEM((tm, RNN_WIDTH), F32),
                        pltpu.VMEM((SUBLANES, RNN_WIDTH), F32),
                        pltpu.VMEM((SUBLANES, RNN_WIDTH), F32)],
        compiler_params=_params(2),
        name="odd_layer",
    )(xf, posi, *consts, wax, *rest)


def _even_mixer(xf, positions, w_in, pool_w, pool_scale, cmp_pos_k, cmp_pos_v, cmp_wk, cmp_wv,
                w_out, casts, batch, seq):
    assert seq // SEL_BLOCK == LANES and seq % SEL_KEY_TILE == 0 and seq >= WINDOW + Q_BLOCK
    hd, g, r = NSA_HEAD_DIM, NSA_KV_HEADS, NSA_GROUP
    c0 = POOL_WIDTH
    c1 = c0 + Q_WIDTH
    c2 = c1 + GATE_WIDTH
    wp = w_in[:, :c0].astype(BF16)
    wq = (w_in[:, c0:c1].reshape(D_MODEL, g, r, hd).transpose(0, 2, 1, 3).reshape(D_MODEL, Q_WIDTH)
          * (hd ** -0.5 * LOG2E)).astype(BF16)
    wg = jnp.pad(w_in[:, c1:c2], ((0, 0), (0, LANES - GATE_WIDTH))).astype(BF16)
    wkv = w_in[:, c2:].astype(BF16)
    n = batch * seq
    nc = seq // CMP_STRIDE
    n_cmp = (seq - CMP_BLOCK) // CMP_STRIDE + 1
    posc = jnp.pad(positions[:, CMP_BLOCK - 1::CMP_STRIDE][:, :n_cmp], ((0, 0), (0, nc - n_cmp)))
    pos_all = jnp.concatenate([positions.reshape(1, n), posc.reshape(1, batch * nc)], axis=1).astype(F32)
    cos_t, sin_t = _rope_angles(pos_all)
    ec, es = _rope_spread()
    pool, q, kc_raw, vc_raw, ks, vs, kw, vw, gates = _proj_even(
        xf, cos_t, sin_t, ec, es, wp, wq, wkv, wg, pool_w.astype(BF16), pool_scale[None, :], batch, seq)

    def halves(w):
        w3 = w.reshape(CMP_BLOCK, hd, hd).astype(BF16)
        zero = jnp.zeros_like(w3)
        w4 = jnp.concatenate([jnp.concatenate([w3 if k == j else zero for k in range(g)], axis=2)
                              for j in range(g)], axis=1)
        return (w4[:CMP_STRIDE].reshape(CMP_STRIDE * g * hd, g * hd),
                w4[CMP_STRIDE:].reshape(CMP_STRIDE * g * hd, g * hd))

    def pos_halves(p):
        p2 = jnp.tile(p[:, None, :], (1, g, 1)).reshape(CMP_BLOCK, g * hd)
        return p2[:CMP_STRIDE].reshape(1, -1), p2[CMP_STRIDE:].reshape(1, -1)

    pk1, pk2 = pos_halves(cmp_pos_k)
    pv1, pv2 = pos_halves(cmp_pos_v)
    wk1, wk2 = halves(cmp_wk)
    wv1, wv2 = halves(cmp_wv)
    seq3 = lambda a: a.reshape(batch, seq, KV_WIDTH)
    kc, vc = _compress(seq3(kc_raw), seq3(vc_raw), cos_t, sin_t, n // nc,
                       (ec, es, pk1, pk2, pv1, pv2, wk1, wk2, wv1, wv2))

    n_sb = seq // SEL_BLOCK
    starts = np.arange(nc) * CMP_STRIDE
    jb = np.arange(n_sb)
    ov = ((starts[:, None] < (jb[None, :] + 1) * SEL_BLOCK)
          & (starts[:, None] + CMP_BLOCK > jb[None, :] * SEL_BLOCK) & (np.arange(nc)[:, None] < n_cmp))
    ov = jnp.asarray(ov, BF16)
    pt = jnp.asarray(np.where(np.arange(seq)[:, None] // SEL_BLOCK == jb[None, :], NEG, 0.0), BF16)

    nsa, cast_weights = _nsa(q, gates, kc, vc, seq3(ks), seq3(vs), seq3(kw), seq3(vw), pt, ov, casts, batch, seq)

    w_pool_out = w_out[:POOL_WIDTH].astype(BF16)
    w_nsa_out = (w_out[POOL_WIDTH:].reshape(g, r, hd, D_MODEL).transpose(1, 0, 2, 3)
                 .reshape(Q_WIDTH, D_MODEL).astype(BF16))
    return [pool, nsa], [w_pool_out, w_nsa_out], cast_weights


def _odd_mixer(xf, posi, w_in, conv_w, conv_b, wa, ba, wx, bx, lam, w_out, ln_g, ln_b, batch, seq):
    wax = (0.5 * jnp.concatenate([wa, wx], axis=2)).astype(BF16)
    return _odd_layer(xf, posi, w_in, conv_w, conv_b[None, :], wax, 0.5 * ba[None, :],
                      0.5 * bx[None, :], lam[None, :], w_out, ln_g, ln_b, batch, seq)


def kernel(x, mem, positions, e_w_in, e_pool_w, e_pool_scale, e_cmp_pos_k, e_cmp_pos_v, e_cmp_wk, e_cmp_wv, e_w_out, o_w_in, o_conv_w, o_conv_b, o_wa, o_ba, o_wx, o_bx, o_lambda, o_w_out, x_wq, x_wkv, x_wo, f_w_up, f_w_down, ln_g, ln_b):
    batch, seq, d = x.shape
    n = batch * seq
    assert d == D_MODEL and seq % ROW_TILE == 0
    xf = x.reshape(n, d)
    posi = positions.reshape(n, 1)
    memf = mem.reshape(batch * mem.shape[1], d)
    kv = _mem_kv(memf, x_wkv).reshape(DEPTH, batch, mem.shape[1], 2 * d)
    casts = [(x_wq, X_HEAD_DIM ** -0.5 * LOG2E), (x_wo, 1.0), (f_w_up, 1.0), (f_w_down, 1.0),
             (o_w_in, 1.0), (o_w_out, 1.0)]
    for layer in range(DEPTH):
        j = layer // 2
        if layer % 2 == 0:
            parts, ws, cast_weights = _even_mixer(
                xf, positions, e_w_in[j], e_pool_w[j], e_pool_scale[j], e_cmp_pos_k[j], e_cmp_pos_v[j],
                e_cmp_wk[j], e_cmp_wv[j], e_w_out[j], casts if layer == 0 else [], batch, seq)
            if layer == 0:
                wq_all, wo_all, wu_all, wd_all, o_w_in16, o_w_out16 = cast_weights
        else:
            xf = _odd_mixer(xf, posi, o_w_in16[j], o_conv_w[j], o_conv_b[j], o_wa[j], o_ba[j], o_wx[j], o_bx[j],
                            o_lambda[j], o_w_out16[j], ln_g[layer, 0][None, :], ln_b[layer, 0][None, :],
                            batch, seq)
            parts, ws = [], []
        xf = _tail(xf, parts, ws, kv, wq_all, wo_all, wu_all, wd_all, ln_g, ln_b, layer, batch, seq)
    return xf.reshape(batch, seq, d)
```

```python
import functools

import numpy as np
import jax
import jax.numpy as jnp
from jax import lax
from jax.experimental import pallas as pl
from jax.experimental.pallas import tpu as pltpu

F32 = jnp.float32
BF16 = jnp.bfloat16

D_MODEL = 1024
DEPTH = 2
ALPHA = (2.0 * DEPTH) ** 0.25
LN_EPS = 1e-5
NEG = -1e30
POOL_WIDTH = D_MODEL // 2
POOL_WINDOWS = (2, 4, 8, 16)
POOL_GROUP = POOL_WIDTH // len(POOL_WINDOWS)
POOL_HALO = 16
NSA_HEADS = 8
NSA_KV_HEADS = 2
NSA_HEAD_DIM = 64
NSA_GROUP = NSA_HEADS // NSA_KV_HEADS
CMP_BLOCK = 32
CMP_STRIDE = 16
SEL_BLOCK = 64
SEL_COUNT = 16
WINDOW = 512
Q_BLOCK = 256
N_BRANCH = 3
N_FORCED = 3
LOG2E = 1.4426950408889634
ROPE_THETA = 500000.0
ROT_DIM = NSA_HEAD_DIM // 4
ROT_HALF = ROT_DIM // 2
Q_WIDTH = NSA_HEADS * NSA_HEAD_DIM
KV_WIDTH = NSA_KV_HEADS * NSA_HEAD_DIM
GATE_WIDTH = NSA_HEADS * N_BRANCH
RNN_WIDTH = 1280
RNN_BLOCKS = 10
RNN_BLOCK_W = RNN_WIDTH // RNN_BLOCKS
CONV_WIDTH = 4
LRU_C = 8.0
X_HEADS = 4
X_HEAD_DIM = D_MODEL // X_HEADS
FFN_HIDDEN = 2816
GELU_C0 = 0.7978845608028654
GELU_C1 = GELU_C0 * 0.044715

LANES = 128
SUBLANES = 8
VMEM_LIMIT = 56 * 1024 * 1024
ROW_TILE = 512
SEL_KEY_TILE = 512
FFN_CHUNKS = (256,) * 11


def _params(n_axes, vmem=VMEM_LIMIT):
    return pltpu.CompilerParams(dimension_semantics=("arbitrary",) * n_axes,
                                vmem_limit_bytes=vmem)


def _resident(shape, index_map):
    return pl.BlockSpec(shape, index_map, pipeline_mode=pl.Buffered(1))


def _emit_round_robin(*stage_lists):
    lists = [list(s) for s in stage_lists]
    while any(lists):
        for s in lists:
            if s:
                s.pop(0)()


def _dot(a, b):
    return jnp.dot(a, b, preferred_element_type=F32)


def _dot_nt(a, b):
    return lax.dot_general(a, b, (((1,), (1,)), ((), ())), preferred_element_type=F32)


def _rep_rows(a, k):
    return jnp.concatenate([a] * k, axis=0)


def _rep_lanes(a, k):
    return jnp.concatenate([a] * k, axis=1) if k > 1 else a


def _layer_norm(z, g, b):
    mu = jnp.mean(z, axis=-1, keepdims=True)
    d = z - mu
    var = jnp.mean(d * d, axis=-1, keepdims=True)
    return d * lax.rsqrt(var + LN_EPS) * g + b


def _rope(v, cos, sin):
    k = v.shape[1] // LANES
    up = pltpu.roll(v, v.shape[1] - ROT_HALF, 1)
    dn = pltpu.roll(v, ROT_HALF, 1)
    lane = lax.broadcasted_iota(jnp.int32, v.shape, 1)
    partner = jnp.where((lane & (NSA_HEAD_DIM - 1)) < ROT_HALF, up, dn)
    return v * _rep_lanes(cos, k) + partner * _rep_lanes(sin, k)


def _rope_angle_kernel(pos_ref, inv_ref, cos_o, sin_o):
    ang = inv_ref[...] * pos_ref[...]
    cos_o[...] = jnp.cos(ang)
    sin_o[...] = jnp.sin(ang)


def _rope_angles(pos_row):
    n = pos_row.shape[1]
    inv = (ROPE_THETA ** (-jnp.arange(ROT_HALF, dtype=F32) * 2.0 / ROT_DIM))[:, None]
    out = jax.ShapeDtypeStruct((ROT_HALF, n), F32)
    return pl.pallas_call(
        _rope_angle_kernel,
        out_shape=[out, out],
        name="rope_angles",
    )(pos_row, inv)


def _rope_spread():
    lane = np.arange(LANES) % NSA_HEAD_DIM
    f = np.arange(ROT_HALF)[:, None]
    lo = (lane[None, :] == f).astype(np.float32)
    hi = (lane[None, :] == f + ROT_HALF).astype(np.float32)
    return jnp.asarray(lo + hi, BF16), jnp.asarray(hi - lo, BF16)


def _rope_patterns(cos_t, sin_t, ec, es):
    def spread(t, e):
        hi = t.astype(BF16)
        r1 = t - hi.astype(F32)
        mid = r1.astype(BF16)
        lo = (r1 - mid.astype(F32)).astype(BF16)
        tn = (((0,), (0,)), ((), ()))
        return sum(lax.dot_general(p, e, tn, preferred_element_type=F32) for p in (hi, mid, lo))

    lane = lax.broadcasted_iota(jnp.int32, (1, LANES), 1)
    unrotated = jnp.where((lane & (NSA_HEAD_DIM - 1)) >= ROT_DIM, 1.0, 0.0)
    return spread(cos_t, ec) + unrotated, spread(sin_t, es)


def _proj_even_kernel(x_ref, cos_ref, sin_ref, ec_ref, es_ref, wp_ref, wq_ref, wkv_ref, wg_ref, poolw_ref, pscale_ref,
                      pool_o, q_o, kc_o, vc_o, ks_o, vs_o, kw_o, vw_o, gate_o, pbuf):
    i = pl.program_id(1)
    tm = x_ref.shape[0]
    half = tm // 2

    @pl.when(i == 0)
    def _():
        pbuf[0:POOL_HALO, :] = jnp.zeros((POOL_HALO, POOL_WIDTH), F32)

    def project(h0, st):
        rs = slice(h0, h0 + half)

        def cast():
            st["xb"] = x_ref[rs, :].astype(BF16)

        def pool_in():
            pbuf[POOL_HALO + h0:POOL_HALO + h0 + half, :] = _dot(st["xb"], wp_ref[...])

        return [cast, pool_in,
                lambda: st.__setitem__("q", _dot(st["xb"], wq_ref[...])),
                lambda: st.__setitem__("kv", _dot(st["xb"], wkv_ref[...])),
                lambda: st.__setitem__("gl", _dot(st["xb"], wg_ref[...]))]

    def emit(h0, st):
        rs = slice(h0, h0 + half)

        def patterns():
            st["cos"], st["sin"] = _rope_patterns(cos_ref[:, rs], sin_ref[:, rs], ec_ref[...], es_ref[...])

        def queries():
            q_o[rs, :] = _rope(st["q"], st["cos"], st["sin"]).astype(BF16)

        def keys_values():
            kv, cos, sin = st["kv"], st["cos"], st["sin"]
            kc_o[rs, :] = kv[:, 0 * LANES:1 * LANES]
            vc_o[rs, :] = kv[:, 1 * LANES:2 * LANES]
            ks_o[rs, :] = _rope(kv[:, 2 * LANES:3 * LANES], cos, sin).astype(BF16)
            vs_o[rs, :] = kv[:, 3 * LANES:4 * LANES].astype(BF16)
            kw_o[rs, :] = _rope(kv[:, 4 * LANES:5 * LANES], cos, sin).astype(BF16)
            vw_o[rs, :] = kv[:, 5 * LANES:6 * LANES].astype(BF16)
            gate_o[rs, :] = _sigmoid_tanh(st["gl"])

        return [patterns, queries, keys_values]

    def pool(h0):
        rs = slice(h0, h0 + half)
        t1 = i * tm + h0 + lax.broadcasted_iota(jnp.int32, (half, 1), 0) + 1

        def group(g, w):
            sl = slice(g * POOL_GROUP, (g + 1) * POOL_GROUP)
            u = pbuf[POOL_HALO + h0:POOL_HALO + h0 + half, sl]
            tot = u
            for j in range(1, w):
                tot = tot + pbuf[pl.ds(POOL_HALO + h0 - j, half), sl]
            cnt = jnp.minimum(t1, w).astype(F32)
            pooled = tot / cnt - u
            mixed = _dot(pooled.astype(BF16), poolw_ref[g]) * pscale_ref[:, sl]
            pool_o[rs, sl] = mixed.astype(BF16)

        return [functools.partial(group, g, w) for g, w in enumerate(POOL_WINDOWS)]

    first, second = {}, {}
    _emit_round_robin(project(0, first))
    _emit_round_robin(emit(0, first) + pool(0), project(half, second))
    _emit_round_robin(emit(half, second) + pool(half))
    pbuf[0:POOL_HALO, :] = pbuf[tm:tm + POOL_HALO, :]


def _proj_even(xf, cos_t, sin_t, ec, es, wp, wq, wkv, wg, poolw, pscale, batch, seq):
    n = xf.shape[0]
    tm = ROW_TILE
    nt = seq // tm
    row = lambda b, i: (b * nt + i, 0)
    full = lambda b, i: (0, 0)
    outs = [
        jax.ShapeDtypeStruct((n, POOL_WIDTH), BF16),
        jax.ShapeDtypeStruct((n, Q_WIDTH), BF16),
        jax.ShapeDtypeStruct((n, KV_WIDTH), F32),
        jax.ShapeDtypeStruct((n, KV_WIDTH), F32),
        jax.ShapeDtypeStruct((n, KV_WIDTH), BF16),
        jax.ShapeDtypeStruct((n, KV_WIDTH), BF16),
        jax.ShapeDtypeStruct((n, KV_WIDTH), BF16),
        jax.ShapeDtypeStruct((n, KV_WIDTH), BF16),
        jax.ShapeDtypeStruct((n, LANES), F32),
    ]
    return pl.pallas_call(
        _proj_even_kernel,
        grid=(batch, nt),
        in_specs=[
            pl.BlockSpec((tm, D_MODEL), row),
            pl.BlockSpec((ROT_HALF, tm), lambda b, i: (0, b * nt + i)),
            pl.BlockSpec((ROT_HALF, tm), lambda b, i: (0, b * nt + i)),
            pl.BlockSpec(ec.shape, full),
            pl.BlockSpec(es.shape, full),
            pl.BlockSpec(wp.shape, full),
            pl.BlockSpec(wq.shape, full),
            pl.BlockSpec(wkv.shape, full),
            pl.BlockSpec(wg.shape, full),
            pl.BlockSpec(poolw.shape, lambda b, i: (0, 0, 0)),
            pl.BlockSpec(pscale.shape, full),
        ],
        out_specs=[pl.BlockSpec((tm, o.shape[1]), row) for o in outs],
        out_shape=outs,
        scratch_shapes=[pltpu.VMEM((POOL_HALO + tm, POOL_WIDTH), F32)],
        compiler_params=_params(2),
        name="proj_even",
    )(xf, cos_t, sin_t, ec, es, wp, wq, wkv, wg, poolw, pscale)


def _compress_kernel(rk_ref, rv_ref, cos_ref, sin_ref, ec_ref, es_ref, pk1_ref, pk2_ref, pv1_ref, pv2_ref,
                     wk1_ref, wk2_ref, wv1_ref, wv2_ref, kc_o, vc_o):
    nc = kc_o.shape[0]

    def compress(raw_ref, p1, p2, w1, w2):
        r = jnp.concatenate([raw_ref[pl.ds(j, nc, stride=CMP_STRIDE), :] for j in range(CMP_STRIDE)], axis=1)
        a = _dot((r + p1[...]).astype(BF16), w1[...])
        b = _dot((r + p2[...]).astype(BF16), w2[...])
        return a + pltpu.roll(b, nc - 1, 0)

    kc = compress(rk_ref, pk1_ref, pk2_ref, wk1_ref, wk2_ref)
    cos, sin = _rope_patterns(cos_ref[...], sin_ref[...], ec_ref[...], es_ref[...])
    kc_o[...] = _rope(kc, cos, sin).astype(BF16)
    vc_o[...] = compress(rv_ref, pv1_ref, pv2_ref, wv1_ref, wv2_ref).astype(BF16)


def _compress(rk, rv, cos_t, sin_t, first_col_block, consts):
    batch, seq, width = rk.shape
    nc = seq // CMP_STRIDE
    blk = lambda b: (b, 0, 0)
    full = lambda b: (0, 0)
    out = jax.ShapeDtypeStruct((batch, nc, KV_WIDTH), BF16)
    return pl.pallas_call(
        _compress_kernel,
        grid=(batch,),
        in_specs=[pl.BlockSpec((None, seq, width), blk), pl.BlockSpec((None, seq, width), blk),
                  pl.BlockSpec((ROT_HALF, nc), lambda b: (0, first_col_block + b)),
                  pl.BlockSpec((ROT_HALF, nc), lambda b: (0, first_col_block + b))]
                 + [pl.BlockSpec(a.shape, full) for a in consts],
        out_specs=[pl.BlockSpec((None, nc, KV_WIDTH), blk)] * 2,
        out_shape=[out, out],
        compiler_params=_params(1),
        name="compress",
    )(rk, rv, cos_t, sin_t, *consts)


def _softmax_rows(s):
    m = jnp.max(s, axis=1, keepdims=True)
    e = jnp.exp2(s - m)
    return e, jnp.sum(e, axis=1, keepdims=True)


def _dot_exact01(x, m01):
    hi = x.astype(BF16)
    r1 = x - hi.astype(F32)
    mid = r1.astype(BF16)
    lo = (r1 - mid.astype(F32)).astype(BF16)
    return _dot(hi, m01) + _dot(mid, m01) + _dot(lo, m01)


def _unselected_blocks(imp, q0, n_pick):
    nq, nb = imp.shape
    col = lax.broadcasted_iota(jnp.int32, (nq, nb), 1)
    t_row = q0 + lax.broadcasted_iota(jnp.int32, (nq, nb), 0)
    cur = lax.shift_right_arithmetic(t_row, SEL_BLOCK.bit_length() - 1)
    forced = (col == 0) | (col == cur) | (col == cur - 1)
    val = jnp.where(forced, -jnp.inf, jnp.where(col > cur, -1.0, imp))
    vt = val.T
    blk = lax.broadcasted_iota(jnp.int32, (nb, nq), 0).astype(F32)
    for _ in range(n_pick - N_FORCED):
        m = jnp.max(vt, axis=0, keepdims=True)
        first = jnp.min(jnp.where(vt == m, blk, float(nb)), axis=0, keepdims=True)
        vt = jnp.where(blk == first, -jnp.inf, vt)
    return jnp.where(vt == -jnp.inf, 0.0, 1.0).T


def _nsa_kernel(*refs, n_pick, cast_scales):
    n_cast = len(cast_scales)
    q_ref, gate_ref, kc_ref, vc_ref, ks_ref, vs_ref, kw_ref, vw_ref, pt_ref, ov_ref = refs[:10]
    cast_in = refs[10:10 + n_cast]
    o_ref = refs[10 + n_cast]
    cast_out = refs[11 + n_cast:11 + 2 * n_cast]
    m_scr, l_scr, acc_scr, lhs_scr, oc_scr, ow_scr, s0_scr, s1_scr, w_scr, gx_scr = refs[11 + 2 * n_cast:]

    for src, dst, scale in zip(cast_in, cast_out, cast_scales):
        w = src[...] if scale == 1.0 else src[...] * scale
        dst[...] = w.astype(BF16)

    qb = pl.program_id(1)
    q0 = qb * Q_BLOCK
    tk = SEL_KEY_TILE
    ncmp = kc_ref.shape[0]
    wk = WINDOW + Q_BLOCK
    rows = NSA_GROUP * Q_BLOCK
    lane = lax.broadcasted_iota(jnp.int32, (Q_BLOCK, LANES), 1)

    def trow(width):
        return q0 + lax.broadcasted_iota(jnp.int32, (Q_BLOCK, width), 0)

    def kcol(width):
        return lax.broadcasted_iota(jnp.int32, (Q_BLOCK, width), 1)

    bias_c = jnp.where(kcol(ncmp) * CMP_STRIDE + (CMP_BLOCK - 1) <= trow(ncmp), 0.0, NEG)
    has_cmp = (trow(1) >= CMP_BLOCK - 1).astype(F32)
    wstart = pl.multiple_of(jnp.maximum(q0 - WINDOW, 0), Q_BLOCK)
    kpos_w = wstart + kcol(wk)
    bias_w = jnp.where(kpos_w <= trow(wk), jnp.where(kpos_w > trow(wk) - WINDOW, 0.0, NEG), NEG)

    for g in range(NSA_KV_HEADS):
        mine = (lane >= NSA_HEAD_DIM) if g else (lane < NSA_HEAD_DIM)
        lhs_scr[g, :, 0:LANES] = jnp.concatenate(
            [jnp.where(mine, q_ref[:, r * LANES:(r + 1) * LANES], jnp.zeros((), BF16))
             for r in range(NSA_GROUP)], axis=0)
        m_scr[g] = jnp.full((rows, LANES), -jnp.inf, F32)
        l_scr[g] = jnp.zeros((rows, LANES), F32)
        acc_scr[g] = jnp.zeros((rows, LANES), F32)

    def cmp_scores(g):
        s0_scr[g] = _dot_nt(lhs_scr[g, :, 0:LANES], kc_ref[...])

    def cmp_attend(g):
        e, l = _softmax_rows(s0_scr[g] + _rep_rows(bias_c, NSA_GROUP))
        p = e * (_rep_rows(has_cmp, NSA_GROUP) / l)
        oc_scr[g] = _dot(p.astype(BF16), vc_ref[...])
        psum = p[0:Q_BLOCK]
        for r in range(1, NSA_GROUP):
            psum = psum + p[r * Q_BLOCK:(r + 1) * Q_BLOCK]
        return _dot_exact01(psum, ov_ref[...])

    def select(g, imp):
        unsel = _unselected_blocks(imp, q0, n_pick)
        lhs_scr[g, :, LANES:2 * LANES] = _rep_rows(unsel.astype(BF16), NSA_GROUP)

    def win_scores(g):
        w_scr[g] = _dot_nt(lhs_scr[g, :, 0:LANES], kw_ref[pl.ds(wstart, wk), :])

    def win_attend(g):
        e, l = _softmax_rows(w_scr[g] + _rep_rows(bias_w, NSA_GROUP))
        ow_scr[g] = _dot(e.astype(BF16), vw_ref[pl.ds(wstart, wk), :]) / l

    def expand_gates():
        low = lane < NSA_HEAD_DIM
        for r in range(NSA_GROUP):
            for br in range(N_BRANCH):
                c_lo = r * N_BRANCH + br
                c_hi = (NSA_GROUP + r) * N_BRANCH + br
                gx_scr[c_lo] = jnp.where(low, gate_ref[:, c_lo:c_lo + 1], gate_ref[:, c_hi:c_hi + 1])

    cmp_scores(0)
    cmp_scores(1)
    expand_gates()
    win_scores(0)
    imp0 = cmp_attend(0)
    win_scores(1)
    imp1 = cmp_attend(1)
    select(0, imp0)
    win_attend(0)
    select(1, imp1)
    win_attend(1)

    def scores(kt, buf, g):
        k0 = pl.multiple_of(kt * tk, tk)
        rhs = jnp.concatenate([ks_ref[pl.ds(k0, tk), :], pt_ref[pl.ds(k0, tk), :]], axis=1)
        buf[g] = _dot_nt(lhs_scr[g], rhs)

    def consume(kt, buf, g, causal):
        k0 = pl.multiple_of(kt * tk, tk)
        v = vs_ref[pl.ds(k0, tk), :]
        bias = jnp.where(k0 + kcol(tk) <= trow(tk), 0.0, NEG) if causal else None
        for r in range(NSA_GROUP):
            rs = slice(r * Q_BLOCK, (r + 1) * Q_BLOCK)
            s = buf[g, rs, :]
            if causal:
                s = s + bias
            m_prev = m_scr[g, rs, :]
            m_new = jnp.maximum(m_prev, jnp.max(s, axis=1, keepdims=True))
            alpha = jnp.exp2(m_prev - m_new)
            pe = jnp.exp2(s - _rep_lanes(m_new, tk // LANES))
            part = pe[:, 0:LANES]
            for c in range(1, tk // LANES):
                part = part + pe[:, c * LANES:(c + 1) * LANES]
            l_scr[g, rs, :] = alpha * l_scr[g, rs, :] + part
            acc_scr[g, rs, :] = alpha * acc_scr[g, rs, :] + _dot(pe.astype(BF16), v)
            m_scr[g, rs, :] = m_new

    def step(nxt, nxt_buf, cur, cur_buf, causal=False):
        for g in range(NSA_KV_HEADS):
            if nxt is not None:
                scores(nxt, nxt_buf, g)
        for g in range(NSA_KV_HEADS):
            consume(cur, cur_buf, g, causal)

    n_full = q0 // tk
    for g in range(NSA_KV_HEADS):
        scores(0, s0_scr, g)

    def pair(j, carry):
        step(2 * j + 1, s1_scr, 2 * j, s0_scr)
        step(2 * j + 2, s0_scr, 2 * j + 1, s1_scr)
        return carry

    lax.fori_loop(0, n_full // 2, pair, 0)

    @pl.when(n_full % 2 == 0)
    def _():
        step(None, None, n_full, s0_scr, causal=True)

    @pl.when(n_full % 2 == 1)
    def _():
        step(n_full, s1_scr, n_full - 1, s0_scr)
        step(None, None, n_full, s1_scr, causal=True)

    low = lane < NSA_HEAD_DIM
    for r in range(NSA_GROUP):
        rs = slice(r * Q_BLOCK, (r + 1) * Q_BLOCK)
        merged = lambda scr: jnp.where(low, scr[0, rs, :], scr[1, rs, :])
        l_sel = jnp.where(low, jnp.sum(l_scr[0, rs, :], axis=1, keepdims=True),
                          jnp.sum(l_scr[1, rs, :], axis=1, keepdims=True))
        out = (gx_scr[r * N_BRANCH] * merged(oc_scr)
               + gx_scr[r * N_BRANCH + 1] * (merged(acc_scr) / l_sel)
               + gx_scr[r * N_BRANCH + 2] * merged(ow_scr))
        o_ref[:, r * LANES:(r + 1) * LANES] = out.astype(BF16)


def _nsa(q, gates, kc, vc, ks, vs, kw, vw, pt, ov, casts, batch, seq):
    n = q.shape[0]
    nqb = seq // Q_BLOCK
    steps = batch * nqb
    rows = NSA_GROUP * Q_BLOCK
    row = lambda b, i: (b * nqb + i, 0)
    per_b = lambda b, i: (b, 0, 0)
    full = lambda b, i: (0, 0)
    seq_spec = _resident((None, seq, KV_WIDTH), per_b)
    cmp_spec = _resident((None, kc.shape[1], KV_WIDTH), per_b)
    n_pick = min(SEL_COUNT, seq // SEL_BLOCK)
    sliced = [w.reshape(steps, -1, w.shape[-1]) for w, _ in casts]
    slice_specs = [pl.BlockSpec((None,) + w.shape[1:], lambda b, i: (b * nqb + i, 0, 0)) for w in sliced]
    outs = pl.pallas_call(
        functools.partial(_nsa_kernel, n_pick=n_pick, cast_scales=tuple(s for _, s in casts)),
        grid=(batch, nqb),
        in_specs=[pl.BlockSpec((Q_BLOCK, Q_WIDTH), row), pl.BlockSpec((Q_BLOCK, LANES), row),
                  cmp_spec, cmp_spec, seq_spec, seq_spec, seq_spec, seq_spec,
                  _resident(pt.shape, full), _resident(ov.shape, full)] + slice_specs,
        out_specs=[pl.BlockSpec((Q_BLOCK, Q_WIDTH), row)] + slice_specs,
        out_shape=[jax.ShapeDtypeStruct((n, Q_WIDTH), BF16)]
                  + [jax.ShapeDtypeStruct(w.shape, BF16) for w in sliced],
        scratch_shapes=[pltpu.VMEM((NSA_KV_HEADS, rows, LANES), F32),
                        pltpu.VMEM((NSA_KV_HEADS, rows, LANES), F32),
                        pltpu.VMEM((NSA_KV_HEADS, rows, LANES), F32),
                        pltpu.VMEM((NSA_KV_HEADS, rows, 2 * LANES), BF16),
                        pltpu.VMEM((NSA_KV_HEADS, rows, LANES), F32),
                        pltpu.VMEM((NSA_KV_HEADS, rows, LANES), F32),
                        pltpu.VMEM((NSA_KV_HEADS, rows, SEL_KEY_TILE), F32),
                        pltpu.VMEM((NSA_KV_HEADS, rows, SEL_KEY_TILE), F32),
                        pltpu.VMEM((NSA_KV_HEADS, rows, WINDOW + Q_BLOCK), F32),
                        pltpu.VMEM((NSA_GROUP * N_BRANCH, Q_BLOCK, LANES), F32)],
        compiler_params=_params(2),
        name="nsa",
    )(q, gates, kc, vc, ks, vs, kw, vw, pt, ov, *sliced)
    return outs[0], [o.reshape(w.shape) for o, (w, _) in zip(outs[1:], casts)]


def _matmul_kernel(a_ref, w_ref, o_ref):
    o_ref[...] = _dot(a_ref[...].astype(BF16), w_ref[...].astype(BF16)).astype(o_ref.dtype)


def _mem_kv(memf, wkv_all, layer):
    m, k = memf.shape
    n = wkv_all.shape[2]
    return pl.pallas_call(
        _matmul_kernel,
        grid=(1,),
        in_specs=[pl.BlockSpec((m, k), lambda i: (0, 0)), pl.BlockSpec((None, k, n), lambda i: (layer, 0, 0))],
        out_specs=pl.BlockSpec((m, n), lambda i: (0, 0)),
        out_shape=jax.ShapeDtypeStruct((m, n), BF16),
        compiler_params=_params(1),
        name="mem_kv",
    )(memf, wkv_all)


def _tail_kernel(*refs, n_parts):
    x_ref = refs[0]
    parts = refs[1:1 + n_parts]
    ws = refs[1 + n_parts:1 + 2 * n_parts]
    wq_ref, k_ref, v_ref, wo_ref, wu_ref, wd_ref, g_ref, b_ref, o_ref, x1_scr, x2_scr, x2_prev = refs[1 + 2 * n_parts:]
    t = pl.program_id(0)
    n_tiles = pl.num_programs(0) - 1
    tm = x_ref.shape[0]
    half = tm // 2
    halves = (slice(0, half), slice(half, tm))
    x1 = x1_scr if n_parts else x_ref

    def norm(k, z):
        return _layer_norm(z, g_ref[k:k + 1, :], b_ref[k:k + 1, :])

    def mixer_out(rs):
        z = ALPHA * x_ref[rs, :]
        for a, w in zip(parts, ws):
            z = z + _dot(a[rs, :], w[...])
        x1_scr[rs, :] = norm(0, z)

    def project(rs):
        return _dot(x1[rs, :].astype(BF16), wq_ref[...]).astype(BF16)

    heads = [slice(h * X_HEAD_DIM, (h + 1) * X_HEAD_DIM) for h in range(X_HEADS)]

    def scores(q):
        return [_dot_nt(q[:, sl], k_ref[:, sl]) for sl in heads]

    def probabilities(ss):
        probs = []
        for s in ss:
            e, l = _softmax_rows(s)
            probs.append((e / l).astype(BF16))
        return probs

    def mix(probs):
        outs = [_dot(p, v_ref[:, sl]) for p, sl in zip(probs, heads)]
        return jnp.concatenate(outs, axis=1).astype(BF16)

    def finish(rs, o):
        x2_scr[rs, :] = norm(1, ALPHA * x1[rs, :] + _dot(o, wo_ref[...]))

    def attention_stages():
        st = {}
        stages = []
        both = range(len(halves))
        if n_parts:
            stages += [functools.partial(mixer_out, h) for h in halves]
        stages += [lambda i=i: st.__setitem__(("q", i), project(halves[i])) for i in both]
        stages += [lambda i=i: st.__setitem__(("s", i), scores(st["q", i])) for i in both]
        stages += [lambda i=i: st.__setitem__(("p", i), probabilities(st["s", i])) for i in both]
        stages += [lambda i=i: st.__setitem__(("o", i), mix(st["p", i])) for i in both]
        stages += [lambda i=i: finish(halves[i], st["o", i]) for i in both]
        return stages

    def ffn_stages(src, out):
        st = {}

        def start():
            x2 = src[...]
            st.update(xb=x2.astype(BF16), z=ALPHA * x2)

        def chunk(c0, width):
            h1 = _dot(st["xb"], wu_ref[:, c0:c0 + width])
            h2 = _dot(st["xb"], wu_ref[:, FFN_HIDDEN + c0:FFN_HIDDEN + c0 + width])
            act = (h1 * _sigmoid_tanh(h1) * h2).astype(BF16)
            st["z"] = st["z"] + _dot(act, wd_ref[c0:c0 + width, :])

        stages = [start]
        c0 = 0
        for width in FFN_CHUNKS:
            stages.append(functools.partial(chunk, c0, width))
            c0 += width
        stages.append(lambda: out.__setitem__(Ellipsis, norm(2, st["z"])))
        return stages

    @pl.when(t == 0)
    def _():
        for stage in attention_stages():
            stage()
        x2_prev[...] = x2_scr[...]

    @pl.when((t > 0) & (t < n_tiles))
    def _():
        _emit_round_robin(ffn_stages(x2_prev, o_ref), attention_stages())
        x2_prev[...] = x2_scr[...]

    @pl.when(t == n_tiles)
    def _():
        for stage in ffn_stages(x2_prev, o_ref):
            stage()


def _tail(xf, parts, ws, kv, wq_all, wo_all, wu_all, wd_all, g_all, b_all, layer, batch, seq):
    n = xf.shape[0]
    tm = ROW_TILE
    nt = seq // tm
    n_tiles = n // tm
    n_mem = kv.shape[1]
    cur = lambda t: jnp.minimum(t, n_tiles - 1)
    row = lambda t: (cur(t), 0)
    lagged = lambda t: (jnp.maximum(t - 1, 0), 0)
    full = lambda t: (0, 0)
    this_layer = lambda t: (layer, 0, 0)
    stacked = lambda a: _resident((None,) + a.shape[1:], this_layer)
    return pl.pallas_call(
        functools.partial(_tail_kernel, n_parts=len(parts)),
        grid=(n_tiles + 1,),
        in_specs=[pl.BlockSpec((tm, D_MODEL), row)]
                 + [pl.BlockSpec((tm, a.shape[1]), row) for a in parts]
                 + [_resident(w.shape, full) for w in ws]
                 + [stacked(wq_all),
                    _resident((None, n_mem, D_MODEL), lambda t: (cur(t) // nt, 0, 0)),
                    _resident((None, n_mem, D_MODEL), lambda t: (cur(t) // nt, 0, 1)),
                    stacked(wo_all), stacked(wu_all), stacked(wd_all), stacked(g_all), stacked(b_all)],
        out_specs=pl.BlockSpec((tm, D_MODEL), lagged),
        out_shape=jax.ShapeDtypeStruct((n, D_MODEL), F32),
        scratch_shapes=[pltpu.VMEM((tm, D_MODEL), F32), pltpu.VMEM((tm, D_MODEL), F32),
                        pltpu.VMEM((tm, D_MODEL), F32)],
        compiler_params=_params(1),
        name="tail",
    )(xf, *parts, *ws, wq_all, kv, kv, wo_all, wu_all, wd_all, g_all, b_all)


def _gelu_tanh(x):
    inner = x * (GELU_C0 + GELU_C1 * (x * x))
    return x * (0.5 * jnp.tanh(inner) + 0.5)


def _sigmoid_tanh(z):
    return 0.5 * jnp.tanh(0.5 * z) + 0.5


def _odd_kernel(x_ref, pos_ref, perm_ref, unperm_ref, win_ref, cw_ref, cb_ref, wax_ref, ba_ref, bx_ref, lam_ref,
                wout_ref, g_ref, b_ref, o_ref, xbuf, gate_scr, a_scr, b_scr, tail_scr, h_scr):
    i = pl.program_id(1)
    tm = x_ref.shape[0]
    half = tm // 2
    nv = half // SUBLANES
    halo = (CONV_WIDTH - 1) * SUBLANES

    @pl.when(i == 0)
    def _():
        tail_scr[...] = jnp.zeros((SUBLANES, RNN_WIDTH), F32)
        h_scr[...] = jnp.zeros((SUBLANES, RNN_WIDTH), F32)

    def permute(v2d):
        return jnp.swapaxes(v2d.reshape(SUBLANES, nv, v2d.shape[1]), 0, 1).reshape(v2d.shape)

    sub = lax.broadcasted_iota(jnp.int32, (SUBLANES, RNN_WIDTH), 0)
    lam = lam_ref[...]
    half_scale = (-0.5 * LRU_C) * (jnp.maximum(-lam, 0.0) + jnp.log1p(jnp.exp(-jnp.abs(lam))))

    col_chunk = 2 * LANES

    def in_proj(hi, h0):
        st = {}

        def permute_rows():
            st["xb"] = _dot(perm_ref[...], x_ref[h0:h0 + half, :].astype(BF16)).astype(BF16)

        def gate_cols(c):
            gate_scr[h0:h0 + half, c:c + col_chunk] = _dot(st["xb"], win_ref[:, c:c + col_chunk])

        def input_cols(c):
            xbuf[hi, halo:halo + half, c:c + col_chunk] = _dot(
                st["xb"], win_ref[:, RNN_WIDTH + c:RNN_WIDTH + c + col_chunk])

        cols = range(0, RNN_WIDTH, col_chunk)
        return ([permute_rows] + [functools.partial(gate_cols, c) for c in cols]
                + [functools.partial(input_cols, c) for c in cols])

    def gates(hi, h0):
        st = {}

        def prepare():
            for j in range(1, CONV_WIDTH):
                last = xbuf[hi, halo + (nv - j) * SUBLANES:halo + (nv - j + 1) * SUBLANES, :]
                before = jnp.where(sub == 0, tail_scr[j:j + 1, :], pltpu.roll(last, 1, 0))
                xbuf[hi, halo - j * SUBLANES:halo - (j - 1) * SUBLANES, :] = before
                tail_scr[j:j + 1, :] = last[SUBLANES - 1:SUBLANES, :]
            is_reset = (pos_ref[h0:h0 + half, :] == 0).astype(F32)
            st["reset"] = permute(jnp.broadcast_to(is_reset, (half, RNN_BLOCK_W))) > 0.5

        def block(h):
            reset = st["reset"]
            sl = slice(h * RNN_BLOCK_W, (h + 1) * RNN_BLOCK_W)
            xc = cb_ref[:, sl]
            for k in range(CONV_WIDTH):
                xc = xc + xbuf[hi, k * SUBLANES:k * SUBLANES + half, sl] * cw_ref[k:k + 1, sl]
            ri = _dot(xc.astype(BF16), wax_ref[h])
            t_r = jnp.tanh(ri[:, :RNN_BLOCK_W] + ba_ref[:, sl])
            t_i = jnp.tanh(ri[:, RNN_BLOCK_W:] + bx_ref[:, sl])
            log_a = t_r * half_scale[:, sl] + half_scale[:, sl]
            a = jnp.where(reset, 0.0, jnp.exp(log_a))
            th = jnp.tanh(log_a)
            q2 = -0.5 * th / (1.0 - th)
            half_mult = jnp.where(reset, 0.5, jnp.where(q2 > 0.0, q2 * lax.rsqrt(q2), 0.0))
            b = half_mult * (t_i * xc + xc)
            a_scr[h0:h0 + half, sl] = a
            b_scr[h0:h0 + half, sl] = b

        return [prepare] + [functools.partial(block, h) for h in range(RNN_BLOCKS)]

    def recur(h0, h_prev):
        decay = jnp.ones((SUBLANES, RNN_WIDTH), F32)
        resp = jnp.zeros((SUBLANES, RNN_WIDTH), F32)
        for v in range(nv):
            rs = slice(h0 + v * SUBLANES, h0 + (v + 1) * SUBLANES)
            av = a_scr[rs, :]
            decay = av * decay
            resp = av * resp + b_scr[rs, :]
            a_scr[rs, :] = decay
            b_scr[rs, :] = resp
        ca, cb_ = decay, resp
        for d in (1, 2, 4):
            ok = sub >= d
            cb_ = jnp.where(ok, ca * pltpu.roll(cb_, d, 0) + cb_, cb_)
            ca = jnp.where(ok, ca * pltpu.roll(ca, d, 0), ca)
        h_end = ca * h_prev + cb_
        h_in = jnp.where(sub == 0, h_prev, pltpu.roll(h_end, 1, 0))
        rs = slice(h0, h0 + half)
        hh = (a_scr[rs, :].reshape(nv, SUBLANES, RNN_WIDTH) * h_in[None]
              + b_scr[rs, :].reshape(nv, SUBLANES, RNN_WIDTH))
        b_scr[rs, :] = hh.reshape(half, RNN_WIDTH)
        return jnp.broadcast_to(h_end[SUBLANES - 1:SUBLANES, :], (SUBLANES, RNN_WIDTH))

    def out_proj(h0):
        rs = slice(h0, h0 + half)
        st = {}

        def gate_output():
            st["yp"] = (b_scr[rs, :] * _gelu_tanh(gate_scr[rs, :])).astype(BF16)

        def unpermute_rows():
            st["y"] = _dot(unperm_ref[...], st["yp"]).astype(BF16)

        def project():
            z = ALPHA * x_ref[rs, :] + _dot(st["y"], wout_ref[...])
            o_ref[rs, :] = _layer_norm(z, g_ref[...], b_ref[...])

        return [gate_output, unpermute_rows, project]

    _emit_round_robin(in_proj(0, 0))
    _emit_round_robin(gates(0, 0), in_proj(1, half))
    h_mid = recur(0, h_scr[...])
    _emit_round_robin(gates(1, half), out_proj(0))
    h_scr[...] = recur(half, h_mid)
    _emit_round_robin(out_proj(half))


def _odd_layer(xf, posi, w_in, cw, cb, wax, ba, bx, lam, w_out, g, b, batch, seq):
    n = xf.shape[0]
    tm = ROW_TILE
    nt = seq // tm
    row = lambda bi, i: (bi * nt + i, 0)
    full = lambda bi, i: (0, 0)
    half = tm // 2
    rho = np.arange(half)
    src_time = (rho % SUBLANES) * (half // SUBLANES) + rho // SUBLANES
    perm = np.zeros((half, half), np.float32)
    perm[rho, src_time] = 1.0
    consts = [jnp.asarray(perm, BF16), jnp.asarray(perm.T, BF16), w_in, cw, cb]
    rest = [ba, bx, lam, w_out, g, b]
    return pl.pallas_call(
        _odd_kernel,
        grid=(batch, nt),
        in_specs=[pl.BlockSpec((tm, D_MODEL), row), pl.BlockSpec((tm, 1), row)]
                 + [pl.BlockSpec(a.shape, full) for a in consts]
                 + [pl.BlockSpec(wax.shape, lambda bi, i: (0, 0, 0))]
                 + [pl.BlockSpec(a.shape, full) for a in rest],
        out_specs=pl.BlockSpec((tm, D_MODEL), row),
        out_shape=jax.ShapeDtypeStruct((n, D_MODEL), F32),
        scratch_shapes=[pltpu.VMEM((2, (CONV_WIDTH - 1) * SUBLANES + tm // 2, RNN_WIDTH), F32),
                        pltpu.VMEM((tm, RNN_WIDTH), F32),
                        pltpu.VMEM((tm, RNN_WIDTH), F32),
                        pltpu.VMEM((tm, RNN_WIDTH), F32),
                        pltpu.VMEM((SUBLANES, RNN_WIDTH), F32),
                        pltpu.VMEM((SUBLANES, RNN_WIDTH), F32)],
        compiler_params=_params(2),
        name="odd_layer",
    )(xf, posi, *consts, wax, *rest)


def _even_mixer(xf, positions, w_in, pool_w, pool_scale, cmp_pos_k, cmp_pos_v, cmp_wk, cmp_wv,
                w_out, casts, batch, seq):
    assert seq // SEL_BLOCK == LANES and seq % SEL_KEY_TILE == 0 and seq >= WINDOW + Q_BLOCK
    hd, g, r = NSA_HEAD_DIM, NSA_KV_HEADS, NSA_GROUP
    c0 = POOL_WIDTH
    c1 = c0 + Q_WIDTH
    c2 = c1 + GATE_WIDTH
    wp = w_in[:, :c0].astype(BF16)
    wq = (w_in[:, c0:c1].reshape(D_MODEL, g, r, hd).transpose(0, 2, 1, 3).reshape(D_MODEL, Q_WIDTH)
          * (hd ** -0.5 * LOG2E)).astype(BF16)
    wg = jnp.pad(w_in[:, c1:c2], ((0, 0), (0, LANES - GATE_WIDTH))).astype(BF16)
    wkv = w_in[:, c2:].astype(BF16)
    n = batch * seq
    nc = seq // CMP_STRIDE
    n_cmp = (seq - CMP_BLOCK) // CMP_STRIDE + 1
    posc = jnp.pad(positions[:, CMP_BLOCK - 1::CMP_STRIDE][:, :n_cmp], ((0, 0), (0, nc - n_cmp)))
    pos_all = jnp.concatenate([positions.reshape(1, n), posc.reshape(1, batch * nc)], axis=1).astype(F32)
    cos_t, sin_t = _rope_angles(pos_all)
    ec, es = _rope_spread()
    pool, q, kc_raw, vc_raw, ks, vs, kw, vw, gates = _proj_even(
        xf, cos_t, sin_t, ec, es, wp, wq, wkv, wg, pool_w.astype(BF16), pool_scale[None, :], batch, seq)

    def halves(w):
        w3 = w.reshape(CMP_BLOCK, hd, hd).astype(BF16)
        zero = jnp.zeros_like(w3)
        w4 = jnp.concatenate([jnp.concatenate([w3 if k == j else zero for k in range(g)], axis=2)
                              for j in range(g)], axis=1)
        return (w4[:CMP_STRIDE].reshape(CMP_STRIDE * g * hd, g * hd),
                w4[CMP_STRIDE:].reshape(CMP_STRIDE * g * hd, g * hd))

    def pos_halves(p):
        p2 = jnp.tile(p[:, None, :], (1, g, 1)).reshape(CMP_BLOCK, g * hd)
        return p2[:CMP_STRIDE].reshape(1, -1), p2[CMP_STRIDE:].reshape(1, -1)

    pk1, pk2 = pos_halves(cmp_pos_k)
    pv1, pv2 = pos_halves(cmp_pos_v)
    wk1, wk2 = halves(cmp_wk)
    wv1, wv2 = halves(cmp_wv)
    seq3 = lambda a: a.reshape(batch, seq, KV_WIDTH)
    kc, vc = _compress(seq3(kc_raw), seq3(vc_raw), cos_t, sin_t, n // nc,
                       (ec, es, pk1, pk2, pv1, pv2, wk1, wk2, wv1, wv2))

    n_sb = seq // SEL_BLOCK
    starts = np.arange(nc) * CMP_STRIDE
    jb = np.arange(n_sb)
    ov = ((starts[:, None] < (jb[None, :] + 1) * SEL_BLOCK)
          & (starts[:, None] + CMP_BLOCK > jb[None, :] * SEL_BLOCK) & (np.arange(nc)[:, None] < n_cmp))
    ov = jnp.asarray(ov, BF16)
    pt = jnp.asarray(np.where(np.arange(seq)[:, None] // SEL_BLOCK == jb[None, :], NEG, 0.0), BF16)

    nsa, cast_weights = _nsa(q, gates, kc, vc, seq3(ks), seq3(vs), seq3(kw), seq3(vw), pt, ov, casts, batch, seq)

    w_pool_out = w_out[:POOL_WIDTH].astype(BF16)
    w_nsa_out = (w_out[POOL_WIDTH:].reshape(g, r, hd, D_MODEL).transpose(1, 0, 2, 3)
                 .reshape(Q_WIDTH, D_MODEL).astype(BF16))
    return [pool, nsa], [w_pool_out, w_nsa_out], cast_weights


def _odd_mixer(xf, posi, w_in, conv_w, conv_b, wa, ba, wx, bx, lam, w_out, ln_g, ln_b, batch, seq):
    wax = (0.5 * jnp.concatenate([wa, wx], axis=2)).astype(BF16)
    return _odd_layer(xf, posi, w_in, conv_w, conv_b[None, :], wax, 0.5 * ba[None, :],
                      0.5 * bx[None, :], lam[None, :], w_out, ln_g, ln_b, batch, seq)


def kernel(x, mem, positions, e_w_in, e_pool_w, e_pool_scale, e_cmp_pos_k, e_cmp_pos_v, e_cmp_wk, e_cmp_wv, e_w_out, o_w_in, o_conv_w, o_conv_b, o_wa, o_ba, o_wx, o_bx, o_lambda, o_w_out, x_wq, x_wkv, x_wo, f_w_up, f_w_down, ln_g, ln_b):
    batch, seq, d = x.shape
    n = batch * seq
    assert d == D_MODEL and seq % ROW_TILE == 0
    xf = x.reshape(n, d)
    posi = positions.reshape(n, 1)
    memf = mem.reshape(batch * mem.shape[1], d)
    casts = [(x_wq, X_HEAD_DIM ** -0.5 * LOG2E), (x_wo, 1.0), (f_w_up, 1.0), (f_w_down, 1.0),
             (o_w_in, 1.0), (o_w_out, 1.0)]
    for layer in range(DEPTH):
        j = layer // 2
        if layer % 2 == 0:
            parts, ws, cast_weights = _even_mixer(
                xf, positions, e_w_in[j], e_pool_w[j], e_pool_scale[j], e_cmp_pos_k[j], e_cmp_pos_v[j],
                e_cmp_wk[j], e_cmp_wv[j], e_w_out[j], casts if layer == 0 else [], batch, seq)
            if layer == 0:
                wq_all, wo_all, wu_all, wd_all, o_w_in16, o_w_out16 = cast_weights
        else:
            xf = _odd_mixer(xf, posi, o_w_in16[j], o_conv_w[j], o_conv_b[j], o_wa[j], o_ba[j], o_wx[j], o_bx[j],
                            o_lambda[j], o_w_out16[j], ln_g[layer, 0][None, :], ln_b[layer, 0][None, :],
                            batch, seq)
            parts, ws = [], []
        kv = _mem_kv(memf, x_wkv, layer).reshape(batch, mem.shape[1], 2 * d)
        xf = _tail(xf, parts, ws, kv, wq_all, wo_all, wu_all, wd_all, ln_g, ln_b, layer, batch, seq)
    return xf.reshape(batch, seq, d)
```

```python
import functools

import numpy as np
import jax
import jax.numpy as jnp
from jax import lax
from jax.experimental import pallas as pl
from jax.experimental.pallas import tpu as pltpu

F32 = jnp.float32
BF16 = jnp.bfloat16

D_MODEL = 1024
DEPTH = 2
ALPHA = (2.0 * DEPTH) ** 0.25
LN_EPS = 1e-5
NEG = -1e30
POOL_WIDTH = D_MODEL // 2
POOL_WINDOWS = (2, 4, 8, 16)
POOL_GROUP = POOL_WIDTH // len(POOL_WINDOWS)
POOL_HALO = 16
NSA_HEADS = 8
NSA_KV_HEADS = 2
NSA_HEAD_DIM = 64
NSA_GROUP = NSA_HEADS // NSA_KV_HEADS
CMP_BLOCK = 32
CMP_STRIDE = 16
SEL_BLOCK = 64
SEL_COUNT = 16
WINDOW = 512
Q_BLOCK = 256
N_BRANCH = 3
N_FORCED = 3
LOG2E = 1.4426950408889634
ROPE_THETA = 500000.0
ROT_DIM = NSA_HEAD_DIM // 4
ROT_HALF = ROT_DIM // 2
Q_WIDTH = NSA_HEADS * NSA_HEAD_DIM
KV_WIDTH = NSA_KV_HEADS * NSA_HEAD_DIM
GATE_WIDTH = NSA_HEADS * N_BRANCH
RNN_WIDTH = 1280
RNN_BLOCKS = 10
RNN_BLOCK_W = RNN_WIDTH // RNN_BLOCKS
CONV_WIDTH = 4
LRU_C = 8.0
X_HEADS = 4
X_HEAD_DIM = D_MODEL // X_HEADS
FFN_HIDDEN = 2816
GELU_C0 = 0.7978845608028654
GELU_C1 = GELU_C0 * 0.044715

LANES = 128
SUBLANES = 8
VMEM_LIMIT = 56 * 1024 * 1024
ROW_TILE = 512
SEL_KEY_TILE = 512
FFN_CHUNKS = (256,) * 11


def _params(n_axes, vmem=VMEM_LIMIT):
    return pltpu.CompilerParams(dimension_semantics=("arbitrary",) * n_axes,
                                vmem_limit_bytes=vmem)


def _resident(shape, index_map):
    return pl.BlockSpec(shape, index_map, pipeline_mode=pl.Buffered(1))


def _emit_round_robin(*stage_lists):
    lists = [list(s) for s in stage_lists]
    while any(lists):
        for s in lists:
            if s:
                s.pop(0)()


def _dot(a, b):
    return jnp.dot(a, b, preferred_element_type=F32)


def _dot_nt(a, b):
    return lax.dot_general(a, b, (((1,), (1,)), ((), ())), preferred_element_type=F32)


def _rep_rows(a, k):
    return jnp.concatenate([a] * k, axis=0)


def _rep_lanes(a, k):
    return jnp.concatenate([a] * k, axis=1) if k > 1 else a


def _layer_norm(z, g, b):
    mu = jnp.mean(z, axis=-1, keepdims=True)
    d = z - mu
    var = jnp.mean(d * d, axis=-1, keepdims=True)
    return d * lax.rsqrt(var + LN_EPS) * g + b


def _rope(v, cos, sin):
    k = v.shape[1] // LANES
    up = pltpu.roll(v, v.shape[1] - ROT_HALF, 1)
    dn = pltpu.roll(v, ROT_HALF, 1)
    lane = lax.broadcasted_iota(jnp.int32, v.shape, 1)
    partner = jnp.where((lane & (NSA_HEAD_DIM - 1)) < ROT_HALF, up, dn)
    return v * _rep_lanes(cos, k) + partner * _rep_lanes(sin, k)


def _rope_angle_kernel(pos_ref, inv_ref, cos_o, sin_o):
    ang = inv_ref[...] * pos_ref[...]
    cos_o[...] = jnp.cos(ang)
    sin_o[...] = jnp.sin(ang)


def _rope_angles(pos_row):
    n = pos_row.shape[1]
    inv = (ROPE_THETA ** (-jnp.arange(ROT_HALF, dtype=F32) * 2.0 / ROT_DIM))[:, None]
    out = jax.ShapeDtypeStruct((ROT_HALF, n), F32)
    return pl.pallas_call(
        _rope_angle_kernel,
        out_shape=[out, out],
        name="rope_angles",
    )(pos_row, inv)


def _rope_spread():
    lane = np.arange(LANES) % NSA_HEAD_DIM
    f = np.arange(ROT_HALF)[:, None]
    lo = (lane[None, :] == f).astype(np.float32)
    hi = (lane[None, :] == f + ROT_HALF).astype(np.float32)
    return jnp.asarray(lo + hi, BF16), jnp.asarray(hi - lo, BF16)


def _rope_patterns(cos_t, sin_t, ec, es):
    def spread(t, e):
        hi = t.astype(BF16)
        r1 = t - hi.astype(F32)
        mid = r1.astype(BF16)
        lo = (r1 - mid.astype(F32)).astype(BF16)
        tn = (((0,), (0,)), ((), ()))
        return sum(lax.dot_general(p, e, tn, preferred_element_type=F32) for p in (hi, mid, lo))

    lane = lax.broadcasted_iota(jnp.int32, (1, LANES), 1)
    unrotated = jnp.where((lane & (NSA_HEAD_DIM - 1)) >= ROT_DIM, 1.0, 0.0)
    return spread(cos_t, ec) + unrotated, spread(sin_t, es)


def _proj_even_kernel(x_ref, cos_ref, sin_ref, ec_ref, es_ref, wp_ref, wq_ref, wkv_ref, wg_ref, poolw_ref, pscale_ref,
                      pool_o, q_o, kc_o, vc_o, ks_o, vs_o, kw_o, vw_o, gate_o, pbuf):
    i = pl.program_id(1)
    tm = x_ref.shape[0]
    half = tm // 2

    @pl.when(i == 0)
    def _():
        pbuf[0:POOL_HALO, :] = jnp.zeros((POOL_HALO, POOL_WIDTH), F32)

    def project(h0, st):
        rs = slice(h0, h0 + half)

        def cast():
            st["xb"] = x_ref[rs, :].astype(BF16)

        def pool_in():
            pbuf[POOL_HALO + h0:POOL_HALO + h0 + half, :] = _dot(st["xb"], wp_ref[...])

        return [cast, pool_in,
                lambda: st.__setitem__("q", _dot(st["xb"], wq_ref[...])),
                lambda: st.__setitem__("kv", _dot(st["xb"], wkv_ref[...])),
                lambda: st.__setitem__("gl", _dot(st["xb"], wg_ref[...]))]

    def emit(h0, st):
        rs = slice(h0, h0 + half)

        def patterns():
            st["cos"], st["sin"] = _rope_patterns(cos_ref[:, rs], sin_ref[:, rs], ec_ref[...], es_ref[...])

        def queries():
            q_o[rs, :] = _rope(st["q"], st["cos"], st["sin"]).astype(BF16)

        def keys_values():
            kv, cos, sin = st["kv"], st["cos"], st["sin"]
            kc_o[rs, :] = kv[:, 0 * LANES:1 * LANES]
            vc_o[rs, :] = kv[:, 1 * LANES:2 * LANES]
            ks_o[rs, :] = _rope(kv[:, 2 * LANES:3 * LANES], cos, sin).astype(BF16)
            vs_o[rs, :] = kv[:, 3 * LANES:4 * LANES].astype(BF16)
            kw_o[rs, :] = _rope(kv[:, 4 * LANES:5 * LANES], cos, sin).astype(BF16)
            vw_o[rs, :] = kv[:, 5 * LANES:6 * LANES].astype(BF16)
            gate_o[rs, :] = _sigmoid_tanh(st["gl"])

        return [patterns, queries, keys_values]

    def pool(h0):
        rs = slice(h0, h0 + half)
        t1 = i * tm + h0 + lax.broadcasted_iota(jnp.int32, (half, 1), 0) + 1

        def group(g, w):
            sl = slice(g * POOL_GROUP, (g + 1) * POOL_GROUP)
            u = pbuf[POOL_HALO + h0:POOL_HALO + h0 + half, sl]
            tot = u
            for j in range(1, w):
                tot = tot + pbuf[pl.ds(POOL_HALO + h0 - j, half), sl]
            cnt = jnp.minimum(t1, w).astype(F32)
            pooled = tot / cnt - u
            mixed = _dot(pooled.astype(BF16), poolw_ref[g]) * pscale_ref[:, sl]
            pool_o[rs, sl] = mixed.astype(BF16)

        return [functools.partial(group, g, w) for g, w in enumerate(POOL_WINDOWS)]

    first, second = {}, {}
    _emit_round_robin(project(0, first))
    _emit_round_robin(emit(0, first) + pool(0), project(half, second))
    _emit_round_robin(emit(half, second) + pool(half))
    pbuf[0:POOL_HALO, :] = pbuf[tm:tm + POOL_HALO, :]


def _proj_even(xf, cos_t, sin_t, ec, es, wp, wq, wkv, wg, poolw, pscale, batch, seq):
    n = xf.shape[0]
    tm = ROW_TILE
    nt = seq // tm
    row = lambda b, i: (b * nt + i, 0)
    full = lambda b, i: (0, 0)
    outs = [
        jax.ShapeDtypeStruct((n, POOL_WIDTH), BF16),
        jax.ShapeDtypeStruct((n, Q_WIDTH), BF16),
        jax.ShapeDtypeStruct((n, KV_WIDTH), F32),
        jax.ShapeDtypeStruct((n, KV_WIDTH), F32),
        jax.ShapeDtypeStruct((n, KV_WIDTH), BF16),
        jax.ShapeDtypeStruct((n, KV_WIDTH), BF16),
        jax.ShapeDtypeStruct((n, KV_WIDTH), BF16),
        jax.ShapeDtypeStruct((n, KV_WIDTH), BF16),
        jax.ShapeDtypeStruct((n, LANES), F32),
    ]
    return pl.pallas_call(
        _proj_even_kernel,
        grid=(batch, nt),
        in_specs=[
            pl.BlockSpec((tm, D_MODEL), row),
            pl.BlockSpec((ROT_HALF, tm), lambda b, i: (0, b * nt + i)),
            pl.BlockSpec((ROT_HALF, tm), lambda b, i: (0, b * nt + i)),
            pl.BlockSpec(ec.shape, full),
            pl.BlockSpec(es.shape, full),
            pl.BlockSpec(wp.shape, full),
            pl.BlockSpec(wq.shape, full),
            pl.BlockSpec(wkv.shape, full),
            pl.BlockSpec(wg.shape, full),
            pl.BlockSpec(poolw.shape, lambda b, i: (0, 0, 0)),
            pl.BlockSpec(pscale.shape, full),
        ],
        out_specs=[pl.BlockSpec((tm, o.shape[1]), row) for o in outs],
        out_shape=outs,
        scratch_shapes=[pltpu.VMEM((POOL_HALO + tm, POOL_WIDTH), F32)],
        compiler_params=_params(2),
        name="proj_even",
    )(xf, cos_t, sin_t, ec, es, wp, wq, wkv, wg, poolw, pscale)


def _compress_kernel(rk_ref, rv_ref, cos_ref, sin_ref, ec_ref, es_ref, pk1_ref, pk2_ref, pv1_ref, pv2_ref,
                     wk1_ref, wk2_ref, wv1_ref, wv2_ref, kc_o, vc_o):
    nc = kc_o.shape[0]

    def compress(raw_ref, p1, p2, w1, w2):
        r = jnp.concatenate([raw_ref[pl.ds(j, nc, stride=CMP_STRIDE), :] for j in range(CMP_STRIDE)], axis=1)
        a = _dot((r + p1[...]).astype(BF16), w1[...])
        b = _dot((r + p2[...]).astype(BF16), w2[...])
        return a + pltpu.roll(b, nc - 1, 0)

    kc = compress(rk_ref, pk1_ref, pk2_ref, wk1_ref, wk2_ref)
    cos, sin = _rope_patterns(cos_ref[...], sin_ref[...], ec_ref[...], es_ref[...])
    kc_o[...] = _rope(kc, cos, sin).astype(BF16)
    vc_o[...] = compress(rv_ref, pv1_ref, pv2_ref, wv1_ref, wv2_ref).astype(BF16)


def _compress(rk, rv, cos_t, sin_t, first_col_block, consts):
    batch, seq, width = rk.shape
    nc = seq // CMP_STRIDE
    blk = lambda b: (b, 0, 0)
    full = lambda b: (0, 0)
    out = jax.ShapeDtypeStruct((batch, nc, KV_WIDTH), BF16)
    return pl.pallas_call(
        _compress_kernel,
        grid=(batch,),
        in_specs=[pl.BlockSpec((None, seq, width), blk), pl.BlockSpec((None, seq, width), blk),
                  pl.BlockSpec((ROT_HALF, nc), lambda b: (0, first_col_block + b)),
                  pl.BlockSpec((ROT_HALF, nc), lambda b: (0, first_col_block + b))]
                 + [pl.BlockSpec(a.shape, full) for a in consts],
        out_specs=[pl.BlockSpec((None, nc, KV_WIDTH), blk)] * 2,
        out_shape=[out, out],
        compiler_params=_params(1),
        name="compress",
    )(rk, rv, cos_t, sin_t, *consts)


def _softmax_rows(s):
    m = jnp.max(s, axis=1, keepdims=True)
    e = jnp.exp2(s - m)
    return e, jnp.sum(e, axis=1, keepdims=True)


def _dot_exact01(x, m01):
    hi = x.astype(BF16)
    r1 = x - hi.astype(F32)
    mid = r1.astype(BF16)
    lo = (r1 - mid.astype(F32)).astype(BF16)
    return _dot(hi, m01) + _dot(mid, m01) + _dot(lo, m01)


def _unselected_blocks(imp, q0, n_pick):
    nq, nb = imp.shape
    col = lax.broadcasted_iota(jnp.int32, (nq, nb), 1)
    t_row = q0 + lax.broadcasted_iota(jnp.int32, (nq, nb), 0)
    cur = lax.shift_right_arithmetic(t_row, SEL_BLOCK.bit_length() - 1)
    forced = (col == 0) | (col == cur) | (col == cur - 1)
    val = jnp.where(forced, -jnp.inf, jnp.where(col > cur, -1.0, imp))
    vt = val.T
    blk = lax.broadcasted_iota(jnp.int32, (nb, nq), 0).astype(F32)
    for _ in range(n_pick - N_FORCED):
        m = jnp.max(vt, axis=0, keepdims=True)
        first = jnp.min(jnp.where(vt == m, blk, float(nb)), axis=0, keepdims=True)
        vt = jnp.where(blk == first, -jnp.inf, vt)
    return jnp.where(vt == -jnp.inf, 0.0, 1.0).T


def _nsa_kernel(*refs, n_pick, cast_scales, first_block, n_extra):
    n_cast = len(cast_scales)
    q_ref, gate_ref, kc_ref, vc_ref, ks_ref, vs_ref, kw_ref, vw_ref, pt_ref, ov_ref = refs[:10]
    cast_in = refs[10:10 + n_cast]
    n_in = 10 + n_cast + n_extra
    o_ref = refs[n_in]
    cast_out = refs[n_in + 1:n_in + 1 + n_cast]
    m_scr, l_scr, acc_scr, lhs_scr, oc_scr, ow_scr, s0_scr, s1_scr, w_scr, gx_scr = refs[n_in + 1 + n_cast:]

    for src, dst, scale in zip(cast_in, cast_out, cast_scales):
        w = src[...] if scale == 1.0 else src[...] * scale
        dst[...] = w.astype(BF16)

    qb = pl.program_id(1) + first_block
    q0 = qb * Q_BLOCK
    tk = SEL_KEY_TILE
    ncmp = kc_ref.shape[0]
    wk = WINDOW + Q_BLOCK
    rows = NSA_GROUP * Q_BLOCK
    lane = lax.broadcasted_iota(jnp.int32, (Q_BLOCK, LANES), 1)

    def trow(width):
        return q0 + lax.broadcasted_iota(jnp.int32, (Q_BLOCK, width), 0)

    def kcol(width):
        return lax.broadcasted_iota(jnp.int32, (Q_BLOCK, width), 1)

    bias_c = jnp.where(kcol(ncmp) * CMP_STRIDE + (CMP_BLOCK - 1) <= trow(ncmp), 0.0, NEG)
    has_cmp = (trow(1) >= CMP_BLOCK - 1).astype(F32)
    wstart = pl.multiple_of(jnp.maximum(q0 - WINDOW, 0), Q_BLOCK)
    kpos_w = wstart + kcol(wk)
    bias_w = jnp.where(kpos_w <= trow(wk), jnp.where(kpos_w > trow(wk) - WINDOW, 0.0, NEG), NEG)

    for g in range(NSA_KV_HEADS):
        mine = (lane >= NSA_HEAD_DIM) if g else (lane < NSA_HEAD_DIM)
        lhs_scr[g, :, 0:LANES] = jnp.concatenate(
            [jnp.where(mine, q_ref[:, r * LANES:(r + 1) * LANES], jnp.zeros((), BF16))
             for r in range(NSA_GROUP)], axis=0)
        m_scr[g] = jnp.full((rows, LANES), -jnp.inf, F32)
        l_scr[g] = jnp.zeros((rows, LANES), F32)
        acc_scr[g] = jnp.zeros((rows, LANES), F32)

    def cmp_scores(g):
        s0_scr[g, :, 0:ncmp] = _dot_nt(lhs_scr[g, :, 0:LANES], kc_ref[...])

    def cmp_attend(g):
        e, l = _softmax_rows(s0_scr[g, :, 0:ncmp] + _rep_rows(bias_c, NSA_GROUP))
        p = e * (_rep_rows(has_cmp, NSA_GROUP) / l)
        oc_scr[g] = _dot(p.astype(BF16), vc_ref[...])
        psum = p[0:Q_BLOCK]
        for r in range(1, NSA_GROUP):
            psum = psum + p[r * Q_BLOCK:(r + 1) * Q_BLOCK]
        return _dot_exact01(psum, ov_ref[...])

    def select(g, imp):
        unsel = _unselected_blocks(imp, q0, n_pick)
        lhs_scr[g, :, LANES:2 * LANES] = _rep_rows(unsel.astype(BF16), NSA_GROUP)

    def win_scores(g):
        w_scr[g] = _dot_nt(lhs_scr[g, :, 0:LANES], kw_ref[pl.ds(wstart, wk), :])

    def win_attend(g):
        e, l = _softmax_rows(w_scr[g] + _rep_rows(bias_w, NSA_GROUP))
        ow_scr[g] = _dot(e.astype(BF16), vw_ref[pl.ds(wstart, wk), :]) / l

    def expand_gates():
        low = lane < NSA_HEAD_DIM
        for r in range(NSA_GROUP):
            for br in range(N_BRANCH):
                c_lo = r * N_BRANCH + br
                c_hi = (NSA_GROUP + r) * N_BRANCH + br
                gx_scr[c_lo] = jnp.where(low, gate_ref[:, c_lo:c_lo + 1], gate_ref[:, c_hi:c_hi + 1])

    cmp_scores(0)
    cmp_scores(1)
    expand_gates()
    win_scores(0)
    imp0 = cmp_attend(0)
    win_scores(1)
    imp1 = cmp_attend(1)
    select(0, imp0)
    win_attend(0)
    select(1, imp1)
    win_attend(1)

    def scores(kt, buf, g):
        k0 = pl.multiple_of(kt * tk, tk)
        rhs = jnp.concatenate([ks_ref[pl.ds(k0, tk), :], pt_ref[pl.ds(k0, tk), :]], axis=1)
        buf[g] = _dot_nt(lhs_scr[g], rhs)

    def consume(kt, buf, g, causal):
        k0 = pl.multiple_of(kt * tk, tk)
        v = vs_ref[pl.ds(k0, tk), :]
        bias = jnp.where(k0 + kcol(tk) <= trow(tk), 0.0, NEG) if causal else None
        for r in range(NSA_GROUP):
            rs = slice(r * Q_BLOCK, (r + 1) * Q_BLOCK)
            s = buf[g, rs, :]
            if causal:
                s = s + bias
            m_prev = m_scr[g, rs, :]
            m_new = jnp.maximum(m_prev, jnp.max(s, axis=1, keepdims=True))
            alpha = jnp.exp2(m_prev - m_new)
            pe = jnp.exp2(s - _rep_lanes(m_new, tk // LANES))
            part = pe[:, 0:LANES]
            for c in range(1, tk // LANES):
                part = part + pe[:, c * LANES:(c + 1) * LANES]
            l_scr[g, rs, :] = alpha * l_scr[g, rs, :] + part
            acc_scr[g, rs, :] = alpha * acc_scr[g, rs, :] + _dot(pe.astype(BF16), v)
            m_scr[g, rs, :] = m_new

    def step(nxt, nxt_buf, cur, cur_buf, causal=False):
        for g in range(NSA_KV_HEADS):
            if nxt is not None:
                scores(nxt, nxt_buf, g)
        for g in range(NSA_KV_HEADS):
            consume(cur, cur_buf, g, causal)

    n_full = q0 // tk
    for g in range(NSA_KV_HEADS):
        scores(0, s0_scr, g)

    def pair(j, carry):
        step(2 * j + 1, s1_scr, 2 * j, s0_scr)
        step(2 * j + 2, s0_scr, 2 * j + 1, s1_scr)
        return carry

    lax.fori_loop(0, n_full // 2, pair, 0)

    @pl.when(n_full % 2 == 0)
    def _():
        step(None, None, n_full, s0_scr, causal=True)

    @pl.when(n_full % 2 == 1)
    def _():
        step(n_full, s1_scr, n_full - 1, s0_scr)
        step(None, None, n_full, s1_scr, causal=True)

    low = lane < NSA_HEAD_DIM
    for r in range(NSA_GROUP):
        rs = slice(r * Q_BLOCK, (r + 1) * Q_BLOCK)
        merged = lambda scr: jnp.where(low, scr[0, rs, :], scr[1, rs, :])
        l_sel = jnp.where(low, jnp.sum(l_scr[0, rs, :], axis=1, keepdims=True),
                          jnp.sum(l_scr[1, rs, :], axis=1, keepdims=True))
        out = (gx_scr[r * N_BRANCH] * merged(oc_scr)
               + gx_scr[r * N_BRANCH + 1] * (merged(acc_scr) / l_sel)
               + gx_scr[r * N_BRANCH + 2] * merged(ow_scr))
        o_ref[:, r * LANES:(r + 1) * LANES] = out.astype(BF16)


def _nsa(q, gates, kc, vc, ks, vs, kw, vw, pt, ov, casts, batch, seq, first_block, n_blocks, prev_out=None):
    n = q.shape[0]
    nqb = seq // Q_BLOCK
    steps = batch * n_blocks
    rows = NSA_GROUP * Q_BLOCK
    row = lambda b, i: (b * nqb + first_block + i, 0)
    per_b = lambda b, i: (b, 0, 0)
    full = lambda b, i: (0, 0)
    seq_spec = _resident((None, seq, KV_WIDTH), per_b)
    last_t = (first_block + n_blocks) * Q_BLOCK - 1
    n_visible = (last_t - (CMP_BLOCK - 1)) // CMP_STRIDE + 1
    ncmp = min(kc.shape[1], -(-n_visible // (2 * LANES)) * (2 * LANES))
    cmp_spec = _resident((None, ncmp, KV_WIDTH), per_b)
    n_pick = min(SEL_COUNT, seq // SEL_BLOCK)
    sliced = [w.reshape(steps, -1, w.shape[-1]) for w, _ in casts]
    slice_specs = [pl.BlockSpec((None,) + w.shape[1:], lambda b, i: (b * n_blocks + i, 0, 0)) for w in sliced]
    extra = [] if prev_out is None else [prev_out]
    outs = pl.pallas_call(
        functools.partial(_nsa_kernel, n_pick=n_pick, cast_scales=tuple(s for _, s in casts),
                          first_block=first_block, n_extra=len(extra)),
        grid=(batch, n_blocks),
        in_specs=[pl.BlockSpec((Q_BLOCK, Q_WIDTH), row), pl.BlockSpec((Q_BLOCK, LANES), row),
                  cmp_spec, cmp_spec, seq_spec, seq_spec, seq_spec, seq_spec,
                  _resident(pt.shape, full), _resident((ncmp, ov.shape[1]), full)] + slice_specs
                 + [pl.BlockSpec(memory_space=pl.ANY)] * len(extra),
        out_specs=[pl.BlockSpec((Q_BLOCK, Q_WIDTH), row)] + slice_specs,
        out_shape=[jax.ShapeDtypeStruct((n, Q_WIDTH), BF16)]
                  + [jax.ShapeDtypeStruct(w.shape, BF16) for w in sliced],
        input_output_aliases={10 + len(sliced): 0} if extra else {},
        scratch_shapes=[pltpu.VMEM((NSA_KV_HEADS, rows, LANES), F32),
                        pltpu.VMEM((NSA_KV_HEADS, rows, LANES), F32),
                        pltpu.VMEM((NSA_KV_HEADS, rows, LANES), F32),
                        pltpu.VMEM((NSA_KV_HEADS, rows, 2 * LANES), BF16),
                        pltpu.VMEM((NSA_KV_HEADS, rows, LANES), F32),
                        pltpu.VMEM((NSA_KV_HEADS, rows, LANES), F32),
                        pltpu.VMEM((NSA_KV_HEADS, rows, SEL_KEY_TILE), F32),
                        pltpu.VMEM((NSA_KV_HEADS, rows, SEL_KEY_TILE), F32),
                        pltpu.VMEM((NSA_KV_HEADS, rows, WINDOW + Q_BLOCK), F32),
                        pltpu.VMEM((NSA_GROUP * N_BRANCH, Q_BLOCK, LANES), F32)],
        compiler_params=_params(2),
        name="nsa",
    )(q, gates, kc, vc, ks, vs, kw, vw, pt, ov, *sliced, *extra)
    return outs[0], [o.reshape(w.shape) for o, (w, _) in zip(outs[1:], casts)]


def _matmul_kernel(a_ref, w_ref, o_ref):
    o_ref[...] = _dot(a_ref[...].astype(BF16), w_ref[...].astype(BF16)).astype(o_ref.dtype)


def _mem_kv(memf, wkv_all, layer):
    m, k = memf.shape
    n = wkv_all.shape[2]
    return pl.pallas_call(
        _matmul_kernel,
        grid=(1,),
        in_specs=[pl.BlockSpec((m, k), lambda i: (0, 0)), pl.BlockSpec((None, k, n), lambda i: (layer, 0, 0))],
        out_specs=pl.BlockSpec((m, n), lambda i: (0, 0)),
        out_shape=jax.ShapeDtypeStruct((m, n), BF16),
        compiler_params=_params(1),
        name="mem_kv",
    )(memf, wkv_all)


def _tail_kernel(*refs, n_parts):
    x_ref = refs[0]
    parts = refs[1:1 + n_parts]
    ws = refs[1 + n_parts:1 + 2 * n_parts]
    wq_ref, k_ref, v_ref, wo_ref, wu_ref, wd_ref, g_ref, b_ref, o_ref, x1_scr, x2_scr, x2_prev = refs[1 + 2 * n_parts:]
    t = pl.program_id(0)
    n_tiles = pl.num_programs(0) - 1
    tm = x_ref.shape[0]
    half = tm // 2
    halves = (slice(0, half), slice(half, tm))
    x1 = x1_scr if n_parts else x_ref

    def norm(k, z):
        return _layer_norm(z, g_ref[k:k + 1, :], b_ref[k:k + 1, :])

    def mixer_out(rs):
        z = ALPHA * x_ref[rs, :]
        for a, w in zip(parts, ws):
            z = z + _dot(a[rs, :], w[...])
        x1_scr[rs, :] = norm(0, z)

    def project(rs):
        return _dot(x1[rs, :].astype(BF16), wq_ref[...]).astype(BF16)

    heads = [slice(h * X_HEAD_DIM, (h + 1) * X_HEAD_DIM) for h in range(X_HEADS)]

    def scores(q):
        return [_dot_nt(q[:, sl], k_ref[:, sl]) for sl in heads]

    def probabilities(ss):
        probs = []
        for s in ss:
            e, l = _softmax_rows(s)
            probs.append((e / l).astype(BF16))
        return probs

    def mix(probs):
        outs = [_dot(p, v_ref[:, sl]) for p, sl in zip(probs, heads)]
        return jnp.concatenate(outs, axis=1).astype(BF16)

    def finish(rs, o):
        x2_scr[rs, :] = norm(1, ALPHA * x1[rs, :] + _dot(o, wo_ref[...]))

    def attention_stages():
        st = {}
        stages = []
        both = range(len(halves))
        if n_parts:
            stages += [functools.partial(mixer_out, h) for h in halves]
        stages += [lambda i=i: st.__setitem__(("q", i), project(halves[i])) for i in both]
        stages += [lambda i=i: st.__setitem__(("s", i), scores(st["q", i])) for i in both]
        stages += [lambda i=i: st.__setitem__(("p", i), probabilities(st["s", i])) for i in both]
        stages += [lambda i=i: st.__setitem__(("o", i), mix(st["p", i])) for i in both]
        stages += [lambda i=i: finish(halves[i], st["o", i]) for i in both]
        return stages

    def ffn_stages(src, out):
        st = {}

        def start():
            x2 = src[...]
            st.update(xb=x2.astype(BF16), z=ALPHA * x2)

        def chunk(c0, width):
            h1 = _dot(st["xb"], wu_ref[:, c0:c0 + width])
            h2 = _dot(st["xb"], wu_ref[:, FFN_HIDDEN + c0:FFN_HIDDEN + c0 + width])
            act = (h1 * _sigmoid_tanh(h1) * h2).astype(BF16)
            st["z"] = st["z"] + _dot(act, wd_ref[c0:c0 + width, :])

        stages = [start]
        c0 = 0
        for width in FFN_CHUNKS:
            stages.append(functools.partial(chunk, c0, width))
            c0 += width
        stages.append(lambda: out.__setitem__(Ellipsis, norm(2, st["z"])))
        return stages

    @pl.when(t == 0)
    def _():
        for stage in attention_stages():
            stage()
        x2_prev[...] = x2_scr[...]

    @pl.when((t > 0) & (t < n_tiles))
    def _():
        _emit_round_robin(ffn_stages(x2_prev, o_ref), attention_stages())
        x2_prev[...] = x2_scr[...]

    @pl.when(t == n_tiles)
    def _():
        for stage in ffn_stages(x2_prev, o_ref):
            stage()


def _tail(xf, parts, ws, kv, wq_all, wo_all, wu_all, wd_all, g_all, b_all, layer, batch, seq):
    n = xf.shape[0]
    tm = ROW_TILE
    nt = seq // tm
    n_tiles = n // tm
    n_mem = kv.shape[1]
    cur = lambda t: jnp.minimum(t, n_tiles - 1)
    row = lambda t: (cur(t), 0)
    lagged = lambda t: (jnp.maximum(t - 1, 0), 0)
    full = lambda t: (0, 0)
    this_layer = lambda t: (layer, 0, 0)
    stacked = lambda a: _resident((None,) + a.shape[1:], this_layer)
    return pl.pallas_call(
        functools.partial(_tail_kernel, n_parts=len(parts)),
        grid=(n_tiles + 1,),
        in_specs=[pl.BlockSpec((tm, D_MODEL), row)]
                 + [pl.BlockSpec((tm, a.shape[1]), row) for a in parts]
                 + [_resident(w.shape, full) for w in ws]
                 + [stacked(wq_all),
                    _resident((None, n_mem, D_MODEL), lambda t: (cur(t) // nt, 0, 0)),
                    _resident((None, n_mem, D_MODEL), lambda t: (cur(t) // nt, 0, 1)),
                    stacked(wo_all), stacked(wu_all), stacked(wd_all), stacked(g_all), stacked(b_all)],
        out_specs=pl.BlockSpec((tm, D_MODEL), lagged),
        out_shape=jax.ShapeDtypeStruct((n, D_MODEL), F32),
        scratch_shapes=[pltpu.VMEM((tm, D_MODEL), F32), pltpu.VMEM((tm, D_MODEL), F32),
                        pltpu.VMEM((tm, D_MODEL), F32)],
        compiler_params=_params(1),
        name="tail",
    )(xf, *parts, *ws, wq_all, kv, kv, wo_all, wu_all, wd_all, g_all, b_all)


def _gelu_tanh(x):
    inner = x * (GELU_C0 + GELU_C1 * (x * x))
    return x * (0.5 * jnp.tanh(inner) + 0.5)


def _sigmoid_tanh(z):
    return 0.5 * jnp.tanh(0.5 * z) + 0.5


def _odd_kernel(x_ref, pos_ref, perm_ref, unperm_ref, win_ref, cw_ref, cb_ref, wax_ref, ba_ref, bx_ref, lam_ref,
                wout_ref, g_ref, b_ref, o_ref, xbuf, gate_scr, a_scr, b_scr, tail_scr, h_scr):
    i = pl.program_id(1)
    tm = x_ref.shape[0]
    half = tm // 2
    nv = half // SUBLANES
    halo = (CONV_WIDTH - 1) * SUBLANES

    @pl.when(i == 0)
    def _():
        tail_scr[...] = jnp.zeros((SUBLANES, RNN_WIDTH), F32)
        h_scr[...] = jnp.zeros((SUBLANES, RNN_WIDTH), F32)

    def permute(v2d):
        return jnp.swapaxes(v2d.reshape(SUBLANES, nv, v2d.shape[1]), 0, 1).reshape(v2d.shape)

    sub = lax.broadcasted_iota(jnp.int32, (SUBLANES, RNN_WIDTH), 0)
    lam = lam_ref[...]
    half_scale = (-0.5 * LRU_C) * (jnp.maximum(-lam, 0.0) + jnp.log1p(jnp.exp(-jnp.abs(lam))))

    col_chunk = 2 * LANES

    def in_proj(hi, h0):
        st = {}

        def permute_rows():
            st["xb"] = _dot(perm_ref[...], x_ref[h0:h0 + half, :].astype(BF16)).astype(BF16)

        def gate_cols(c):
            gate_scr[h0:h0 + half, c:c + col_chunk] = _dot(st["xb"], win_ref[:, c:c + col_chunk])

        def input_cols(c):
            xbuf[hi, halo:halo + half, c:c + col_chunk] = _dot(
                st["xb"], win_ref[:, RNN_WIDTH + c:RNN_WIDTH + c + col_chunk])

        cols = range(0, RNN_WIDTH, col_chunk)
        return ([permute_rows] + [functools.partial(gate_cols, c) for c in cols]
                + [functools.partial(input_cols, c) for c in cols])

    def gates(hi, h0):
        st = {}

        def prepare():
            for j in range(1, CONV_WIDTH):
                last = xbuf[hi, halo + (nv - j) * SUBLANES:halo + (nv - j + 1) * SUBLANES, :]
                before = jnp.where(sub == 0, tail_scr[j:j + 1, :], pltpu.roll(last, 1, 0))
                xbuf[hi, halo - j * SUBLANES:halo - (j - 1) * SUBLANES, :] = before
                tail_scr[j:j + 1, :] = last[SUBLANES - 1:SUBLANES, :]
            is_reset = (pos_ref[h0:h0 + half, :] == 0).astype(F32)
            st["reset"] = permute(jnp.broadcast_to(is_reset, (half, RNN_BLOCK_W))) > 0.5

        def block(h):
            reset = st["reset"]
            sl = slice(h * RNN_BLOCK_W, (h + 1) * RNN_BLOCK_W)
            xc = cb_ref[:, sl]
            for k in range(CONV_WIDTH):
                xc = xc + xbuf[hi, k * SUBLANES:k * SUBLANES + half, sl] * cw_ref[k:k + 1, sl]
            ri = _dot(xc.astype(BF16), wax_ref[h])
            t_r = jnp.tanh(ri[:, :RNN_BLOCK_W] + ba_ref[:, sl])
            t_i = jnp.tanh(ri[:, RNN_BLOCK_W:] + bx_ref[:, sl])
            log_a = t_r * half_scale[:, sl] + half_scale[:, sl]
            a = jnp.where(reset, 0.0, jnp.exp(log_a))
            th = jnp.tanh(log_a)
            q2 = -0.5 * th / (1.0 - th)
            half_mult = jnp.where(reset, 0.5, jnp.where(q2 > 0.0, q2 * lax.rsqrt(q2), 0.0))
            b = half_mult * (t_i * xc + xc)
            a_scr[h0:h0 + half, sl] = a
            b_scr[h0:h0 + half, sl] = b

        return [prepare] + [functools.partial(block, h) for h in range(RNN_BLOCKS)]

    def recur(h0, h_prev):
        decay = jnp.ones((SUBLANES, RNN_WIDTH), F32)
        resp = jnp.zeros((SUBLANES, RNN_WIDTH), F32)
        for v in range(nv):
            rs = slice(h0 + v * SUBLANES, h0 + (v + 1) * SUBLANES)
            av = a_scr[rs, :]
            decay = av * decay
            resp = av * resp + b_scr[rs, :]
            a_scr[rs, :] = decay
            b_scr[rs, :] = resp
        ca, cb_ = decay, resp
        for d in (1, 2, 4):
            ok = sub >= d
            cb_ = jnp.where(ok, ca * pltpu.roll(cb_, d, 0) + cb_, cb_)
            ca = jnp.where(ok, ca * pltpu.roll(ca, d, 0), ca)
        h_end = ca * h_prev + cb_
        h_in = jnp.where(sub == 0, h_prev, pltpu.roll(h_end, 1, 0))
        rs = slice(h0, h0 + half)
        hh = (a_scr[rs, :].reshape(nv, SUBLANES, RNN_WIDTH) * h_in[None]
              + b_scr[rs, :].reshape(nv, SUBLANES, RNN_WIDTH))
        b_scr[rs, :] = hh.reshape(half, RNN_WIDTH)
        return jnp.broadcast_to(h_end[SUBLANES - 1:SUBLANES, :], (SUBLANES, RNN_WIDTH))

    def out_proj(h0):
        rs = slice(h0, h0 + half)
        st = {}

        def gate_output():
            st["yp"] = (b_scr[rs, :] * _gelu_tanh(gate_scr[rs, :])).astype(BF16)

        def unpermute_rows():
            st["y"] = _dot(unperm_ref[...], st["yp"]).astype(BF16)

        def project():
            z = ALPHA * x_ref[rs, :] + _dot(st["y"], wout_ref[...])
            o_ref[rs, :] = _layer_norm(z, g_ref[...], b_ref[...])

        return [gate_output, unpermute_rows, project]

    _emit_round_robin(in_proj(0, 0))
    _emit_round_robin(gates(0, 0), in_proj(1, half))
    h_mid = recur(0, h_scr[...])
    _emit_round_robin(gates(1, half), out_proj(0))
    h_scr[...] = recur(half, h_mid)
    _emit_round_robin(out_proj(half))


def _odd_layer(xf, posi, w_in, cw, cb, wax, ba, bx, lam, w_out, g, b, batch, seq):
    n = xf.shape[0]
    tm = ROW_TILE
    nt = seq // tm
    row = lambda bi, i: (bi * nt + i, 0)
    full = lambda bi, i: (0, 0)
    half = tm // 2
    rho = np.arange(half)
    src_time = (rho % SUBLANES) * (half // SUBLANES) + rho // SUBLANES
    perm = np.zeros((half, half), np.float32)
    perm[rho, src_time] = 1.0
    consts = [jnp.asarray(perm, BF16), jnp.asarray(perm.T, BF16), w_in, cw, cb]
    rest = [ba, bx, lam, w_out, g, b]
    return pl.pallas_call(
        _odd_kernel,
        grid=(batch, nt),
        in_specs=[pl.BlockSpec((tm, D_MODEL), row), pl.BlockSpec((tm, 1), row)]
                 + [pl.BlockSpec(a.shape, full) for a in consts]
                 + [pl.BlockSpec(wax.shape, lambda bi, i: (0, 0, 0))]
                 + [pl.BlockSpec(a.shape, full) for a in rest],
        out_specs=pl.BlockSpec((tm, D_MODEL), row),
        out_shape=jax.ShapeDtypeStruct((n, D_MODEL), F32),
        scratch_shapes=[pltpu.VMEM((2, (CONV_WIDTH - 1) * SUBLANES + tm // 2, RNN_WIDTH), F32),
                        pltpu.VMEM((tm, RNN_WIDTH), F32),
                        pltpu.VMEM((tm, RNN_WIDTH), F32),
                        pltpu.VMEM((tm, RNN_WIDTH), F32),
                        pltpu.VMEM((SUBLANES, RNN_WIDTH), F32),
                        pltpu.VMEM((SUBLANES, RNN_WIDTH), F32)],
        compiler_params=_params(2),
        name="odd_layer",
    )(xf, posi, *consts, wax, *rest)


def _even_mixer(xf, positions, w_in, pool_w, pool_scale, cmp_pos_k, cmp_pos_v, cmp_wk, cmp_wv,
                w_out, casts, batch, seq):
    assert seq // SEL_BLOCK == LANES and seq % SEL_KEY_TILE == 0 and seq >= WINDOW + Q_BLOCK
    hd, g, r = NSA_HEAD_DIM, NSA_KV_HEADS, NSA_GROUP
    c0 = POOL_WIDTH
    c1 = c0 + Q_WIDTH
    c2 = c1 + GATE_WIDTH
    wp = w_in[:, :c0].astype(BF16)
    wq = (w_in[:, c0:c1].reshape(D_MODEL, g, r, hd).transpose(0, 2, 1, 3).reshape(D_MODEL, Q_WIDTH)
          * (hd ** -0.5 * LOG2E)).astype(BF16)
    wg = jnp.pad(w_in[:, c1:c2], ((0, 0), (0, LANES - GATE_WIDTH))).astype(BF16)
    wkv = w_in[:, c2:].astype(BF16)
    n = batch * seq
    nc = seq // CMP_STRIDE
    n_cmp = (seq - CMP_BLOCK) // CMP_STRIDE + 1
    posc = jnp.pad(positions[:, CMP_BLOCK - 1::CMP_STRIDE][:, :n_cmp], ((0, 0), (0, nc - n_cmp)))
    pos_all = jnp.concatenate([positions.reshape(1, n), posc.reshape(1, batch * nc)], axis=1).astype(F32)
    cos_t, sin_t = _rope_angles(pos_all)
    ec, es = _rope_spread()
    pool, q, kc_raw, vc_raw, ks, vs, kw, vw, gates = _proj_even(
        xf, cos_t, sin_t, ec, es, wp, wq, wkv, wg, pool_w.astype(BF16), pool_scale[None, :], batch, seq)

    def halves(w):
        w3 = w.reshape(CMP_BLOCK, hd, hd).astype(BF16)
        zero = jnp.zeros_like(w3)
        w4 = jnp.concatenate([jnp.concatenate([w3 if k == j else zero for k in range(g)], axis=2)
                              for j in range(g)], axis=1)
        return (w4[:CMP_STRIDE].reshape(CMP_STRIDE * g * hd, g * hd),
                w4[CMP_STRIDE:].reshape(CMP_STRIDE * g * hd, g * hd))

    def pos_halves(p):
        p2 = jnp.tile(p[:, None, :], (1, g, 1)).reshape(CMP_BLOCK, g * hd)
        return p2[:CMP_STRIDE].reshape(1, -1), p2[CMP_STRIDE:].reshape(1, -1)

    pk1, pk2 = pos_halves(cmp_pos_k)
    pv1, pv2 = pos_halves(cmp_pos_v)
    wk1, wk2 = halves(cmp_wk)
    wv1, wv2 = halves(cmp_wv)
    seq3 = lambda a: a.reshape(batch, seq, KV_WIDTH)
    kc, vc = _compress(seq3(kc_raw), seq3(vc_raw), cos_t, sin_t, n // nc,
                       (ec, es, pk1, pk2, pv1, pv2, wk1, wk2, wv1, wv2))

    n_sb = seq // SEL_BLOCK
    starts = np.arange(nc) * CMP_STRIDE
    jb = np.arange(n_sb)
    ov = ((starts[:, None] < (jb[None, :] + 1) * SEL_BLOCK)
          & (starts[:, None] + CMP_BLOCK > jb[None, :] * SEL_BLOCK) & (np.arange(nc)[:, None] < n_cmp))
    ov = jnp.asarray(ov, BF16)
    pt = jnp.asarray(np.where(np.arange(seq)[:, None] // SEL_BLOCK == jb[None, :], NEG, 0.0), BF16)

    nqb = seq // Q_BLOCK
    nsa_args = (q, gates, kc, vc, seq3(ks), seq3(vs), seq3(kw), seq3(vw), pt, ov)
    nsa, cast_a = _nsa(*nsa_args, casts[:len(casts) // 2], batch, seq, 0, nqb // 2)
    nsa, cast_b = _nsa(*nsa_args, casts[len(casts) // 2:], batch, seq, nqb // 2, nqb - nqb // 2, prev_out=nsa)
    cast_weights = cast_a + cast_b

    w_pool_out = w_out[:POOL_WIDTH].astype(BF16)
    w_nsa_out = (w_out[POOL_WIDTH:].reshape(g, r, hd, D_MODEL).transpose(1, 0, 2, 3)
                 .reshape(Q_WIDTH, D_MODEL).astype(BF16))
    return [pool, nsa], [w_pool_out, w_nsa_out], cast_weights


def _odd_mixer(xf, posi, w_in, conv_w, conv_b, wa, ba, wx, bx, lam, w_out, ln_g, ln_b, batch, seq):
    wax = (0.5 * jnp.concatenate([wa, wx], axis=2)).astype(BF16)
    return _odd_layer(xf, posi, w_in, conv_w, conv_b[None, :], wax, 0.5 * ba[None, :],
                      0.5 * bx[None, :], lam[None, :], w_out, ln_g, ln_b, batch, seq)


def kernel(x, mem, positions, e_w_in, e_pool_w, e_pool_scale, e_cmp_pos_k, e_cmp_pos_v, e_cmp_wk, e_cmp_wv, e_w_out, o_w_in, o_conv_w, o_conv_b, o_wa, o_ba, o_wx, o_bx, o_lambda, o_w_out, x_wq, x_wkv, x_wo, f_w_up, f_w_down, ln_g, ln_b):
    batch, seq, d = x.shape
    n = batch * seq
    assert d == D_MODEL and seq % ROW_TILE == 0
    xf = x.reshape(n, d)
    posi = positions.reshape(n, 1)
    memf = mem.reshape(batch * mem.shape[1], d)
    casts = [(x_wq, X_HEAD_DIM ** -0.5 * LOG2E), (x_wo, 1.0), (f_w_up, 1.0), (f_w_down, 1.0),
             (o_w_in, 1.0), (o_w_out, 1.0)]
    for layer in range(DEPTH):
        j = layer // 2
        if layer % 2 == 0:
            parts, ws, cast_weights = _even_mixer(
                xf, positions, e_w_in[j], e_pool_w[j], e_pool_scale[j], e_cmp_pos_k[j], e_cmp_pos_v[j],
                e_cmp_wk[j], e_cmp_wv[j], e_w_out[j], casts if layer == 0 else [], batch, seq)
            if layer == 0:
                wq_all, wo_all, wu_all, wd_all, o_w_in16, o_w_out16 = cast_weights
        else:
            xf = _odd_mixer(xf, posi, o_w_in16[j], o_conv_w[j], o_conv_b[j], o_wa[j], o_ba[j], o_wx[j], o_bx[j],
                            o_lambda[j], o_w_out16[j], ln_g[layer, 0][None, :], ln_b[layer, 0][None, :],
                            batch, seq)
            parts, ws = [], []
        kv = _mem_kv(memf, x_wkv, layer).reshape(batch, mem.shape[1], 2 * d)
        xf = _tail(xf, parts, ws, kv, wq_all, wo_all, wu_all, wd_all, ln_g, ln_b, layer, batch, seq)
    return xf.reshape(batch, seq, d)
```

```python
import functools

import numpy as np
import jax
import jax.numpy as jnp
from jax import lax
from jax.experimental import pallas as pl
from jax.experimental.pallas import tpu as pltpu

F32 = jnp.float32
BF16 = jnp.bfloat16

D_MODEL = 1024
DEPTH = 2
ALPHA = (2.0 * DEPTH) ** 0.25
LN_EPS = 1e-5
NEG = -1e30
POOL_WIDTH = D_MODEL // 2
POOL_WINDOWS = (2, 4, 8, 16)
POOL_GROUP = POOL_WIDTH // len(POOL_WINDOWS)
POOL_HALO = 16
NSA_HEADS = 8
NSA_KV_HEADS = 2
NSA_HEAD_DIM = 64
NSA_GROUP = NSA_HEADS // NSA_KV_HEADS
CMP_BLOCK = 32
CMP_STRIDE = 16
SEL_BLOCK = 64
SEL_COUNT = 16
WINDOW = 512
Q_BLOCK = 256
N_BRANCH = 3
N_FORCED = 3
LOG2E = 1.4426950408889634
ROPE_THETA = 500000.0
ROT_DIM = NSA_HEAD_DIM // 4
ROT_HALF = ROT_DIM // 2
Q_WIDTH = NSA_HEADS * NSA_HEAD_DIM
KV_WIDTH = NSA_KV_HEADS * NSA_HEAD_DIM
GATE_WIDTH = NSA_HEADS * N_BRANCH
RNN_WIDTH = 1280
RNN_BLOCKS = 10
RNN_BLOCK_W = RNN_WIDTH // RNN_BLOCKS
CONV_WIDTH = 4
LRU_C = 8.0
X_HEADS = 4
X_HEAD_DIM = D_MODEL // X_HEADS
FFN_HIDDEN = 2816
GELU_C0 = 0.7978845608028654
GELU_C1 = GELU_C0 * 0.044715

LANES = 128
SUBLANES = 8
VMEM_LIMIT = 56 * 1024 * 1024
ROW_TILE = 512
SEL_KEY_TILE = 512
FFN_CHUNKS = (256,) * 11


def _params(n_axes, vmem=VMEM_LIMIT):
    return pltpu.CompilerParams(dimension_semantics=("arbitrary",) * n_axes,
                                vmem_limit_bytes=vmem)


def _resident(shape, index_map):
    return pl.BlockSpec(shape, index_map, pipeline_mode=pl.Buffered(1))


def _emit_round_robin(*stage_lists):
    lists = [list(s) for s in stage_lists]
    while any(lists):
        for s in lists:
            if s:
                s.pop(0)()


def _dot(a, b):
    return jnp.dot(a, b, preferred_element_type=F32)


def _dot_nt(a, b):
    return lax.dot_general(a, b, (((1,), (1,)), ((), ())), preferred_element_type=F32)


def _rep_rows(a, k):
    return jnp.concatenate([a] * k, axis=0)


def _rep_lanes(a, k):
    return jnp.concatenate([a] * k, axis=1) if k > 1 else a


def _layer_norm(z, g, b):
    mu = jnp.mean(z, axis=-1, keepdims=True)
    d = z - mu
    var = jnp.mean(d * d, axis=-1, keepdims=True)
    return d * lax.rsqrt(var + LN_EPS) * g + b


def _rope(v, cos, sin):
    k = v.shape[1] // LANES
    up = pltpu.roll(v, v.shape[1] - ROT_HALF, 1)
    dn = pltpu.roll(v, ROT_HALF, 1)
    lane = lax.broadcasted_iota(jnp.int32, v.shape, 1)
    partner = jnp.where((lane & (NSA_HEAD_DIM - 1)) < ROT_HALF, up, dn)
    return v * _rep_lanes(cos, k) + partner * _rep_lanes(sin, k)


def _rope_angle_kernel(pos_ref, inv_ref, cos_o, sin_o):
    ang = inv_ref[...] * pos_ref[...]
    cos_o[...] = jnp.cos(ang)
    sin_o[...] = jnp.sin(ang)


def _rope_angles(pos_row):
    n = pos_row.shape[1]
    inv = (ROPE_THETA ** (-jnp.arange(ROT_HALF, dtype=F32) * 2.0 / ROT_DIM))[:, None]
    out = jax.ShapeDtypeStruct((ROT_HALF, n), F32)
    return pl.pallas_call(
        _rope_angle_kernel,
        out_shape=[out, out],
        name="rope_angles",
    )(pos_row, inv)


def _rope_spread():
    lane = np.arange(LANES) % NSA_HEAD_DIM
    f = np.arange(ROT_HALF)[:, None]
    lo = (lane[None, :] == f).astype(np.float32)
    hi = (lane[None, :] == f + ROT_HALF).astype(np.float32)
    return jnp.asarray(lo + hi, BF16), jnp.asarray(hi - lo, BF16)


def _rope_patterns(cos_t, sin_t, ec, es):
    def spread(t, e):
        hi = t.astype(BF16)
        r1 = t - hi.astype(F32)
        mid = r1.astype(BF16)
        lo = (r1 - mid.astype(F32)).astype(BF16)
        tn = (((0,), (0,)), ((), ()))
        return sum(lax.dot_general(p, e, tn, preferred_element_type=F32) for p in (hi, mid, lo))

    lane = lax.broadcasted_iota(jnp.int32, (1, LANES), 1)
    unrotated = jnp.where((lane & (NSA_HEAD_DIM - 1)) >= ROT_DIM, 1.0, 0.0)
    return spread(cos_t, ec) + unrotated, spread(sin_t, es)


def _proj_even_kernel(x_ref, cos_ref, sin_ref, ec_ref, es_ref, wp_ref, wq_ref, wkv_ref, wg_ref, poolw_ref, pscale_ref,
                      pool_o, q_o, kc_o, vc_o, ks_o, vs_o, kw_o, vw_o, gate_o, pbuf):
    i = pl.program_id(1)
    tm = x_ref.shape[0]
    half = tm // 2

    @pl.when(i == 0)
    def _():
        pbuf[0:POOL_HALO, :] = jnp.zeros((POOL_HALO, POOL_WIDTH), F32)

    def project(h0, st):
        rs = slice(h0, h0 + half)

        def cast():
            st["xb"] = x_ref[rs, :].astype(BF16)

        def pool_in():
            pbuf[POOL_HALO + h0:POOL_HALO + h0 + half, :] = _dot(st["xb"], wp_ref[...])

        return [cast, pool_in,
                lambda: st.__setitem__("q", _dot(st["xb"], wq_ref[...])),
                lambda: st.__setitem__("kv", _dot(st["xb"], wkv_ref[...])),
                lambda: st.__setitem__("gl", _dot(st["xb"], wg_ref[...]))]

    def emit(h0, st):
        rs = slice(h0, h0 + half)

        def patterns():
            st["cos"], st["sin"] = _rope_patterns(cos_ref[:, rs], sin_ref[:, rs], ec_ref[...], es_ref[...])

        def queries():
            q_o[rs, :] = _rope(st["q"], st["cos"], st["sin"]).astype(BF16)

        def keys_values():
            kv, cos, sin = st["kv"], st["cos"], st["sin"]
            kc_o[rs, :] = kv[:, 0 * LANES:1 * LANES]
            vc_o[rs, :] = kv[:, 1 * LANES:2 * LANES]
            ks_o[rs, :] = _rope(kv[:, 2 * LANES:3 * LANES], cos, sin).astype(BF16)
            vs_o[rs, :] = kv[:, 3 * LANES:4 * LANES].astype(BF16)
            kw_o[rs, :] = _rope(kv[:, 4 * LANES:5 * LANES], cos, sin).astype(BF16)
            vw_o[rs, :] = kv[:, 5 * LANES:6 * LANES].astype(BF16)
            gate_o[rs, :] = _sigmoid_tanh(st["gl"])

        return [patterns, queries, keys_values]

    def pool(h0):
        rs = slice(h0, h0 + half)
        t1 = i * tm + h0 + lax.broadcasted_iota(jnp.int32, (half, 1), 0) + 1

        def group(g, w):
            sl = slice(g * POOL_GROUP, (g + 1) * POOL_GROUP)
            u = pbuf[POOL_HALO + h0:POOL_HALO + h0 + half, sl]
            tot = u
            for j in range(1, w):
                tot = tot + pbuf[pl.ds(POOL_HALO + h0 - j, half), sl]
            cnt = jnp.minimum(t1, w).astype(F32)
            pooled = tot / cnt - u
            mixed = _dot(pooled.astype(BF16), poolw_ref[g]) * pscale_ref[:, sl]
            pool_o[rs, sl] = mixed.astype(BF16)

        return [functools.partial(group, g, w) for g, w in enumerate(POOL_WINDOWS)]

    first, second = {}, {}
    _emit_round_robin(project(0, first))
    _emit_round_robin(emit(0, first) + pool(0), project(half, second))
    _emit_round_robin(emit(half, second) + pool(half))
    pbuf[0:POOL_HALO, :] = pbuf[tm:tm + POOL_HALO, :]


def _proj_even(xf, cos_t, sin_t, ec, es, wp, wq, wkv, wg, poolw, pscale, batch, seq):
    n = xf.shape[0]
    tm = ROW_TILE
    nt = seq // tm
    row = lambda b, i: (b * nt + i, 0)
    full = lambda b, i: (0, 0)
    outs = [
        jax.ShapeDtypeStruct((n, POOL_WIDTH), BF16),
        jax.ShapeDtypeStruct((n, Q_WIDTH), BF16),
        jax.ShapeDtypeStruct((n, KV_WIDTH), F32),
        jax.ShapeDtypeStruct((n, KV_WIDTH), F32),
        jax.ShapeDtypeStruct((n, KV_WIDTH), BF16),
        jax.ShapeDtypeStruct((n, KV_WIDTH), BF16),
        jax.ShapeDtypeStruct((n, KV_WIDTH), BF16),
        jax.ShapeDtypeStruct((n, KV_WIDTH), BF16),
        jax.ShapeDtypeStruct((n, LANES), F32),
    ]
    return pl.pallas_call(
        _proj_even_kernel,
        grid=(batch, nt),
        in_specs=[
            pl.BlockSpec((tm, D_MODEL), row),
            pl.BlockSpec((ROT_HALF, tm), lambda b, i: (0, b * nt + i)),
            pl.BlockSpec((ROT_HALF, tm), lambda b, i: (0, b * nt + i)),
            pl.BlockSpec(ec.shape, full),
            pl.BlockSpec(es.shape, full),
            pl.BlockSpec(wp.shape, full),
            pl.BlockSpec(wq.shape, full),
            pl.BlockSpec(wkv.shape, full),
            pl.BlockSpec(wg.shape, full),
            pl.BlockSpec(poolw.shape, lambda b, i: (0, 0, 0)),
            pl.BlockSpec(pscale.shape, full),
        ],
        out_specs=[pl.BlockSpec((tm, o.shape[1]), row) for o in outs],
        out_shape=outs,
        scratch_shapes=[pltpu.VMEM((POOL_HALO + tm, POOL_WIDTH), F32)],
        compiler_params=_params(2),
        name="proj_even",
    )(xf, cos_t, sin_t, ec, es, wp, wq, wkv, wg, poolw, pscale)


def _compress_kernel(rk_ref, rv_ref, cos_ref, sin_ref, ec_ref, es_ref, pk1_ref, pk2_ref, pv1_ref, pv2_ref,
                     wk1_ref, wk2_ref, wv1_ref, wv2_ref, kc_o, vc_o):
    nc = kc_o.shape[0]

    def compress(raw_ref, p1, p2, w1, w2):
        r = jnp.concatenate([raw_ref[pl.ds(j, nc, stride=CMP_STRIDE), :] for j in range(CMP_STRIDE)], axis=1)
        a = _dot((r + p1[...]).astype(BF16), w1[...])
        b = _dot((r + p2[...]).astype(BF16), w2[...])
        return a + pltpu.roll(b, nc - 1, 0)

    kc = compress(rk_ref, pk1_ref, pk2_ref, wk1_ref, wk2_ref)
    cos, sin = _rope_patterns(cos_ref[...], sin_ref[...], ec_ref[...], es_ref[...])
    kc_o[...] = _rope(kc, cos, sin).astype(BF16)
    vc_o[...] = compress(rv_ref, pv1_ref, pv2_ref, wv1_ref, wv2_ref).astype(BF16)


def _compress(rk, rv, cos_t, sin_t, first_col_block, consts):
    batch, seq, width = rk.shape
    nc = seq // CMP_STRIDE
    blk = lambda b: (b, 0, 0)
    full = lambda b: (0, 0)
    out = jax.ShapeDtypeStruct((batch, nc, KV_WIDTH), BF16)
    return pl.pallas_call(
        _compress_kernel,
        grid=(batch,),
        in_specs=[pl.BlockSpec((None, seq, width), blk), pl.BlockSpec((None, seq, width), blk),
                  pl.BlockSpec((ROT_HALF, nc), lambda b: (0, first_col_block + b)),
                  pl.BlockSpec((ROT_HALF, nc), lambda b: (0, first_col_block + b))]
                 + [pl.BlockSpec(a.shape, full) for a in consts],
        out_specs=[pl.BlockSpec((None, nc, KV_WIDTH), blk)] * 2,
        out_shape=[out, out],
        compiler_params=_params(1),
        name="compress",
    )(rk, rv, cos_t, sin_t, *consts)


def _softmax_rows(s):
    m = jnp.max(s, axis=1, keepdims=True)
    e = jnp.exp2(s - m)
    return e, jnp.sum(e, axis=1, keepdims=True)


def _dot_exact01(x, m01):
    hi = x.astype(BF16)
    r1 = x - hi.astype(F32)
    mid = r1.astype(BF16)
    lo = (r1 - mid.astype(F32)).astype(BF16)
    return _dot(hi, m01) + _dot(mid, m01) + _dot(lo, m01)


def _unselected_blocks(imp, q0, n_pick):
    nq, nb = imp.shape
    col = lax.broadcasted_iota(jnp.int32, (nq, nb), 1)
    t_row = q0 + lax.broadcasted_iota(jnp.int32, (nq, nb), 0)
    cur = lax.shift_right_arithmetic(t_row, SEL_BLOCK.bit_length() - 1)
    forced = (col == 0) | (col == cur) | (col == cur - 1)
    val = jnp.where(forced, -jnp.inf, jnp.where(col > cur, -1.0, imp))
    vt = val.T
    blk = lax.broadcasted_iota(jnp.int32, (nb, nq), 0).astype(F32)
    for _ in range(n_pick - N_FORCED):
        m = jnp.max(vt, axis=0, keepdims=True)
        first = jnp.min(jnp.where(vt == m, blk, float(nb)), axis=0, keepdims=True)
        vt = jnp.where(blk == first, -jnp.inf, vt)
    return jnp.where(vt == -jnp.inf, 0.0, 1.0).T


def _nsa_kernel(*refs, n_pick, cast_scales, first_block, n_extra):
    n_cast = len(cast_scales)
    q_ref, gate_ref, kc_ref, vc_ref, ks_ref, vs_ref, kw_ref, vw_ref, pt_ref, ov_ref = refs[:10]
    cast_in = refs[10:10 + n_cast]
    n_in = 10 + n_cast + n_extra
    o_ref = refs[n_in]
    cast_out = refs[n_in + 1:n_in + 1 + n_cast]
    m_scr, l_scr, acc_scr, lhs_scr, oc_scr, ow_scr, s0_scr, s1_scr, w_scr, gx_scr = refs[n_in + 1 + n_cast:]

    for src, dst, scale in zip(cast_in, cast_out, cast_scales):
        w = src[...] if scale == 1.0 else src[...] * scale
        dst[...] = w.astype(BF16)

    qb = pl.program_id(1) + first_block
    q0 = qb * Q_BLOCK
    tk = SEL_KEY_TILE
    ncmp = kc_ref.shape[0]
    wk = WINDOW + Q_BLOCK
    rows = NSA_GROUP * Q_BLOCK
    lane = lax.broadcasted_iota(jnp.int32, (Q_BLOCK, LANES), 1)

    def trow(width):
        return q0 + lax.broadcasted_iota(jnp.int32, (Q_BLOCK, width), 0)

    def kcol(width):
        return lax.broadcasted_iota(jnp.int32, (Q_BLOCK, width), 1)

    bias_c = jnp.where(kcol(ncmp) * CMP_STRIDE + (CMP_BLOCK - 1) <= trow(ncmp), 0.0, NEG)
    has_cmp = (trow(1) >= CMP_BLOCK - 1).astype(F32)
    wstart = pl.multiple_of(jnp.maximum(q0 - WINDOW, 0), Q_BLOCK)
    kpos_w = wstart + kcol(wk)
    bias_w = jnp.where(kpos_w <= trow(wk), jnp.where(kpos_w > trow(wk) - WINDOW, 0.0, NEG), NEG)

    for g in range(NSA_KV_HEADS):
        mine = (lane >= NSA_HEAD_DIM) if g else (lane < NSA_HEAD_DIM)
        lhs_scr[g, :, 0:LANES] = jnp.concatenate(
            [jnp.where(mine, q_ref[:, r * LANES:(r + 1) * LANES], jnp.zeros((), BF16))
             for r in range(NSA_GROUP)], axis=0)
        m_scr[g] = jnp.full((rows, LANES), -jnp.inf, F32)
        l_scr[g] = jnp.zeros((rows, LANES), F32)
        acc_scr[g] = jnp.zeros((rows, LANES), F32)

    def cmp_scores(g):
        s0_scr[g, :, 0:ncmp] = _dot_nt(lhs_scr[g, :, 0:LANES], kc_ref[...])

    def cmp_attend(g):
        e, l = _softmax_rows(s0_scr[g, :, 0:ncmp] + _rep_rows(bias_c, NSA_GROUP))
        p = e * (_rep_rows(has_cmp, NSA_GROUP) / l)
        oc_scr[g] = _dot(p.astype(BF16), vc_ref[...])
        psum = p[0:Q_BLOCK]
        for r in range(1, NSA_GROUP):
            psum = psum + p[r * Q_BLOCK:(r + 1) * Q_BLOCK]
        return _dot_exact01(psum, ov_ref[...])

    def select(g, imp):
        unsel = _unselected_blocks(imp, q0, n_pick)
        lhs_scr[g, :, LANES:2 * LANES] = _rep_rows(unsel.astype(BF16), NSA_GROUP)

    def win_scores(g):
        w_scr[g] = _dot_nt(lhs_scr[g, :, 0:LANES], kw_ref[pl.ds(wstart, wk), :])

    def win_attend(g):
        e, l = _softmax_rows(w_scr[g] + _rep_rows(bias_w, NSA_GROUP))
        ow_scr[g] = _dot(e.astype(BF16), vw_ref[pl.ds(wstart, wk), :]) / l

    def expand_gates():
        low = lane < NSA_HEAD_DIM
        for r in range(NSA_GROUP):
            for br in range(N_BRANCH):
                c_lo = r * N_BRANCH + br
                c_hi = (NSA_GROUP + r) * N_BRANCH + br
                gx_scr[c_lo] = jnp.where(low, gate_ref[:, c_lo:c_lo + 1], gate_ref[:, c_hi:c_hi + 1])

    cmp_scores(0)
    cmp_scores(1)
    expand_gates()
    win_scores(0)
    imp0 = cmp_attend(0)
    win_scores(1)
    imp1 = cmp_attend(1)
    select(0, imp0)
    win_attend(0)
    select(1, imp1)
    win_attend(1)

    def scores(kt, buf, g):
        k0 = pl.multiple_of(kt * tk, tk)
        rhs = jnp.concatenate([ks_ref[pl.ds(k0, tk), :], pt_ref[pl.ds(k0, tk), :]], axis=1)
        buf[g] = _dot_nt(lhs_scr[g], rhs)

    def consume(kt, buf, g, causal):
        k0 = pl.multiple_of(kt * tk, tk)
        v = vs_ref[pl.ds(k0, tk), :]
        bias = jnp.where(k0 + kcol(tk) <= trow(tk), 0.0, NEG) if causal else None
        for r in range(NSA_GROUP):
            rs = slice(r * Q_BLOCK, (r + 1) * Q_BLOCK)
            s = buf[g, rs, :]
            if causal:
                s = s + bias
            m_prev = m_scr[g, rs, :]
            m_new = jnp.maximum(m_prev, jnp.max(s, axis=1, keepdims=True))
            alpha = jnp.exp2(m_prev - m_new)
            pe = jnp.exp2(s - _rep_lanes(m_new, tk // LANES))
            part = pe[:, 0:LANES]
            for c in range(1, tk // LANES):
                part = part + pe[:, c * LANES:(c + 1) * LANES]
            l_scr[g, rs, :] = alpha * l_scr[g, rs, :] + part
            acc_scr[g, rs, :] = alpha * acc_scr[g, rs, :] + _dot(pe.astype(BF16), v)
            m_scr[g, rs, :] = m_new

    def step(nxt, nxt_buf, cur, cur_buf, causal=False):
        for g in range(NSA_KV_HEADS):
            if nxt is not None:
                scores(nxt, nxt_buf, g)
        for g in range(NSA_KV_HEADS):
            consume(cur, cur_buf, g, causal)

    n_full = q0 // tk
    for g in range(NSA_KV_HEADS):
        scores(0, s0_scr, g)

    def pair(j, carry):
        step(2 * j + 1, s1_scr, 2 * j, s0_scr)
        step(2 * j + 2, s0_scr, 2 * j + 1, s1_scr)
        return carry

    lax.fori_loop(0, n_full // 2, pair, 0)

    @pl.when(n_full % 2 == 0)
    def _():
        step(None, None, n_full, s0_scr, causal=True)

    @pl.when(n_full % 2 == 1)
    def _():
        step(n_full, s1_scr, n_full - 1, s0_scr)
        step(None, None, n_full, s1_scr, causal=True)

    low = lane < NSA_HEAD_DIM
    for r in range(NSA_GROUP):
        rs = slice(r * Q_BLOCK, (r + 1) * Q_BLOCK)
        merged = lambda scr: jnp.where(low, scr[0, rs, :], scr[1, rs, :])
        l_sel = jnp.where(low, jnp.sum(l_scr[0, rs, :], axis=1, keepdims=True),
                          jnp.sum(l_scr[1, rs, :], axis=1, keepdims=True))
        out = (gx_scr[r * N_BRANCH] * merged(oc_scr)
               + gx_scr[r * N_BRANCH + 1] * (merged(acc_scr) / l_sel)
               + gx_scr[r * N_BRANCH + 2] * merged(ow_scr))
        o_ref[:, r * LANES:(r + 1) * LANES] = out.astype(BF16)


def _nsa(q, gates, kc, vc, ks, vs, kw, vw, pt, ov, casts, batch, seq, first_block, n_blocks, prev_out=None):
    n = q.shape[0]
    nqb = seq // Q_BLOCK
    steps = batch * n_blocks
    rows = NSA_GROUP * Q_BLOCK
    row = lambda b, i: (b * nqb + first_block + i, 0)
    per_b = lambda b, i: (b, 0, 0)
    full = lambda b, i: (0, 0)
    seq_spec = _resident((None, seq, KV_WIDTH), per_b)
    last_t = (first_block + n_blocks) * Q_BLOCK - 1
    n_visible = (last_t - (CMP_BLOCK - 1)) // CMP_STRIDE + 1
    ncmp = min(kc.shape[1], -(-n_visible // (2 * LANES)) * (2 * LANES))
    cmp_spec = _resident((None, ncmp, KV_WIDTH), per_b)
    n_pick = min(SEL_COUNT, seq // SEL_BLOCK)
    sliced = [w.reshape(steps, -1, w.shape[-1]) for w, _ in casts]
    slice_specs = [pl.BlockSpec((None,) + w.shape[1:], lambda b, i: (b * n_blocks + i, 0, 0)) for w in sliced]
    extra = [] if prev_out is None else [prev_out]
    outs = pl.pallas_call(
        functools.partial(_nsa_kernel, n_pick=n_pick, cast_scales=tuple(s for _, s in casts),
                          first_block=first_block, n_extra=len(extra)),
        grid=(batch, n_blocks),
        in_specs=[pl.BlockSpec((Q_BLOCK, Q_WIDTH), row), pl.BlockSpec((Q_BLOCK, LANES), row),
                  cmp_spec, cmp_spec, seq_spec, seq_spec, seq_spec, seq_spec,
                  _resident(pt.shape, full), _resident((ncmp, ov.shape[1]), full)] + slice_specs
                 + [pl.BlockSpec(memory_space=pl.ANY)] * len(extra),
        out_specs=[pl.BlockSpec((Q_BLOCK, Q_WIDTH), row)] + slice_specs,
        out_shape=[jax.ShapeDtypeStruct((n, Q_WIDTH), BF16)]
                  + [jax.ShapeDtypeStruct(w.shape, BF16) for w in sliced],
        input_output_aliases={10 + len(sliced): 0} if extra else {},
        scratch_shapes=[pltpu.VMEM((NSA_KV_HEADS, rows, LANES), F32),
                        pltpu.VMEM((NSA_KV_HEADS, rows, LANES), F32),
                        pltpu.VMEM((NSA_KV_HEADS, rows, LANES), F32),
                        pltpu.VMEM((NSA_KV_HEADS, rows, 2 * LANES), BF16),
                        pltpu.VMEM((NSA_KV_HEADS, rows, LANES), F32),
                        pltpu.VMEM((NSA_KV_HEADS, rows, LANES), F32),
                        pltpu.VMEM((NSA_KV_HEADS, rows, SEL_KEY_TILE), F32),
                        pltpu.VMEM((NSA_KV_HEADS, rows, SEL_KEY_TILE), F32),
                        pltpu.VMEM((NSA_KV_HEADS, rows, WINDOW + Q_BLOCK), F32),
                        pltpu.VMEM((NSA_GROUP * N_BRANCH, Q_BLOCK, LANES), F32)],
        compiler_params=_params(2),
        name="nsa",
    )(q, gates, kc, vc, ks, vs, kw, vw, pt, ov, *sliced, *extra)
    return outs[0], [o.reshape(w.shape) for o, (w, _) in zip(outs[1:], casts)]


def _tail_kernel(*refs, n_parts, tiles_per_batch):
    x_ref = refs[0]
    parts = refs[1:1 + n_parts]
    ws = refs[1 + n_parts:1 + 2 * n_parts]
    (wq_ref, mem_ref, wkv_ref, wo_ref, wu_ref, wd_ref, g_ref, b_ref, o_ref,
     x1_scr, x2_scr, x2_prev, kv_scr) = refs[1 + 2 * n_parts:]
    t = pl.program_id(0)
    n_tiles = pl.num_programs(0) - 1

    @pl.when((t < n_tiles) & (t % tiles_per_batch == 0))
    def _():
        kv_scr[...] = _dot(mem_ref[...].astype(BF16), wkv_ref[...]).astype(BF16)
    tm = x_ref.shape[0]
    half = tm // 2
    halves = (slice(0, half), slice(half, tm))
    x1 = x1_scr if n_parts else x_ref

    def norm(k, z):
        return _layer_norm(z, g_ref[k:k + 1, :], b_ref[k:k + 1, :])

    def mixer_out(rs):
        z = ALPHA * x_ref[rs, :]
        for a, w in zip(parts, ws):
            z = z + _dot(a[rs, :], w[...])
        x1_scr[rs, :] = norm(0, z)

    def project(rs):
        return _dot(x1[rs, :].astype(BF16), wq_ref[...]).astype(BF16)

    heads = [slice(h * X_HEAD_DIM, (h + 1) * X_HEAD_DIM) for h in range(X_HEADS)]

    def scores(q):
        return [_dot_nt(q[:, sl], kv_scr[:, sl]) for sl in heads]

    def probabilities(ss):
        probs = []
        for s in ss:
            e, l = _softmax_rows(s)
            probs.append((e / l).astype(BF16))
        return probs

    def mix(probs):
        outs = [_dot(p, kv_scr[:, D_MODEL + sl.start:D_MODEL + sl.stop]) for p, sl in zip(probs, heads)]
        return jnp.concatenate(outs, axis=1).astype(BF16)

    def finish(rs, o):
        x2_scr[rs, :] = norm(1, ALPHA * x1[rs, :] + _dot(o, wo_ref[...]))

    def attention_stages():
        st = {}
        stages = []
        both = range(len(halves))
        if n_parts:
            stages += [functools.partial(mixer_out, h) for h in halves]
        stages += [lambda i=i: st.__setitem__(("q", i), project(halves[i])) for i in both]
        stages += [lambda i=i: st.__setitem__(("s", i), scores(st["q", i])) for i in both]
        stages += [lambda i=i: st.__setitem__(("p", i), probabilities(st["s", i])) for i in both]
        stages += [lambda i=i: st.__setitem__(("o", i), mix(st["p", i])) for i in both]
        stages += [lambda i=i: finish(halves[i], st["o", i]) for i in both]
        return stages

    def ffn_stages(src, out):
        st = {}

        def start():
            x2 = src[...]
            st.update(xb=x2.astype(BF16), z=ALPHA * x2)

        def chunk(c0, width):
            h1 = _dot(st["xb"], wu_ref[:, c0:c0 + width])
            h2 = _dot(st["xb"], wu_ref[:, FFN_HIDDEN + c0:FFN_HIDDEN + c0 + width])
            act = (h1 * _sigmoid_tanh(h1) * h2).astype(BF16)
            st["z"] = st["z"] + _dot(act, wd_ref[c0:c0 + width, :])

        stages = [start]
        c0 = 0
        for width in FFN_CHUNKS:
            stages.append(functools.partial(chunk, c0, width))
            c0 += width
        stages.append(lambda: out.__setitem__(Ellipsis, norm(2, st["z"])))
        return stages

    @pl.when(t == 0)
    def _():
        for stage in attention_stages():
            stage()
        x2_prev[...] = x2_scr[...]

    @pl.when((t > 0) & (t < n_tiles))
    def _():
        _emit_round_robin(ffn_stages(x2_prev, o_ref), attention_stages())
        x2_prev[...] = x2_scr[...]

    @pl.when(t == n_tiles)
    def _():
        for stage in ffn_stages(x2_prev, o_ref):
            stage()


def _tail(xf, parts, ws, mem, wq_all, wkv_all, wo_all, wu_all, wd_all, g_all, b_all, layer, batch, seq):
    n = xf.shape[0]
    tm = ROW_TILE
    nt = seq // tm
    n_tiles = n // tm
    n_mem = mem.shape[1]
    cur = lambda t: jnp.minimum(t, n_tiles - 1)
    row = lambda t: (cur(t), 0)
    lagged = lambda t: (jnp.maximum(t - 1, 0), 0)
    full = lambda t: (0, 0)
    this_layer = lambda t: (layer, 0, 0)
    stacked = lambda a: _resident((None,) + a.shape[1:], this_layer)
    return pl.pallas_call(
        functools.partial(_tail_kernel, n_parts=len(parts), tiles_per_batch=nt),
        grid=(n_tiles + 1,),
        in_specs=[pl.BlockSpec((tm, D_MODEL), row)]
                 + [pl.BlockSpec((tm, a.shape[1]), row) for a in parts]
                 + [_resident(w.shape, full) for w in ws]
                 + [stacked(wq_all),
                    _resident((None, n_mem, D_MODEL), lambda t: (cur(t) // nt, 0, 0)),
                    stacked(wkv_all),
                    stacked(wo_all), stacked(wu_all), stacked(wd_all), stacked(g_all), stacked(b_all)],
        out_specs=pl.BlockSpec((tm, D_MODEL), lagged),
        out_shape=jax.ShapeDtypeStruct((n, D_MODEL), F32),
        scratch_shapes=[pltpu.VMEM((tm, D_MODEL), F32), pltpu.VMEM((tm, D_MODEL), F32),
                        pltpu.VMEM((tm, D_MODEL), F32), pltpu.VMEM((n_mem, 2 * D_MODEL), BF16)],
        compiler_params=_params(1),
        name="tail",
    )(xf, *parts, *ws, wq_all, mem, wkv_all, wo_all, wu_all, wd_all, g_all, b_all)


def _gelu_tanh(x):
    inner = x * (GELU_C0 + GELU_C1 * (x * x))
    return x * (0.5 * jnp.tanh(inner) + 0.5)


def _sigmoid_tanh(z):
    return 0.5 * jnp.tanh(0.5 * z) + 0.5


def _odd_kernel(x_ref, pos_ref, perm_ref, unperm_ref, win_ref, cw_ref, cb_ref, wax_ref, ba_ref, bx_ref, lam_ref,
                wout_ref, g_ref, b_ref, o_ref, xbuf, gate_scr, a_scr, b_scr, tail_scr, h_scr):
    i = pl.program_id(1)
    tm = x_ref.shape[0]
    half = tm // 2
    nv = half // SUBLANES
    halo = (CONV_WIDTH - 1) * SUBLANES

    @pl.when(i == 0)
    def _():
        tail_scr[...] = jnp.zeros((SUBLANES, RNN_WIDTH), F32)
        h_scr[...] = jnp.zeros((SUBLANES, RNN_WIDTH), F32)

    def permute(v2d):
        return jnp.swapaxes(v2d.reshape(SUBLANES, nv, v2d.shape[1]), 0, 1).reshape(v2d.shape)

    sub = lax.broadcasted_iota(jnp.int32, (SUBLANES, RNN_WIDTH), 0)
    lam = lam_ref[...]
    half_scale = (-0.5 * LRU_C) * (jnp.maximum(-lam, 0.0) + jnp.log1p(jnp.exp(-jnp.abs(lam))))

    col_chunk = 2 * LANES

    def in_proj(hi, h0):
        st = {}

        def permute_rows():
            st["xb"] = _dot(perm_ref[...], x_ref[h0:h0 + half, :].astype(BF16)).astype(BF16)

        def gate_cols(c):
            gate_scr[h0:h0 + half, c:c + col_chunk] = _dot(st["xb"], win_ref[:, c:c + col_chunk])

        def input_cols(c):
            xbuf[hi, halo:halo + half, c:c + col_chunk] = _dot(
                st["xb"], win_ref[:, RNN_WIDTH + c:RNN_WIDTH + c + col_chunk])

        cols = range(0, RNN_WIDTH, col_chunk)
        return ([permute_rows] + [functools.partial(gate_cols, c) for c in cols]
                + [functools.partial(input_cols, c) for c in cols])

    def gates(hi, h0):
        st = {}

        def prepare():
            for j in range(1, CONV_WIDTH):
                last = xbuf[hi, halo + (nv - j) * SUBLANES:halo + (nv - j + 1) * SUBLANES, :]
                before = jnp.where(sub == 0, tail_scr[j:j + 1, :], pltpu.roll(last, 1, 0))
                xbuf[hi, halo - j * SUBLANES:halo - (j - 1) * SUBLANES, :] = before
                tail_scr[j:j + 1, :] = last[SUBLANES - 1:SUBLANES, :]
            is_reset = (pos_ref[h0:h0 + half, :] == 0).astype(F32)
            st["reset"] = permute(jnp.broadcast_to(is_reset, (half, RNN_BLOCK_W))) > 0.5

        def block(h):
            reset = st["reset"]
            sl = slice(h * RNN_BLOCK_W, (h + 1) * RNN_BLOCK_W)
            xc = cb_ref[:, sl]
            for k in range(CONV_WIDTH):
                xc = xc + xbuf[hi, k * SUBLANES:k * SUBLANES + half, sl] * cw_ref[k:k + 1, sl]
            ri = _dot(xc.astype(BF16), wax_ref[h])
            t_r = jnp.tanh(ri[:, :RNN_BLOCK_W] + ba_ref[:, sl])
            t_i = jnp.tanh(ri[:, RNN_BLOCK_W:] + bx_ref[:, sl])
            log_a = t_r * half_scale[:, sl] + half_scale[:, sl]
            a = jnp.where(reset, 0.0, jnp.exp(log_a))
            th = jnp.tanh(log_a)
            q2 = -0.5 * th / (1.0 - th)
            half_mult = jnp.where(reset, 0.5, jnp.where(q2 > 0.0, q2 * lax.rsqrt(q2), 0.0))
            b = half_mult * (t_i * xc + xc)
            a_scr[h0:h0 + half, sl] = a
            b_scr[h0:h0 + half, sl] = b

        return [prepare] + [functools.partial(block, h) for h in range(RNN_BLOCKS)]

    def recur(h0, h_prev):
        decay = jnp.ones((SUBLANES, RNN_WIDTH), F32)
        resp = jnp.zeros((SUBLANES, RNN_WIDTH), F32)
        for v in range(nv):
            rs = slice(h0 + v * SUBLANES, h0 + (v + 1) * SUBLANES)
            av = a_scr[rs, :]
            decay = av * decay
            resp = av * resp + b_scr[rs, :]
            a_scr[rs, :] = decay
            b_scr[rs, :] = resp
        ca, cb_ = decay, resp
        for d in (1, 2, 4):
            ok = sub >= d
            cb_ = jnp.where(ok, ca * pltpu.roll(cb_, d, 0) + cb_, cb_)
            ca = jnp.where(ok, ca * pltpu.roll(ca, d, 0), ca)
        h_end = ca * h_prev + cb_
        h_in = jnp.where(sub == 0, h_prev, pltpu.roll(h_end, 1, 0))
        rs = slice(h0, h0 + half)
        hh = (a_scr[rs, :].reshape(nv, SUBLANES, RNN_WIDTH) * h_in[None]
              + b_scr[rs, :].reshape(nv, SUBLANES, RNN_WIDTH))
        b_scr[rs, :] = hh.reshape(half, RNN_WIDTH)
        return jnp.broadcast_to(h_end[SUBLANES - 1:SUBLANES, :], (SUBLANES, RNN_WIDTH))

    def out_proj(h0):
        rs = slice(h0, h0 + half)
        st = {}

        def gate_output():
            st["yp"] = (b_scr[rs, :] * _gelu_tanh(gate_scr[rs, :])).astype(BF16)

        def unpermute_rows():
            st["y"] = _dot(unperm_ref[...], st["yp"]).astype(BF16)

        def project():
            z = ALPHA * x_ref[rs, :] + _dot(st["y"], wout_ref[...])
            o_ref[rs, :] = _layer_norm(z, g_ref[...], b_ref[...])

        return [gate_output, unpermute_rows, project]

    _emit_round_robin(in_proj(0, 0))
    _emit_round_robin(gates(0, 0), in_proj(1, half))
    h_mid = recur(0, h_scr[...])
    _emit_round_robin(gates(1, half), out_proj(0))
    h_scr[...] = recur(half, h_mid)
    _emit_round_robin(out_proj(half))


def _odd_layer(xf, posi, w_in, cw, cb, wax, ba, bx, lam, w_out, g, b, batch, seq):
    n = xf.shape[0]
    tm = ROW_TILE
    nt = seq // tm
    row = lambda bi, i: (bi * nt + i, 0)
    full = lambda bi, i: (0, 0)
    half = tm // 2
    rho = np.arange(half)
    src_time = (rho % SUBLANES) * (half // SUBLANES) + rho // SUBLANES
    perm = np.zeros((half, half), np.float32)
    perm[rho, src_time] = 1.0
    consts = [jnp.asarray(perm, BF16), jnp.asarray(perm.T, BF16), w_in, cw, cb]
    rest = [ba, bx, lam, w_out, g, b]
    return pl.pallas_call(
        _odd_kernel,
        grid=(batch, nt),
        in_specs=[pl.BlockSpec((tm, D_MODEL), row), pl.BlockSpec((tm, 1), row)]
                 + [pl.BlockSpec(a.shape, full) for a in consts]
                 + [pl.BlockSpec(wax.shape, lambda bi, i: (0, 0, 0))]
                 + [pl.BlockSpec(a.shape, full) for a in rest],
        out_specs=pl.BlockSpec((tm, D_MODEL), row),
        out_shape=jax.ShapeDtypeStruct((n, D_MODEL), F32),
        scratch_shapes=[pltpu.VMEM((2, (CONV_WIDTH - 1) * SUBLANES + tm // 2, RNN_WIDTH), F32),
                        pltpu.VMEM((tm, RNN_WIDTH), F32),
                        pltpu.VMEM((tm, RNN_WIDTH), F32),
                        pltpu.VMEM((tm, RNN_WIDTH), F32),
                        pltpu.VMEM((SUBLANES, RNN_WIDTH), F32),
                        pltpu.VMEM((SUBLANES, RNN_WIDTH), F32)],
        compiler_params=_params(2),
        name="odd_layer",
    )(xf, posi, *consts, wax, *rest)


def _even_mixer(xf, positions, w_in, pool_w, pool_scale, cmp_pos_k, cmp_pos_v, cmp_wk, cmp_wv,
                w_out, casts, batch, seq):
    assert seq // SEL_BLOCK == LANES and seq % SEL_KEY_TILE == 0 and seq >= WINDOW + Q_BLOCK
    hd, g, r = NSA_HEAD_DIM, NSA_KV_HEADS, NSA_GROUP
    c0 = POOL_WIDTH
    c1 = c0 + Q_WIDTH
    c2 = c1 + GATE_WIDTH
    wp = w_in[:, :c0].astype(BF16)
    wq = (w_in[:, c0:c1].reshape(D_MODEL, g, r, hd).transpose(0, 2, 1, 3).reshape(D_MODEL, Q_WIDTH)
          * (hd ** -0.5 * LOG2E)).astype(BF16)
    wg = jnp.pad(w_in[:, c1:c2], ((0, 0), (0, LANES - GATE_WIDTH))).astype(BF16)
    wkv = w_in[:, c2:].astype(BF16)
    n = batch * seq
    nc = seq // CMP_STRIDE
    n_cmp = (seq - CMP_BLOCK) // CMP_STRIDE + 1
    posc = jnp.pad(positions[:, CMP_BLOCK - 1::CMP_STRIDE][:, :n_cmp], ((0, 0), (0, nc - n_cmp)))
    pos_all = jnp.concatenate([positions.reshape(1, n), posc.reshape(1, batch * nc)], axis=1).astype(F32)
    cos_t, sin_t = _rope_angles(pos_all)
    ec, es = _rope_spread()
    pool, q, kc_raw, vc_raw, ks, vs, kw, vw, gates = _proj_even(
        xf, cos_t, sin_t, ec, es, wp, wq, wkv, wg, pool_w.astype(BF16), pool_scale[None, :], batch, seq)

    def halves(w):
        w3 = w.reshape(CMP_BLOCK, hd, hd).astype(BF16)
        zero = jnp.zeros_like(w3)
        w4 = jnp.concatenate([jnp.concatenate([w3 if k == j else zero for k in range(g)], axis=2)
                              for j in range(g)], axis=1)
        return (w4[:CMP_STRIDE].reshape(CMP_STRIDE * g * hd, g * hd),
                w4[CMP_STRIDE:].reshape(CMP_STRIDE * g * hd, g * hd))

    def pos_halves(p):
        p2 = jnp.tile(p[:, None, :], (1, g, 1)).reshape(CMP_BLOCK, g * hd)
        return p2[:CMP_STRIDE].reshape(1, -1), p2[CMP_STRIDE:].reshape(1, -1)

    pk1, pk2 = pos_halves(cmp_pos_k)
    pv1, pv2 = pos_halves(cmp_pos_v)
    wk1, wk2 = halves(cmp_wk)
    wv1, wv2 = halves(cmp_wv)
    seq3 = lambda a: a.reshape(batch, seq, KV_WIDTH)
    kc, vc = _compress(seq3(kc_raw), seq3(vc_raw), cos_t, sin_t, n // nc,
                       (ec, es, pk1, pk2, pv1, pv2, wk1, wk2, wv1, wv2))

    n_sb = seq // SEL_BLOCK
    starts = np.arange(nc) * CMP_STRIDE
    jb = np.arange(n_sb)
    ov = ((starts[:, None] < (jb[None, :] + 1) * SEL_BLOCK)
          & (starts[:, None] + CMP_BLOCK > jb[None, :] * SEL_BLOCK) & (np.arange(nc)[:, None] < n_cmp))
    ov = jnp.asarray(ov, BF16)
    pt = jnp.asarray(np.where(np.arange(seq)[:, None] // SEL_BLOCK == jb[None, :], NEG, 0.0), BF16)

    nqb = seq // Q_BLOCK
    nsa_args = (q, gates, kc, vc, seq3(ks), seq3(vs), seq3(kw), seq3(vw), pt, ov)
    nsa, cast_a = _nsa(*nsa_args, casts[:len(casts) // 2], batch, seq, 0, nqb // 2)
    nsa, cast_b = _nsa(*nsa_args, casts[len(casts) // 2:], batch, seq, nqb // 2, nqb - nqb // 2, prev_out=nsa)
    cast_weights = cast_a + cast_b

    w_pool_out = w_out[:POOL_WIDTH].astype(BF16)
    w_nsa_out = (w_out[POOL_WIDTH:].reshape(g, r, hd, D_MODEL).transpose(1, 0, 2, 3)
                 .reshape(Q_WIDTH, D_MODEL).astype(BF16))
    return [pool, nsa], [w_pool_out, w_nsa_out], cast_weights


def _odd_mixer(xf, posi, w_in, conv_w, conv_b, wa, ba, wx, bx, lam, w_out, ln_g, ln_b, batch, seq):
    wax = (0.5 * jnp.concatenate([wa, wx], axis=2)).astype(BF16)
    return _odd_layer(xf, posi, w_in, conv_w, conv_b[None, :], wax, 0.5 * ba[None, :],
                      0.5 * bx[None, :], lam[None, :], w_out, ln_g, ln_b, batch, seq)


def kernel(x, mem, positions, e_w_in, e_pool_w, e_pool_scale, e_cmp_pos_k, e_cmp_pos_v, e_cmp_wk, e_cmp_wv, e_w_out, o_w_in, o_conv_w, o_conv_b, o_wa, o_ba, o_wx, o_bx, o_lambda, o_w_out, x_wq, x_wkv, x_wo, f_w_up, f_w_down, ln_g, ln_b):
    batch, seq, d = x.shape
    n = batch * seq
    assert d == D_MODEL and seq % ROW_TILE == 0
    xf = x.reshape(n, d)
    posi = positions.reshape(n, 1)
    casts = [(x_wq, X_HEAD_DIM ** -0.5 * LOG2E), (x_wo, 1.0), (f_w_up, 1.0), (f_w_down, 1.0),
             (o_w_in, 1.0), (o_w_out, 1.0), (x_wkv, 1.0)]
    for layer in range(DEPTH):
        j = layer // 2
        if layer % 2 == 0:
            parts, ws, cast_weights = _even_mixer(
                xf, positions, e_w_in[j], e_pool_w[j], e_pool_scale[j], e_cmp_pos_k[j], e_cmp_pos_v[j],
                e_cmp_wk[j], e_cmp_wv[j], e_w_out[j], casts if layer == 0 else [], batch, seq)
            if layer == 0:
                wq_all, wo_all, wu_all, wd_all, o_w_in16, o_w_out16, wkv_all = cast_weights
        else:
            xf = _odd_mixer(xf, posi, o_w_in16[j], o_conv_w[j], o_conv_b[j], o_wa[j], o_ba[j], o_wx[j], o_bx[j],
                            o_lambda[j], o_w_out16[j], ln_g[layer, 0][None, :], ln_b[layer, 0][None, :],
                            batch, seq)
            parts, ws = [], []
        xf = _tail(xf, parts, ws, mem, wq_all, wkv_all, wo_all, wu_all, wd_all, ln_g, ln_b, layer, batch, seq)
    return xf.reshape(batch, seq, d)
```
